```python
import math
import jax, jax.numpy as jnp
from jax import lax
import numpy as np

D_MODEL = 1024
BATCH = 16
SEQ = 256
DEPTH = 4
DEC_BATCH = 8
DEC_SEQ = 1024
PAST_LEN = 512

GRID_W = 64
N_HEADS = 4
DK = D_MODEL // 8
DV = D_MODEL // 4
DK_TOT = N_HEADS * DK
DV_TOT = N_HEADS * DV
GATE_RANK = 16
GATE_TAU = 16.0
CHUNK = 64
N_GLA = (DEPTH + 1) // 2
N_MLSTM = DEPTH // 2
GLA_IN = 2 * DK_TOT + 2 * DV_TOT + 2 * GATE_RANK
MLSTM_IN = 2 * DK_TOT + 2 * DV_TOT + 4 * N_HEADS
N_GROUPS = 4
EXPERTS_PER_GROUP = 8
N_EXPERTS = N_GROUPS * EXPERTS_PER_GROUP
D_EXPERT = D_MODEL // 4
TOP_K_IN_GROUP = 2
DN_ALPHA = (2.0 * DEPTH) ** 0.25
DN_BETA = (8.0 * DEPTH) ** -0.25
LN_EPS = 1e-5

kernel_name = "hybrid_gla_mlstm_hmoe_diffusion_step"

F32 = jnp.float32


def _layernorm(x, g, b):
    xf = x.astype(F32)
    mu = jnp.mean(xf, -1, keepdims=True)
    var = jnp.mean(jnp.square(xf - mu), -1, keepdims=True)
    return ((xf - mu) * lax.rsqrt(var + LN_EPS) * g + b).astype(x.dtype)


def _to_heads(a, d):
    B, T, _ = a.shape
    return a.reshape(B, T, N_HEADS, d).transpose(0, 2, 1, 3).astype(F32)


def _flip(a):
    return jnp.flip(a, axis=2)


def _head_norm(o, g):
    mu = jnp.mean(o, -1, keepdims=True)
    var = jnp.mean(jnp.square(o - mu), -1, keepdims=True)
    o = (o - mu) * lax.rsqrt(var + LN_EPS)
    B, H, T, _ = o.shape
    return o.transpose(0, 2, 1, 3).reshape(B, T, H * DV) * g


def _grid_posemb(T):
    rows = T // GRID_W
    r, cidx = jnp.meshgrid(jnp.arange(rows), jnp.arange(GRID_W), indexing="ij")
    nf = D_MODEL // 4
    freqs = jnp.exp(-math.log(10000.0) * jnp.arange(nf, dtype=F32) / nf)

    def emb(p):
        a = p.reshape(-1).astype(F32)[:, None] * freqs
        return jnp.concatenate([jnp.sin(a), jnp.cos(a)], -1)

    return jnp.concatenate([emb(r), emb(cidx)], -1)


def _gla_chunked(q, k, v, g, S0):
    B, H, T, _ = q.shape
    N = T // CHUNK
    q = q.reshape(B, H, N, CHUNK, DK)
    k = k.reshape(B, H, N, CHUNK, DK)
    v = v.reshape(B, H, N, CHUNK, DV)
    b = jnp.cumsum(g.reshape(B, H, N, CHUNK, DK), axis=3)
    b_last = b[:, :, :, -1]
    b_ref = b[:, :, :, CHUNK // 2:CHUNK // 2 + 1]
    mask = jnp.tril(jnp.ones((CHUNK, CHUNK), dtype=bool))
    a = jnp.einsum("bhncd,bhnsd->bhncs", q * jnp.exp(b - b_ref), k * jnp.exp(b_ref - b))
    a = jnp.where(mask, a, 0.0)
    o_intra = jnp.einsum("bhncs,bhnse->bhnce", a, v)
    d_state = jnp.einsum("bhncd,bhnce->bhnde", k * jnp.exp(b_last[:, :, :, None] - b), v)
    decay = jnp.exp(b_last)

    def step(S, inp):
        d, u = inp
        return d[..., None] * S + u, S

    S_f, S_starts = lax.scan(step, S0, (jnp.moveaxis(decay, 2, 0), jnp.moveaxis(d_state, 2, 0)))
    S_starts = jnp.moveaxis(S_starts, 0, 2)
    o_inter = jnp.einsum("bhncd,bhnde->bhnce", q * jnp.exp(b), S_starts)
    return (o_intra + o_inter).reshape(B, H, T, DV), S_f


def _mlstm_chunked(q, k, v, i_pre, log_f, C0, n0, m0):
    B, H, T, _ = q.shape
    N = T // CHUNK
    q = q.reshape(B, H, N, CHUNK, DK)
    k = k.reshape(B, H, N, CHUNK, DK)
    v = v.reshape(B, H, N, CHUNK, DV)
    i_pre = i_pre.reshape(B, H, N, CHUNK)
    b = jnp.cumsum(log_f.reshape(B, H, N, CHUNK), axis=-1)
    b_last = b[..., -1]
    mask = jnp.tril(jnp.ones((CHUNK, CHUNK), dtype=bool))
    d_log = jnp.where(mask, b[..., :, None] - b[..., None, :] + i_pre[..., None, :], -jnp.inf)
    a = b_last[..., None] - b + i_pre
    m_loc = jnp.max(a, -1)
    kw = k * jnp.exp(a - m_loc[..., None])[..., None]
    kv_loc = jnp.einsum("bhncd,bhnce->bhnde", kw, v)
    k_loc = jnp.sum(kw, axis=3)

    def step(carry, inp):
        C, n, m = carry
        bl, ml, kvl, kl = inp
        m_new = jnp.maximum(bl + m, ml)
        s_old = jnp.exp(bl + m - m_new)
        s_loc = jnp.exp(ml - m_new)
        C_new = s_old[..., None, None] * C + s_loc[..., None, None] * kvl
        n_new = s_old[..., None] * n + s_loc[..., None] * kl
        return (C_new, n_new, m_new), (C, n, m)

    mv = lambda t: jnp.moveaxis(t, 2, 0)
    (C_f, n_f, m_f), (C_s, n_s, m_s) = lax.scan(
        step, (C0, n0, m0), (mv(b_last), mv(m_loc), mv(kv_loc), mv(k_loc)))
    C_s = jnp.moveaxis(C_s, 0, 2)
    n_s = jnp.moveaxis(n_s, 0, 2)
    m_s = jnp.moveaxis(m_s, 0, 2)
    l_inter = b + m_s[..., None]
    m_i = jnp.maximum(l_inter, jnp.max(d_log, -1))
    s = jnp.einsum("bhncd,bhnsd->bhncs", q, k) * jnp.exp(d_log - m_i[..., None])
    e_inter = jnp.exp(l_inter - m_i)
    num = jnp.einsum("bhncs,bhnse->bhnce", s, v) + e_inter[..., None] * jnp.einsum("bhncd,bhnde->bhnce", q, C_s)
    den = jnp.sum(s, -1) + e_inter * jnp.einsum("bhncd,bhnd->bhnc", q, n_s)
    h = num / jnp.maximum(jnp.abs(den), jnp.exp(-m_i))[..., None]
    return h.reshape(B, H, T, DV), C_f, n_f, m_f


def _gla_mixer(h, w_in, w_gate, b_gate, norm_g, w_out, S0_f, S0_b):
    B, T, _ = h.shape
    p = h @ w_in
    q, k, v, r, glr = jnp.split(p, [DK_TOT, 2 * DK_TOT, 2 * DK_TOT + DV_TOT, 2 * DK_TOT + 2 * DV_TOT], axis=-1)
    glr = glr.reshape(B, T, 2, GATE_RANK)
    g = jax.nn.log_sigmoid((jnp.einsum("btzr,zrk->btzk", glr, w_gate) + b_gate).astype(F32)) / GATE_TAU
    g_f = _to_heads(g[:, :, 0], DK)
    g_b = _to_heads(g[:, :, 1], DK)
    qh = _to_heads(q, DK) * (DK ** -0.5)
    kh = _to_heads(k, DK)
    vh = _to_heads(v, DV)
    o_f, S_f = _gla_chunked(qh, kh, vh, g_f, S0_f.astype(F32))
    o_b, S_b = _gla_chunked(_flip(qh), _flip(kh), _flip(vh), _flip(g_b), S0_b.astype(F32))
    o = _head_norm(o_f + _flip(o_b), norm_g) * jax.nn.silu(r.astype(F32))
    return o.astype(h.dtype) @ w_out, S_f, S_b


def _mlstm_mixer(h, w_in, b_gates, norm_g, w_out, C0_f, n0_f, m0_f, C0_b, n0_b, m0_b):
    B, T, _ = h.shape
    p = h @ w_in
    q, k, v, og, gt = jnp.split(p, [DK_TOT, 2 * DK_TOT, 2 * DK_TOT + DV_TOT, 2 * DK_TOT + 2 * DV_TOT], axis=-1)
    gt = (gt.reshape(B, T, 4, N_HEADS).astype(F32) + b_gates).transpose(2, 0, 3, 1)
    i_f, f_f, i_b, f_b = gt[0], gt[1], gt[2], gt[3]
    qh = _to_heads(q, DK) * (DK ** -0.5)
    kh = _to_heads(k, DK)
    vh = _to_heads(v, DV)
    h_f, C_f, n_f, m_f = _mlstm_chunked(qh, kh, vh, i_f, jax.nn.log_sigmoid(f_f),
                                        C0_f.astype(F32), n0_f.astype(F32), m0_f.astype(F32))
    h_b, C_b, n_b, m_b = _mlstm_chunked(_flip(qh), _flip(kh), _flip(vh), _flip(i_b), _flip(jax.nn.log_sigmoid(f_b)),
                                        C0_b.astype(F32), n0_b.astype(F32), m0_b.astype(F32))
    o = _head_norm(h_f + _flip(h_b), norm_g) * jax.nn.sigmoid(og.astype(F32))
    return o.astype(h.dtype) @ w_out, (C_f, n_f, m_f), (C_b, n_b, m_b)


def _hier_moe(h, w_group, b_group, w_expert, b_expert, w_gate, w_up, w_down):
    B, T, D = h.shape
    hf = h.reshape(B * T, D)
    glog = (hf @ w_group + b_group).astype(F32)
    gprob = jax.nn.softmax(glog, -1)
    gsel = jnp.argmax(glog, -1)
    gw = jnp.take_along_axis(gprob, gsel[:, None], 1)[:, 0]
    elog = (hf @ w_expert + b_expert).astype(F32).reshape(-1, N_GROUPS, EXPERTS_PER_GROUP)
    elog_sel = jnp.take_along_axis(elog, gsel[:, None, None], 1)[:, 0]
    top_p, top_i = lax.top_k(jax.nn.softmax(elog_sel, -1), TOP_K_IN_GROUP)
    top_p = top_p / jnp.sum(top_p, -1, keepdims=True)
    eid = gsel[:, None] * EXPERTS_PER_GROUP + top_i
    comb = jnp.sum(jax.nn.one_hot(eid, N_EXPERTS, dtype=F32) * (gw[:, None] * top_p)[..., None], axis=1)
    a = jnp.einsum("nd,edf->nef", hf, w_gate)
    u = jnp.einsum("nd,edf->nef", hf, w_up)
    hid = jax.nn.silu(a) * u * comb[..., None].astype(h.dtype)
    return jnp.einsum("nef,efd->nd", hid, w_down).reshape(B, T, D)


def _modulation(cvec, w, b):
    return (jax.nn.silu(cvec) @ w + b).reshape(cvec.shape[0], 6, D_MODEL)


def setup_inputs(seed: int = 0) -> dict:
    key = jax.random.key(seed)
    ks = jax.random.split(key, 32)
    nrm = lambda k, shape, s: jax.random.normal(k, shape, F32) * s
    gla_w_in = nrm(ks[8], (N_GLA, D_MODEL, GLA_IN), D_MODEL ** -0.5)
    gla_w_in = gla_w_in.at[:, :, 2 * DK_TOT:2 * DK_TOT + DV_TOT].multiply(DN_BETA)
    mlstm_w_in = nrm(ks[13], (N_MLSTM, D_MODEL, MLSTM_IN), D_MODEL ** -0.5)
    mlstm_w_in = mlstm_w_in.at[:, :, 2 * DK_TOT:2 * DK_TOT + DV_TOT].multiply(DN_BETA)
    f_base = jnp.linspace(3.0, 6.0, N_HEADS, dtype=F32)
    gate_base = jnp.zeros((4, N_HEADS), F32).at[1].set(f_base).at[3].set(f_base)
    return {
        "x_prompt": nrm(ks[0], (BATCH, SEQ, D_MODEL), 1.0),
        "x_sample": nrm(ks[1], (DEC_BATCH, DEC_SEQ, D_MODEL), 1.0),
        "state_gla_S": nrm(ks[2], (DEC_BATCH, N_GLA, 2, N_HEADS, DK, DV), 0.5),
        "state_mlstm_C": nrm(ks[3], (DEC_BATCH, N_MLSTM, 2, N_HEADS, DK, DV), 0.5),
        "state_mlstm_n": nrm(ks[4], (DEC_BATCH, N_MLSTM, 2, N_HEADS, DK), 0.5),
        "state_mlstm_m": nrm(ks[5], (DEC_BATCH, N_MLSTM, 2, N_HEADS), 1.0),
        "c": nrm(ks[6], (DEC_BATCH, D_MODEL), 1.0),
        "c_ctx": nrm(ks[7], (D_MODEL,), 1.0),
        "gla_w_in": gla_w_in,
        "gla_w_gate": nrm(ks[9], (N_GLA, 2, GATE_RANK, DK_TOT), GATE_RANK ** -0.5),
        "gla_b_gate": nrm(ks[10], (N_GLA, 2, DK_TOT), 0.1),
        "gla_norm_g": 1.0 + nrm(ks[11], (N_GLA, DV_TOT), 0.02),
        "gla_w_out": nrm(ks[12], (N_GLA, DV_TOT, D_MODEL), DV_TOT ** -0.5 * DN_BETA),
        "mlstm_w_in": mlstm_w_in,
        "mlstm_b_gates": gate_base + nrm(ks[14], (N_MLSTM, 4, N_HEADS), 0.1),
        "mlstm_norm_g": 1.0 + nrm(ks[15], (N_MLSTM, DV_TOT), 0.02),
        "mlstm_w_out": nrm(ks[16], (N_MLSTM, DV_TOT, D_MODEL), DV_TOT ** -0.5 * DN_BETA),
        "adaln_w": nrm(ks[17], (DEPTH, D_MODEL, 6 * D_MODEL), 0.5 * D_MODEL ** -0.5),
        "adaln_b": nrm(ks[18], (DEPTH, 6 * D_MODEL), 0.02),
        "ln_g": 1.0 + nrm(ks[19], (DEPTH, 2, D_MODEL), 0.02),
        "ln_b": nrm(ks[20], (DEPTH, 2, D_MODEL), 0.02),
        "moe_w_group": nrm(ks[21], (DEPTH, D_MODEL, N_GROUPS), D_MODEL ** -0.5),
        "moe_b_group": nrm(ks[22], (DEPTH, N_GROUPS), 0.01),
        "moe_w_expert": nrm(ks[23], (DEPTH, D_MODEL, N_EXPERTS), D_MODEL ** -0.5),
        "moe_b_expert": nrm(ks[24], (DEPTH, N_EXPERTS), 0.01),
        "moe_w_gate": nrm(ks[25], (DEPTH, N_EXPERTS, D_MODEL, D_EXPERT), D_MODEL ** -0.5),
        "moe_w_up": nrm(ks[26], (DEPTH, N_EXPERTS, D_MODEL, D_EXPERT), D_MODEL ** -0.5),
        "moe_w_down": nrm(ks[27], (DEPTH, N_EXPERTS, D_EXPERT, D_MODEL), D_EXPERT ** -0.5 * DN_BETA),
    }


def reference(x_prompt, x_sample, state_gla_S, state_mlstm_C, state_mlstm_n, state_mlstm_m, c, c_ctx,
              gla_w_in, gla_w_gate, gla_b_gate, gla_norm_g, gla_w_out,
              mlstm_w_in, mlstm_b_gates, mlstm_norm_g, mlstm_w_out,
              adaln_w, adaln_b, ln_g, ln_b,
              moe_w_group, moe_b_group, moe_w_expert, moe_b_expert, moe_w_gate, moe_w_up, moe_w_down):
    Bc = x_prompt.shape[0]
    x_ctx = x_prompt
    x_lat = x_sample + _grid_posemb(x_sample.shape[1]).astype(x_sample.dtype)[None]
    gla_states, ml_C, ml_n, ml_m = [], [], [], []
    for l in range(DEPTH):
        s = l // 2
        mod_c = _modulation(c_ctx[None], adaln_w[l], adaln_b[l])[:, None]
        mod_l = _modulation(c, adaln_w[l], adaln_b[l])[:, None]
        h_c = x_ctx * (1.0 + mod_c[:, :, 1]) + mod_c[:, :, 0]
        h_l = x_lat * (1.0 + mod_l[:, :, 1]) + mod_l[:, :, 0]
        if l % 2 == 0:
            z = jnp.zeros((Bc, N_HEADS, DK, DV), F32)
            y_c, S_f, S_b = _gla_mixer(h_c, gla_w_in[s], gla_w_gate[s], gla_b_gate[s], gla_norm_g[s], gla_w_out[s], z, z)
            y_l, _, _ = _gla_mixer(h_l, gla_w_in[s], gla_w_gate[s], gla_b_gate[s], gla_norm_g[s], gla_w_out[s],
                                   state_gla_S[:, s, 0], state_gla_S[:, s, 1])
            gla_states.append(jnp.stack([S_f, S_b], 1))
        else:
            zC = jnp.zeros((Bc, N_HEADS, DK, DV), F32)
            zn = jnp.zeros((Bc, N_HEADS, DK), F32)
            zm = jnp.zeros((Bc, N_HEADS), F32)
            y_c, st_f, st_b = _mlstm_mixer(h_c, mlstm_w_in[s], mlstm_b_gates[s], mlstm_norm_g[s], mlstm_w_out[s],
                                           zC, zn, zm, zC, zn, zm)
            y_l, _, _ = _mlstm_mixer(h_l, mlstm_w_in[s], mlstm_b_gates[s], mlstm_norm_g[s], mlstm_w_out[s],
                                     state_mlstm_C[:, s, 0], state_mlstm_n[:, s, 0], state_mlstm_m[:, s, 0],
                                     state_mlstm_C[:, s, 1], state_mlstm_n[:, s, 1], state_mlstm_m[:, s, 1])
            ml_C.append(jnp.stack([st_f[0], st_b[0]], 1))
            ml_n.append(jnp.stack([st_f[1], st_b[1]], 1))
            ml_m.append(jnp.stack([st_f[2], st_b[2]], 1))
        x_ctx = _layernorm(DN_ALPHA * x_ctx + mod_c[:, :, 2] * y_c, ln_g[l, 0], ln_b[l, 0])
        x_lat = _layernorm(DN_ALPHA * x_lat + mod_l[:, :, 2] * y_l, ln_g[l, 0], ln_b[l, 0])
        h_c = x_ctx * (1.0 + mod_c[:, :, 4]) + mod_c[:, :, 3]
        h_l = x_lat * (1.0 + mod_l[:, :, 4]) + mod_l[:, :, 3]
        moe_args = (moe_w_group[l], moe_b_group[l], moe_w_expert[l], moe_b_expert[l],
                    moe_w_gate[l], moe_w_up[l], moe_w_down[l])
        x_ctx = _layernorm(DN_ALPHA * x_ctx + mod_c[:, :, 5] * _hier_moe(h_c, *moe_args), ln_g[l, 1], ln_b[l, 1])
        x_lat = _layernorm(DN_ALPHA * x_lat + mod_l[:, :, 5] * _hier_moe(h_l, *moe_args), ln_g[l, 1], ln_b[l, 1])
    dt = x_prompt.dtype
    new_gla_S = jnp.stack(gla_states, 1).astype(dt)
    new_mlstm_C = jnp.stack(ml_C, 1).astype(dt)
    new_mlstm_n = jnp.stack(ml_n, 1).astype(dt)
    new_mlstm_m = jnp.stack(ml_m, 1).astype(dt)
    return (x_ctx, x_lat, new_gla_S, new_mlstm_C, new_mlstm_n, new_mlstm_m)
```

```python
import functools
import math

import jax
import jax.numpy as jnp
from jax import lax
from jax.experimental import pallas as pl
from jax.experimental.pallas import tpu as pltpu

F32 = jnp.float32
BF16 = jnp.bfloat16

D = 1024
N_CTX_SEQ, T_CTX = 16, 256
N_LAT_SEQ, T_LAT = 8, 1024
N_CTX = N_CTX_SEQ * T_CTX
N_LAT = N_LAT_SEQ * T_LAT
N_TOK = N_CTX + N_LAT
DEPTH = 4
GRID_W = 64
NH = 4
DK = 128
DV = 256
DKT = NH * DK
DVT = NH * DV
GATE_RANK = 16
GATE_TAU = 16.0
CH = 64
N_GROUPS = 4
EPG = 8
NE = N_GROUPS * EPG
DE = 256
DN_ALPHA = (2.0 * DEPTH) ** 0.25
LN_EPS = 1e-5
MOD_ROWS = 16
TB = 256
NB = N_TOK // TB
TM = 1024
ROUTE_W = 128
NEG_INF = float("-inf")
VMEM_LIMIT = 48 * 1024 * 1024


def _cp(sem):
    return pltpu.CompilerParams(dimension_semantics=sem, vmem_limit_bytes=VMEM_LIMIT)


def _dot(a, b):
    return jnp.dot(a, b, preferred_element_type=F32)


def _dot_nt(a, b):
    return lax.dot_general(a, b, (((1,), (1,)), ((), ())), preferred_element_type=F32)


def _dot_tn(a, b):
    return lax.dot_general(a, b, (((0,), (0,)), ((), ())), preferred_element_type=F32)


def _dot_f32(a, b):
    return jnp.dot(a, b, preferred_element_type=F32, precision=lax.Precision.HIGHEST)


def _split3(x):
    hi = x.astype(BF16)
    r1 = x - hi.astype(F32)
    mid = r1.astype(BF16)
    lo = (r1 - mid.astype(F32)).astype(BF16)
    return hi, mid, lo


def _tri_sum(tri, x):
    hi, mid, lo = _split3(x)
    return _dot(tri, hi) + _dot(tri, mid) + _dot(tri, lo)


def _log_sigmoid(x):
    return -(jnp.maximum(-x, 0.0) + jnp.log1p(jnp.exp(-jnp.abs(x))))


def _silu(x):
    return x * jax.nn.sigmoid(x)


def _layernorm(x, g, b):
    mu = jnp.mean(x, axis=-1, keepdims=True)
    xc = x - mu
    var = jnp.mean(xc * xc, axis=-1, keepdims=True)
    return xc * lax.rsqrt(var + LN_EPS) * g + b


def _mod_row_tb(i):
    return jnp.where(i < N_CTX // TB, 0, 1 + (i - N_CTX // TB) // (T_LAT // TB))


def _mod_row_tm(i):
    return jnp.where(i < N_CTX // TM, 0, 1 + (i - N_CTX // TM) // (T_LAT // TM))


def _mod_spec(j, row_fn):
    return pl.BlockSpec((1, 1, D), lambda i, *_: (row_fn(i) * 6 + j, 0, 0))


def _mod_kernel(c_ref, w_ref, b_ref, o_ref):
    o_ref[0] = _dot_f32(_silu(c_ref[...]), w_ref[0]) + b_ref[0]


def _modulation(cvecs, adaln_w, adaln_b):
    nj = 6
    return pl.pallas_call(
        _mod_kernel,
        out_shape=jax.ShapeDtypeStruct((DEPTH, MOD_ROWS, 6 * D), F32),
        grid=(DEPTH, nj),
        in_specs=[
            pl.BlockSpec((MOD_ROWS, D), lambda l, j: (0, 0)),
            pl.BlockSpec((1, D, D), lambda l, j: (l, 0, j)),
            pl.BlockSpec((1, 1, D), lambda l, j: (l, 0, j)),
        ],
        out_specs=pl.BlockSpec((1, MOD_ROWS, D), lambda l, j: (l, 0, j)),
        compiler_params=_cp(("parallel", "parallel")),
        name="adaln_modulation",
    )(cvecs, adaln_w, adaln_b.reshape(DEPTH, 1, 6 * D))


def _embed_kernel(xp_ref, xs_ref, pos_ref, o_ref):
    i = pl.program_id(0)

    @pl.when(i < N_CTX // TB)
    def _():
        o_ref[...] = xp_ref[...]

    @pl.when(i >= N_CTX // TB)
    def _():
        o_ref[...] = xs_ref[...] + pos_ref[...]


def _embed(x_prompt, x_sample, pos):
    nc = N_CTX // TB
    return pl.pallas_call(
        _embed_kernel,
        out_shape=jax.ShapeDtypeStruct((N_TOK, D), F32),
        grid=(NB,),
        in_specs=[
            pl.BlockSpec((TB, D), lambda i: (jnp.minimum(i, nc - 1), 0)),
            pl.BlockSpec((TB, D), lambda i: (jnp.maximum(i - nc, 0), 0)),
            pl.BlockSpec((TB, D), lambda i: (jnp.maximum(i - nc, 0) % (T_LAT // TB), 0)),
        ],
        out_specs=pl.BlockSpec((TB, D), lambda i: (i, 0)),
        compiler_params=_cp(("parallel",)),
        name="embed_tokens",
    )(x_prompt.reshape(N_CTX, D), x_sample.reshape(N_LAT, D), pos)


def _grid_posemb(T):
    rows = T // GRID_W
    r, cidx = jnp.meshgrid(jnp.arange(rows), jnp.arange(GRID_W), indexing="ij")
    nf = D // 4
    freqs = jnp.exp(-math.log(10000.0) * jnp.arange(nf, dtype=F32) / nf)

    def emb(p):
        a = p.reshape(-1).astype(F32)[:, None] * freqs
        return jnp.concatenate([jnp.sin(a), jnp.cos(a)], -1)

    return jnp.concatenate([emb(r), emb(cidx)], -1)


def _gla_in_kernel(x_ref, sh_ref, sc_ref, wq_ref, wk_ref, wv_ref, wr_ref, wg_ref, wgate_ref, bgate_ref,
                   q_ref, k_ref, v_ref, r_ref, g_ref):
    h = (x_ref[...] * (1.0 + sc_ref[0]) + sh_ref[0]).astype(BF16)
    q_ref[...] = _dot(h, wq_ref[...]) * (DK ** -0.5)
    k_ref[...] = _dot(h, wk_ref[...])
    v_ref[...] = _dot(h, wv_ref[...])
    r_ref[...] = _silu(_dot(h, wr_ref[...]))
    glr = _dot(h, wg_ref[...]).astype(BF16)
    z = _dot(glr, wgate_ref[...]) + bgate_ref[...]
    g_ref[...] = _log_sigmoid(z) * (1.0 / GATE_TAU)


def _gla_in(x, mods, w_in, w_gate, b_gate):
    wq = w_in[:, :DKT].astype(BF16)
    wk = w_in[:, DKT:2 * DKT].astype(BF16)
    wv = w_in[:, 2 * DKT:2 * DKT + DVT].astype(BF16)
    wr = w_in[:, 2 * DKT + DVT:2 * DKT + 2 * DVT].astype(BF16)
    wg = w_in[:, 2 * DKT + 2 * DVT:].astype(BF16)
    wgate = jnp.zeros((2 * GATE_RANK, 2 * DKT), F32)
    wgate = wgate.at[:GATE_RANK, :DKT].set(w_gate[0]).at[GATE_RANK:, DKT:].set(w_gate[1]).astype(BF16)
    bgate = b_gate.reshape(1, 2 * DKT)
    full = lambda s: pl.BlockSpec(s, lambda i: (0,) * len(s))
    tok = lambda n: pl.BlockSpec((TB, n), lambda i: (i, 0))
    return pl.pallas_call(
        _gla_in_kernel,
        out_shape=[jax.ShapeDtypeStruct((N_TOK, DKT), F32), jax.ShapeDtypeStruct((N_TOK, DKT), F32),
                   jax.ShapeDtypeStruct((N_TOK, DVT), F32), jax.ShapeDtypeStruct((N_TOK, DVT), F32),
                   jax.ShapeDtypeStruct((N_TOK, 2 * DKT), F32)],
        grid=(NB,),
        in_specs=[tok(D), _mod_spec(0, _mod_row_tb), _mod_spec(1, _mod_row_tb),
                  full((D, DKT)), full((D, DKT)), full((D, DVT)), full((D, DVT)), full((D, 2 * GATE_RANK)),
                  full((2 * GATE_RANK, 2 * DKT)), full((1, 2 * DKT))],
        out_specs=[tok(DKT), tok(DKT), tok(DVT), tok(DVT), tok(2 * DKT)],
        compiler_params=_cp(("parallel",)),
        name="gla_in_proj",
    )(x, mods, mods, wq, wk, wv, wr, wg, wgate, bgate)


def _mlstm_in_kernel(x_ref, sh_ref, sc_ref, wq_ref, wk_ref, wv_ref, wr_ref, wg_ref, wgt_ref, bg_ref, bgt_ref,
                     q_ref, k_ref, v_ref, r_ref, gc_ref, gr_ref):
    h = (x_ref[...] * (1.0 + sc_ref[0]) + sh_ref[0]).astype(BF16)
    q_ref[...] = _dot(h, wq_ref[...]) * (DK ** -0.5)
    k_ref[...] = _dot(h, wk_ref[...])
    v_ref[...] = _dot(h, wv_ref[...])
    r_ref[...] = jax.nn.sigmoid(_dot(h, wr_ref[...]))
    gc = _dot(h, wg_ref[...]) + bg_ref[...]
    gr = _dot_nt(wgt_ref[...], h) + bgt_ref[...]
    is_f_c = (lax.broadcasted_iota(jnp.int32, gc.shape, 1) % 2) == 1
    is_f_r = (lax.broadcasted_iota(jnp.int32, gr.shape, 0) % 2) == 1
    gc = jnp.where(is_f_c, _log_sigmoid(gc), gc)
    gr = jnp.where(is_f_r, _log_sigmoid(gr), gr)
    for hd in range(NH):
        gc_ref[hd] = gc[:, 4 * hd:4 * hd + 4]
        for c in range(TB // CH):
            gr_ref[hd, c] = gr[4 * hd:4 * hd + 4, c * CH:(c + 1) * CH]


def _mlstm_in(x, mods, w_in, b_gates):
    wq = w_in[:, :DKT].astype(BF16)
    wk = w_in[:, DKT:2 * DKT].astype(BF16)
    wv = w_in[:, 2 * DKT:2 * DKT + DVT].astype(BF16)
    wr = w_in[:, 2 * DKT + DVT:2 * DKT + 2 * DVT].astype(BF16)
    wg = w_in[:, 2 * DKT + 2 * DVT:].reshape(D, 4, NH).transpose(0, 2, 1).reshape(D, 4 * NH).astype(BF16)
    bg = b_gates.reshape(4, NH).T.reshape(1, 4 * NH)
    full = lambda s: pl.BlockSpec(s, lambda i: (0,) * len(s))
    tok = lambda n: pl.BlockSpec((TB, n), lambda i: (i, 0))
    return pl.pallas_call(
        _mlstm_in_kernel,
        out_shape=[jax.ShapeDtypeStruct((N_TOK, DKT), F32), jax.ShapeDtypeStruct((N_TOK, DKT), F32),
                   jax.ShapeDtypeStruct((N_TOK, DVT), F32), jax.ShapeDtypeStruct((N_TOK, DVT), F32),
                   jax.ShapeDtypeStruct((NH, N_TOK, 4), F32),
                   jax.ShapeDtypeStruct((NH, N_TOK // CH, 4, CH), F32)],
        grid=(NB,),
        in_specs=[tok(D), _mod_spec(0, _mod_row_tb), _mod_spec(1, _mod_row_tb),
                  full((D, DKT)), full((D, DKT)), full((D, DVT)), full((D, DVT)), full((D, 4 * NH)),
                  full((4 * NH, D)), full((1, 4 * NH)), full((4 * NH, 1))],
        out_specs=[tok(DKT), tok(DKT), tok(DVT), tok(DVT),
                   pl.BlockSpec((NH, TB, 4), lambda i: (0, i, 0)),
                   pl.BlockSpec((NH, TB // CH, 4, CH), lambda i: (0, i, 0, 0))],
        compiler_params=_cp(("parallel",)),
        name="mlstm_in_proj",
    )(x, mods, mods, wq, wk, wv, wr, wg, wg.T, bg, bg.T)


def _tri_masks():
    row = lax.broadcasted_iota(jnp.int32, (CH, CH), 0)
    col = lax.broadcasted_iota(jnp.int32, (CH, CH), 1)
    return row >= col, row <= col


def _head_norm_rows(o):
    mu = jnp.mean(o, axis=-1, keepdims=True)
    oc = o - mu
    var = jnp.mean(oc * oc, axis=-1, keepdims=True)
    return oc * lax.rsqrt(var + LN_EPS)


def _gla_scan_kernel(*refs, T, has_state, emit_state):
    q_ref, k_ref, v_ref, gf_ref, gb_ref = refs[:5]
    pos = 5
    if has_state:
        s0f_ref, s0b_ref = refs[pos:pos + 2]
        pos += 3
    o_ref = refs[pos]
    pos += 1
    if emit_state:
        sout_ref = refs[pos]
        pos += 1
    st_ref = refs[pos]

    lower, upper = _tri_masks()
    n_chunks = T // CH

    def chunk(n, forward):
        rows = pl.ds(pl.multiple_of(n * CH, CH), CH)
        q = q_ref[rows, :]
        k = k_ref[rows, :]
        v = v_ref[rows, :].astype(BF16)
        g = (gf_ref if forward else gb_ref)[rows, :]
        mask = lower if forward else upper
        b = _tri_sum(mask.astype(BF16), g)
        if forward:
            b_last, b_mid = b[CH - 1:CH, :], b[CH // 2:CH // 2 + 1, :]
        else:
            b_last, b_mid = b[0:1, :], b[CH - 1 - CH // 2:CH - CH // 2, :]
        a = _dot_nt((q * jnp.exp(b - b_mid)).astype(BF16), (k * jnp.exp(b_mid - b)).astype(BF16))
        a = jnp.where(mask, a, 0.0)
        st = st_ref[...]
        o = _dot(a.astype(BF16), v) + _dot_nt((q * jnp.exp(b)).astype(BF16), st.astype(BF16))
        d_t = _dot_tn(v, (k * jnp.exp(b_last - b)).astype(BF16))
        st_ref[...] = st * jnp.exp(b_last) + d_t
        return rows, o

    if has_state:
        st_ref[...] = s0f_ref[...].T
    else:
        st_ref[...] = jnp.zeros_like(st_ref)

    def fwd_body(n, carry):
        rows, o = chunk(n, True)
        o_ref[rows, :] = o
        return carry

    lax.fori_loop(0, n_chunks, fwd_body, 0)
    if emit_state:
        sout_ref[0] = st_ref[...].T

    if has_state:
        st_ref[...] = s0b_ref[...].T
    else:
        st_ref[...] = jnp.zeros_like(st_ref)

    def bwd_body(j, carry):
        rows, o = chunk(n_chunks - 1 - j, False)
        o_ref[rows, :] = _head_norm_rows(o_ref[rows, :] + o)
        return carry

    lax.fori_loop(0, n_chunks, bwd_body, 0)
    if emit_state:
        sout_ref[1] = st_ref[...].T


def _gla_scan(q, k, v, g, state, slot, o_prev):
    ctx = state is None
    T = T_CTX if ctx else T_LAT
    n_seq = N_CTX_SEQ if ctx else N_LAT_SEQ
    off = 0 if ctx else N_CTX // T_LAT
    in_specs = [
        pl.BlockSpec((T, DK), lambda b, h: (off + b, h)),
        pl.BlockSpec((T, DK), lambda b, h: (off + b, h)),
        pl.BlockSpec((T, DV), lambda b, h: (off + b, h)),
        pl.BlockSpec((T, DK), lambda b, h: (off + b, h)),
        pl.BlockSpec((T, DK), lambda b, h: (off + b, NH + h)),
    ]
    args = [q, k, v, g, g]
    out_shape = [jax.ShapeDtypeStruct((N_TOK, DVT), F32)]
    out_specs = [pl.BlockSpec((T, DV), lambda b, h: (off + b, h))]
    aliases = {}
    if ctx:
        out_shape.append(jax.ShapeDtypeStruct((N_CTX_SEQ, 2, NH, DK, DV), F32))
        out_specs.append(pl.BlockSpec((None, 2, None, DK, DV), lambda b, h: (b, 0, h, 0, 0)))
    else:
        sq = (None, None, None, None, DK, DV)
        in_specs += [pl.BlockSpec(sq, lambda b, h: (b, slot, 0, h, 0, 0)),
                     pl.BlockSpec(sq, lambda b, h: (b, slot, 1, h, 0, 0)),
                     pl.BlockSpec(memory_space=pl.ANY)]
        args += [state, state, o_prev]
        aliases = {7: 0}
    res = pl.pallas_call(
        functools.partial(_gla_scan_kernel, T=T, has_state=not ctx, emit_state=ctx),
        out_shape=out_shape,
        grid=(n_seq, NH),
        in_specs=in_specs,
        out_specs=out_specs,
        scratch_shapes=[pltpu.VMEM((DV, DK), F32)],
        input_output_aliases=aliases,
        compiler_params=_cp(("parallel", "parallel")),
        name="gla_scan_ctx" if ctx else "gla_scan_lat",
    )(*args)
    return res


def _mlstm_scan_kernel(*refs, T, has_state, emit_state):
    q_ref, k_ref, v_ref, gc_ref, gr_ref = refs[:5]
    pos = 5
    if has_state:
        c0f_ref, c0b_ref, n0f_ref, n0b_ref, m0f_ref, m0b_ref = refs[pos:pos + 6]
        pos += 7
    o_ref = refs[pos]
    pos += 1
    if emit_state:
        cout_ref, nout_ref, mout_ref = refs[pos:pos + 3]
        pos += 3
    ct_ref, n_ref, m_ref = refs[pos:pos + 3]

    lower, upper = _tri_masks()
    n_chunks = T // CH

    def chunk(n, forward):
        rows = pl.ds(pl.multiple_of(n * CH, CH), CH)
        q = q_ref[rows, :]
        k = k_ref[rows, :]
        v = v_ref[rows, :].astype(BF16)
        gc = gc_ref[rows, :]
        gr = gr_ref[n]
        o0 = 0 if forward else 2
        i_c, f_c = gc[:, o0:o0 + 1], gc[:, o0 + 1:o0 + 2]
        i_r, f_r = gr[o0:o0 + 1, :], gr[o0 + 1:o0 + 2, :]
        mask = lower if forward else upper
        other = upper if forward else lower
        b_c = jnp.sum(jnp.where(mask, f_r, 0.0), axis=1, keepdims=True)
        b_r = jnp.sum(jnp.where(other, f_c, 0.0), axis=0, keepdims=True)
        b_last = jnp.sum(f_r, axis=1, keepdims=True)
        d_log = jnp.where(mask, b_c - b_r + i_r, NEG_INF)
        a_c = b_last - b_c + i_c
        m_loc = jnp.max(a_c, axis=0, keepdims=True)
        kw = k * jnp.exp(a_c - m_loc)
        kv_t = _dot_tn(v, kw.astype(BF16))
        k_loc = jnp.sum(kw, axis=0, keepdims=True)

        ct = ct_ref[...]
        nrm = n_ref[...]
        m_old = m_ref[:, 0:1]

        l_inter = b_c + m_old
        m_i = jnp.maximum(l_inter, jnp.max(d_log, axis=1, keepdims=True))
        qb = q.astype(BF16)
        s = _dot_nt(qb, k.astype(BF16)) * jnp.exp(d_log - m_i)
        e_inter = jnp.exp(l_inter - m_i)
        num = _dot(s.astype(BF16), v) + e_inter * _dot_nt(qb, ct.astype(BF16))
        den = jnp.sum(s, axis=1, keepdims=True) + e_inter * jnp.sum(q * nrm, axis=1, keepdims=True)
        hout = num / jnp.maximum(jnp.abs(den), jnp.exp(-m_i))

        m_new = jnp.maximum(b_last + m_old, m_loc)
        s_old = jnp.exp(b_last + m_old - m_new)
        s_loc = jnp.exp(m_loc - m_new)
        ct_ref[...] = s_old * ct + s_loc * kv_t
        n_ref[...] = s_old * nrm + s_loc * k_loc
        m_ref[...] = jnp.broadcast_to(m_new, m_ref.shape)
        return rows, hout

    def init(c0_ref, n0_ref, m0_ref):
        if has_state:
            ct_ref[...] = c0_ref[...].T
            n_ref[...] = n0_ref[...]
            m_ref[...] = m0_ref[...]
        else:
            ct_ref[...] = jnp.zeros_like(ct_ref)
            n_ref[...] = jnp.zeros_like(n_ref)
            m_ref[...] = jnp.zeros_like(m_ref)

    def emit(d):
        if emit_state:
            cout_ref[d] = ct_ref[...].T
            nout_ref[d] = n_ref[...]
            mout_ref[d] = m_ref[...]

    init(*( (c0f_ref, n0f_ref, m0f_ref) if has_state else (None, None, None)))

    def fwd_body(n, carry):
        rows, o = chunk(n, True)
        o_ref[rows, :] = o
        return carry

    lax.fori_loop(0, n_chunks, fwd_body, 0)
    emit(0)
    init(*( (c0b_ref, n0b_ref, m0b_ref) if has_state else (None, None, None)))

    def bwd_body(j, carry):
        rows, o = chunk(n_chunks - 1 - j, False)
        o_ref[rows, :] = _head_norm_rows(o_ref[rows, :] + o)
        return carry

    lax.fori_loop(0, n_chunks, bwd_body, 0)
    emit(1)


def _mlstm_scan(q, k, v, gc, gr, states, slot, o_prev):
    ctx = states is None
    T = T_CTX if ctx else T_LAT
    n_seq = N_CTX_SEQ if ctx else N_LAT_SEQ
    off = 0 if ctx else N_CTX // T_LAT
    in_specs = [
        pl.BlockSpec((T, DK), lambda b, h: (off + b, h)),
        pl.BlockSpec((T, DK), lambda b, h: (off + b, h)),
        pl.BlockSpec((T, DV), lambda b, h: (off + b, h)),
        pl.BlockSpec((None, T, 4), lambda b, h: (h, off + b, 0)),
        pl.BlockSpec((None, T // CH, 4, CH), lambda b, h: (h, off + b, 0, 0)),
    ]
    args = [q, k, v, gc, gr]
    out_shape = [jax.ShapeDtypeStruct((N_TOK, DVT), F32)]
    out_specs = [pl.BlockSpec((T, DV), lambda b, h: (off + b, h))]
    aliases = {}
    if ctx:
        out_shape += [jax.ShapeDtypeStruct((N_CTX_SEQ, 2, NH, DK, DV), F32),
                      jax.ShapeDtypeStruct((N_CTX_SEQ, 2, NH, 1, DK), F32),
                      jax.ShapeDtypeStruct((N_CTX_SEQ, 2, NH, 1, DK), F32)]
        out_specs += [pl.BlockSpec((None, 2, None, DK, DV), lambda b, h: (b, 0, h, 0, 0)),
                      pl.BlockSpec((None, 2, None, 1, DK), lambda b, h: (b, 0, h, 0, 0)),
                      pl.BlockSpec((None, 2, None, 1, DK), lambda b, h: (b, 0, h, 0, 0))]
    else:
        c0, n0, m0 = states
        n0 = n0.reshape(N_LAT_SEQ, -1, 2, NH, 1, DK)
        m0 = jnp.broadcast_to(m0[..., None, None], m0.shape + (1, DK))
        sq_c = (None, None, None, None, DK, DV)
        sq_v = (None, None, None, None, 1, DK)
        for arr, sq in ((c0, sq_c), (n0, sq_v), (m0, sq_v)):
            for d in range(2):
                in_specs.append(pl.BlockSpec(sq, functools.partial(lambda b, h, d: (b, slot, d, h, 0, 0), d=d)))
                args.append(arr)
        in_specs.append(pl.BlockSpec(memory_space=pl.ANY))
        args.append(o_prev)
        aliases = {11: 0}
    return pl.pallas_call(
        functools.partial(_mlstm_scan_kernel, T=T, has_state=not ctx, emit_state=ctx),
        out_shape=out_shape,
        grid=(n_seq, NH),
        in_specs=in_specs,
        out_specs=out_specs,
        scratch_shapes=[pltpu.VMEM((DV, DK), F32), pltpu.VMEM((1, DK), F32), pltpu.VMEM((1, DK), F32)],
        input_output_aliases=aliases,
        compiler_params=_cp(("parallel", "parallel")),
        name="mlstm_scan_ctx" if ctx else "mlstm_scan_lat",
    )(*args)


def _route(lg):
    lane = lax.broadcasted_iota(jnp.int32, lg.shape, 1)
    big = jnp.int32(ROUTE_W)
    is_g = (lane >= NE) & (lane < NE + N_GROUPS)
    gl = jnp.where(is_g, lg, NEG_INF)
    gmax = jnp.max(gl, axis=1, keepdims=True)
    gsel = jnp.min(jnp.where(gl == gmax, lane, big), axis=1, keepdims=True) - NE
    gw = 1.0 / jnp.sum(jnp.exp(gl - gmax), axis=1, keepdims=True)
    ing = (lane < NE) & ((lane >> 3) == gsel)
    el = jnp.where(ing, lg, NEG_INF)
    emax = jnp.max(el, axis=1, keepdims=True)
    p = jnp.exp(el - emax)
    prob = p / jnp.sum(p, axis=1, keepdims=True)
    p1 = jnp.max(prob, axis=1, keepdims=True)
    i1 = jnp.min(jnp.where(ing & (prob == p1), lane, big), axis=1, keepdims=True)
    rest = ing & (lane != i1)
    prob2 = jnp.where(rest, prob, -1.0)
    p2 = jnp.max(prob2, axis=1, keepdims=True)
    i2 = jnp.min(jnp.where(rest & (prob2 == p2), lane, big), axis=1, keepdims=True)
    tot = p1 + p2
    return gw * jnp.where(lane == i1, p1 / tot, jnp.where(lane == i2, p2 / tot, 0.0))


def _out_kernel(o_ref, r_ref, x_ref, gate_ref, sh_ref, sc_ref, ng_ref, wo_ref, lg_ref, lb_ref, wr_ref, br_ref,
                x1_ref, h2_ref, comb_ref):
    y = _dot((o_ref[...] * ng_ref[...] * r_ref[...]).astype(BF16), wo_ref[...])
    x1 = _layernorm(DN_ALPHA * x_ref[...] + gate_ref[0] * y, lg_ref[...], lb_ref[...])
    x1_ref[...] = x1
    h2 = x1 * (1.0 + sc_ref[0]) + sh_ref[0]
    h2_ref[...] = h2.astype(BF16)
    comb_ref[...] = _route(_dot_f32(h2, wr_ref[...]) + br_ref[...])


def _out_proj(o, r, x, mods, norm_g, w_out, ln_g, ln_b, w_route, b_route):
    full = lambda s: pl.BlockSpec(s, lambda i: (0,) * len(s))
    tok = lambda n: pl.BlockSpec((TB, n), lambda i: (i, 0))
    return pl.pallas_call(
        _out_kernel,
        out_shape=[jax.ShapeDtypeStruct((N_TOK, D), F32), jax.ShapeDtypeStruct((N_TOK, D), BF16),
                   jax.ShapeDtypeStruct((N_TOK, ROUTE_W), F32)],
        grid=(NB,),
        in_specs=[tok(DVT), tok(DVT), tok(D),
                  _mod_spec(2, _mod_row_tb), _mod_spec(3, _mod_row_tb), _mod_spec(4, _mod_row_tb),
                  full((1, DVT)), full((DVT, D)), full((1, D)), full((1, D)),
                  full((D, ROUTE_W)), full((1, ROUTE_W))],
        out_specs=[tok(D), tok(D), tok(ROUTE_W)],
        compiler_params=_cp(("parallel",)),
        name="out_proj_route",
    )(o, r, x, mods, mods, mods, norm_g.reshape(1, DVT), w_out.astype(BF16),
      ln_g.reshape(1, D), ln_b.reshape(1, D), w_route, b_route)


def _moe_kernel(h_ref, comb_ref, x1_ref, gate_ref, lg_ref, lb_ref, wg_ref, wu_ref, wd_ref, o_ref, acc_ref):
    e = pl.program_id(1)

    @pl.when(e == 0)
    def _():
        acc_ref[...] = jnp.zeros_like(acc_ref)

    h = h_ref[...]
    a = _dot(h, wg_ref[0])
    u = _dot(h, wu_ref[0])
    comb = comb_ref[...]
    lane = lax.broadcasted_iota(jnp.int32, comb.shape, 1)
    c_e = jnp.sum(jnp.where(lane == e, comb, 0.0), axis=1, keepdims=True)
    hid = _silu(a) * u * c_e
    acc_ref[...] += _dot(hid.astype(BF16), wd_ref[0])

    @pl.when(e == NE - 1)
    def _():
        o_ref[...] = _layernorm(DN_ALPHA * x1_ref[...] + gate_ref[0] * acc_ref[...], lg_ref[...], lb_ref[...])


def _moe(h2, comb, x1, mods, ln_g, ln_b, w_gate, w_up, w_down):
    tok = lambda n: pl.BlockSpec((TM, n), lambda i, e: (i, 0))
    full = lambda s: pl.BlockSpec(s, lambda i, e: (0,) * len(s))
    return pl.pallas_call(
        _moe_kernel,
        out_shape=jax.ShapeDtypeStruct((N_TOK, D), F32),
        grid=(N_TOK // TM, NE),
        in_specs=[tok(D), tok(ROUTE_W), tok(D), _mod_spec(5, _mod_row_tm), full((1, D)), full((1, D)),
                  pl.BlockSpec((1, D, DE), lambda i, e: (e, 0, 0)),
                  pl.BlockSpec((1, D, DE), lambda i, e: (e, 0, 0)),
                  pl.BlockSpec((1, DE, D), lambda i, e: (e, 0, 0))],
        out_specs=tok(D),
        scratch_shapes=[pltpu.VMEM((TM, D), F32)],
        compiler_params=_cp(("parallel", "arbitrary")),
        name="moe_dense",
    )(h2, comb, x1, mods, ln_g.reshape(1, D), ln_b.reshape(1, D),
      w_gate.astype(BF16), w_up.astype(BF16), w_down.astype(BF16))


def kernel(x_prompt, x_sample, state_gla_S, state_mlstm_C, state_mlstm_n, state_mlstm_m, c, c_ctx,
           gla_w_in, gla_w_gate, gla_b_gate, gla_norm_g, gla_w_out,
           mlstm_w_in, mlstm_b_gates, mlstm_norm_g, mlstm_w_out,
           adaln_w, adaln_b, ln_g, ln_b,
           moe_w_group, moe_b_group, moe_w_expert, moe_b_expert, moe_w_gate, moe_w_up, moe_w_down):
    cvecs = jnp.zeros((MOD_ROWS, D), F32).at[0].set(c_ctx).at[1:1 + N_LAT_SEQ].set(c)
    mods_all = _modulation(cvecs, adaln_w, adaln_b)
    x = _embed(x_prompt, x_sample, _grid_posemb(T_LAT))
    gla_states, ml_c, ml_n, ml_m = [], [], [], []
    for l in range(DEPTH):
        s = l // 2
        mods = mods_all[l].reshape(MOD_ROWS * 6, 1, D)
        if l % 2 == 0:
            q, k, v, r, g = _gla_in(x, mods, gla_w_in[s], gla_w_gate[s], gla_b_gate[s])
            o, s_new = _gla_scan(q, k, v, g, None, s, None)
            (o,) = _gla_scan(q, k, v, g, state_gla_S, s, o)
            gla_states.append(s_new)
            norm_g, w_out = gla_norm_g[s], gla_w_out[s]
        else:
            q, k, v, r, gc, gr = _mlstm_in(x, mods, mlstm_w_in[s], mlstm_b_gates[s])
            o, c_new, n_new, m_new = _mlstm_scan(q, k, v, gc, gr, None, s, None)
            (o,) = _mlstm_scan(q, k, v, gc, gr, (state_mlstm_C, state_mlstm_n, state_mlstm_m), s, o)
            ml_c.append(c_new)
            ml_n.append(n_new[:, :, :, 0, :])
            ml_m.append(m_new[:, :, :, 0, 0])
            norm_g, w_out = mlstm_norm_g[s], mlstm_w_out[s]
        w_route = jnp.zeros((D, ROUTE_W), F32).at[:, :NE].set(moe_w_expert[l]).at[:, NE:NE + N_GROUPS].set(
            moe_w_group[l])
        b_route = jnp.zeros((1, ROUTE_W), F32).at[0, :NE].set(moe_b_expert[l]).at[0, NE:NE + N_GROUPS].set(
            moe_b_group[l])
        x1, h2, comb = _out_proj(o, r, x, mods, norm_g, w_out, ln_g[l, 0], ln_b[l, 0], w_route, b_route)
        x = _moe(h2, comb, x1, mods, ln_g[l, 1], ln_b[l, 1], moe_w_gate[l], moe_w_up[l], moe_w_down[l])
    y_ctx = x[:N_CTX].reshape(N_CTX_SEQ, T_CTX, D)
    y_lat = x[N_CTX:].reshape(N_LAT_SEQ, T_LAT, D)
    return (y_ctx, y_lat, jnp.stack(gla_states, 1), jnp.stack(ml_c, 1), jnp.stack(ml_n, 1), jnp.stack(ml_m, 1))
```

```python
import functools
import math

import jax
import jax.numpy as jnp
from jax import lax
from jax.experimental import pallas as pl
from jax.experimental.pallas import tpu as pltpu

F32 = jnp.float32
BF16 = jnp.bfloat16

D = 1024
N_CTX_SEQ, T_CTX = 16, 256
N_LAT_SEQ, T_LAT = 8, 1024
N_CTX = N_CTX_SEQ * T_CTX
N_LAT = N_LAT_SEQ * T_LAT
N_TOK = N_CTX + N_LAT
DEPTH = 4
GRID_W = 64
NH = 4
DK = 128
DV = 256
DKT = NH * DK
DVT = NH * DV
GATE_RANK = 16
GATE_TAU = 16.0
CH = 64
N_GROUPS = 4
EPG = 8
NE = N_GROUPS * EPG
DE = 256
DN_ALPHA = (2.0 * DEPTH) ** 0.25
LN_EPS = 1e-5
MOD_ROWS = 16
TB = 256
NB = N_TOK // TB
TM = 1024
ROUTE_W = 128
NEG_INF = float("-inf")
VMEM_LIMIT = 48 * 1024 * 1024


def _cp(sem):
    return pltpu.CompilerParams(dimension_semantics=sem, vmem_limit_bytes=VMEM_LIMIT)


def _dot(a, b):
    return jnp.dot(a, b, preferred_element_type=F32)


def _dot_nt(a, b):
    return lax.dot_general(a, b, (((1,), (1,)), ((), ())), preferred_element_type=F32)


def _dot_f32(a, b):
    return jnp.dot(a, b, preferred_element_type=F32, precision=lax.Precision.HIGHEST)


def _split2(x):
    hi = x.astype(BF16)
    lo = (x - hi.astype(F32)).astype(BF16)
    return hi, lo


def _log_sigmoid(x):
    return -(jnp.maximum(-x, 0.0) + jnp.log1p(jnp.exp(-jnp.abs(x))))


def _silu(x):
    return x * jax.nn.sigmoid(x)


def _layernorm(x, g, b):
    mu = jnp.mean(x, axis=-1, keepdims=True)
    xc = x - mu
    var = jnp.mean(xc * xc, axis=-1, keepdims=True)
    return xc * lax.rsqrt(var + LN_EPS) * g + b


def _mod_row_tb(i):
    return jnp.where(i < N_CTX // TB, 0, 1 + (i - N_CTX // TB) // (T_LAT // TB))


def _mod_row_tm(i):
    return jnp.where(i < N_CTX // TM, 0, 1 + (i - N_CTX // TM) // (T_LAT // TM))


def _mod_spec(j, row_fn):
    return pl.BlockSpec((1, 1, D), lambda i, *_: (row_fn(i) * 6 + j, 0, 0))


def _mod_kernel(c_ref, w_ref, b_ref, o_ref):
    o_ref[0] = _dot_f32(_silu(c_ref[...]), w_ref[0]) + b_ref[0]


def _modulation(cvecs, adaln_w, adaln_b):
    nj = 6
    return pl.pallas_call(
        _mod_kernel,
        out_shape=jax.ShapeDtypeStruct((DEPTH, MOD_ROWS, 6 * D), F32),
        grid=(DEPTH, nj),
        in_specs=[
            pl.BlockSpec((MOD_ROWS, D), lambda l, j: (0, 0)),
            pl.BlockSpec((1, D, D), lambda l, j: (l, 0, j)),
            pl.BlockSpec((1, 1, D), lambda l, j: (l, 0, j)),
        ],
        out_specs=pl.BlockSpec((1, MOD_ROWS, D), lambda l, j: (l, 0, j)),
        compiler_params=_cp(("parallel", "parallel")),
        name="adaln_modulation",
    )(cvecs, adaln_w, adaln_b.reshape(DEPTH, 1, 6 * D))


def _embed_kernel(xp_ref, xs_ref, pos_ref, o_ref):
    i = pl.program_id(0)

    @pl.when(i < N_CTX // TB)
    def _():
        o_ref[...] = xp_ref[...]

    @pl.when(i >= N_CTX // TB)
    def _():
        o_ref[...] = xs_ref[...] + pos_ref[...]


def _embed(x_prompt, x_sample, pos):
    nc = N_CTX // TB
    return pl.pallas_call(
        _embed_kernel,
        out_shape=jax.ShapeDtypeStruct((N_TOK, D), F32),
        grid=(NB,),
        in_specs=[
            pl.BlockSpec((TB, D), lambda i: (jnp.minimum(i, nc - 1), 0)),
            pl.BlockSpec((TB, D), lambda i: (jnp.maximum(i - nc, 0), 0)),
            pl.BlockSpec((TB, D), lambda i: (jnp.maximum(i - nc, 0) % (T_LAT // TB), 0)),
        ],
        out_specs=pl.BlockSpec((TB, D), lambda i: (i, 0)),
        compiler_params=_cp(("parallel",)),
        name="embed_tokens",
    )(x_prompt.reshape(N_CTX, D), x_sample.reshape(N_LAT, D), pos)


def _grid_posemb(T):
    rows = T // GRID_W
    r, cidx = jnp.meshgrid(jnp.arange(rows), jnp.arange(GRID_W), indexing="ij")
    nf = D // 4
    freqs = jnp.exp(-math.log(10000.0) * jnp.arange(nf, dtype=F32) / nf)

    def emb(p):
        a = p.reshape(-1).astype(F32)[:, None] * freqs
        return jnp.concatenate([jnp.sin(a), jnp.cos(a)], -1)

    return jnp.concatenate([emb(r), emb(cidx)], -1)


def _gla_in_kernel(x_ref, sh_ref, sc_ref, wq_ref, wk_ref, wv_ref, wr_ref, wg_ref, wgate_ref, bgate_ref,
                   q_ref, k_ref, v_ref, r_ref, g_ref):
    h = (x_ref[...] * (1.0 + sc_ref[0]) + sh_ref[0]).astype(BF16)
    q_ref[...] = _dot(h, wq_ref[...]) * (DK ** -0.5)
    k_ref[...] = _dot(h, wk_ref[...])
    v_ref[...] = _dot(h, wv_ref[...])
    r_ref[...] = _silu(_dot(h, wr_ref[...]))
    glr = _dot(h, wg_ref[...]).astype(BF16)
    z = _dot(glr, wgate_ref[...]) + bgate_ref[...]
    g_ref[...] = _log_sigmoid(z) * (1.0 / GATE_TAU)


def _gla_in(x, mods, w_in, w_gate, b_gate):
    wq = w_in[:, :DKT].astype(BF16)
    wk = w_in[:, DKT:2 * DKT].astype(BF16)
    wv = w_in[:, 2 * DKT:2 * DKT + DVT].astype(BF16)
    wr = w_in[:, 2 * DKT + DVT:2 * DKT + 2 * DVT].astype(BF16)
    wg = w_in[:, 2 * DKT + 2 * DVT:].astype(BF16)
    wgate = jnp.zeros((2 * GATE_RANK, 2 * DKT), F32)
    wgate = wgate.at[:GATE_RANK, :DKT].set(w_gate[0]).at[GATE_RANK:, DKT:].set(w_gate[1]).astype(BF16)
    bgate = b_gate.reshape(1, 2 * DKT)
    full = lambda s: pl.BlockSpec(s, lambda i: (0,) * len(s))
    tok = lambda n: pl.BlockSpec((TB, n), lambda i: (i, 0))
    return pl.pallas_call(
        _gla_in_kernel,
        out_shape=[jax.ShapeDtypeStruct((N_TOK, DKT), F32), jax.ShapeDtypeStruct((N_TOK, DKT), F32),
                   jax.ShapeDtypeStruct((N_TOK, DVT), F32), jax.ShapeDtypeStruct((N_TOK, DVT), F32),
                   jax.ShapeDtypeStruct((N_TOK, 2 * DKT), F32)],
        grid=(NB,),
        in_specs=[tok(D), _mod_spec(0, _mod_row_tb), _mod_spec(1, _mod_row_tb),
                  full((D, DKT)), full((D, DKT)), full((D, DVT)), full((D, DVT)), full((D, 2 * GATE_RANK)),
                  full((2 * GATE_RANK, 2 * DKT)), full((1, 2 * DKT))],
        out_specs=[tok(DKT), tok(DKT), tok(DVT), tok(DVT), tok(2 * DKT)],
        compiler_params=_cp(("parallel",)),
        name="gla_in_proj",
    )(x, mods, mods, wq, wk, wv, wr, wg, wgate, bgate)


def _mlstm_in_kernel(x_ref, sh_ref, sc_ref, wq_ref, wk_ref, wv_ref, wr_ref, wg_ref, wgt_ref, bg_ref, bgt_ref,
                     q_ref, k_ref, v_ref, r_ref, gc_ref, gr_ref):
    h = (x_ref[...] * (1.0 + sc_ref[0]) + sh_ref[0]).astype(BF16)
    q_ref[...] = _dot(h, wq_ref[...]) * (DK ** -0.5)
    k_ref[...] = _dot(h, wk_ref[...])
    v_ref[...] = _dot(h, wv_ref[...])
    r_ref[...] = jax.nn.sigmoid(_dot(h, wr_ref[...]))
    gc = _dot(h, wg_ref[...]) + bg_ref[...]
    gr = _dot_nt(wgt_ref[...], h) + bgt_ref[...]
    is_f_c = (lax.broadcasted_iota(jnp.int32, gc.shape, 1) % 2) == 1
    is_f_r = (lax.broadcasted_iota(jnp.int32, gr.shape, 0) % 2) == 1
    gc = jnp.where(is_f_c, _log_sigmoid(gc), gc)
    gr = jnp.where(is_f_r, _log_sigmoid(gr), gr)
    for hd in range(NH):
        gc_ref[hd] = gc[:, 4 * hd:4 * hd + 4]
        gr_ref[hd, 0] = gr[4 * hd:4 * hd + 4, :]


def _mlstm_in(x, mods, w_in, b_gates):
    wq = w_in[:, :DKT].astype(BF16)
    wk = w_in[:, DKT:2 * DKT].astype(BF16)
    wv = w_in[:, 2 * DKT:2 * DKT + DVT].astype(BF16)
    wr = w_in[:, 2 * DKT + DVT:2 * DKT + 2 * DVT].astype(BF16)
    wg = w_in[:, 2 * DKT + 2 * DVT:].reshape(D, 4, NH).transpose(0, 2, 1).reshape(D, 4 * NH).astype(BF16)
    bg = b_gates.reshape(4, NH).T.reshape(1, 4 * NH)
    full = lambda s: pl.BlockSpec(s, lambda i: (0,) * len(s))
    tok = lambda n: pl.BlockSpec((TB, n), lambda i: (i, 0))
    return pl.pallas_call(
        _mlstm_in_kernel,
        out_shape=[jax.ShapeDtypeStruct((N_TOK, DKT), F32), jax.ShapeDtypeStruct((N_TOK, DKT), F32),
                   jax.ShapeDtypeStruct((N_TOK, DVT), F32), jax.ShapeDtypeStruct((N_TOK, DVT), F32),
                   jax.ShapeDtypeStruct((NH, N_TOK, 4), F32),
                   jax.ShapeDtypeStruct((NH, N_TOK // TB, 4, TB), F32)],
        grid=(NB,),
        in_specs=[tok(D), _mod_spec(0, _mod_row_tb), _mod_spec(1, _mod_row_tb),
                  full((D, DKT)), full((D, DKT)), full((D, DVT)), full((D, DVT)), full((D, 4 * NH)),
                  full((4 * NH, D)), full((1, 4 * NH)), full((4 * NH, 1))],
        out_specs=[tok(DKT), tok(DKT), tok(DVT), tok(DVT),
                   pl.BlockSpec((NH, TB, 4), lambda i: (0, i, 0)),
                   pl.BlockSpec((NH, 1, 4, TB), lambda i: (0, i, 0, 0))],
        compiler_params=_cp(("parallel",)),
        name="mlstm_in_proj",
    )(x, mods, mods, wq, wk, wv, wr, wg, wg.T, bg, bg.T)


SB = 256
NCB = SB // CH
CH_SHIFT = 6


def _block_tri_masks():
    row = lax.broadcasted_iota(jnp.int32, (SB, SB), 0)
    col = lax.broadcasted_iota(jnp.int32, (SB, SB), 1)
    same = (row >> CH_SHIFT) == (col >> CH_SHIFT)
    return same & (row >= col), same & (row <= col)


def _chunk_cumsum(x, forward):
    rin = lax.broadcasted_iota(jnp.int32, x.shape, 0) & (CH - 1)
    s = 1
    while s < CH:
        if forward:
            x = x + jnp.where(rin >= s, pltpu.roll(x, s, 0), 0.0)
        else:
            x = x + jnp.where(rin < CH - s, pltpu.roll(x, SB - s, 0), 0.0)
        s *= 2
    return x


def _per_chunk(vals, n):
    return jnp.concatenate([jnp.broadcast_to(v, (CH, n)) for v in vals], axis=0)


def _chunk_masked_cat(x):
    rowc = lax.broadcasted_iota(jnp.int32, x.shape, 0) >> CH_SHIFT
    return jnp.concatenate([jnp.where(rowc == c, x, 0.0).astype(BF16) for c in range(NCB)], axis=1)


def _head_norm_rows(o):
    mu = jnp.mean(o, axis=-1, keepdims=True)
    oc = o - mu
    var = jnp.mean(oc * oc, axis=-1, keepdims=True)
    return oc * lax.rsqrt(var + LN_EPS)


def _gla_superblock(q, k, v, v_t, g, st, forward, mask):
    b = _chunk_cumsum(g, forward)
    r_last = CH - 1 if forward else 0
    r_mid = CH // 2 if forward else CH - 1 - CH // 2
    lasts = [b[c * CH + r_last:c * CH + r_last + 1, :] for c in range(NCB)]
    b_last = _per_chunk(lasts, DK)
    b_mid = _per_chunk([b[c * CH + r_mid:c * CH + r_mid + 1, :] for c in range(NCB)], DK)
    qe = (q * jnp.exp(b - b_mid)).astype(BF16)
    ke = (k * jnp.exp(b_mid - b)).astype(BF16)
    qb = (q * jnp.exp(b)).astype(BF16)
    kd = k * jnp.exp(b_last - b)
    a = jnp.where(mask, _dot_nt(qe, ke), 0.0).astype(BF16)
    o_intra = _dot(a, v)
    d_t = _dot(v_t, _chunk_masked_cat(kd))
    inter = [None] * NCB
    for c in (range(NCB) if forward else reversed(range(NCB))):
        inter[c] = _dot_nt(qb[c * CH:(c + 1) * CH, :], st.astype(BF16))
        st = st * jnp.exp(lasts[c]) + d_t[:, c * DK:(c + 1) * DK]
    return o_intra + jnp.concatenate(inter, axis=0), st


def _scan_driver(T, o_ref, ob_ref, step):
    nsb = T // SB
    if nsb == 1:
        step(0, 0)
    else:
        def body(i, carry):
            step(i, nsb - 1 - i)
            return carry

        lax.fori_loop(0, nsb, body, 0)

    def norm_body(j, carry):
        rows = pl.ds(pl.multiple_of(j * SB, SB), SB)
        o_ref[rows, :] = _head_norm_rows(o_ref[rows, :] + ob_ref[rows, :])
        return carry

    if nsb == 1:
        norm_body(0, 0)
    else:
        lax.fori_loop(0, nsb, norm_body, 0)


def _gla_scan_kernel(*refs, T, has_state, emit_state):
    q_ref, k_ref, v_ref, gf_ref, gb_ref = refs[:5]
    pos = 5
    if has_state:
        s0f_ref, s0b_ref = refs[pos:pos + 2]
        pos += 3
    o_ref = refs[pos]
    pos += 1
    if emit_state:
        sout_ref = refs[pos]
        pos += 1
    stf_ref, stb_ref, vt_ref, ob_ref = refs[pos:pos + 4]

    nsb = T // SB
    for j in range(nsb):
        vt_ref[j] = v_ref[j * SB:(j + 1) * SB, :].T.astype(BF16)
    if has_state:
        stf_ref[...] = s0f_ref[...].T
        stb_ref[...] = s0b_ref[...].T
    else:
        stf_ref[...] = jnp.zeros_like(stf_ref)
        stb_ref[...] = jnp.zeros_like(stb_ref)

    lower, upper = _block_tri_masks()

    def one(j, forward):
        rows = pl.ds(pl.multiple_of(j * SB, SB), SB)
        st_ref = stf_ref if forward else stb_ref
        o, st = _gla_superblock(q_ref[rows, :], k_ref[rows, :], v_ref[rows, :].astype(BF16), vt_ref[j],
                                (gf_ref if forward else gb_ref)[rows, :], st_ref[...], forward,
                                lower if forward else upper)
        st_ref[...] = st
        (o_ref if forward else ob_ref)[rows, :] = o

    def step(jf, jb):
        one(jf, True)
        one(jb, False)

    _scan_driver(T, o_ref, ob_ref, step)
    if emit_state:
        sout_ref[0] = stf_ref[...].T
        sout_ref[1] = stb_ref[...].T


def _gla_scan(q, k, v, g, state, slot, o_prev):
    ctx = state is None
    T = T_CTX if ctx else T_LAT
    n_seq = N_CTX_SEQ if ctx else N_LAT_SEQ
    off = 0 if ctx else N_CTX // T_LAT
    in_specs = [
        pl.BlockSpec((T, DK), lambda b, h: (off + b, h)),
        pl.BlockSpec((T, DK), lambda b, h: (off + b, h)),
        pl.BlockSpec((T, DV), lambda b, h: (off + b, h)),
        pl.BlockSpec((T, DK), lambda b, h: (off + b, h)),
        pl.BlockSpec((T, DK), lambda b, h: (off + b, NH + h)),
    ]
    args = [q, k, v, g, g]
    out_shape = [jax.ShapeDtypeStruct((N_TOK, DVT), F32)]
    out_specs = [pl.BlockSpec((T, DV), lambda b, h: (off + b, h))]
    aliases = {}
    if ctx:
        out_shape.append(jax.ShapeDtypeStruct((N_CTX_SEQ, 2, NH, DK, DV), F32))
        out_specs.append(pl.BlockSpec((None, 2, None, DK, DV), lambda b, h: (b, 0, h, 0, 0)))
    else:
        sq = (None, None, None, None, DK, DV)
        in_specs += [pl.BlockSpec(sq, lambda b, h: (b, slot, 0, h, 0, 0)),
                     pl.BlockSpec(sq, lambda b, h: (b, slot, 1, h, 0, 0)),
                     pl.BlockSpec(memory_space=pl.ANY)]
        args += [state, state, o_prev]
        aliases = {7: 0}
    res = pl.pallas_call(
        functools.partial(_gla_scan_kernel, T=T, has_state=not ctx, emit_state=ctx),
        out_shape=out_shape,
        grid=(n_seq, NH),
        in_specs=in_specs,
        out_specs=out_specs,
        scratch_shapes=[pltpu.VMEM((DV, DK), F32), pltpu.VMEM((DV, DK), F32),
                        pltpu.VMEM((T // SB, DV, SB), BF16), pltpu.VMEM((T, DV), F32)],
        input_output_aliases=aliases,
        compiler_params=_cp(("parallel", "parallel")),
        name="gla_scan_ctx" if ctx else "gla_scan_lat",
    )(*args)
    return res


def _mlstm_scan_kernel(*refs, T, has_state, emit_state):
    q_ref, k_ref, v_ref, gc_ref, gr_ref = refs[:5]
    pos = 5
    if has_state:
        c0f_ref, c0b_ref, n0f_ref, n0b_ref, m0f_ref, m0b_ref = refs[pos:pos + 6]
        pos += 7
    o_ref = refs[pos]
    pos += 1
    if emit_state:
        cout_ref, nout_ref, mout_ref = refs[pos:pos + 3]
        pos += 3
    ct_refs = refs[pos:pos + 2]
    n_refs = refs[pos + 2:pos + 4]
    m_refs = refs[pos + 4:pos + 6]
    vt_ref, ob_ref = refs[pos + 6:pos + 8]

    nsb = T // SB
    for j in range(nsb):
        vt_ref[j] = v_ref[j * SB:(j + 1) * SB, :].T.astype(BF16)
    if has_state:
        for d, (c0, n0, m0) in enumerate(((c0f_ref, n0f_ref, m0f_ref), (c0b_ref, n0b_ref, m0b_ref))):
            ct_refs[d][...] = c0[...].T
            n_refs[d][...] = n0[...]
            m_refs[d][...] = m0[...]
    else:
        for r in tuple(ct_refs) + tuple(n_refs) + tuple(m_refs):
            r[...] = jnp.zeros_like(r)

    lower, upper = _block_tri_masks()

    def one(j, forward):
        d = 0 if forward else 1
        rows = pl.ds(pl.multiple_of(j * SB, SB), SB)
        q = q_ref[rows, :]
        k = k_ref[rows, :]
        v = v_ref[rows, :].astype(BF16)
        gc = gc_ref[rows, :]
        gr = gr_ref[j]
        o0 = 2 * d
        i_c, f_c = gc[:, o0:o0 + 1], gc[:, o0 + 1:o0 + 2]
        i_r, f_r = gr[o0:o0 + 1, :], gr[o0 + 1:o0 + 2, :]
        mask = lower if forward else upper
        other = upper if forward else lower
        order = list(range(NCB)) if forward else list(reversed(range(NCB)))
        b_c = jnp.sum(jnp.where(mask, f_r, 0.0), axis=1, keepdims=True)
        b_r = jnp.sum(jnp.where(other, f_c, 0.0), axis=0, keepdims=True)
        b_last = [jnp.sum(f_c[c * CH:(c + 1) * CH, :], axis=0, keepdims=True) for c in range(NCB)]
        d_log = jnp.where(mask, b_c - b_r + i_r, NEG_INF)
        a_c = _per_chunk(b_last, 1) - b_c + i_c
        m_loc = [jnp.max(a_c[c * CH:(c + 1) * CH, :], axis=0, keepdims=True) for c in range(NCB)]
        kw = k * jnp.exp(a_c - _per_chunk(m_loc, 1))
        kv_t = _dot(vt_ref[j], _chunk_masked_cat(kw))
        k_loc = [jnp.sum(kw[c * CH:(c + 1) * CH, :], axis=0, keepdims=True) for c in range(NCB)]

        m = m_refs[d][:, 0:1]
        m_start, s_old, s_loc = [None] * NCB, [None] * NCB, [None] * NCB
        for c in order:
            m_start[c] = m
            m_new = jnp.maximum(b_last[c] + m, m_loc[c])
            s_old[c] = jnp.exp(b_last[c] + m - m_new)
            s_loc[c] = jnp.exp(m_loc[c] - m_new)
            m = m_new
        m_refs[d][...] = jnp.broadcast_to(m, m_refs[d].shape)

        l_inter = b_c + _per_chunk(m_start, 1)
        m_i = jnp.maximum(l_inter, jnp.max(d_log, axis=1, keepdims=True))
        qb = q.astype(BF16)
        s = _dot_nt(qb, k.astype(BF16)) * jnp.exp(d_log - m_i)
        e_inter = jnp.exp(l_inter - m_i)
        num = _dot(s.astype(BF16), v)
        den = jnp.sum(s, axis=1, keepdims=True)

        ct = ct_refs[d][...]
        nrm = n_refs[d][...]
        inter, qn = [None] * NCB, [None] * NCB
        for c in order:
            rc = slice(c * CH, (c + 1) * CH)
            inter[c] = _dot_nt(qb[rc, :], ct.astype(BF16))
            qn[c] = jnp.sum(q[rc, :] * nrm, axis=1, keepdims=True)
            ct = s_old[c] * ct + s_loc[c] * kv_t[:, c * DK:(c + 1) * DK]
            nrm = s_old[c] * nrm + s_loc[c] * k_loc[c]
        ct_refs[d][...] = ct
        n_refs[d][...] = nrm

        num = num + e_inter * jnp.concatenate(inter, axis=0)
        den = den + e_inter * jnp.concatenate(qn, axis=0)
        (o_ref if forward else ob_ref)[rows, :] = num / jnp.maximum(jnp.abs(den), jnp.exp(-m_i))

    def step(jf, jb):
        one(jf, True)
        one(jb, False)

    _scan_driver(T, o_ref, ob_ref, step)
    if emit_state:
        for d in range(2):
            cout_ref[d] = ct_refs[d][...].T
            nout_ref[d] = n_refs[d][...]
            mout_ref[d] = m_refs[d][...]


def _mlstm_scan(q, k, v, gc, gr, states, slot, o_prev):
    ctx = states is None
    T = T_CTX if ctx else T_LAT
    n_seq = N_CTX_SEQ if ctx else N_LAT_SEQ
    off = 0 if ctx else N_CTX // T_LAT
    in_specs = [
        pl.BlockSpec((T, DK), lambda b, h: (off + b, h)),
        pl.BlockSpec((T, DK), lambda b, h: (off + b, h)),
        pl.BlockSpec((T, DV), lambda b, h: (off + b, h)),
        pl.BlockSpec((None, T, 4), lambda b, h: (h, off + b, 0)),
        pl.BlockSpec((None, T // SB, 4, SB), lambda b, h: (h, off + b, 0, 0)),
    ]
    args = [q, k, v, gc, gr]
    out_shape = [jax.ShapeDtypeStruct((N_TOK, DVT), F32)]
    out_specs = [pl.BlockSpec((T, DV), lambda b, h: (off + b, h))]
    aliases = {}
    if ctx:
        out_shape += [jax.ShapeDtypeStruct((N_CTX_SEQ, 2, NH, DK, DV), F32),
                      jax.ShapeDtypeStruct((N_CTX_SEQ, 2, NH, 1, DK), F32),
                      jax.ShapeDtypeStruct((N_CTX_SEQ, 2, NH, 1, DK), F32)]
        out_specs += [pl.BlockSpec((None, 2, None, DK, DV), lambda b, h: (b, 0, h, 0, 0)),
                      pl.BlockSpec((None, 2, None, 1, DK), lambda b, h: (b, 0, h, 0, 0)),
                      pl.BlockSpec((None, 2, None, 1, DK), lambda b, h: (b, 0, h, 0, 0))]
    else:
        c0, n0, m0 = states
        n0 = n0.reshape(N_LAT_SEQ, -1, 2, NH, 1, DK)
        m0 = jnp.broadcast_to(m0[..., None, None], m0.shape + (1, DK))
        sq_c = (None, None, None, None, DK, DV)
        sq_v = (None, None, None, None, 1, DK)
        for arr, sq in ((c0, sq_c), (n0, sq_v), (m0, sq_v)):
            for d in range(2):
                in_specs.append(pl.BlockSpec(sq, functools.partial(lambda b, h, d: (b, slot, d, h, 0, 0), d=d)))
                args.append(arr)
        in_specs.append(pl.BlockSpec(memory_space=pl.ANY))
        args.append(o_prev)
        aliases = {11: 0}
    return pl.pallas_call(
        functools.partial(_mlstm_scan_kernel, T=T, has_state=not ctx, emit_state=ctx),
        out_shape=out_shape,
        grid=(n_seq, NH),
        in_specs=in_specs,
        out_specs=out_specs,
        scratch_shapes=[pltpu.VMEM((DV, DK), F32), pltpu.VMEM((DV, DK), F32),
                        pltpu.VMEM((1, DK), F32), pltpu.VMEM((1, DK), F32),
                        pltpu.VMEM((1, DK), F32), pltpu.VMEM((1, DK), F32),
                        pltpu.VMEM((T // SB, DV, SB), BF16), pltpu.VMEM((T, DV), F32)],
        input_output_aliases=aliases,
        compiler_params=_cp(("parallel", "parallel")),
        name="mlstm_scan_ctx" if ctx else "mlstm_scan_lat",
    )(*args)


def _route(lg):
    lane = lax.broadcasted_iota(jnp.int32, lg.shape, 1)
    big = jnp.int32(ROUTE_W)
    is_g = (lane >= NE) & (lane < NE + N_GROUPS)
    gl = jnp.where(is_g, lg, NEG_INF)
    gmax = jnp.max(gl, axis=1, keepdims=True)
    gsel = jnp.min(jnp.where(gl == gmax, lane, big), axis=1, keepdims=True) - NE
    gw = 1.0 / jnp.sum(jnp.exp(gl - gmax), axis=1, keepdims=True)
    ing = (lane < NE) & ((lane >> 3) == gsel)
    el = jnp.where(ing, lg, NEG_INF)
    emax = jnp.max(el, axis=1, keepdims=True)
    p = jnp.exp(el - emax)
    prob = p / jnp.sum(p, axis=1, keepdims=True)
    p1 = jnp.max(prob, axis=1, keepdims=True)
    i1 = jnp.min(jnp.where(ing & (prob == p1), lane, big), axis=1, keepdims=True)
    rest = ing & (lane != i1)
    prob2 = jnp.where(rest, prob, -1.0)
    p2 = jnp.max(prob2, axis=1, keepdims=True)
    i2 = jnp.min(jnp.where(rest & (prob2 == p2), lane, big), axis=1, keepdims=True)
    tot = p1 + p2
    return gw * jnp.where(lane == i1, p1 / tot, jnp.where(lane == i2, p2 / tot, 0.0))


def _out_kernel(o_ref, r_ref, x_ref, gate_ref, sh_ref, sc_ref, ng_ref, wo_ref, lg_ref, lb_ref, wr_ref, br_ref,
                x1_ref, h2_ref, comb_ref):
    y = _dot((o_ref[...] * ng_ref[...] * r_ref[...]).astype(BF16), wo_ref[...])
    x1 = _layernorm(DN_ALPHA * x_ref[...] + gate_ref[0] * y, lg_ref[...], lb_ref[...])
    x1_ref[...] = x1
    h2 = x1 * (1.0 + sc_ref[0]) + sh_ref[0]
    h_hi, h_lo = _split2(h2)
    h2_ref[...] = h_hi
    t = _dot(h_hi, wr_ref[...])
    lg = t[:, :ROUTE_W] + t[:, ROUTE_W:] + _dot(h_lo, wr_ref[:, :ROUTE_W]) + br_ref[...]
    comb_ref[...] = _route(lg)


def _out_proj(o, r, x, mods, norm_g, w_out, ln_g, ln_b, w_route, b_route):
    full = lambda s: pl.BlockSpec(s, lambda i: (0,) * len(s))
    tok = lambda n: pl.BlockSpec((TB, n), lambda i: (i, 0))
    return pl.pallas_call(
        _out_kernel,
        out_shape=[jax.ShapeDtypeStruct((N_TOK, D), F32), jax.ShapeDtypeStruct((N_TOK, D), BF16),
                   jax.ShapeDtypeStruct((N_TOK, ROUTE_W), F32)],
        grid=(NB,),
        in_specs=[tok(DVT), tok(DVT), tok(D),
                  _mod_spec(2, _mod_row_tb), _mod_spec(3, _mod_row_tb), _mod_spec(4, _mod_row_tb),
                  full((1, DVT)), full((DVT, D)), full((1, D)), full((1, D)),
                  full((D, 2 * ROUTE_W)), full((1, ROUTE_W))],
        out_specs=[tok(D), tok(D), tok(ROUTE_W)],
        compiler_params=_cp(("parallel",)),
        name="out_proj_route",
    )(o, r, x, mods, mods, mods, norm_g.reshape(1, DVT), w_out.astype(BF16),
      ln_g.reshape(1, D), ln_b.reshape(1, D), jnp.concatenate(_split2(w_route), axis=1), b_route)


def _moe_kernel(h_ref, comb_ref, x1_ref, gate_ref, lg_ref, lb_ref, wg_ref, wu_ref, wd_ref, o_ref, acc_ref):
    e = pl.program_id(1)

    @pl.when(e == 0)
    def _():
        acc_ref[...] = jnp.zeros_like(acc_ref)

    h = h_ref[...]
    a = _dot(h, wg_ref[0])
    u = _dot(h, wu_ref[0])
    comb = comb_ref[...]
    lane = lax.broadcasted_iota(jnp.int32, comb.shape, 1)
    c_e = jnp.sum(jnp.where(lane == e, comb, 0.0), axis=1, keepdims=True)
    hid = _silu(a) * u * c_e
    acc_ref[...] += _dot(hid.astype(BF16), wd_ref[0])

    @pl.when(e == NE - 1)
    def _():
        o_ref[...] = _layernorm(DN_ALPHA * x1_ref[...] + gate_ref[0] * acc_ref[...], lg_ref[...], lb_ref[...])


def _moe(h2, comb, x1, mods, ln_g, ln_b, w_gate, w_up, w_down):
    tok = lambda n: pl.BlockSpec((TM, n), lambda i, e: (i, 0))
    full = lambda s: pl.BlockSpec(s, lambda i, e: (0,) * len(s))
    return pl.pallas_call(
        _moe_kernel,
        out_shape=jax.ShapeDtypeStruct((N_TOK, D), F32),
        grid=(N_TOK // TM, NE),
        in_specs=[tok(D), tok(ROUTE_W), tok(D), _mod_spec(5, _mod_row_tm), full((1, D)), full((1, D)),
                  pl.BlockSpec((1, D, DE), lambda i, e: (e, 0, 0)),
                  pl.BlockSpec((1, D, DE), lambda i, e: (e, 0, 0)),
                  pl.BlockSpec((1, DE, D), lambda i, e: (e, 0, 0))],
        out_specs=tok(D),
        scratch_shapes=[pltpu.VMEM((TM, D), F32)],
        compiler_params=_cp(("parallel", "arbitrary")),
        name="moe_dense",
    )(h2, comb, x1, mods, ln_g.reshape(1, D), ln_b.reshape(1, D),
      w_gate.astype(BF16), w_up.astype(BF16), w_down.astype(BF16))


def kernel(x_prompt, x_sample, state_gla_S, state_mlstm_C, state_mlstm_n, state_mlstm_m, c, c_ctx,
           gla_w_in, gla_w_gate, gla_b_gate, gla_norm_g, gla_w_out,
           mlstm_w_in, mlstm_b_gates, mlstm_norm_g, mlstm_w_out,
           adaln_w, adaln_b, ln_g, ln_b,
           moe_w_group, moe_b_group, moe_w_expert, moe_b_expert, moe_w_gate, moe_w_up, moe_w_down):
    cvecs = jnp.zeros((MOD_ROWS, D), F32).at[0].set(c_ctx).at[1:1 + N_LAT_SEQ].set(c)
    mods_all = _modulation(cvecs, adaln_w, adaln_b)
    x = _embed(x_prompt, x_sample, _grid_posemb(T_LAT))
    gla_states, ml_c, ml_n, ml_m = [], [], [], []
    for l in range(DEPTH):
        s = l // 2
        mods = mods_all[l].reshape(MOD_ROWS * 6, 1, D)
        if l % 2 == 0:
            q, k, v, r, g = _gla_in(x, mods, gla_w_in[s], gla_w_gate[s], gla_b_gate[s])
            o, s_new = _gla_scan(q, k, v, g, None, s, None)
            (o,) = _gla_scan(q, k, v, g, state_gla_S, s, o)
            gla_states.append(s_new)
            norm_g, w_out = gla_norm_g[s], gla_w_out[s]
        else:
            q, k, v, r, gc, gr = _mlstm_in(x, mods, mlstm_w_in[s], mlstm_b_gates[s])
            o, c_new, n_new, m_new = _mlstm_scan(q, k, v, gc, gr, None, s, None)
            (o,) = _mlstm_scan(q, k, v, gc, gr, (state_mlstm_C, state_mlstm_n, state_mlstm_m), s, o)
            ml_c.append(c_new)
            ml_n.append(n_new[:, :, :, 0, :])
            ml_m.append(m_new[:, :, :, 0, 0])
            norm_g, w_out = mlstm_norm_g[s], mlstm_w_out[s]
        w_route = jnp.zeros((D, ROUTE_W), F32).at[:, :NE].set(moe_w_expert[l]).at[:, NE:NE + N_GROUPS].set(
            moe_w_group[l])
        b_route = jnp.zeros((1, ROUTE_W), F32).at[0, :NE].set(moe_b_expert[l]).at[0, NE:NE + N_GROUPS].set(
            moe_b_group[l])
        x1, h2, comb = _out_proj(o, r, x, mods, norm_g, w_out, ln_g[l, 0], ln_b[l, 0], w_route, b_route)
        x = _moe(h2, comb, x1, mods, ln_g[l, 1], ln_b[l, 1], moe_w_gate[l], moe_w_up[l], moe_w_down[l])
    y_ctx = x[:N_CTX].reshape(N_CTX_SEQ, T_CTX, D)
    y_lat = x[N_CTX:].reshape(N_LAT_SEQ, T_LAT, D)
    return (y_ctx, y_lat, jnp.stack(gla_states, 1), jnp.stack(ml_c, 1), jnp.stack(ml_n, 1), jnp.stack(ml_m, 1))
```

```python
import functools
import math

import jax
import jax.numpy as jnp
from jax import lax
from jax.experimental import pallas as pl
from jax.experimental.pallas import tpu as pltpu

F32 = jnp.float32
BF16 = jnp.bfloat16

D = 1024
N_CTX_SEQ, T_CTX = 16, 256
N_LAT_SEQ, T_LAT = 8, 1024
N_CTX = N_CTX_SEQ * T_CTX
N_LAT = N_LAT_SEQ * T_LAT
N_TOK = N_CTX + N_LAT
DEPTH = 4
GRID_W = 64
NH = 4
DK = 128
DV = 256
DKT = NH * DK
DVT = NH * DV
GATE_RANK = 16
GATE_TAU = 16.0
CH = 64
N_GROUPS = 4
EPG = 8
NE = N_GROUPS * EPG
DE = 256
DN_ALPHA = (2.0 * DEPTH) ** 0.25
LN_EPS = 1e-5
MOD_ROWS = 16
TB = 256
NB = N_TOK // TB
ROUTE_W = 128
NEG_INF = float("-inf")
VMEM_LIMIT = 48 * 1024 * 1024


def _cp(sem):
    return pltpu.CompilerParams(dimension_semantics=sem, vmem_limit_bytes=VMEM_LIMIT)


def _dot(a, b):
    return jnp.dot(a, b, preferred_element_type=F32)


def _dot_nt(a, b):
    return lax.dot_general(a, b, (((1,), (1,)), ((), ())), preferred_element_type=F32)


def _dot_f32(a, b):
    return jnp.dot(a, b, preferred_element_type=F32, precision=lax.Precision.HIGHEST)


def _split2(x):
    hi = x.astype(BF16)
    lo = (x - hi.astype(F32)).astype(BF16)
    return hi, lo


def _log_sigmoid(x):
    return -(jnp.maximum(-x, 0.0) + jnp.log1p(jnp.exp(-jnp.abs(x))))


def _silu(x):
    return x * jax.nn.sigmoid(x)


def _layernorm(x, g, b):
    mu = jnp.mean(x, axis=-1, keepdims=True)
    xc = x - mu
    var = jnp.mean(xc * xc, axis=-1, keepdims=True)
    return xc * lax.rsqrt(var + LN_EPS) * g + b


def _mod_row_tb(i):
    return jnp.where(i < N_CTX // TB, 0, 1 + (i - N_CTX // TB) // (T_LAT // TB))


def _mod_spec(j, row_fn):
    return pl.BlockSpec((1, 1, D), lambda i, *_: (row_fn(i) * 6 + j, 0, 0))


def _mod_kernel(c_ref, w_ref, b_ref, o_ref):
    o_ref[0] = _dot_f32(_silu(c_ref[...]), w_ref[0]) + b_ref[0]


def _modulation(cvecs, adaln_w, adaln_b):
    nj = 6
    return pl.pallas_call(
        _mod_kernel,
        out_shape=jax.ShapeDtypeStruct((DEPTH, MOD_ROWS, 6 * D), F32),
        grid=(DEPTH, nj),
        in_specs=[
            pl.BlockSpec((MOD_ROWS, D), lambda l, j: (0, 0)),
            pl.BlockSpec((1, D, D), lambda l, j: (l, 0, j)),
            pl.BlockSpec((1, 1, D), lambda l, j: (l, 0, j)),
        ],
        out_specs=pl.BlockSpec((1, MOD_ROWS, D), lambda l, j: (l, 0, j)),
        compiler_params=_cp(("parallel", "parallel")),
        name="adaln_modulation",
    )(cvecs, adaln_w, adaln_b.reshape(DEPTH, 1, 6 * D))


def _embed_kernel(xp_ref, xs_ref, pos_ref, o_ref):
    i = pl.program_id(0)

    @pl.when(i < N_CTX // TB)
    def _():
        o_ref[...] = xp_ref[...]

    @pl.when(i >= N_CTX // TB)
    def _():
        o_ref[...] = xs_ref[...] + pos_ref[...]


def _embed(x_prompt, x_sample, pos):
    nc = N_CTX // TB
    return pl.pallas_call(
        _embed_kernel,
        out_shape=jax.ShapeDtypeStruct((N_TOK, D), F32),
        grid=(NB,),
        in_specs=[
            pl.BlockSpec((TB, D), lambda i: (jnp.minimum(i, nc - 1), 0)),
            pl.BlockSpec((TB, D), lambda i: (jnp.maximum(i - nc, 0), 0)),
            pl.BlockSpec((TB, D), lambda i: (jnp.maximum(i - nc, 0) % (T_LAT // TB), 0)),
        ],
        out_specs=pl.BlockSpec((TB, D), lambda i: (i, 0)),
        compiler_params=_cp(("parallel",)),
        name="embed_tokens",
    )(x_prompt.reshape(N_CTX, D), x_sample.reshape(N_LAT, D), pos)


def _grid_posemb(T):
    rows = T // GRID_W
    r, cidx = jnp.meshgrid(jnp.arange(rows), jnp.arange(GRID_W), indexing="ij")
    nf = D // 4
    freqs = jnp.exp(-math.log(10000.0) * jnp.arange(nf, dtype=F32) / nf)

    def emb(p):
        a = p.reshape(-1).astype(F32)[:, None] * freqs
        return jnp.concatenate([jnp.sin(a), jnp.cos(a)], -1)

    return jnp.concatenate([emb(r), emb(cidx)], -1)


def _gla_in_kernel(x_ref, sh_ref, sc_ref, wq_ref, wk_ref, wv_ref, wr_ref, wg_ref, wgate_ref, bgate_ref,
                   q_ref, k_ref, v_ref, r_ref, g_ref):
    h = (x_ref[...] * (1.0 + sc_ref[0]) + sh_ref[0]).astype(BF16)
    q_ref[...] = _dot(h, wq_ref[...]) * (DK ** -0.5)
    k_ref[...] = _dot(h, wk_ref[...])
    v_ref[...] = _dot(h, wv_ref[...])
    r_ref[...] = _silu(_dot(h, wr_ref[...]))
    glr = _dot(h, wg_ref[...]).astype(BF16)
    z = _dot(glr, wgate_ref[...]) + bgate_ref[...]
    g_ref[...] = _log_sigmoid(z) * (1.0 / GATE_TAU)


def _gla_in(x, mods, w_in, w_gate, b_gate):
    wq = w_in[:, :DKT].astype(BF16)
    wk = w_in[:, DKT:2 * DKT].astype(BF16)
    wv = w_in[:, 2 * DKT:2 * DKT + DVT].astype(BF16)
    wr = w_in[:, 2 * DKT + DVT:2 * DKT + 2 * DVT].astype(BF16)
    wg = w_in[:, 2 * DKT + 2 * DVT:].astype(BF16)
    wgate = jnp.zeros((2 * GATE_RANK, 2 * DKT), F32)
    wgate = wgate.at[:GATE_RANK, :DKT].set(w_gate[0]).at[GATE_RANK:, DKT:].set(w_gate[1]).astype(BF16)
    bgate = b_gate.reshape(1, 2 * DKT)
    full = lambda s: pl.BlockSpec(s, lambda i: (0,) * len(s))
    tok = lambda n: pl.BlockSpec((TB, n), lambda i: (i, 0))
    return pl.pallas_call(
        _gla_in_kernel,
        out_shape=[jax.ShapeDtypeStruct((N_TOK, DKT), F32), jax.ShapeDtypeStruct((N_TOK, DKT), F32),
                   jax.ShapeDtypeStruct((N_TOK, DVT), F32), jax.ShapeDtypeStruct((N_TOK, DVT), F32),
                   jax.ShapeDtypeStruct((N_TOK, 2 * DKT), F32)],
        grid=(NB,),
        in_specs=[tok(D), _mod_spec(0, _mod_row_tb), _mod_spec(1, _mod_row_tb),
                  full((D, DKT)), full((D, DKT)), full((D, DVT)), full((D, DVT)), full((D, 2 * GATE_RANK)),
                  full((2 * GATE_RANK, 2 * DKT)), full((1, 2 * DKT))],
        out_specs=[tok(DKT), tok(DKT), tok(DVT), tok(DVT), tok(2 * DKT)],
        compiler_params=_cp(("parallel",)),
        name="gla_in_proj",
    )(x, mods, mods, wq, wk, wv, wr, wg, wgate, bgate)


def _mlstm_in_kernel(x_ref, sh_ref, sc_ref, wq_ref, wk_ref, wv_ref, wr_ref, wg_ref, wgt_ref, bg_ref, bgt_ref,
                     q_ref, k_ref, v_ref, r_ref, gc_ref, gr_ref):
    h = (x_ref[...] * (1.0 + sc_ref[0]) + sh_ref[0]).astype(BF16)
    q_ref[...] = _dot(h, wq_ref[...]) * (DK ** -0.5)
    k_ref[...] = _dot(h, wk_ref[...])
    v_ref[...] = _dot(h, wv_ref[...])
    r_ref[...] = jax.nn.sigmoid(_dot(h, wr_ref[...]))
    gc = _dot(h, wg_ref[...]) + bg_ref[...]
    gr = _dot_nt(wgt_ref[...], h) + bgt_ref[...]
    is_f_c = (lax.broadcasted_iota(jnp.int32, gc.shape, 1) % 2) == 1
    is_f_r = (lax.broadcasted_iota(jnp.int32, gr.shape, 0) % 2) == 1
    gc = jnp.where(is_f_c, _log_sigmoid(gc), gc)
    gr = jnp.where(is_f_r, _log_sigmoid(gr), gr)
    for hd in range(NH):
        gc_ref[hd] = gc[:, 4 * hd:4 * hd + 4]
        gr_ref[hd, 0] = gr[4 * hd:4 * hd + 4, :]


def _mlstm_in(x, mods, w_in, b_gates):
    wq = w_in[:, :DKT].astype(BF16)
    wk = w_in[:, DKT:2 * DKT].astype(BF16)
    wv = w_in[:, 2 * DKT:2 * DKT + DVT].astype(BF16)
    wr = w_in[:, 2 * DKT + DVT:2 * DKT + 2 * DVT].astype(BF16)
    wg = w_in[:, 2 * DKT + 2 * DVT:].reshape(D, 4, NH).transpose(0, 2, 1).reshape(D, 4 * NH).astype(BF16)
    bg = b_gates.reshape(4, NH).T.reshape(1, 4 * NH)
    full = lambda s: pl.BlockSpec(s, lambda i: (0,) * len(s))
    tok = lambda n: pl.BlockSpec((TB, n), lambda i: (i, 0))
    return pl.pallas_call(
        _mlstm_in_kernel,
        out_shape=[jax.ShapeDtypeStruct((N_TOK, DKT), F32), jax.ShapeDtypeStruct((N_TOK, DKT), F32),
                   jax.ShapeDtypeStruct((N_TOK, DVT), F32), jax.ShapeDtypeStruct((N_TOK, DVT), F32),
                   jax.ShapeDtypeStruct((NH, N_TOK, 4), F32),
                   jax.ShapeDtypeStruct((NH, N_TOK // TB, 4, TB), F32)],
        grid=(NB,),
        in_specs=[tok(D), _mod_spec(0, _mod_row_tb), _mod_spec(1, _mod_row_tb),
                  full((D, DKT)), full((D, DKT)), full((D, DVT)), full((D, DVT)), full((D, 4 * NH)),
                  full((4 * NH, D)), full((1, 4 * NH)), full((4 * NH, 1))],
        out_specs=[tok(DKT), tok(DKT), tok(DVT), tok(DVT),
                   pl.BlockSpec((NH, TB, 4), lambda i: (0, i, 0)),
                   pl.BlockSpec((NH, 1, 4, TB), lambda i: (0, i, 0, 0))],
        compiler_params=_cp(("parallel",)),
        name="mlstm_in_proj",
    )(x, mods, mods, wq, wk, wv, wr, wg, wg.T, bg, bg.T)


SB = 256
NCB = SB // CH
CH_SHIFT = 6


def _block_tri_masks():
    row = lax.broadcasted_iota(jnp.int32, (SB, SB), 0)
    col = lax.broadcasted_iota(jnp.int32, (SB, SB), 1)
    same = (row >> CH_SHIFT) == (col >> CH_SHIFT)
    return same & (row >= col), same & (row <= col)


def _chunk_cumsum(x, forward):
    rin = lax.broadcasted_iota(jnp.int32, x.shape, 0) & (CH - 1)
    s = 1
    while s < CH:
        if forward:
            x = x + jnp.where(rin >= s, pltpu.roll(x, s, 0), 0.0)
        else:
            x = x + jnp.where(rin < CH - s, pltpu.roll(x, SB - s, 0), 0.0)
        s *= 2
    return x


def _per_chunk(vals, n):
    return jnp.concatenate([jnp.broadcast_to(v, (CH, n)) for v in vals], axis=0)


def _chunk_masked_cat(x):
    rowc = lax.broadcasted_iota(jnp.int32, x.shape, 0) >> CH_SHIFT
    return jnp.concatenate([jnp.where(rowc == c, x, 0.0).astype(BF16) for c in range(NCB)], axis=1)


def _head_norm_rows(o):
    mu = jnp.mean(o, axis=-1, keepdims=True)
    oc = o - mu
    var = jnp.mean(oc * oc, axis=-1, keepdims=True)
    return oc * lax.rsqrt(var + LN_EPS)


def _gla_superblock(q, k, v, v_t, g, st, forward, mask):
    b = _chunk_cumsum(g, forward)
    r_last = CH - 1 if forward else 0
    r_mid = CH // 2 if forward else CH - 1 - CH // 2
    lasts = [b[c * CH + r_last:c * CH + r_last + 1, :] for c in range(NCB)]
    b_last = _per_chunk(lasts, DK)
    b_mid = _per_chunk([b[c * CH + r_mid:c * CH + r_mid + 1, :] for c in range(NCB)], DK)
    qe = (q * jnp.exp(b - b_mid)).astype(BF16)
    ke = (k * jnp.exp(b_mid - b)).astype(BF16)
    qb = (q * jnp.exp(b)).astype(BF16)
    kd = k * jnp.exp(b_last - b)
    a = jnp.where(mask, _dot_nt(qe, ke), 0.0).astype(BF16)
    o_intra = _dot(a, v)
    d_t = _dot(v_t, _chunk_masked_cat(kd))
    inter = [None] * NCB
    for c in (range(NCB) if forward else reversed(range(NCB))):
        inter[c] = _dot_nt(qb[c * CH:(c + 1) * CH, :], st.astype(BF16))
        st = st * jnp.exp(lasts[c]) + d_t[:, c * DK:(c + 1) * DK]
    return o_intra + jnp.concatenate(inter, axis=0), st


def _scan_driver(T, o_ref, ob_ref, step):
    nsb = T // SB
    if nsb == 1:
        step(0, 0)
    else:
        def body(i, carry):
            step(i, nsb - 1 - i)
            return carry

        lax.fori_loop(0, nsb, body, 0)

    def norm_body(j, carry):
        rows = pl.ds(pl.multiple_of(j * SB, SB), SB)
        o_ref[rows, :] = _head_norm_rows(o_ref[rows, :] + ob_ref[rows, :])
        return carry

    if nsb == 1:
        norm_body(0, 0)
    else:
        lax.fori_loop(0, nsb, norm_body, 0)


def _gla_scan_kernel(*refs, T, has_state, emit_state):
    q_ref, k_ref, v_ref, gf_ref, gb_ref = refs[:5]
    pos = 5
    if has_state:
        s0f_ref, s0b_ref = refs[pos:pos + 2]
        pos += 2
    o_ref = refs[pos]
    pos += 1
    if emit_state:
        sout_ref = refs[pos]
        pos += 1
    stf_ref, stb_ref, vt_ref, ob_ref = refs[pos:pos + 4]

    nsb = T // SB
    for j in range(nsb):
        vt_ref[j] = v_ref[j * SB:(j + 1) * SB, :].T.astype(BF16)
    if has_state:
        stf_ref[...] = s0f_ref[...].T
        stb_ref[...] = s0b_ref[...].T
    else:
        stf_ref[...] = jnp.zeros_like(stf_ref)
        stb_ref[...] = jnp.zeros_like(stb_ref)

    lower, upper = _block_tri_masks()

    def one(j, forward):
        rows = pl.ds(pl.multiple_of(j * SB, SB), SB)
        st_ref = stf_ref if forward else stb_ref
        o, st = _gla_superblock(q_ref[rows, :], k_ref[rows, :], v_ref[rows, :].astype(BF16), vt_ref[j],
                                (gf_ref if forward else gb_ref)[rows, :], st_ref[...], forward,
                                lower if forward else upper)
        st_ref[...] = st
        (o_ref if forward else ob_ref)[rows, :] = o

    def step(jf, jb):
        one(jf, True)
        one(jb, False)

    _scan_driver(T, o_ref, ob_ref, step)
    if emit_state:
        sout_ref[0] = stf_ref[...].T
        sout_ref[1] = stb_ref[...].T


def _gla_scan(q, k, v, g, state, slot):
    ctx = state is None
    T = T_CTX if ctx else T_LAT
    n_seq = N_CTX_SEQ if ctx else N_LAT_SEQ
    off = 0 if ctx else N_CTX // T_LAT
    in_specs = [
        pl.BlockSpec((T, DK), lambda b, h: (off + b, h)),
        pl.BlockSpec((T, DK), lambda b, h: (off + b, h)),
        pl.BlockSpec((T, DV), lambda b, h: (off + b, h)),
        pl.BlockSpec((T, DK), lambda b, h: (off + b, h)),
        pl.BlockSpec((T, DK), lambda b, h: (off + b, NH + h)),
    ]
    args = [q, k, v, g, g]
    out_shape = [jax.ShapeDtypeStruct((n_seq * T, DVT), F32)]
    out_specs = [pl.BlockSpec((T, DV), lambda b, h: (b, h))]
    if ctx:
        out_shape.append(jax.ShapeDtypeStruct((N_CTX_SEQ, 2, NH, DK, DV), F32))
        out_specs.append(pl.BlockSpec((None, 2, None, DK, DV), lambda b, h: (b, 0, h, 0, 0)))
    else:
        sq = (None, None, None, None, DK, DV)
        in_specs += [pl.BlockSpec(sq, lambda b, h: (b, slot, 0, h, 0, 0)),
                     pl.BlockSpec(sq, lambda b, h: (b, slot, 1, h, 0, 0))]
        args += [state, state]
    res = pl.pallas_call(
        functools.partial(_gla_scan_kernel, T=T, has_state=not ctx, emit_state=ctx),
        out_shape=out_shape,
        grid=(n_seq, NH),
        in_specs=in_specs,
        out_specs=out_specs,
        scratch_shapes=[pltpu.VMEM((DV, DK), F32), pltpu.VMEM((DV, DK), F32),
                        pltpu.VMEM((T // SB, DV, SB), BF16), pltpu.VMEM((T, DV), F32)],
        compiler_params=_cp(("parallel", "parallel")),
        name="gla_scan_ctx" if ctx else "gla_scan_lat",
    )(*args)
    return res


def _mlstm_scan_kernel(*refs, T, has_state, emit_state):
    q_ref, k_ref, v_ref, gc_ref, gr_ref = refs[:5]
    pos = 5
    if has_state:
        c0f_ref, c0b_ref, n0f_ref, n0b_ref, m0f_ref, m0b_ref = refs[pos:pos + 6]
        pos += 6
    o_ref = refs[pos]
    pos += 1
    if emit_state:
        cout_ref, nout_ref, mout_ref = refs[pos:pos + 3]
        pos += 3
    ct_refs = refs[pos:pos + 2]
    n_refs = refs[pos + 2:pos + 4]
    m_refs = refs[pos + 4:pos + 6]
    vt_ref, ob_ref = refs[pos + 6:pos + 8]

    nsb = T // SB
    for j in range(nsb):
        vt_ref[j] = v_ref[j * SB:(j + 1) * SB, :].T.astype(BF16)
    if has_state:
        for d, (c0, n0, m0) in enumerate(((c0f_ref, n0f_ref, m0f_ref), (c0b_ref, n0b_ref, m0b_ref))):
            ct_refs[d][...] = c0[...].T
            n_refs[d][...] = n0[...]
            m_refs[d][...] = m0[...]
    else:
        for r in tuple(ct_refs) + tuple(n_refs) + tuple(m_refs):
            r[...] = jnp.zeros_like(r)

    lower, upper = _block_tri_masks()

    def one(j, forward):
        d = 0 if forward else 1
        rows = pl.ds(pl.multiple_of(j * SB, SB), SB)
        q = q_ref[rows, :]
        k = k_ref[rows, :]
        v = v_ref[rows, :].astype(BF16)
        gc = gc_ref[rows, :]
        gr = gr_ref[j]
        o0 = 2 * d
        i_c, f_c = gc[:, o0:o0 + 1], gc[:, o0 + 1:o0 + 2]
        i_r, f_r = gr[o0:o0 + 1, :], gr[o0 + 1:o0 + 2, :]
        mask = lower if forward else upper
        other = upper if forward else lower
        order = list(range(NCB)) if forward else list(reversed(range(NCB)))
        b_c = jnp.sum(jnp.where(mask, f_r, 0.0), axis=1, keepdims=True)
        b_r = jnp.sum(jnp.where(other, f_c, 0.0), axis=0, keepdims=True)
        b_last = [jnp.sum(f_c[c * CH:(c + 1) * CH, :], axis=0, keepdims=True) for c in range(NCB)]
        d_log = jnp.where(mask, b_c - b_r + i_r, NEG_INF)
        a_c = _per_chunk(b_last, 1) - b_c + i_c
        m_loc = [jnp.max(a_c[c * CH:(c + 1) * CH, :], axis=0, keepdims=True) for c in range(NCB)]
        kw = k * jnp.exp(a_c - _per_chunk(m_loc, 1))
        kv_t = _dot(vt_ref[j], _chunk_masked_cat(kw))
        k_loc = [jnp.sum(kw[c * CH:(c + 1) * CH, :], axis=0, keepdims=True) for c in range(NCB)]

        m = m_refs[d][:, 0:1]
        m_start, s_old, s_loc = [None] * NCB, [None] * NCB, [None] * NCB
        for c in order:
            m_start[c] = m
            m_new = jnp.maximum(b_last[c] + m, m_loc[c])
            s_old[c] = jnp.exp(b_last[c] + m - m_new)
            s_loc[c] = jnp.exp(m_loc[c] - m_new)
            m = m_new
        m_refs[d][...] = jnp.broadcast_to(m, m_refs[d].shape)

        l_inter = b_c + _per_chunk(m_start, 1)
        m_i = jnp.maximum(l_inter, jnp.max(d_log, axis=1, keepdims=True))
        qb = q.astype(BF16)
        s = _dot_nt(qb, k.astype(BF16)) * jnp.exp(d_log - m_i)
        e_inter = jnp.exp(l_inter - m_i)
        num = _dot(s.astype(BF16), v)
        den = jnp.sum(s, axis=1, keepdims=True)

        ct = ct_refs[d][...]
        nrm = n_refs[d][...]
        inter, qn = [None] * NCB, [None] * NCB
        for c in order:
            rc = slice(c * CH, (c + 1) * CH)
            inter[c] = _dot_nt(qb[rc, :], ct.astype(BF16))
            qn[c] = jnp.sum(q[rc, :] * nrm, axis=1, keepdims=True)
            ct = s_old[c] * ct + s_loc[c] * kv_t[:, c * DK:(c + 1) * DK]
            nrm = s_old[c] * nrm + s_loc[c] * k_loc[c]
        ct_refs[d][...] = ct
        n_refs[d][...] = nrm

        num = num + e_inter * jnp.concatenate(inter, axis=0)
        den = den + e_inter * jnp.concatenate(qn, axis=0)
        (o_ref if forward else ob_ref)[rows, :] = num / jnp.maximum(jnp.abs(den), jnp.exp(-m_i))

    def step(jf, jb):
        one(jf, True)
        one(jb, False)

    _scan_driver(T, o_ref, ob_ref, step)
    if emit_state:
        for d in range(2):
            cout_ref[d] = ct_refs[d][...].T
            nout_ref[d] = n_refs[d][...]
            mout_ref[d] = m_refs[d][...]


def _mlstm_scan(q, k, v, gc, gr, states, slot):
    ctx = states is None
    T = T_CTX if ctx else T_LAT
    n_seq = N_CTX_SEQ if ctx else N_LAT_SEQ
    off = 0 if ctx else N_CTX // T_LAT
    in_specs = [
        pl.BlockSpec((T, DK), lambda b, h: (off + b, h)),
        pl.BlockSpec((T, DK), lambda b, h: (off + b, h)),
        pl.BlockSpec((T, DV), lambda b, h: (off + b, h)),
        pl.BlockSpec((None, T, 4), lambda b, h: (h, off + b, 0)),
        pl.BlockSpec((None, T // SB, 4, SB), lambda b, h: (h, off + b, 0, 0)),
    ]
    args = [q, k, v, gc, gr]
    out_shape = [jax.ShapeDtypeStruct((n_seq * T, DVT), F32)]
    out_specs = [pl.BlockSpec((T, DV), lambda b, h: (b, h))]
    if ctx:
        out_shape += [jax.ShapeDtypeStruct((N_CTX_SEQ, 2, NH, DK, DV), F32),
                      jax.ShapeDtypeStruct((N_CTX_SEQ, 2, NH, 1, DK), F32),
                      jax.ShapeDtypeStruct((N_CTX_SEQ, 2, NH, 1, DK), F32)]
        out_specs += [pl.BlockSpec((None, 2, None, DK, DV), lambda b, h: (b, 0, h, 0, 0)),
                      pl.BlockSpec((None, 2, None, 1, DK), lambda b, h: (b, 0, h, 0, 0)),
                      pl.BlockSpec((None, 2, None, 1, DK), lambda b, h: (b, 0, h, 0, 0))]
    else:
        c0, n0, m0 = states
        n0 = n0.reshape(N_LAT_SEQ, -1, 2, NH, 1, DK)
        m0 = jnp.broadcast_to(m0[..., None, None], m0.shape + (1, DK))
        sq_c = (None, None, None, None, DK, DV)
        sq_v = (None, None, None, None, 1, DK)
        for arr, sq in ((c0, sq_c), (n0, sq_v), (m0, sq_v)):
            for d in range(2):
                in_specs.append(pl.BlockSpec(sq, functools.partial(lambda b, h, d: (b, slot, d, h, 0, 0), d=d)))
                args.append(arr)
    return pl.pallas_call(
        functools.partial(_mlstm_scan_kernel, T=T, has_state=not ctx, emit_state=ctx),
        out_shape=out_shape,
        grid=(n_seq, NH),
        in_specs=in_specs,
        out_specs=out_specs,
        scratch_shapes=[pltpu.VMEM((DV, DK), F32), pltpu.VMEM((DV, DK), F32),
                        pltpu.VMEM((1, DK), F32), pltpu.VMEM((1, DK), F32),
                        pltpu.VMEM((1, DK), F32), pltpu.VMEM((1, DK), F32),
                        pltpu.VMEM((T // SB, DV, SB), BF16), pltpu.VMEM((T, DV), F32)],
        compiler_params=_cp(("parallel", "parallel")),
        name="mlstm_scan_ctx" if ctx else "mlstm_scan_lat",
    )(*args)


def _route(lg):
    lane = lax.broadcasted_iota(jnp.int32, lg.shape, 1)
    big = jnp.int32(ROUTE_W)
    is_g = (lane >= NE) & (lane < NE + N_GROUPS)
    gl = jnp.where(is_g, lg, NEG_INF)
    gmax = jnp.max(gl, axis=1, keepdims=True)
    gsel = jnp.min(jnp.where(gl == gmax, lane, big), axis=1, keepdims=True) - NE
    gw = 1.0 / jnp.sum(jnp.exp(gl - gmax), axis=1, keepdims=True)
    ing = (lane < NE) & ((lane >> 3) == gsel)
    el = jnp.where(ing, lg, NEG_INF)
    emax = jnp.max(el, axis=1, keepdims=True)
    p = jnp.exp(el - emax)
    prob = p / jnp.sum(p, axis=1, keepdims=True)
    p1 = jnp.max(prob, axis=1, keepdims=True)
    i1 = jnp.min(jnp.where(ing & (prob == p1), lane, big), axis=1, keepdims=True)
    rest = ing & (lane != i1)
    prob2 = jnp.where(rest, prob, -1.0)
    p2 = jnp.max(prob2, axis=1, keepdims=True)
    i2 = jnp.min(jnp.where(rest & (prob2 == p2), lane, big), axis=1, keepdims=True)
    tot = p1 + p2
    rinfo = jnp.where(lane == 0, i1.astype(F32),
                      jnp.where(lane == 1, i2.astype(F32),
                                jnp.where(lane == 2, gw * (p1 / tot), jnp.where(lane == 3, gw * (p2 / tot), 0.0))))
    count = jnp.sum(jnp.where((lane == i1) | (lane == i2), 1.0, 0.0), axis=0, keepdims=True)
    return rinfo, count


def _out_kernel(oc_ref, ol_ref, r_ref, x_ref, gate_ref, sh_ref, sc_ref, ng_ref, wo_ref, lg_ref, lb_ref, wr_ref,
                br_ref, x1_ref, h2_ref, rinfo_ref, cnt_ref):
    o = jnp.where(pl.program_id(0) < N_CTX // TB, oc_ref[...], ol_ref[...])
    y = _dot((o * ng_ref[...] * r_ref[...]).astype(BF16), wo_ref[...])
    x1 = _layernorm(DN_ALPHA * x_ref[...] + gate_ref[0] * y, lg_ref[...], lb_ref[...])
    x1_ref[...] = x1
    h2 = x1 * (1.0 + sc_ref[0]) + sh_ref[0]
    h_hi, h_lo = _split2(h2)
    h2_ref[...] = h_hi
    t = _dot(h_hi, wr_ref[...])
    lg = t[:, :ROUTE_W] + t[:, ROUTE_W:] + _dot(h_lo, wr_ref[:, :ROUTE_W]) + br_ref[...]
    rinfo_ref[...], cnt_ref[0] = _route(lg)


def _out_proj(o_ctx, o_lat, r, x, mods, norm_g, w_out, ln_g, ln_b, w_route, b_route):
    nc = N_CTX // TB
    full = lambda s: pl.BlockSpec(s, lambda i: (0,) * len(s))
    tok = lambda n: pl.BlockSpec((TB, n), lambda i: (i, 0))
    return pl.pallas_call(
        _out_kernel,
        out_shape=[jax.ShapeDtypeStruct((N_TOK, D), F32), jax.ShapeDtypeStruct((N_TOK, D), BF16),
                   jax.ShapeDtypeStruct((N_TOK, ROUTE_W), F32), jax.ShapeDtypeStruct((NB, 1, ROUTE_W), F32)],
        grid=(NB,),
        in_specs=[pl.BlockSpec((TB, DVT), lambda i: (jnp.minimum(i, nc - 1), 0)),
                  pl.BlockSpec((TB, DVT), lambda i: (jnp.maximum(i - nc, 0), 0)),
                  tok(DVT), tok(D),
                  _mod_spec(2, _mod_row_tb), _mod_spec(3, _mod_row_tb), _mod_spec(4, _mod_row_tb),
                  full((1, DVT)), full((DVT, D)), full((1, D)), full((1, D)),
                  full((D, 2 * ROUTE_W)), full((1, ROUTE_W))],
        out_specs=[tok(D), tok(D), tok(ROUTE_W), pl.BlockSpec((1, 1, ROUTE_W), lambda i: (i, 0, 0))],
        compiler_params=_cp(("parallel",)),
        name="out_proj_route",
    )(o_ctx, o_lat, r, x, mods, mods, mods, norm_g.reshape(1, DVT), w_out.astype(BF16),
      ln_g.reshape(1, D), ln_b.reshape(1, D), jnp.concatenate(_split2(w_route), axis=1), b_route)


TP = 512
NBP = N_TOK // TP
SEG = 16
R_LOC = 1536
PT = 256
XW = D + ROUTE_W
TMS = 256
R_TOT = -(-(2 * N_TOK + NBP * NE * (SEG - 1) + NE * (TMS - 1)) // TMS) * TMS
NT = R_TOT // TMS
POS_SPLIT = 64.0


def _mod_row_tp(i):
    return jnp.where(i < N_CTX // TP, 0, 1 + (i - N_CTX // TP) // (T_LAT // TP))


def _lane_pack(cols, shape):
    lane = lax.broadcasted_iota(jnp.int32, shape, 1)
    out = jnp.zeros(shape, F32)
    for j, c in enumerate(cols):
        out = jnp.where(lane == j, c, out)
    return out


def _segment_copies(nch_ref, loff_ref, goff_ref, blk, make_copy):
    def per_expert(e, issued):
        idx = blk * NE + e
        n = nch_ref[idx]
        l0 = loff_ref[idx]
        g0 = goff_ref[idx]

        def per_chunk(k, carry):
            make_copy(pl.multiple_of(l0 + k * SEG, SEG), pl.multiple_of(g0 + k * SEG, SEG)).start()
            return carry

        lax.fori_loop(0, n, per_chunk, 0)
        return issued + n

    issued = lax.fori_loop(0, NE, per_expert, 0)

    def wait_one(k, carry):
        make_copy(0, 0).wait()
        return carry

    lax.fori_loop(0, issued, wait_one, 0)


def _perm_kernel(nch_ref, loff_ref, goff_ref, h_ref, ri_ref, loffv_ref, xs_init_ref, pos_ref, xs_ref, xs_scr, sem):
    del xs_init_ref
    blk = pl.program_id(0)
    ri = ri_ref[...]
    e_a, e_b, w_a, w_b = ri[:, 0:1], ri[:, 1:2], ri[:, 2:3], ri[:, 3:4]
    lanef = lax.broadcasted_iota(jnp.int32, ri.shape, 1).astype(F32)
    is_a = lanef == e_a
    is_b = lanef == e_b
    sel = jnp.where(is_a | is_b, 1.0, 0.0).astype(BF16)
    row = lax.broadcasted_iota(jnp.int32, (TP, TP), 0)
    col = lax.broadcasted_iota(jnp.int32, (TP, TP), 1)
    earlier = jnp.where(row > col, 1.0, 0.0).astype(BF16)
    lpos = loffv_ref[0] + _dot(earlier, sel)
    pos_a = jnp.sum(jnp.where(is_a, lpos, 0.0), axis=1, keepdims=True)
    pos_b = jnp.sum(jnp.where(is_b, lpos, 0.0), axis=1, keepdims=True)
    pos_ref[...] = _lane_pack([pos_a, pos_b], ri.shape)

    hi_a = jnp.floor(pos_a * (1.0 / POS_SPLIT))
    hi_b = jnp.floor(pos_b * (1.0 / POS_SPLIT))
    parts = _lane_pack([hi_a, pos_a - POS_SPLIT * hi_a, hi_b, pos_b - POS_SPLIT * hi_b], ri.shape).astype(BF16)
    pick = jnp.where(lax.broadcasted_iota(jnp.int32, (8, ROUTE_W), 0) == lax.broadcasted_iota(jnp.int32, (8, ROUTE_W), 1),
                     1.0, 0.0).astype(BF16)
    pr = _dot_nt(pick, parts)
    pos_a_r = POS_SPLIT * pr[0:1, :] + pr[1:2, :]
    pos_b_r = POS_SPLIT * pr[2:3, :] + pr[3:4, :]

    wa1, wa2 = _split2(w_a)
    wa3 = (w_a - wa1.astype(F32) - wa2.astype(F32))
    wb1, wb2 = _split2(w_b)
    wb3 = (w_b - wb1.astype(F32) - wb2.astype(F32))
    wl = _lane_pack([wa1.astype(F32), wa2.astype(F32), wa3, wb1.astype(F32), wb2.astype(F32), wb3, e_a],
                    ri.shape).astype(BF16)
    hcat = jnp.concatenate([h_ref[...], wl], axis=1)
    for t in range(R_LOC // PT):
        rio = (lax.broadcasted_iota(jnp.int32, (PT, TP), 0) + t * PT).astype(F32)
        onehot = jnp.where((rio == pos_a_r) | (rio == pos_b_r), 1.0, 0.0).astype(BF16)
        xs_scr[t * PT:(t + 1) * PT, :] = _dot(onehot, hcat).astype(BF16)

    _segment_copies(nch_ref, loff_ref, goff_ref, blk,
                    lambda l, g: pltpu.make_async_copy(xs_scr.at[pl.ds(l, SEG)], xs_ref.at[pl.ds(g, SEG)], sem))


def _moe_permute(h2, rinfo, tables):
    nch, loff, goff, loffv = tables
    xs0 = jnp.zeros((R_TOT, XW), BF16)
    grid_spec = pltpu.PrefetchScalarGridSpec(
        num_scalar_prefetch=3,
        grid=(NBP,),
        in_specs=[pl.BlockSpec((TP, D), lambda i, *_: (i, 0)),
                  pl.BlockSpec((TP, ROUTE_W), lambda i, *_: (i, 0)),
                  pl.BlockSpec((1, 1, ROUTE_W), lambda i, *_: (i, 0, 0)),
                  pl.BlockSpec(memory_space=pl.ANY)],
        out_specs=[pl.BlockSpec((TP, ROUTE_W), lambda i, *_: (i, 0)),
                   pl.BlockSpec(memory_space=pl.ANY)],
        scratch_shapes=[pltpu.VMEM((R_LOC, XW), BF16), pltpu.SemaphoreType.DMA],
    )
    return pl.pallas_call(
        _perm_kernel,
        out_shape=[jax.ShapeDtypeStruct((N_TOK, ROUTE_W), F32), jax.ShapeDtypeStruct((R_TOT, XW), BF16)],
        grid_spec=grid_spec,
        input_output_aliases={6: 1},
        compiler_params=_cp(("arbitrary",)),
        name="moe_permute",
    )(nch, loff, goff, h2, rinfo, loffv, xs0)


def _ffn_kernel(te_ref, nt_ref, xs_ref, wg_ref, wu_ref, wd_ref, ys_ref):
    i = pl.program_id(0)

    @pl.when(i < nt_ref[0])
    def _():
        e = te_ref[i].astype(F32)
        xs = xs_ref[...]
        x = xs[:, :D]
        r = xs[:, D:].astype(F32)
        w_a = r[:, 0:1] + r[:, 1:2] + r[:, 2:3]
        w_b = r[:, 3:4] + r[:, 4:5] + r[:, 5:6]
        w = jnp.where(r[:, 6:7] == e, w_a, w_b)
        hid = _silu(_dot(x, wg_ref[0])) * _dot(x, wu_ref[0]) * w
        ys_ref[...] = _dot(hid.astype(BF16), wd_ref[0]).astype(BF16)

    @pl.when(i >= nt_ref[0])
    def _():
        ys_ref[...] = jnp.zeros_like(ys_ref)


def _moe_ffn(xs, tile_expert, n_tiles, w_gate, w_up, w_down):
    last = lambda i, nt: jnp.minimum(i, nt[0] - 1)
    grid_spec = pltpu.PrefetchScalarGridSpec(
        num_scalar_prefetch=2,
        grid=(NT,),
        in_specs=[pl.BlockSpec((TMS, XW), lambda i, te, nt: (last(i, nt), 0)),
                  pl.BlockSpec((1, D, DE), lambda i, te, nt: (te[last(i, nt)], 0, 0)),
                  pl.BlockSpec((1, D, DE), lambda i, te, nt: (te[last(i, nt)], 0, 0)),
                  pl.BlockSpec((1, DE, D), lambda i, te, nt: (te[last(i, nt)], 0, 0))],
        out_specs=pl.BlockSpec((TMS, D), lambda i, te, nt: (i, 0)),
    )
    return pl.pallas_call(
        _ffn_kernel,
        out_shape=jax.ShapeDtypeStruct((R_TOT, D), BF16),
        grid_spec=grid_spec,
        compiler_params=_cp(("arbitrary",)),
        name="moe_ffn",
    )(tile_expert, n_tiles, xs, w_gate.astype(BF16), w_up.astype(BF16), w_down.astype(BF16))


def _unperm_kernel(nch_ref, loff_ref, goff_ref, pos_ref, x1_ref, gate_ref, lg_ref, lb_ref, ys_ref, o_ref,
                   ys_scr, sem):
    blk = pl.program_id(0)
    ys_scr[...] = jnp.zeros_like(ys_scr)
    _segment_copies(nch_ref, loff_ref, goff_ref, blk,
                    lambda l, g: pltpu.make_async_copy(ys_ref.at[pl.ds(g, SEG)], ys_scr.at[pl.ds(l, SEG)], sem))
    pos = pos_ref[...]
    pos_a, pos_b = pos[:, 0:1], pos[:, 1:2]
    acc = jnp.zeros((TP, D), F32)
    for t in range(R_LOC // PT):
        lio = (lax.broadcasted_iota(jnp.int32, (TP, PT), 1) + t * PT).astype(F32)
        onehot = jnp.where((lio == pos_a) | (lio == pos_b), 1.0, 0.0).astype(BF16)
        acc = acc + _dot(onehot, ys_scr[t * PT:(t + 1) * PT, :])
    o_ref[...] = _layernorm(DN_ALPHA * x1_ref[...] + gate_ref[0] * acc, lg_ref[...], lb_ref[...])


def _moe_unpermute(ys, pos, x1, mods, ln_g, ln_b, tables):
    nch, loff, goff, _ = tables
    grid_spec = pltpu.PrefetchScalarGridSpec(
        num_scalar_prefetch=3,
        grid=(NBP,),
        in_specs=[pl.BlockSpec((TP, ROUTE_W), lambda i, *_: (i, 0)),
                  pl.BlockSpec((TP, D), lambda i, *_: (i, 0)),
                  _mod_spec(5, _mod_row_tp),
                  pl.BlockSpec((1, D), lambda i, *_: (0, 0)),
                  pl.BlockSpec((1, D), lambda i, *_: (0, 0)),
                  pl.BlockSpec(memory_space=pl.ANY)],
        out_specs=pl.BlockSpec((TP, D), lambda i, *_: (i, 0)),
        scratch_shapes=[pltpu.VMEM((R_LOC, D), BF16), pltpu.SemaphoreType.DMA],
    )
    return pl.pallas_call(
        _unperm_kernel,
        out_shape=jax.ShapeDtypeStruct((N_TOK, D), F32),
        grid_spec=grid_spec,
        compiler_params=_cp(("arbitrary",)),
        name="moe_unpermute",
    )(nch, loff, goff, pos, x1, mods, ln_g.reshape(1, D), ln_b.reshape(1, D), ys)


def _segment_tables(cnt):
    c = cnt[:, 0, :NE].astype(jnp.int32).reshape(NBP, TP // TB, NE).sum(axis=1)
    pc = (c + SEG - 1) // SEG * SEG
    loff = jnp.cumsum(pc, axis=1) - pc
    tot = pc.sum(axis=0)
    totp = (tot + TMS - 1) // TMS * TMS
    eend = jnp.cumsum(totp)
    goff = (eend - totp)[None, :] + jnp.cumsum(pc, axis=0) - pc
    n_tiles = (eend[-1] // TMS).reshape(1)
    tile_expert = jnp.minimum(
        jnp.searchsorted(eend // TMS, jnp.arange(NT, dtype=jnp.int32), side="right"), NE - 1).astype(jnp.int32)
    loffv = jnp.zeros((NBP, 1, ROUTE_W), F32).at[:, 0, :NE].set(loff.astype(F32))
    flat = lambda a: a.reshape(-1).astype(jnp.int32)
    return (flat(pc // SEG), flat(loff), flat(goff), loffv), tile_expert, n_tiles.astype(jnp.int32)


def _moe(h2, rinfo, cnt, x1, mods, ln_g, ln_b, w_gate, w_up, w_down):
    tables, tile_expert, n_tiles = _segment_tables(cnt)
    pos, xs = _moe_permute(h2, rinfo, tables)
    ys = _moe_ffn(xs, tile_expert, n_tiles, w_gate, w_up, w_down)
    return _moe_unpermute(ys, pos, x1, mods, ln_g, ln_b, tables)


def kernel(x_prompt, x_sample, state_gla_S, state_mlstm_C, state_mlstm_n, state_mlstm_m, c, c_ctx,
           gla_w_in, gla_w_gate, gla_b_gate, gla_norm_g, gla_w_out,
           mlstm_w_in, mlstm_b_gates, mlstm_norm_g, mlstm_w_out,
           adaln_w, adaln_b, ln_g, ln_b,
           moe_w_group, moe_b_group, moe_w_expert, moe_b_expert, moe_w_gate, moe_w_up, moe_w_down):
    cvecs = jnp.zeros((MOD_ROWS, D), F32).at[0].set(c_ctx).at[1:1 + N_LAT_SEQ].set(c)
    mods_all = _modulation(cvecs, adaln_w, adaln_b)
    x = _embed(x_prompt, x_sample, _grid_posemb(T_LAT))
    gla_states, ml_c, ml_n, ml_m = [], [], [], []
    for l in range(DEPTH):
        s = l // 2
        mods = mods_all[l].reshape(MOD_ROWS * 6, 1, D)
        if l % 2 == 0:
            q, k, v, r, g = _gla_in(x, mods, gla_w_in[s], gla_w_gate[s], gla_b_gate[s])
            o_ctx, s_new = _gla_scan(q, k, v, g, None, s)
            (o_lat,) = _gla_scan(q, k, v, g, state_gla_S, s)
            gla_states.append(s_new)
            norm_g, w_out = gla_norm_g[s], gla_w_out[s]
        else:
            q, k, v, r, gc, gr = _mlstm_in(x, mods, mlstm_w_in[s], mlstm_b_gates[s])
            o_ctx, c_new, n_new, m_new = _mlstm_scan(q, k, v, gc, gr, None, s)
            (o_lat,) = _mlstm_scan(q, k, v, gc, gr, (state_mlstm_C, state_mlstm_n, state_mlstm_m), s)
            ml_c.append(c_new)
            ml_n.append(n_new[:, :, :, 0, :])
            ml_m.append(m_new[:, :, :, 0, 0])
            norm_g, w_out = mlstm_norm_g[s], mlstm_w_out[s]
        w_route = jnp.zeros((D, ROUTE_W), F32).at[:, :NE].set(moe_w_expert[l]).at[:, NE:NE + N_GROUPS].set(
            moe_w_group[l])
        b_route = jnp.zeros((1, ROUTE_W), F32).at[0, :NE].set(moe_b_expert[l]).at[0, NE:NE + N_GROUPS].set(
            moe_b_group[l])
        x1, h2, rinfo, cnt = _out_proj(o_ctx, o_lat, r, x, mods, norm_g, w_out, ln_g[l, 0], ln_b[l, 0], w_route, b_route)
        x = _moe(h2, rinfo, cnt, x1, mods, ln_g[l, 1], ln_b[l, 1], moe_w_gate[l], moe_w_up[l], moe_w_down[l])
    y_ctx = x[:N_CTX].reshape(N_CTX_SEQ, T_CTX, D)
    y_lat = x[N_CTX:].reshape(N_LAT_SEQ, T_LAT, D)
    return (y_ctx, y_lat, jnp.stack(gla_states, 1), jnp.stack(ml_c, 1), jnp.stack(ml_n, 1), jnp.stack(ml_m, 1))
```

```python
import functools
import math

import jax
import jax.numpy as jnp
from jax import lax
from jax.experimental import pallas as pl
from jax.experimental.pallas import tpu as pltpu

F32 = jnp.float32
BF16 = jnp.bfloat16

D = 1024
N_CTX_SEQ, T_CTX = 16, 256
N_LAT_SEQ, T_LAT = 8, 1024
N_CTX = N_CTX_SEQ * T_CTX
N_LAT = N_LAT_SEQ * T_LAT
N_TOK = N_CTX + N_LAT
DEPTH = 4
GRID_W = 64
NH = 4
DK = 128
DV = 256
DKT = NH * DK
DVT = NH * DV
GATE_RANK = 16
GATE_TAU = 16.0
CH = 64
N_GROUPS = 4
EPG = 8
NE = N_GROUPS * EPG
DE = 256
DN_ALPHA = (2.0 * DEPTH) ** 0.25
LN_EPS = 1e-5
MOD_ROWS = 16
TB = 512
SB = 256
NCB = SB // CH
CH_SHIFT = 6
NB = N_TOK // TB
ROUTE_W = 128
NEG_INF = float("-inf")
VMEM_LIMIT = 48 * 1024 * 1024


def _cp(sem):
    return pltpu.CompilerParams(dimension_semantics=sem, vmem_limit_bytes=VMEM_LIMIT)


def _dot(a, b):
    return jnp.dot(a, b, preferred_element_type=F32)


def _dot_nt(a, b):
    return lax.dot_general(a, b, (((1,), (1,)), ((), ())), preferred_element_type=F32)


def _dot_f32(a, b):
    return jnp.dot(a, b, preferred_element_type=F32, precision=lax.Precision.HIGHEST)


def _split2(x):
    hi = x.astype(BF16)
    lo = (x - hi.astype(F32)).astype(BF16)
    return hi, lo


def _log_sigmoid(x):
    return -(jnp.maximum(-x, 0.0) + jnp.log1p(jnp.exp(-jnp.abs(x))))


def _silu(x):
    return x * jax.nn.sigmoid(x)


def _layernorm(x, g, b):
    mu = jnp.mean(x, axis=-1, keepdims=True)
    xc = x - mu
    var = jnp.mean(xc * xc, axis=-1, keepdims=True)
    return xc * lax.rsqrt(var + LN_EPS) * g + b


def _mod_row_tb(i):
    return jnp.where(i < N_CTX // TB, 0, 1 + (i - N_CTX // TB) // (T_LAT // TB))


def _mod_spec(j, row_fn):
    return pl.BlockSpec((1, 1, D), lambda i, *_: (row_fn(i) * 6 + j, 0, 0))


def _mod_kernel(c_ref, w_ref, b_ref, o_ref):
    o_ref[0] = _dot_f32(_silu(c_ref[...]), w_ref[0]) + b_ref[0]


def _modulation(cvecs, adaln_w, adaln_b):
    nj = 6
    return pl.pallas_call(
        _mod_kernel,
        out_shape=jax.ShapeDtypeStruct((DEPTH, MOD_ROWS, 6 * D), F32),
        grid=(DEPTH, nj),
        in_specs=[
            pl.BlockSpec((MOD_ROWS, D), lambda l, j: (0, 0)),
            pl.BlockSpec((1, D, D), lambda l, j: (l, 0, j)),
            pl.BlockSpec((1, 1, D), lambda l, j: (l, 0, j)),
        ],
        out_specs=pl.BlockSpec((1, MOD_ROWS, D), lambda l, j: (l, 0, j)),
        compiler_params=_cp(("parallel", "parallel")),
        name="adaln_modulation",
    )(cvecs, adaln_w, adaln_b.reshape(DEPTH, 1, 6 * D))


def _embed_kernel(xp_ref, xs_ref, pos_ref, o_ref):
    i = pl.program_id(0)

    @pl.when(i < N_CTX // TB)
    def _():
        o_ref[...] = xp_ref[...]

    @pl.when(i >= N_CTX // TB)
    def _():
        o_ref[...] = xs_ref[...] + pos_ref[...]


def _embed(x_prompt, x_sample, pos):
    nc = N_CTX // TB
    return pl.pallas_call(
        _embed_kernel,
        out_shape=jax.ShapeDtypeStruct((N_TOK, D), F32),
        grid=(NB,),
        in_specs=[
            pl.BlockSpec((TB, D), lambda i: (jnp.minimum(i, nc - 1), 0)),
            pl.BlockSpec((TB, D), lambda i: (jnp.maximum(i - nc, 0), 0)),
            pl.BlockSpec((TB, D), lambda i: (jnp.maximum(i - nc, 0) % (T_LAT // TB), 0)),
        ],
        out_specs=pl.BlockSpec((TB, D), lambda i: (i, 0)),
        compiler_params=_cp(("parallel",)),
        name="embed_tokens",
    )(x_prompt.reshape(N_CTX, D), x_sample.reshape(N_LAT, D), pos)


def _grid_posemb(T):
    rows = T // GRID_W
    r, cidx = jnp.meshgrid(jnp.arange(rows), jnp.arange(GRID_W), indexing="ij")
    nf = D // 4
    freqs = jnp.exp(-math.log(10000.0) * jnp.arange(nf, dtype=F32) / nf)

    def emb(p):
        a = p.reshape(-1).astype(F32)[:, None] * freqs
        return jnp.concatenate([jnp.sin(a), jnp.cos(a)], -1)

    return jnp.concatenate([emb(r), emb(cidx)], -1)


def _gla_in_kernel(x_ref, sh_ref, sc_ref, wq_ref, wk_ref, wv_ref, wr_ref, wg_ref, wgate_ref, bgate_ref,
                   q_ref, k_ref, v_ref, r_ref, g_ref):
    h = (x_ref[...] * (1.0 + sc_ref[0]) + sh_ref[0]).astype(BF16)
    q_ref[...] = _dot(h, wq_ref[...]) * (DK ** -0.5)
    k_ref[...] = _dot(h, wk_ref[...])
    v_ref[...] = _dot(h, wv_ref[...])
    r_ref[...] = _silu(_dot(h, wr_ref[...]))
    glr = _dot(h, wg_ref[...]).astype(BF16)
    z = _dot(glr, wgate_ref[...]) + bgate_ref[...]
    g_ref[...] = _log_sigmoid(z) * (1.0 / GATE_TAU)


def _gla_in(x, mods, w_in, w_gate, b_gate):
    wq = w_in[:, :DKT].astype(BF16)
    wk = w_in[:, DKT:2 * DKT].astype(BF16)
    wv = w_in[:, 2 * DKT:2 * DKT + DVT].astype(BF16)
    wr = w_in[:, 2 * DKT + DVT:2 * DKT + 2 * DVT].astype(BF16)
    wg = w_in[:, 2 * DKT + 2 * DVT:].astype(BF16)
    wgate = jnp.zeros((2 * GATE_RANK, 2 * DKT), F32)
    wgate = wgate.at[:GATE_RANK, :DKT].set(w_gate[0]).at[GATE_RANK:, DKT:].set(w_gate[1]).astype(BF16)
    bgate = b_gate.reshape(1, 2 * DKT)
    full = lambda s: pl.BlockSpec(s, lambda i: (0,) * len(s))
    tok = lambda n: pl.BlockSpec((TB, n), lambda i: (i, 0))
    return pl.pallas_call(
        _gla_in_kernel,
        out_shape=[jax.ShapeDtypeStruct((N_TOK, DKT), F32), jax.ShapeDtypeStruct((N_TOK, DKT), F32),
                   jax.ShapeDtypeStruct((N_TOK, DVT), F32), jax.ShapeDtypeStruct((N_TOK, DVT), F32),
                   jax.ShapeDtypeStruct((N_TOK, 2 * DKT), F32)],
        grid=(NB,),
        in_specs=[tok(D), _mod_spec(0, _mod_row_tb), _mod_spec(1, _mod_row_tb),
                  full((D, DKT)), full((D, DKT)), full((D, DVT)), full((D, DVT)), full((D, 2 * GATE_RANK)),
                  full((2 * GATE_RANK, 2 * DKT)), full((1, 2 * DKT))],
        out_specs=[tok(DKT), tok(DKT), tok(DVT), tok(DVT), tok(2 * DKT)],
        compiler_params=_cp(("parallel",)),
        name="gla_in_proj",
    )(x, mods, mods, wq, wk, wv, wr, wg, wgate, bgate)


def _mlstm_in_kernel(x_ref, sh_ref, sc_ref, wq_ref, wk_ref, wv_ref, wr_ref, wg_ref, wgt_ref, bg_ref, bgt_ref,
                     q_ref, k_ref, v_ref, r_ref, gc_ref, gr_ref):
    h = (x_ref[...] * (1.0 + sc_ref[0]) + sh_ref[0]).astype(BF16)
    q_ref[...] = _dot(h, wq_ref[...]) * (DK ** -0.5)
    k_ref[...] = _dot(h, wk_ref[...])
    v_ref[...] = _dot(h, wv_ref[...])
    r_ref[...] = jax.nn.sigmoid(_dot(h, wr_ref[...]))
    gc = _dot(h, wg_ref[...]) + bg_ref[...]
    gr = _dot_nt(wgt_ref[...], h) + bgt_ref[...]
    is_f_c = (lax.broadcasted_iota(jnp.int32, gc.shape, 1) % 2) == 1
    is_f_r = (lax.broadcasted_iota(jnp.int32, gr.shape, 0) % 2) == 1
    gc = jnp.where(is_f_c, _log_sigmoid(gc), gc)
    gr = jnp.where(is_f_r, _log_sigmoid(gr), gr)
    for hd in range(NH):
        gc_ref[hd] = gc[:, 4 * hd:4 * hd + 4]
        for c in range(TB // SB):
            gr_ref[hd, c] = gr[4 * hd:4 * hd + 4, c * SB:(c + 1) * SB]


def _mlstm_in(x, mods, w_in, b_gates):
    wq = w_in[:, :DKT].astype(BF16)
    wk = w_in[:, DKT:2 * DKT].astype(BF16)
    wv = w_in[:, 2 * DKT:2 * DKT + DVT].astype(BF16)
    wr = w_in[:, 2 * DKT + DVT:2 * DKT + 2 * DVT].astype(BF16)
    wg = w_in[:, 2 * DKT + 2 * DVT:].reshape(D, 4, NH).transpose(0, 2, 1).reshape(D, 4 * NH).astype(BF16)
    bg = b_gates.reshape(4, NH).T.reshape(1, 4 * NH)
    full = lambda s: pl.BlockSpec(s, lambda i: (0,) * len(s))
    tok = lambda n: pl.BlockSpec((TB, n), lambda i: (i, 0))
    return pl.pallas_call(
        _mlstm_in_kernel,
        out_shape=[jax.ShapeDtypeStruct((N_TOK, DKT), F32), jax.ShapeDtypeStruct((N_TOK, DKT), F32),
                   jax.ShapeDtypeStruct((N_TOK, DVT), F32), jax.ShapeDtypeStruct((N_TOK, DVT), F32),
                   jax.ShapeDtypeStruct((NH, N_TOK, 4), F32),
                   jax.ShapeDtypeStruct((NH, N_TOK // SB, 4, SB), F32)],
        grid=(NB,),
        in_specs=[tok(D), _mod_spec(0, _mod_row_tb), _mod_spec(1, _mod_row_tb),
                  full((D, DKT)), full((D, DKT)), full((D, DVT)), full((D, DVT)), full((D, 4 * NH)),
                  full((4 * NH, D)), full((1, 4 * NH)), full((4 * NH, 1))],
        out_specs=[tok(DKT), tok(DKT), tok(DVT), tok(DVT),
                   pl.BlockSpec((NH, TB, 4), lambda i: (0, i, 0)),
                   pl.BlockSpec((NH, TB // SB, 4, SB), lambda i: (0, i, 0, 0))],
        compiler_params=_cp(("parallel",)),
        name="mlstm_in_proj",
    )(x, mods, mods, wq, wk, wv, wr, wg, wg.T, bg, bg.T)


def _block_tri_masks():
    row = lax.broadcasted_iota(jnp.int32, (SB, SB), 0)
    col = lax.broadcasted_iota(jnp.int32, (SB, SB), 1)
    same = (row >> CH_SHIFT) == (col >> CH_SHIFT)
    return same & (row >= col), same & (row <= col)


def _chunk_scan(x, forward, axis=0, op=jnp.add, fill=0.0):
    n = x.shape[axis]
    pos = lax.broadcasted_iota(jnp.int32, x.shape, axis) & (CH - 1)
    s = 1
    while s < CH:
        if forward:
            x = op(x, jnp.where(pos >= s, pltpu.roll(x, s, axis), fill))
        else:
            x = op(x, jnp.where(pos < CH - s, pltpu.roll(x, n - s, axis), fill))
        s *= 2
    return x


def _chunk_cumsum(x, forward):
    return _chunk_scan(x, forward)


def _cat2(x):
    return jnp.concatenate([x, x], axis=1)


def _per_chunk(vals, n):
    return jnp.concatenate([jnp.broadcast_to(v, (CH, n)) for v in vals], axis=0)


def _chunk_masked_cat(x):
    rowc = lax.broadcasted_iota(jnp.int32, x.shape, 0) >> CH_SHIFT
    return jnp.concatenate([jnp.where(rowc == c, x, 0.0).astype(BF16) for c in range(NCB)], axis=1)


def _head_norm_rows(o):
    mu = jnp.mean(o, axis=-1, keepdims=True)
    oc = o - mu
    var = jnp.mean(oc * oc, axis=-1, keepdims=True)
    return oc * lax.rsqrt(var + LN_EPS)


def _gla_superblock(q, k, v, v_t, g, st, forward, mask):
    b = _chunk_cumsum(g, forward)
    r_last = CH - 1 if forward else 0
    r_mid = CH // 2 if forward else CH - 1 - CH // 2
    lasts = [b[c * CH + r_last:c * CH + r_last + 1, :] for c in range(NCB)]
    b_last = _per_chunk(lasts, DK)
    b_mid = _per_chunk([b[c * CH + r_mid:c * CH + r_mid + 1, :] for c in range(NCB)], DK)
    qe = (q * jnp.exp(b - b_mid)).astype(BF16)
    ke = (k * jnp.exp(b_mid - b)).astype(BF16)
    qb = (q * jnp.exp(b)).astype(BF16)
    kd = k * jnp.exp(b_last - b)
    a = jnp.where(mask, _dot_nt(qe, ke), 0.0).astype(BF16)
    o_intra = _dot(a, v)
    d_t = _dot(v_t, _chunk_masked_cat(kd))
    inter = [None] * NCB
    for c in (range(NCB) if forward else reversed(range(NCB))):
        inter[c] = _dot_nt(qb[c * CH:(c + 1) * CH, :], st.astype(BF16))
        st = st * jnp.exp(lasts[c]) + d_t[:, c * DK:(c + 1) * DK]
    return o_intra + jnp.concatenate(inter, axis=0), st


def _scan_driver(T, o_ref, ob_ref, step):
    nsb = T // SB
    if nsb == 1:
        step(0, 0)
    else:
        def body(i, carry):
            step(i, nsb - 1 - i)
            return carry

        lax.fori_loop(0, nsb, body, 0)

    def norm_body(j, carry):
        rows = pl.ds(pl.multiple_of(j * SB, SB), SB)
        o_ref[rows, :] = _head_norm_rows(o_ref[rows, :] + ob_ref[rows, :])
        return carry

    if nsb == 1:
        norm_body(0, 0)
    else:
        lax.fori_loop(0, nsb, norm_body, 0)


def _gla_scan_kernel(*refs, T, has_state, emit_state):
    q_ref, k_ref, v_ref, gf_ref, gb_ref = refs[:5]
    pos = 5
    if has_state:
        s0f_ref, s0b_ref = refs[pos:pos + 2]
        pos += 2
    o_ref = refs[pos]
    pos += 1
    if emit_state:
        sout_ref = refs[pos]
        pos += 1
    stf_ref, stb_ref, vt_ref, ob_ref = refs[pos:pos + 4]

    nsb = T // SB
    for j in range(nsb):
        vt_ref[j] = v_ref[j * SB:(j + 1) * SB, :].T.astype(BF16)
    if has_state:
        stf_ref[...] = s0f_ref[...].T
        stb_ref[...] = s0b_ref[...].T
    else:
        stf_ref[...] = jnp.zeros_like(stf_ref)
        stb_ref[...] = jnp.zeros_like(stb_ref)

    lower, upper = _block_tri_masks()

    def one(j, forward):
        rows = pl.ds(pl.multiple_of(j * SB, SB), SB)
        st_ref = stf_ref if forward else stb_ref
        o, st = _gla_superblock(q_ref[rows, :], k_ref[rows, :], v_ref[rows, :].astype(BF16), vt_ref[j],
                                (gf_ref if forward else gb_ref)[rows, :], st_ref[...], forward,
                                lower if forward else upper)
        st_ref[...] = st
        (o_ref if forward else ob_ref)[rows, :] = o

    def step(jf, jb):
        one(jf, True)
        one(jb, False)

    _scan_driver(T, o_ref, ob_ref, step)
    if emit_state:
        sout_ref[0] = stf_ref[...].T
        sout_ref[1] = stb_ref[...].T


def _gla_scan(q, k, v, g, state, slot):
    ctx = state is None
    T = T_CTX if ctx else T_LAT
    n_seq = N_CTX_SEQ if ctx else N_LAT_SEQ
    off = 0 if ctx else N_CTX // T_LAT
    in_specs = [
        pl.BlockSpec((T, DK), lambda b, h: (off + b, h)),
        pl.BlockSpec((T, DK), lambda b, h: (off + b, h)),
        pl.BlockSpec((T, DV), lambda b, h: (off + b, h)),
        pl.BlockSpec((T, DK), lambda b, h: (off + b, h)),
        pl.BlockSpec((T, DK), lambda b, h: (off + b, NH + h)),
    ]
    args = [q, k, v, g, g]
    out_shape = [jax.ShapeDtypeStruct((n_seq * T, DVT), F32)]
    out_specs = [pl.BlockSpec((T, DV), lambda b, h: (b, h))]
    if ctx:
        out_shape.append(jax.ShapeDtypeStruct((N_CTX_SEQ, 2, NH, DK, DV), F32))
        out_specs.append(pl.BlockSpec((None, 2, None, DK, DV), lambda b, h: (b, 0, h, 0, 0)))
    else:
        sq = (None, None, None, None, DK, DV)
        in_specs += [pl.BlockSpec(sq, lambda b, h: (b, slot, 0, h, 0, 0)),
                     pl.BlockSpec(sq, lambda b, h: (b, slot, 1, h, 0, 0))]
        args += [state, state]
    res = pl.pallas_call(
        functools.partial(_gla_scan_kernel, T=T, has_state=not ctx, emit_state=ctx),
        out_shape=out_shape,
        grid=(n_seq, NH),
        in_specs=in_specs,
        out_specs=out_specs,
        scratch_shapes=[pltpu.VMEM((DV, DK), F32), pltpu.VMEM((DV, DK), F32),
                        pltpu.VMEM((T // SB, DV, SB), BF16), pltpu.VMEM((T, DV), F32)],
        compiler_params=_cp(("parallel", "parallel")),
        name="gla_scan_ctx" if ctx else "gla_scan_lat",
    )(*args)
    return res


def _mlstm_scan_kernel(*refs, T, has_state, emit_state):
    q_ref, k_ref, v_ref, gc_ref, gr_ref = refs[:5]
    pos = 5
    if has_state:
        c0f_ref, c0b_ref, n0f_ref, n0b_ref, m0f_ref, m0b_ref = refs[pos:pos + 6]
        pos += 6
    o_ref = refs[pos]
    pos += 1
    if emit_state:
        cout_ref, nout_ref, mout_ref = refs[pos:pos + 3]
        pos += 3
    ct_refs = refs[pos:pos + 2]
    n_refs = refs[pos + 2:pos + 4]
    m_refs = refs[pos + 4:pos + 6]
    vt_ref, ob_ref = refs[pos + 6:pos + 8]

    nsb = T // SB
    for j in range(nsb):
        vt_ref[j] = v_ref[j * SB:(j + 1) * SB, :].T.astype(BF16)
    if has_state:
        for d, (c0, n0, m0) in enumerate(((c0f_ref, n0f_ref, m0f_ref), (c0b_ref, n0b_ref, m0b_ref))):
            ct_refs[d][...] = c0[...].T
            n_refs[d][...] = n0[...]
            m_refs[d][...] = m0[...]
    else:
        for r in tuple(ct_refs) + tuple(n_refs) + tuple(m_refs):
            r[...] = jnp.zeros_like(r)

    lower, upper = _block_tri_masks()

    def one(j, forward):
        d = 0 if forward else 1
        rows = pl.ds(pl.multiple_of(j * SB, SB), SB)
        q = q_ref[rows, :]
        k = k_ref[rows, :]
        v = v_ref[rows, :].astype(BF16)
        gc = gc_ref[rows, :]
        gr = gr_ref[j]
        o0 = 2 * d
        mask = lower if forward else upper
        order = list(range(NCB)) if forward else list(reversed(range(NCB)))
        r_last = CH - 1 if forward else 0
        i_c = jnp.broadcast_to(gc[:, o0:o0 + 1], (SB, DK))
        f_c = jnp.broadcast_to(gc[:, o0 + 1:o0 + 2], (SB, DK))
        b_c = _chunk_scan(f_c, forward)
        m_intra = b_c + _chunk_scan(i_c - b_c, forward, op=jnp.maximum, fill=NEG_INF)
        b_last = [b_c[c * CH + r_last:c * CH + r_last + 1, :] for c in range(NCB)]
        a_c = _per_chunk(b_last, DK) - b_c + i_c
        m_loc = [jnp.max(a_c[c * CH:(c + 1) * CH, :], axis=0, keepdims=True) for c in range(NCB)]
        kw = k * jnp.exp(a_c - _per_chunk(m_loc, DK))
        kv_t = _dot(vt_ref[j], _chunk_masked_cat(kw))
        k_loc = [jnp.sum(kw[c * CH:(c + 1) * CH, :], axis=0, keepdims=True) for c in range(NCB)]

        m = m_refs[d][...]
        m_start, s_old, s_loc = [None] * NCB, [None] * NCB, [None] * NCB
        for c in order:
            m_start[c] = m
            m_new = jnp.maximum(b_last[c] + m, m_loc[c])
            s_old[c] = jnp.exp(b_last[c] + m - m_new)
            s_loc[c] = jnp.exp(m_loc[c] - m_new)
            m = m_new
        m_refs[d][...] = m

        l_inter = b_c + _per_chunk(m_start, DK)
        m_i = jnp.maximum(l_inter, m_intra)
        gcum = _chunk_scan(gr, forward, axis=1)
        brow = gcum[o0 + 1:o0 + 2, :] - gr[o0:o0 + 1, :]
        d_log = jnp.where(mask, _cat2(b_c - m_i) - brow, NEG_INF)
        qb = q.astype(BF16)
        s = (_dot_nt(qb, k.astype(BF16)) * jnp.exp(d_log)).astype(BF16)
        e_inter = jnp.exp(l_inter - m_i)
        nd = _dot(s, jnp.concatenate([v, jnp.ones((SB, DK), BF16)], axis=1))
        num, den = nd[:, :DV], nd[:, DV:]

        ct = ct_refs[d][...]
        nrm = n_refs[d][...]
        inter = [None] * NCB
        for c in order:
            state = jnp.concatenate([ct, jnp.broadcast_to(nrm, (DK, DK))], axis=0).astype(BF16)
            inter[c] = _dot_nt(qb[c * CH:(c + 1) * CH, :], state)
            ct = s_old[c] * ct + s_loc[c] * kv_t[:, c * DK:(c + 1) * DK]
            nrm = s_old[c] * nrm + s_loc[c] * k_loc[c]
        ct_refs[d][...] = ct
        n_refs[d][...] = nrm

        inter = jnp.concatenate(inter, axis=0)
        num = num + _cat2(e_inter) * inter[:, :DV]
        den = den + e_inter * inter[:, DV:]
        inv = 1.0 / jnp.maximum(jnp.abs(den), jnp.exp(-m_i))
        (o_ref if forward else ob_ref)[rows, :] = num * _cat2(inv)

    def step(jf, jb):
        one(jf, True)
        one(jb, False)

    _scan_driver(T, o_ref, ob_ref, step)
    if emit_state:
        for d in range(2):
            cout_ref[d] = ct_refs[d][...].T
            nout_ref[d] = n_refs[d][...]
            mout_ref[d] = m_refs[d][...]


def _mlstm_scan(q, k, v, gc, gr, states, slot):
    ctx = states is None
    T = T_CTX if ctx else T_LAT
    n_seq = N_CTX_SEQ if ctx else N_LAT_SEQ
    off = 0 if ctx else N_CTX // T_LAT
    in_specs = [
        pl.BlockSpec((T, DK), lambda b, h: (off + b, h)),
        pl.BlockSpec((T, DK), lambda b, h: (off + b, h)),
        pl.BlockSpec((T, DV), lambda b, h: (off + b, h)),
        pl.BlockSpec((None, T, 4), lambda b, h: (h, off + b, 0)),
        pl.BlockSpec((None, T // SB, 4, SB), lambda b, h: (h, off + b, 0, 0)),
    ]
    args = [q, k, v, gc, gr]
    out_shape = [jax.ShapeDtypeStruct((n_seq * T, DVT), F32)]
    out_specs = [pl.BlockSpec((T, DV), lambda b, h: (b, h))]
    if ctx:
        out_shape += [jax.ShapeDtypeStruct((N_CTX_SEQ, 2, NH, DK, DV), F32),
                      jax.ShapeDtypeStruct((N_CTX_SEQ, 2, NH, 1, DK), F32),
                      jax.ShapeDtypeStruct((N_CTX_SEQ, 2, NH, 1, DK), F32)]
        out_specs += [pl.BlockSpec((None, 2, None, DK, DV), lambda b, h: (b, 0, h, 0, 0)),
                      pl.BlockSpec((None, 2, None, 1, DK), lambda b, h: (b, 0, h, 0, 0)),
                      pl.BlockSpec((None, 2, None, 1, DK), lambda b, h: (b, 0, h, 0, 0))]
    else:
        c0, n0, m0 = states
        n0 = n0.reshape(N_LAT_SEQ, -1, 2, NH, 1, DK)
        m0 = jnp.broadcast_to(m0[..., None, None], m0.shape + (1, DK))
        sq_c = (None, None, None, None, DK, DV)
        sq_v = (None, None, None, None, 1, DK)
        for arr, sq in ((c0, sq_c), (n0, sq_v), (m0, sq_v)):
            for d in range(2):
                in_specs.append(pl.BlockSpec(sq, functools.partial(lambda b, h, d: (b, slot, d, h, 0, 0), d=d)))
                args.append(arr)
    return pl.pallas_call(
        functools.partial(_mlstm_scan_kernel, T=T, has_state=not ctx, emit_state=ctx),
        out_shape=out_shape,
        grid=(n_seq, NH),
        in_specs=in_specs,
        out_specs=out_specs,
        scratch_shapes=[pltpu.VMEM((DV, DK), F32), pltpu.VMEM((DV, DK), F32),
                        pltpu.VMEM((1, DK), F32), pltpu.VMEM((1, DK), F32),
                        pltpu.VMEM((1, DK), F32), pltpu.VMEM((1, DK), F32),
                        pltpu.VMEM((T // SB, DV, SB), BF16), pltpu.VMEM((T, DV), F32)],
        compiler_params=_cp(("parallel", "parallel")),
        name="mlstm_scan_ctx" if ctx else "mlstm_scan_lat",
    )(*args)


def _route(lg):
    lane = lax.broadcasted_iota(jnp.int32, lg.shape, 1)
    big = jnp.int32(ROUTE_W)
    is_g = (lane >= NE) & (lane < NE + N_GROUPS)
    gl = jnp.where(is_g, lg, NEG_INF)
    gmax = jnp.max(gl, axis=1, keepdims=True)
    gsel = jnp.min(jnp.where(gl == gmax, lane, big), axis=1, keepdims=True) - NE
    gw = 1.0 / jnp.sum(jnp.exp(gl - gmax), axis=1, keepdims=True)
    ing = (lane < NE) & ((lane >> 3) == gsel)
    el = jnp.where(ing, lg, NEG_INF)
    emax = jnp.max(el, axis=1, keepdims=True)
    p = jnp.exp(el - emax)
    prob = p / jnp.sum(p, axis=1, keepdims=True)
    p1 = jnp.max(prob, axis=1, keepdims=True)
    i1 = jnp.min(jnp.where(ing & (prob == p1), lane, big), axis=1, keepdims=True)
    rest = ing & (lane != i1)
    prob2 = jnp.where(rest, prob, -1.0)
    p2 = jnp.max(prob2, axis=1, keepdims=True)
    i2 = jnp.min(jnp.where(rest & (prob2 == p2), lane, big), axis=1, keepdims=True)
    tot = p1 + p2
    rinfo = jnp.where(lane == 0, i1.astype(F32),
                      jnp.where(lane == 1, i2.astype(F32),
                                jnp.where(lane == 2, gw * (p1 / tot), jnp.where(lane == 3, gw * (p2 / tot), 0.0))))
    count = jnp.sum(jnp.where((lane == i1) | (lane == i2), 1.0, 0.0), axis=0, keepdims=True)
    return rinfo, count


def _out_kernel(oc_ref, ol_ref, r_ref, x_ref, gate_ref, sh_ref, sc_ref, ng_ref, wo_ref, lg_ref, lb_ref, wr_ref,
                br_ref, x1_ref, h2_ref, rinfo_ref, cnt_ref):
    o = jnp.where(pl.program_id(0) < N_CTX // TB, oc_ref[...], ol_ref[...])
    y = _dot((o * ng_ref[...] * r_ref[...]).astype(BF16), wo_ref[...])
    x1 = _layernorm(DN_ALPHA * x_ref[...] + gate_ref[0] * y, lg_ref[...], lb_ref[...])
    x1_ref[...] = x1
    h2 = x1 * (1.0 + sc_ref[0]) + sh_ref[0]
    h_hi, h_lo = _split2(h2)
    h2_ref[...] = h_hi
    t = _dot(h_hi, wr_ref[...])
    lg = t[:, :ROUTE_W] + t[:, ROUTE_W:] + _dot(h_lo, wr_ref[:, :ROUTE_W]) + br_ref[...]
    rinfo_ref[...], cnt_ref[0] = _route(lg)


def _out_proj(o_ctx, o_lat, r, x, mods, norm_g, w_out, ln_g, ln_b, w_route, b_route):
    nc = N_CTX // TB
    full = lambda s: pl.BlockSpec(s, lambda i: (0,) * len(s))
    tok = lambda n: pl.BlockSpec((TB, n), lambda i: (i, 0))
    return pl.pallas_call(
        _out_kernel,
        out_shape=[jax.ShapeDtypeStruct((N_TOK, D), F32), jax.ShapeDtypeStruct((N_TOK, D), BF16),
                   jax.ShapeDtypeStruct((N_TOK, ROUTE_W), F32), jax.ShapeDtypeStruct((NB, 1, ROUTE_W), F32)],
        grid=(NB,),
        in_specs=[pl.BlockSpec((TB, DVT), lambda i: (jnp.minimum(i, nc - 1), 0)),
                  pl.BlockSpec((TB, DVT), lambda i: (jnp.maximum(i - nc, 0), 0)),
                  tok(DVT), tok(D),
                  _mod_spec(2, _mod_row_tb), _mod_spec(3, _mod_row_tb), _mod_spec(4, _mod_row_tb),
                  full((1, DVT)), full((DVT, D)), full((1, D)), full((1, D)),
                  full((D, 2 * ROUTE_W)), full((1, ROUTE_W))],
        out_specs=[tok(D), tok(D), tok(ROUTE_W), pl.BlockSpec((1, 1, ROUTE_W), lambda i: (i, 0, 0))],
        compiler_params=_cp(("parallel",)),
        name="out_proj_route",
    )(o_ctx, o_lat, r, x, mods, mods, mods, norm_g.reshape(1, DVT), w_out.astype(BF16),
      ln_g.reshape(1, D), ln_b.reshape(1, D), jnp.concatenate(_split2(w_route), axis=1), b_route)


TP = 512
NBP = N_TOK // TP
SEG = 16
R_LOC = 1536
PT = 256
XW = D + ROUTE_W
TMS = 256
R_TOT = -(-(2 * N_TOK + NBP * NE * (SEG - 1) + NE * (TMS - 1)) // TMS) * TMS
NT = R_TOT // TMS
POS_SPLIT = 64.0


def _mod_row_tp(i):
    return jnp.where(i < N_CTX // TP, 0, 1 + (i - N_CTX // TP) // (T_LAT // TP))


def _lane_pack(cols, shape):
    lane = lax.broadcasted_iota(jnp.int32, shape, 1)
    out = jnp.zeros(shape, F32)
    for j, c in enumerate(cols):
        out = jnp.where(lane == j, c, out)
    return out


COPY_CHUNKS = 4
WAIT_CHUNKS = (32, 4, 1)


def _segment_copies(nch_ref, loff_ref, goff_ref, blk, make_copy):
    def per_expert(e, issued):
        idx = blk * NE + e
        n = nch_ref[idx]
        l0 = loff_ref[idx]
        g0 = goff_ref[idx]
        n_big = n // COPY_CHUNKS

        def per_piece(k, carry, first, chunks):
            off = (first + k * chunks) * SEG
            make_copy(pl.multiple_of(l0 + off, SEG), pl.multiple_of(g0 + off, SEG), chunks * SEG).start()
            return carry

        lax.fori_loop(0, n_big, functools.partial(per_piece, first=0, chunks=COPY_CHUNKS), 0)
        lax.fori_loop(0, n - n_big * COPY_CHUNKS, functools.partial(per_piece, first=n_big * COPY_CHUNKS, chunks=1), 0)
        return issued + n

    left = lax.fori_loop(0, NE, per_expert, 0)
    for chunks in WAIT_CHUNKS:
        n_wait = left // chunks

        def wait_piece(k, carry, chunks=chunks):
            make_copy(0, 0, chunks * SEG).wait()
            return carry

        lax.fori_loop(0, n_wait, wait_piece, 0)
        left = left - n_wait * chunks


def _perm_kernel(nch_ref, loff_ref, goff_ref, h_ref, ri_ref, loffv_ref, xs_init_ref, pos_ref, xs_ref, xs_scr, sem):
    del xs_init_ref
    blk = pl.program_id(0)
    ri = ri_ref[...]
    e_a, e_b, w_a, w_b = ri[:, 0:1], ri[:, 1:2], ri[:, 2:3], ri[:, 3:4]
    lanef = lax.broadcasted_iota(jnp.int32, ri.shape, 1).astype(F32)
    is_a = lanef == e_a
    is_b = lanef == e_b
    sel = jnp.where(is_a | is_b, 1.0, 0.0).astype(BF16)
    row = lax.broadcasted_iota(jnp.int32, (TP, TP), 0)
    col = lax.broadcasted_iota(jnp.int32, (TP, TP), 1)
    earlier = jnp.where(row > col, 1.0, 0.0).astype(BF16)
    lpos = loffv_ref[0] + _dot(earlier, sel)
    pos_a = jnp.sum(jnp.where(is_a, lpos, 0.0), axis=1, keepdims=True)
    pos_b = jnp.sum(jnp.where(is_b, lpos, 0.0), axis=1, keepdims=True)
    pos_ref[...] = _lane_pack([pos_a, pos_b], ri.shape)

    hi_a = jnp.floor(pos_a * (1.0 / POS_SPLIT))
    hi_b = jnp.floor(pos_b * (1.0 / POS_SPLIT))
    parts = _lane_pack([hi_a, pos_a - POS_SPLIT * hi_a, hi_b, pos_b - POS_SPLIT * hi_b], ri.shape).astype(BF16)
    pick = jnp.where(lax.broadcasted_iota(jnp.int32, (8, ROUTE_W), 0) == lax.broadcasted_iota(jnp.int32, (8, ROUTE_W), 1),
                     1.0, 0.0).astype(BF16)
    pr = _dot_nt(pick, parts)
    pos_a_r = POS_SPLIT * pr[0:1, :] + pr[1:2, :]
    pos_b_r = POS_SPLIT * pr[2:3, :] + pr[3:4, :]

    wa1, wa2 = _split2(w_a)
    wa3 = (w_a - wa1.astype(F32) - wa2.astype(F32))
    wb1, wb2 = _split2(w_b)
    wb3 = (w_b - wb1.astype(F32) - wb2.astype(F32))
    wl = _lane_pack([wa1.astype(F32), wa2.astype(F32), wa3, wb1.astype(F32), wb2.astype(F32), wb3, e_a],
                    ri.shape).astype(BF16)
    hcat = jnp.concatenate([h_ref[...], wl], axis=1)
    for t in range(R_LOC // PT):
        rio = (lax.broadcasted_iota(jnp.int32, (PT, TP), 0) + t * PT).astype(F32)
        onehot = jnp.where((rio == pos_a_r) | (rio == pos_b_r), 1.0, 0.0).astype(BF16)
        xs_scr[t * PT:(t + 1) * PT, :] = _dot(onehot, hcat).astype(BF16)

    _segment_copies(nch_ref, loff_ref, goff_ref, blk,
                    lambda l, g, n: pltpu.make_async_copy(xs_scr.at[pl.ds(l, n)], xs_ref.at[pl.ds(g, n)], sem))


def _moe_permute(h2, rinfo, tables):
    nch, loff, goff, loffv = tables
    xs0 = jnp.zeros((R_TOT, XW), BF16)
    grid_spec = pltpu.PrefetchScalarGridSpec(
        num_scalar_prefetch=3,
        grid=(NBP,),
        in_specs=[pl.BlockSpec((TP, D), lambda i, *_: (i, 0)),
                  pl.BlockSpec((TP, ROUTE_W), lambda i, *_: (i, 0)),
                  pl.BlockSpec((1, 1, ROUTE_W), lambda i, *_: (i, 0, 0)),
                  pl.BlockSpec(memory_space=pl.ANY)],
        out_specs=[pl.BlockSpec((TP, ROUTE_W), lambda i, *_: (i, 0)),
                   pl.BlockSpec(memory_space=pl.ANY)],
        scratch_shapes=[pltpu.VMEM((R_LOC, XW), BF16), pltpu.SemaphoreType.DMA],
    )
    return pl.pallas_call(
        _perm_kernel,
        out_shape=[jax.ShapeDtypeStruct((N_TOK, ROUTE_W), F32), jax.ShapeDtypeStruct((R_TOT, XW), BF16)],
        grid_spec=grid_spec,
        input_output_aliases={6: 1},
        compiler_params=_cp(("arbitrary",)),
        name="moe_permute",
    )(nch, loff, goff, h2, rinfo, loffv, xs0)


def _ffn_kernel(te_ref, nt_ref, xs_ref, wg32_ref, wu32_ref, wd32_ref, ys_ref, wg_ref, wu_ref, wd_ref):
    i = pl.program_id(0)
    active = i < nt_ref[0]
    new_expert = (i == 0) | (te_ref[i] != te_ref[jnp.maximum(i - 1, 0)])

    @pl.when(active & new_expert)
    def _():
        wg_ref[0] = wg32_ref[0].astype(BF16)
        wu_ref[0] = wu32_ref[0].astype(BF16)
        wd_ref[0] = wd32_ref[0].astype(BF16)

    @pl.when(active)
    def _():
        e = te_ref[i].astype(F32)
        xs = xs_ref[...]
        x = xs[:, :D]
        r = xs[:, D:].astype(F32)
        w_a = r[:, 0:1] + r[:, 1:2] + r[:, 2:3]
        w_b = r[:, 3:4] + r[:, 4:5] + r[:, 5:6]
        w = jnp.where(r[:, 6:7] == e, w_a, w_b)
        hid = _silu(_dot(x, wg_ref[0])) * _dot(x, wu_ref[0]) * w
        ys_ref[...] = _dot(hid.astype(BF16), wd_ref[0]).astype(BF16)

    @pl.when(i >= nt_ref[0])
    def _():
        ys_ref[...] = jnp.zeros_like(ys_ref)


def _moe_ffn(xs, tile_expert, n_tiles, w_gate, w_up, w_down):
    last = lambda i, nt: jnp.minimum(i, nt[0] - 1)
    grid_spec = pltpu.PrefetchScalarGridSpec(
        num_scalar_prefetch=2,
        grid=(NT,),
        in_specs=[pl.BlockSpec((TMS, XW), lambda i, te, nt: (last(i, nt), 0)),
                  pl.BlockSpec((1, D, DE), lambda i, te, nt: (te[last(i, nt)], 0, 0)),
                  pl.BlockSpec((1, D, DE), lambda i, te, nt: (te[last(i, nt)], 0, 0)),
                  pl.BlockSpec((1, DE, D), lambda i, te, nt: (te[last(i, nt)], 0, 0))],
        out_specs=pl.BlockSpec((TMS, D), lambda i, te, nt: (i, 0)),
        scratch_shapes=[pltpu.VMEM((1, D, DE), BF16), pltpu.VMEM((1, D, DE), BF16), pltpu.VMEM((1, DE, D), BF16)],
    )
    return pl.pallas_call(
        _ffn_kernel,
        out_shape=jax.ShapeDtypeStruct((R_TOT, D), BF16),
        grid_spec=grid_spec,
        compiler_params=_cp(("arbitrary",)),
        name="moe_ffn",
    )(tile_expert, n_tiles, xs, w_gate, w_up, w_down)


def _unperm_kernel(nch_ref, loff_ref, goff_ref, pos_ref, x1_ref, gate_ref, lg_ref, lb_ref, ys_ref, o_ref,
                   ys_scr, sem):
    blk = pl.program_id(0)
    ys_scr[...] = jnp.zeros_like(ys_scr)
    _segment_copies(nch_ref, loff_ref, goff_ref, blk,
                    lambda l, g, n: pltpu.make_async_copy(ys_ref.at[pl.ds(g, n)], ys_scr.at[pl.ds(l, n)], sem))
    pos = pos_ref[...]
    pos_a, pos_b = pos[:, 0:1], pos[:, 1:2]
    acc = jnp.zeros((TP, D), F32)
    for t in range(R_LOC // PT):
        lio = (lax.broadcasted_iota(jnp.int32, (TP, PT), 1) + t * PT).astype(F32)
        onehot = jnp.where((lio == pos_a) | (lio == pos_b), 1.0, 0.0).astype(BF16)
        acc = acc + _dot(onehot, ys_scr[t * PT:(t + 1) * PT, :])
    o_ref[...] = _layernorm(DN_ALPHA * x1_ref[...] + gate_ref[0] * acc, lg_ref[...], lb_ref[...])


def _moe_unpermute(ys, pos, x1, mods, ln_g, ln_b, tables):
    nch, loff, goff, _ = tables
    grid_spec = pltpu.PrefetchScalarGridSpec(
        num_scalar_prefetch=3,
        grid=(NBP,),
        in_specs=[pl.BlockSpec((TP, ROUTE_W), lambda i, *_: (i, 0)),
                  pl.BlockSpec((TP, D), lambda i, *_: (i, 0)),
                  _mod_spec(5, _mod_row_tp),
                  pl.BlockSpec((1, D), lambda i, *_: (0, 0)),
                  pl.BlockSpec((1, D), lambda i, *_: (0, 0)),
                  pl.BlockSpec(memory_space=pl.ANY)],
        out_specs=pl.BlockSpec((TP, D), lambda i, *_: (i, 0)),
        scratch_shapes=[pltpu.VMEM((R_LOC, D), BF16), pltpu.SemaphoreType.DMA],
    )
    return pl.pallas_call(
        _unperm_kernel,
        out_shape=jax.ShapeDtypeStruct((N_TOK, D), F32),
        grid_spec=grid_spec,
        compiler_params=_cp(("arbitrary",)),
        name="moe_unpermute",
    )(nch, loff, goff, pos, x1, mods, ln_g.reshape(1, D), ln_b.reshape(1, D), ys)


def _segment_tables(cnt):
    c = cnt[:, 0, :NE].astype(jnp.int32).reshape(NBP, TP // TB, NE).sum(axis=1)
    pc = (c + SEG - 1) // SEG * SEG
    loff = jnp.cumsum(pc, axis=1) - pc
    tot = pc.sum(axis=0)
    totp = (tot + TMS - 1) // TMS * TMS
    eend = jnp.cumsum(totp)
    goff = (eend - totp)[None, :] + jnp.cumsum(pc, axis=0) - pc
    n_tiles = (eend[-1] // TMS).reshape(1)
    tile_id = jnp.arange(NT, dtype=jnp.int32)
    tile_expert = jnp.minimum(jnp.sum(tile_id[:, None] >= (eend // TMS)[None, :], axis=1), NE - 1).astype(jnp.int32)
    loffv = jnp.zeros((NBP, 1, ROUTE_W), F32).at[:, 0, :NE].set(loff.astype(F32))
    flat = lambda a: a.reshape(-1).astype(jnp.int32)
    return (flat(pc // SEG), flat(loff), flat(goff), loffv), tile_expert, n_tiles.astype(jnp.int32)


def _moe(h2, rinfo, cnt, x1, mods, ln_g, ln_b, w_gate, w_up, w_down):
    tables, tile_expert, n_tiles = _segment_tables(cnt)
    pos, xs = _moe_permute(h2, rinfo, tables)
    ys = _moe_ffn(xs, tile_expert, n_tiles, w_gate, w_up, w_down)
    return _moe_unpermute(ys, pos, x1, mods, ln_g, ln_b, tables)


def kernel(x_prompt, x_sample, state_gla_S, state_mlstm_C, state_mlstm_n, state_mlstm_m, c, c_ctx,
           gla_w_in, gla_w_gate, gla_b_gate, gla_norm_g, gla_w_out,
           mlstm_w_in, mlstm_b_gates, mlstm_norm_g, mlstm_w_out,
           adaln_w, adaln_b, ln_g, ln_b,
           moe_w_group, moe_b_group, moe_w_expert, moe_b_expert, moe_w_gate, moe_w_up, moe_w_down):
    cvecs = jnp.zeros((MOD_ROWS, D), F32).at[0].set(c_ctx).at[1:1 + N_LAT_SEQ].set(c)
    mods_all = _modulation(cvecs, adaln_w, adaln_b)
    x = _embed(x_prompt, x_sample, _grid_posemb(T_LAT))
    gla_states, ml_c, ml_n, ml_m = [], [], [], []
    for l in range(DEPTH):
        s = l // 2
        mods = mods_all[l].reshape(MOD_ROWS * 6, 1, D)
        if l % 2 == 0:
            q, k, v, r, g = _gla_in(x, mods, gla_w_in[s], gla_w_gate[s], gla_b_gate[s])
            o_ctx, s_new = _gla_scan(q, k, v, g, None, s)
            (o_lat,) = _gla_scan(q, k, v, g, state_gla_S, s)
            gla_states.append(s_new)
            norm_g, w_out = gla_norm_g[s], gla_w_out[s]
        else:
            q, k, v, r, gc, gr = _mlstm_in(x, mods, mlstm_w_in[s], mlstm_b_gates[s])
            o_ctx, c_new, n_new, m_new = _mlstm_scan(q, k, v, gc, gr, None, s)
            (o_lat,) = _mlstm_scan(q, k, v, gc, gr, (state_mlstm_C, state_mlstm_n, state_mlstm_m), s)
            ml_c.append(c_new)
            ml_n.append(n_new[:, :, :, 0, :])
            ml_m.append(m_new[:, :, :, 0, 0])
            norm_g, w_out = mlstm_norm_g[s], mlstm_w_out[s]
        w_route = jnp.zeros((D, ROUTE_W), F32).at[:, :NE].set(moe_w_expert[l]).at[:, NE:NE + N_GROUPS].set(
            moe_w_group[l])
        b_route = jnp.zeros((1, ROUTE_W), F32).at[0, :NE].set(moe_b_expert[l]).at[0, NE:NE + N_GROUPS].set(
            moe_b_group[l])
        x1, h2, rinfo, cnt = _out_proj(o_ctx, o_lat, r, x, mods, norm_g, w_out, ln_g[l, 0], ln_b[l, 0], w_route, b_route)
        x = _moe(h2, rinfo, cnt, x1, mods, ln_g[l, 1], ln_b[l, 1], moe_w_gate[l], moe_w_up[l], moe_w_down[l])
    y_ctx = x[:N_CTX].reshape(N_CTX_SEQ, T_CTX, D)
    y_lat = x[N_CTX:].reshape(N_LAT_SEQ, T_LAT, D)
    return (y_ctx, y_lat, jnp.stack(gla_states, 1), jnp.stack(ml_c, 1), jnp.stack(ml_n, 1), jnp.stack(ml_m, 1))
```

```python
import functools
import math

import jax
import jax.numpy as jnp
from jax import lax
from jax.experimental import pallas as pl
from jax.experimental.pallas import tpu as pltpu

F32 = jnp.float32
BF16 = jnp.bfloat16

D = 1024
N_CTX_SEQ, T_CTX = 16, 256
N_LAT_SEQ, T_LAT = 8, 1024
N_CTX = N_CTX_SEQ * T_CTX
N_LAT = N_LAT_SEQ * T_LAT
N_TOK = N_CTX + N_LAT
DEPTH = 4
GRID_W = 64
NH = 4
DK = 128
DV = 256
DKT = NH * DK
DVT = NH * DV
GATE_RANK = 16
GATE_TAU = 16.0
CH = 64
N_GROUPS = 4
EPG = 8
NE = N_GROUPS * EPG
DE = 256
DN_ALPHA = (2.0 * DEPTH) ** 0.25
LN_EPS = 1e-5
MOD_ROWS = 16
TB = 512
HPB = 2
SB = 256
NCB = SB // CH
CH_SHIFT = 6
NB = N_TOK // TB
ROUTE_W = 128
NEG_INF = float("-inf")
VMEM_LIMIT = 48 * 1024 * 1024


def _cp(sem):
    return pltpu.CompilerParams(dimension_semantics=sem, vmem_limit_bytes=VMEM_LIMIT)


def _dot(a, b):
    return jnp.dot(a, b, preferred_element_type=F32)


def _dot_nt(a, b):
    return lax.dot_general(a, b, (((1,), (1,)), ((), ())), preferred_element_type=F32)


def _dot_f32(a, b):
    return jnp.dot(a, b, preferred_element_type=F32, precision=lax.Precision.HIGHEST)


def _split2(x):
    hi = x.astype(BF16)
    lo = (x - hi.astype(F32)).astype(BF16)
    return hi, lo


def _log_sigmoid(x):
    return -(jnp.maximum(-x, 0.0) + jnp.log1p(jnp.exp(-jnp.abs(x))))


def _silu(x):
    return x * jax.nn.sigmoid(x)


def _layernorm(x, g, b):
    mu = jnp.mean(x, axis=-1, keepdims=True)
    xc = x - mu
    var = jnp.mean(xc * xc, axis=-1, keepdims=True)
    return xc * lax.rsqrt(var + LN_EPS) * g + b


def _mod_row_tb(i):
    return jnp.where(i < N_CTX // TB, 0, 1 + (i - N_CTX // TB) // (T_LAT // TB))


def _mod_spec(j, row_fn):
    return pl.BlockSpec((1, 1, D), lambda i, *_: (row_fn(i) * 6 + j, 0, 0))


def _mod_kernel(c_ref, w_ref, b_ref, o_ref):
    o_ref[0] = _dot_f32(_silu(c_ref[...]), w_ref[0]) + b_ref[0]


def _modulation(cvecs, adaln_w, adaln_b):
    nj = 6
    return pl.pallas_call(
        _mod_kernel,
        out_shape=jax.ShapeDtypeStruct((DEPTH, MOD_ROWS, 6 * D), F32),
        grid=(DEPTH, nj),
        in_specs=[
            pl.BlockSpec((MOD_ROWS, D), lambda l, j: (0, 0)),
            pl.BlockSpec((1, D, D), lambda l, j: (l, 0, j)),
            pl.BlockSpec((1, 1, D), lambda l, j: (l, 0, j)),
        ],
        out_specs=pl.BlockSpec((1, MOD_ROWS, D), lambda l, j: (l, 0, j)),
        compiler_params=_cp(("parallel", "parallel")),
        name="adaln_modulation",
    )(cvecs, adaln_w, adaln_b.reshape(DEPTH, 1, 6 * D))


def _embed_kernel(xp_ref, xs_ref, pos_ref, o_ref):
    i = pl.program_id(0)

    @pl.when(i < N_CTX // TB)
    def _():
        o_ref[...] = xp_ref[...]

    @pl.when(i >= N_CTX // TB)
    def _():
        o_ref[...] = xs_ref[...] + pos_ref[...]


def _embed(x_prompt, x_sample, pos):
    nc = N_CTX // TB
    return pl.pallas_call(
        _embed_kernel,
        out_shape=jax.ShapeDtypeStruct((N_TOK, D), F32),
        grid=(NB,),
        in_specs=[
            pl.BlockSpec((TB, D), lambda i: (jnp.minimum(i, nc - 1), 0)),
            pl.BlockSpec((TB, D), lambda i: (jnp.maximum(i - nc, 0), 0)),
            pl.BlockSpec((TB, D), lambda i: (jnp.maximum(i - nc, 0) % (T_LAT // TB), 0)),
        ],
        out_specs=pl.BlockSpec((TB, D), lambda i: (i, 0)),
        compiler_params=_cp(("parallel",)),
        name="embed_tokens",
    )(x_prompt.reshape(N_CTX, D), x_sample.reshape(N_LAT, D), pos)


def _grid_posemb(T):
    rows = T // GRID_W
    r, cidx = jnp.meshgrid(jnp.arange(rows), jnp.arange(GRID_W), indexing="ij")
    nf = D // 4
    freqs = jnp.exp(-math.log(10000.0) * jnp.arange(nf, dtype=F32) / nf)

    def emb(p):
        a = p.reshape(-1).astype(F32)[:, None] * freqs
        return jnp.concatenate([jnp.sin(a), jnp.cos(a)], -1)

    return jnp.concatenate([emb(r), emb(cidx)], -1)


def _gla_in_kernel(x_ref, sh_ref, sc_ref, wq_ref, wk_ref, wv_ref, wr_ref, wg_ref, wgate_ref, bgate_ref,
                   q_ref, k_ref, v_ref, r_ref, g_ref):
    h = (x_ref[...] * (1.0 + sc_ref[0]) + sh_ref[0]).astype(BF16)
    q_ref[...] = _dot(h, wq_ref[...]) * (DK ** -0.5)
    k_ref[...] = _dot(h, wk_ref[...])
    v_ref[...] = _dot(h, wv_ref[...])
    r_ref[...] = _silu(_dot(h, wr_ref[...]))
    glr = _dot(h, wg_ref[...]).astype(BF16)
    z = _dot(glr, wgate_ref[...]) + bgate_ref[...]
    g_ref[...] = _log_sigmoid(z) * (1.0 / GATE_TAU)


def _gla_in(x, mods, w_in, w_gate, b_gate):
    wq = w_in[:, :DKT].astype(BF16)
    wk = w_in[:, DKT:2 * DKT].astype(BF16)
    wv = w_in[:, 2 * DKT:2 * DKT + DVT].astype(BF16)
    wr = w_in[:, 2 * DKT + DVT:2 * DKT + 2 * DVT].astype(BF16)
    wg = w_in[:, 2 * DKT + 2 * DVT:].astype(BF16)
    wgate = jnp.zeros((2 * GATE_RANK, 2 * DKT), F32)
    wgate = wgate.at[:GATE_RANK, :DKT].set(w_gate[0]).at[GATE_RANK:, DKT:].set(w_gate[1]).astype(BF16)
    bgate = b_gate.reshape(1, 2 * DKT)
    full = lambda s: pl.BlockSpec(s, lambda i: (0,) * len(s))
    tok = lambda n: pl.BlockSpec((TB, n), lambda i: (i, 0))
    return pl.pallas_call(
        _gla_in_kernel,
        out_shape=[jax.ShapeDtypeStruct((N_TOK, DKT), F32), jax.ShapeDtypeStruct((N_TOK, DKT), F32),
                   jax.ShapeDtypeStruct((N_TOK, DVT), F32), jax.ShapeDtypeStruct((N_TOK, DVT), F32),
                   jax.ShapeDtypeStruct((N_TOK, 2 * DKT), F32)],
        grid=(NB,),
        in_specs=[tok(D), _mod_spec(0, _mod_row_tb), _mod_spec(1, _mod_row_tb),
                  full((D, DKT)), full((D, DKT)), full((D, DVT)), full((D, DVT)), full((D, 2 * GATE_RANK)),
                  full((2 * GATE_RANK, 2 * DKT)), full((1, 2 * DKT))],
        out_specs=[tok(DKT), tok(DKT), tok(DVT), tok(DVT), tok(2 * DKT)],
        compiler_params=_cp(("parallel",)),
        name="gla_in_proj",
    )(x, mods, mods, wq, wk, wv, wr, wg, wgate, bgate)


def _mlstm_in_kernel(x_ref, sh_ref, sc_ref, wq_ref, wk_ref, wv_ref, wr_ref, wg_ref, wgt_ref, bg_ref, bgt_ref,
                     q_ref, k_ref, v_ref, r_ref, gc_ref, gr_ref):
    h = (x_ref[...] * (1.0 + sc_ref[0]) + sh_ref[0]).astype(BF16)
    q_ref[...] = _dot(h, wq_ref[...]) * (DK ** -0.5)
    k_ref[...] = _dot(h, wk_ref[...])
    v_ref[...] = _dot(h, wv_ref[...])
    r_ref[...] = jax.nn.sigmoid(_dot(h, wr_ref[...]))
    gc = _dot(h, wg_ref[...]) + bg_ref[...]
    gr = _dot_nt(wgt_ref[...], h) + bgt_ref[...]
    is_f_c = (lax.broadcasted_iota(jnp.int32, gc.shape, 1) % 2) == 1
    is_f_r = (lax.broadcasted_iota(jnp.int32, gr.shape, 0) % 2) == 1
    gc = jnp.where(is_f_c, _log_sigmoid(gc), gc)
    gr = jnp.where(is_f_r, _log_sigmoid(gr), gr)
    for hd in range(NH):
        gc_ref[hd] = gc[:, 4 * hd:4 * hd + 4]
        for c in range(TB // SB):
            gr_ref[hd, c] = gr[4 * hd:4 * hd + 4, c * SB:(c + 1) * SB]


def _mlstm_in(x, mods, w_in, b_gates):
    wq = w_in[:, :DKT].astype(BF16)
    wk = w_in[:, DKT:2 * DKT].astype(BF16)
    wv = w_in[:, 2 * DKT:2 * DKT + DVT].astype(BF16)
    wr = w_in[:, 2 * DKT + DVT:2 * DKT + 2 * DVT].astype(BF16)
    wg = w_in[:, 2 * DKT + 2 * DVT:].reshape(D, 4, NH).transpose(0, 2, 1).reshape(D, 4 * NH).astype(BF16)
    bg = b_gates.reshape(4, NH).T.reshape(1, 4 * NH)
    full = lambda s: pl.BlockSpec(s, lambda i: (0,) * len(s))
    tok = lambda n: pl.BlockSpec((TB, n), lambda i: (i, 0))
    return pl.pallas_call(
        _mlstm_in_kernel,
        out_shape=[jax.ShapeDtypeStruct((N_TOK, DKT), F32), jax.ShapeDtypeStruct((N_TOK, DKT), F32),
                   jax.ShapeDtypeStruct((N_TOK, DVT), F32), jax.ShapeDtypeStruct((N_TOK, DVT), F32),
                   jax.ShapeDtypeStruct((NH, N_TOK, 4), F32),
                   jax.ShapeDtypeStruct((NH, N_TOK // SB, 4, SB), F32)],
        grid=(NB,),
        in_specs=[tok(D), _mod_spec(0, _mod_row_tb), _mod_spec(1, _mod_row_tb),
                  full((D, DKT)), full((D, DKT)), full((D, DVT)), full((D, DVT)), full((D, 4 * NH)),
                  full((4 * NH, D)), full((1, 4 * NH)), full((4 * NH, 1))],
        out_specs=[tok(DKT), tok(DKT), tok(DVT), tok(DVT),
                   pl.BlockSpec((NH, TB, 4), lambda i: (0, i, 0)),
                   pl.BlockSpec((NH, TB // SB, 4, SB), lambda i: (0, i, 0, 0))],
        compiler_params=_cp(("parallel",)),
        name="mlstm_in_proj",
    )(x, mods, mods, wq, wk, wv, wr, wg, wg.T, bg, bg.T)


def _block_tri_masks():
    row = lax.broadcasted_iota(jnp.int32, (SB, SB), 0)
    col = lax.broadcasted_iota(jnp.int32, (SB, SB), 1)
    same = (row >> CH_SHIFT) == (col >> CH_SHIFT)
    return same & (row >= col), same & (row <= col)


def _chunk_scan(x, forward, axis=0, op=jnp.add, fill=0.0):
    n = x.shape[axis]
    pos = lax.broadcasted_iota(jnp.int32, x.shape, axis) & (CH - 1)
    s = 1
    while s < CH:
        if forward:
            x = op(x, jnp.where(pos >= s, pltpu.roll(x, s, axis), fill))
        else:
            x = op(x, jnp.where(pos < CH - s, pltpu.roll(x, n - s, axis), fill))
        s *= 2
    return x


def _chunk_cumsum(x, forward):
    return _chunk_scan(x, forward)


def _cat2(x):
    return jnp.concatenate([x, x], axis=1)


def _per_chunk(vals, n):
    return jnp.concatenate([jnp.broadcast_to(v, (CH, n)) for v in vals], axis=0)


def _chunk_masked_cat(x):
    rowc = lax.broadcasted_iota(jnp.int32, x.shape, 0) >> CH_SHIFT
    return jnp.concatenate([jnp.where(rowc == c, x, 0.0).astype(BF16) for c in range(NCB)], axis=1)


def _head_norm_rows(o):
    mu = jnp.mean(o, axis=-1, keepdims=True)
    oc = o - mu
    var = jnp.mean(oc * oc, axis=-1, keepdims=True)
    return oc * lax.rsqrt(var + LN_EPS)


def _gla_superblock(q, k, v, v_t, g, st, forward, mask):
    b = _chunk_cumsum(g, forward)
    r_last = CH - 1 if forward else 0
    r_mid = CH // 2 if forward else CH - 1 - CH // 2
    lasts = [b[c * CH + r_last:c * CH + r_last + 1, :] for c in range(NCB)]
    b_last = _per_chunk(lasts, DK)
    b_mid = _per_chunk([b[c * CH + r_mid:c * CH + r_mid + 1, :] for c in range(NCB)], DK)
    qe = (q * jnp.exp(b - b_mid)).astype(BF16)
    ke = (k * jnp.exp(b_mid - b)).astype(BF16)
    qb = (q * jnp.exp(b)).astype(BF16)
    kd = k * jnp.exp(b_last - b)
    a = jnp.where(mask, _dot_nt(qe, ke), 0.0).astype(BF16)
    o_intra = _dot(a, v)
    d_t = _dot(v_t, _chunk_masked_cat(kd))
    inter = [None] * NCB
    for c in (range(NCB) if forward else reversed(range(NCB))):
        inter[c] = _dot_nt(qb[c * CH:(c + 1) * CH, :], st.astype(BF16))
        st = st * jnp.exp(lasts[c]) + d_t[:, c * DK:(c + 1) * DK]
    return o_intra + jnp.concatenate(inter, axis=0), st


def _scan_driver(T, o_ref, ob_ref, step):
    nsb = T // SB
    if nsb == 1:
        step(0, 0)
    else:
        def body(i, carry):
            step(i, nsb - 1 - i)
            return carry

        lax.fori_loop(0, nsb, body, 0)

    def norm_body(j, carry):
        rows = pl.ds(pl.multiple_of(j * SB, SB), SB)
        for hh in range(HPB):
            cv = slice(hh * DV, (hh + 1) * DV)
            o_ref[rows, cv] = _head_norm_rows(o_ref[rows, cv] + ob_ref[rows, cv])
        return carry

    if nsb == 1:
        norm_body(0, 0)
    else:
        lax.fori_loop(0, nsb, norm_body, 0)


def _gla_scan_kernel(*refs, T, has_state, emit_state):
    q_ref, k_ref, v_ref, gf_ref, gb_ref = refs[:5]
    pos = 5
    if has_state:
        s0f_ref, s0b_ref = refs[pos:pos + 2]
        pos += 2
    o_ref = refs[pos]
    pos += 1
    if emit_state:
        sout_ref = refs[pos]
        pos += 1
    stf_ref, stb_ref, vt_ref, ob_ref = refs[pos:pos + 4]

    nsb = T // SB
    for hh in range(HPB):
        for j in range(nsb):
            vt_ref[hh, j] = v_ref[j * SB:(j + 1) * SB, hh * DV:(hh + 1) * DV].T.astype(BF16)
        if has_state:
            stf_ref[hh] = s0f_ref[hh].T
            stb_ref[hh] = s0b_ref[hh].T
        else:
            stf_ref[hh] = jnp.zeros((DV, DK), F32)
            stb_ref[hh] = jnp.zeros((DV, DK), F32)

    lower, upper = _block_tri_masks()

    def one(j, forward, hh):
        rows = pl.ds(pl.multiple_of(j * SB, SB), SB)
        ck = slice(hh * DK, (hh + 1) * DK)
        cv = slice(hh * DV, (hh + 1) * DV)
        st_ref = stf_ref if forward else stb_ref
        o, st = _gla_superblock(q_ref[rows, ck], k_ref[rows, ck], v_ref[rows, cv].astype(BF16), vt_ref[hh, j],
                                (gf_ref if forward else gb_ref)[rows, ck], st_ref[hh], forward,
                                lower if forward else upper)
        st_ref[hh] = st
        (o_ref if forward else ob_ref)[rows, cv] = o

    def step(jf, jb):
        for hh in range(HPB):
            one(jf, True, hh)
            one(jb, False, hh)

    _scan_driver(T, o_ref, ob_ref, step)
    if emit_state:
        for hh in range(HPB):
            sout_ref[0, hh] = stf_ref[hh].T
            sout_ref[1, hh] = stb_ref[hh].T


def _gla_scan(q, k, v, g, state, slot):
    ctx = state is None
    T = T_CTX if ctx else T_LAT
    n_seq = N_CTX_SEQ if ctx else N_LAT_SEQ
    off = 0 if ctx else N_CTX // T_LAT
    in_specs = [
        pl.BlockSpec((T, HPB * DK), lambda b, h: (off + b, h)),
        pl.BlockSpec((T, HPB * DK), lambda b, h: (off + b, h)),
        pl.BlockSpec((T, HPB * DV), lambda b, h: (off + b, h)),
        pl.BlockSpec((T, HPB * DK), lambda b, h: (off + b, h)),
        pl.BlockSpec((T, HPB * DK), lambda b, h: (off + b, NH // HPB + h)),
    ]
    args = [q, k, v, g, g]
    out_shape = [jax.ShapeDtypeStruct((n_seq * T, DVT), F32)]
    out_specs = [pl.BlockSpec((T, HPB * DV), lambda b, h: (b, h))]
    if ctx:
        out_shape.append(jax.ShapeDtypeStruct((N_CTX_SEQ, 2, NH, DK, DV), F32))
        out_specs.append(pl.BlockSpec((None, 2, HPB, DK, DV), lambda b, h: (b, 0, h, 0, 0)))
    else:
        sq = (None, None, None, HPB, DK, DV)
        in_specs += [pl.BlockSpec(sq, lambda b, h: (b, slot, 0, h, 0, 0)),
                     pl.BlockSpec(sq, lambda b, h: (b, slot, 1, h, 0, 0))]
        args += [state, state]
    res = pl.pallas_call(
        functools.partial(_gla_scan_kernel, T=T, has_state=not ctx, emit_state=ctx),
        out_shape=out_shape,
        grid=(n_seq, NH // HPB),
        in_specs=in_specs,
        out_specs=out_specs,
        scratch_shapes=[pltpu.VMEM((HPB, DV, DK), F32), pltpu.VMEM((HPB, DV, DK), F32),
                        pltpu.VMEM((HPB, T // SB, DV, SB), BF16), pltpu.VMEM((T, HPB * DV), F32)],
        compiler_params=_cp(("parallel", "parallel")),
        name="gla_scan_ctx" if ctx else "gla_scan_lat",
    )(*args)
    return res


def _mlstm_scan_kernel(*refs, T, has_state, emit_state):
    q_ref, k_ref, v_ref, gc_ref, gr_ref = refs[:5]
    pos = 5
    if has_state:
        c0f_ref, c0b_ref, n0f_ref, n0b_ref, m0f_ref, m0b_ref = refs[pos:pos + 6]
        pos += 6
    o_ref = refs[pos]
    pos += 1
    if emit_state:
        cout_ref, nout_ref, mout_ref = refs[pos:pos + 3]
        pos += 3
    ct_ref, n_ref, m_ref, vt_ref, ob_ref = refs[pos:pos + 5]

    nsb = T // SB
    for hh in range(HPB):
        for j in range(nsb):
            vt_ref[hh, j] = v_ref[j * SB:(j + 1) * SB, hh * DV:(hh + 1) * DV].T.astype(BF16)
        for d in range(2):
            sl = d * HPB + hh
            if has_state:
                c0, n0, m0 = ((c0f_ref, n0f_ref, m0f_ref), (c0b_ref, n0b_ref, m0b_ref))[d]
                ct_ref[sl] = c0[hh].T
                n_ref[sl] = n0[hh]
                m_ref[sl] = m0[hh]
            else:
                ct_ref[sl] = jnp.zeros((DV, DK), F32)
                n_ref[sl] = jnp.zeros((1, DK), F32)
                m_ref[sl] = jnp.zeros((1, DK), F32)

    lower, upper = _block_tri_masks()

    def one(j, forward, hh):
        d = 0 if forward else 1
        sl = d * HPB + hh
        rows = pl.ds(pl.multiple_of(j * SB, SB), SB)
        q = q_ref[rows, hh * DK:(hh + 1) * DK]
        k = k_ref[rows, hh * DK:(hh + 1) * DK]
        v = v_ref[rows, hh * DV:(hh + 1) * DV].astype(BF16)
        gc = gc_ref[hh, rows, :]
        gr = gr_ref[hh, j]
        o0 = 2 * d
        mask = lower if forward else upper
        order = list(range(NCB)) if forward else list(reversed(range(NCB)))
        r_last = CH - 1 if forward else 0
        i_c = jnp.broadcast_to(gc[:, o0:o0 + 1], (SB, DK))
        f_c = jnp.broadcast_to(gc[:, o0 + 1:o0 + 2], (SB, DK))
        b_c = _chunk_scan(f_c, forward)
        m_intra = b_c + _chunk_scan(i_c - b_c, forward, op=jnp.maximum, fill=NEG_INF)
        b_last = [b_c[c * CH + r_last:c * CH + r_last + 1, :] for c in range(NCB)]
        a_c = _per_chunk(b_last, DK) - b_c + i_c
        m_loc = [jnp.max(a_c[c * CH:(c + 1) * CH, :], axis=0, keepdims=True) for c in range(NCB)]
        kw = k * jnp.exp(a_c - _per_chunk(m_loc, DK))
        kv_t = _dot(vt_ref[hh, j], _chunk_masked_cat(kw))
        k_loc = [jnp.sum(kw[c * CH:(c + 1) * CH, :], axis=0, keepdims=True) for c in range(NCB)]

        m = m_ref[sl]
        m_start, s_old, s_loc = [None] * NCB, [None] * NCB, [None] * NCB
        for c in order:
            m_start[c] = m
            m_new = jnp.maximum(b_last[c] + m, m_loc[c])
            s_old[c] = jnp.exp(b_last[c] + m - m_new)
            s_loc[c] = jnp.exp(m_loc[c] - m_new)
            m = m_new
        m_ref[sl] = m

        l_inter = b_c + _per_chunk(m_start, DK)
        m_i = jnp.maximum(l_inter, m_intra)
        gcum = _chunk_scan(gr, forward, axis=1)
        brow = gcum[o0 + 1:o0 + 2, :] - gr[o0:o0 + 1, :]
        d_log = jnp.where(mask, _cat2(b_c - m_i) - brow, NEG_INF)
        qb = q.astype(BF16)
        s = (_dot_nt(qb, k.astype(BF16)) * jnp.exp(d_log)).astype(BF16)
        e_inter = jnp.exp(l_inter - m_i)
        nd = _dot(s, jnp.concatenate([v, jnp.ones((SB, DK), BF16)], axis=1))
        num, den = nd[:, :DV], nd[:, DV:]

        ct = ct_ref[sl]
        nrm = n_ref[sl]
        inter = [None] * NCB
        for c in order:
            state = jnp.concatenate([ct, jnp.broadcast_to(nrm, (DK, DK))], axis=0).astype(BF16)
            inter[c] = _dot_nt(qb[c * CH:(c + 1) * CH, :], state)
            ct = s_old[c] * ct + s_loc[c] * kv_t[:, c * DK:(c + 1) * DK]
            nrm = s_old[c] * nrm + s_loc[c] * k_loc[c]
        ct_ref[sl] = ct
        n_ref[sl] = nrm

        inter = jnp.concatenate(inter, axis=0)
        num = num + _cat2(e_inter) * inter[:, :DV]
        den = den + e_inter * inter[:, DV:]
        inv = 1.0 / jnp.maximum(jnp.abs(den), jnp.exp(-m_i))
        (o_ref if forward else ob_ref)[rows, hh * DV:(hh + 1) * DV] = num * _cat2(inv)

    def step(jf, jb):
        for hh in range(HPB):
            one(jf, True, hh)
            one(jb, False, hh)

    _scan_driver(T, o_ref, ob_ref, step)
    if emit_state:
        for d in range(2):
            for hh in range(HPB):
                cout_ref[d, hh] = ct_ref[d * HPB + hh].T
                nout_ref[d, hh] = n_ref[d * HPB + hh]
                mout_ref[d, hh] = m_ref[d * HPB + hh]


def _mlstm_scan(q, k, v, gc, gr, states, slot):
    ctx = states is None
    T = T_CTX if ctx else T_LAT
    n_seq = N_CTX_SEQ if ctx else N_LAT_SEQ
    off = 0 if ctx else N_CTX // T_LAT
    in_specs = [
        pl.BlockSpec((T, HPB * DK), lambda b, h: (off + b, h)),
        pl.BlockSpec((T, HPB * DK), lambda b, h: (off + b, h)),
        pl.BlockSpec((T, HPB * DV), lambda b, h: (off + b, h)),
        pl.BlockSpec((HPB, T, 4), lambda b, h: (h, off + b, 0)),
        pl.BlockSpec((HPB, T // SB, 4, SB), lambda b, h: (h, off + b, 0, 0)),
    ]
    args = [q, k, v, gc, gr]
    out_shape = [jax.ShapeDtypeStruct((n_seq * T, DVT), F32)]
    out_specs = [pl.BlockSpec((T, HPB * DV), lambda b, h: (b, h))]
    if ctx:
        out_shape += [jax.ShapeDtypeStruct((N_CTX_SEQ, 2, NH, DK, DV), F32),
                      jax.ShapeDtypeStruct((N_CTX_SEQ, 2, NH, 1, DK), F32),
                      jax.ShapeDtypeStruct((N_CTX_SEQ, 2, NH, 1, DK), F32)]
        out_specs += [pl.BlockSpec((None, 2, HPB, DK, DV), lambda b, h: (b, 0, h, 0, 0)),
                      pl.BlockSpec((None, 2, HPB, 1, DK), lambda b, h: (b, 0, h, 0, 0)),
                      pl.BlockSpec((None, 2, HPB, 1, DK), lambda b, h: (b, 0, h, 0, 0))]
    else:
        c0, n0, m0 = states
        n0 = n0.reshape(N_LAT_SEQ, -1, 2, NH, 1, DK)
        m0 = jnp.broadcast_to(m0[..., None, None], m0.shape + (1, DK))
        sq_c = (None, None, None, HPB, DK, DV)
        sq_v = (None, None, None, HPB, 1, DK)
        for arr, sq in ((c0, sq_c), (n0, sq_v), (m0, sq_v)):
            for d in range(2):
                in_specs.append(pl.BlockSpec(sq, functools.partial(lambda b, h, d: (b, slot, d, h, 0, 0), d=d)))
                args.append(arr)
    return pl.pallas_call(
        functools.partial(_mlstm_scan_kernel, T=T, has_state=not ctx, emit_state=ctx),
        out_shape=out_shape,
        grid=(n_seq, NH // HPB),
        in_specs=in_specs,
        out_specs=out_specs,
        scratch_shapes=[pltpu.VMEM((2 * HPB, DV, DK), F32), pltpu.VMEM((2 * HPB, 1, DK), F32),
                        pltpu.VMEM((2 * HPB, 1, DK), F32),
                        pltpu.VMEM((HPB, T // SB, DV, SB), BF16), pltpu.VMEM((T, HPB * DV), F32)],
        compiler_params=_cp(("parallel", "parallel")),
        name="mlstm_scan_ctx" if ctx else "mlstm_scan_lat",
    )(*args)


def _route(lg):
    lane = lax.broadcasted_iota(jnp.int32, lg.shape, 1)
    big = jnp.int32(ROUTE_W)
    is_g = (lane >= NE) & (lane < NE + N_GROUPS)
    gl = jnp.where(is_g, lg, NEG_INF)
    gmax = jnp.max(gl, axis=1, keepdims=True)
    gsel = jnp.min(jnp.where(gl == gmax, lane, big), axis=1, keepdims=True) - NE
    gw = 1.0 / jnp.sum(jnp.exp(gl - gmax), axis=1, keepdims=True)
    ing = (lane < NE) & ((lane >> 3) == gsel)
    el = jnp.where(ing, lg, NEG_INF)
    emax = jnp.max(el, axis=1, keepdims=True)
    p = jnp.exp(el - emax)
    prob = p / jnp.sum(p, axis=1, keepdims=True)
    p1 = jnp.max(prob, axis=1, keepdims=True)
    i1 = jnp.min(jnp.where(ing & (prob == p1), lane, big), axis=1, keepdims=True)
    rest = ing & (lane != i1)
    prob2 = jnp.where(rest, prob, -1.0)
    p2 = jnp.max(prob2, axis=1, keepdims=True)
    i2 = jnp.min(jnp.where(rest & (prob2 == p2), lane, big), axis=1, keepdims=True)
    tot = p1 + p2
    rinfo = jnp.where(lane == 0, i1.astype(F32),
                      jnp.where(lane == 1, i2.astype(F32),
                                jnp.where(lane == 2, gw * (p1 / tot), jnp.where(lane == 3, gw * (p2 / tot), 0.0))))
    count = jnp.sum(jnp.where((lane == i1) | (lane == i2), 1.0, 0.0), axis=0, keepdims=True)
    return rinfo, count


def _out_kernel(oc_ref, ol_ref, r_ref, x_ref, gate_ref, sh_ref, sc_ref, ng_ref, wo_ref, lg_ref, lb_ref, wr_ref,
                br_ref, x1_ref, h2_ref, rinfo_ref, cnt_ref):
    o = jnp.where(pl.program_id(0) < N_CTX // TB, oc_ref[...], ol_ref[...])
    y = _dot((o * ng_ref[...] * r_ref[...]).astype(BF16), wo_ref[...])
    x1 = _layernorm(DN_ALPHA * x_ref[...] + gate_ref[0] * y, lg_ref[...], lb_ref[...])
    x1_ref[...] = x1
    h2 = x1 * (1.0 + sc_ref[0]) + sh_ref[0]
    h_hi, h_lo = _split2(h2)
    h2_ref[...] = h_hi
    t = _dot(h_hi, wr_ref[...])
    lg = t[:, :ROUTE_W] + t[:, ROUTE_W:] + _dot(h_lo, wr_ref[:, :ROUTE_W]) + br_ref[...]
    rinfo_ref[...], cnt_ref[0] = _route(lg)


def _out_proj(o_ctx, o_lat, r, x, mods, norm_g, w_out, ln_g, ln_b, w_route, b_route):
    nc = N_CTX // TB
    full = lambda s: pl.BlockSpec(s, lambda i: (0,) * len(s))
    tok = lambda n: pl.BlockSpec((TB, n), lambda i: (i, 0))
    return pl.pallas_call(
        _out_kernel,
        out_shape=[jax.ShapeDtypeStruct((N_TOK, D), F32), jax.ShapeDtypeStruct((N_TOK, D), BF16),
                   jax.ShapeDtypeStruct((N_TOK, ROUTE_W), F32), jax.ShapeDtypeStruct((NB, 1, ROUTE_W), F32)],
        grid=(NB,),
        in_specs=[pl.BlockSpec((TB, DVT), lambda i: (jnp.minimum(i, nc - 1), 0)),
                  pl.BlockSpec((TB, DVT), lambda i: (jnp.maximum(i - nc, 0), 0)),
                  tok(DVT), tok(D),
                  _mod_spec(2, _mod_row_tb), _mod_spec(3, _mod_row_tb), _mod_spec(4, _mod_row_tb),
                  full((1, DVT)), full((DVT, D)), full((1, D)), full((1, D)),
                  full((D, 2 * ROUTE_W)), full((1, ROUTE_W))],
        out_specs=[tok(D), tok(D), tok(ROUTE_W), pl.BlockSpec((1, 1, ROUTE_W), lambda i: (i, 0, 0))],
        compiler_params=_cp(("parallel",)),
        name="out_proj_route",
    )(o_ctx, o_lat, r, x, mods, mods, mods, norm_g.reshape(1, DVT), w_out.astype(BF16),
      ln_g.reshape(1, D), ln_b.reshape(1, D), jnp.concatenate(_split2(w_route), axis=1), b_route)


TP = 512
NBP = N_TOK // TP
SEG = 16
R_LOC = 1536
PT = 256
XW = D + ROUTE_W
TMS = 512
R_TOT = -(-(2 * N_TOK + NBP * NE * (SEG - 1) + NE * (TMS - 1)) // TMS) * TMS
NT = R_TOT // TMS
POS_SPLIT = 64.0


def _mod_row_tp(i):
    return jnp.where(i < N_CTX // TP, 0, 1 + (i - N_CTX // TP) // (T_LAT // TP))


def _lane_pack(cols, shape):
    lane = lax.broadcasted_iota(jnp.int32, shape, 1)
    out = jnp.zeros(shape, F32)
    for j, c in enumerate(cols):
        out = jnp.where(lane == j, c, out)
    return out


COPY_CHUNKS = 4
WAIT_CHUNKS = (32, 4, 1)


def _segment_start(nch_ref, loff_ref, goff_ref, blk, make_copy):
    def per_expert(e, carry):
        idx = blk * NE + e
        n = nch_ref[idx]
        l0 = loff_ref[idx]
        g0 = goff_ref[idx]
        n_big = n // COPY_CHUNKS

        def per_piece(k, c, first, chunks):
            off = (first + k * chunks) * SEG
            make_copy(pl.multiple_of(l0 + off, SEG), pl.multiple_of(g0 + off, SEG), chunks * SEG).start()
            return c

        lax.fori_loop(0, n_big, functools.partial(per_piece, first=0, chunks=COPY_CHUNKS), 0)
        lax.fori_loop(0, n - n_big * COPY_CHUNKS, functools.partial(per_piece, first=n_big * COPY_CHUNKS, chunks=1), 0)
        return carry

    lax.fori_loop(0, NE, per_expert, 0)


def _segment_wait(n_chunks, make_copy):
    left = n_chunks
    for chunks in WAIT_CHUNKS:
        n_wait = left // chunks

        def wait_piece(k, c, chunks=chunks):
            make_copy(0, 0, chunks * SEG).wait()
            return c

        lax.fori_loop(0, n_wait, wait_piece, 0)
        left = left - n_wait * chunks


def _perm_kernel(nch_ref, loff_ref, goff_ref, ntot_ref, h_ref, ri_ref, loffv_ref, xs_init_ref, pos_ref, xs_ref,
                 xs_scr, sem):
    del xs_init_ref
    blk = pl.program_id(0)
    slot = blk % 2
    ri = ri_ref[...]
    e_a, e_b, w_a, w_b = ri[:, 0:1], ri[:, 1:2], ri[:, 2:3], ri[:, 3:4]
    lanef = lax.broadcasted_iota(jnp.int32, ri.shape, 1).astype(F32)
    is_a = lanef == e_a
    is_b = lanef == e_b
    sel = jnp.where(is_a | is_b, 1.0, 0.0).astype(BF16)
    row = lax.broadcasted_iota(jnp.int32, (TP, TP), 0)
    col = lax.broadcasted_iota(jnp.int32, (TP, TP), 1)
    earlier = jnp.where(row > col, 1.0, 0.0).astype(BF16)
    lpos = loffv_ref[0] + _dot(earlier, sel)
    pos_a = jnp.sum(jnp.where(is_a, lpos, 0.0), axis=1, keepdims=True)
    pos_b = jnp.sum(jnp.where(is_b, lpos, 0.0), axis=1, keepdims=True)
    pos_ref[...] = _lane_pack([pos_a, pos_b], ri.shape)

    hi_a = jnp.floor(pos_a * (1.0 / POS_SPLIT))
    hi_b = jnp.floor(pos_b * (1.0 / POS_SPLIT))
    parts = _lane_pack([hi_a, pos_a - POS_SPLIT * hi_a, hi_b, pos_b - POS_SPLIT * hi_b], ri.shape).astype(BF16)
    pick = jnp.where(lax.broadcasted_iota(jnp.int32, (8, ROUTE_W), 0) == lax.broadcasted_iota(jnp.int32, (8, ROUTE_W), 1),
                     1.0, 0.0).astype(BF16)
    pr = _dot_nt(pick, parts)
    pos_a_r = POS_SPLIT * pr[0:1, :] + pr[1:2, :]
    pos_b_r = POS_SPLIT * pr[2:3, :] + pr[3:4, :]

    wa1, wa2 = _split2(w_a)
    wa3 = (w_a - wa1.astype(F32) - wa2.astype(F32))
    wb1, wb2 = _split2(w_b)
    wb3 = (w_b - wb1.astype(F32) - wb2.astype(F32))
    wl = _lane_pack([wa1.astype(F32), wa2.astype(F32), wa3, wb1.astype(F32), wb2.astype(F32), wb3, e_a],
                    ri.shape).astype(BF16)
    hcat = jnp.concatenate([h_ref[...], wl], axis=1)
    for t in range(R_LOC // PT):
        rio = (lax.broadcasted_iota(jnp.int32, (PT, TP), 0) + t * PT).astype(F32)
        onehot = jnp.where((rio == pos_a_r) | (rio == pos_b_r), 1.0, 0.0).astype(BF16)
        xs_scr[slot, t * PT:(t + 1) * PT, :] = _dot(onehot, hcat).astype(BF16)

    def copy_from(s):
        return lambda l, g, n: pltpu.make_async_copy(xs_scr.at[s, pl.ds(l, n)], xs_ref.at[pl.ds(g, n)], sem.at[s])

    _segment_start(nch_ref, loff_ref, goff_ref, blk, copy_from(slot))

    @pl.when(blk > 0)
    def _():
        _segment_wait(ntot_ref[blk - 1], copy_from(1 - slot))

    @pl.when(blk == NBP - 1)
    def _():
        _segment_wait(ntot_ref[blk], copy_from(slot))


def _moe_permute(h2, rinfo, tables):
    nch, loff, goff, ntot, loffv = tables
    xs0 = jnp.zeros((R_TOT, XW), BF16)
    grid_spec = pltpu.PrefetchScalarGridSpec(
        num_scalar_prefetch=4,
        grid=(NBP,),
        in_specs=[pl.BlockSpec((TP, D), lambda i, *_: (i, 0)),
                  pl.BlockSpec((TP, ROUTE_W), lambda i, *_: (i, 0)),
                  pl.BlockSpec((1, 1, ROUTE_W), lambda i, *_: (i, 0, 0)),
                  pl.BlockSpec(memory_space=pl.ANY)],
        out_specs=[pl.BlockSpec((TP, ROUTE_W), lambda i, *_: (i, 0)),
                   pl.BlockSpec(memory_space=pl.ANY)],
        scratch_shapes=[pltpu.VMEM((2, R_LOC, XW), BF16), pltpu.SemaphoreType.DMA((2,))],
    )
    return pl.pallas_call(
        _perm_kernel,
        out_shape=[jax.ShapeDtypeStruct((N_TOK, ROUTE_W), F32), jax.ShapeDtypeStruct((R_TOT, XW), BF16)],
        grid_spec=grid_spec,
        input_output_aliases={7: 1},
        compiler_params=_cp(("arbitrary",)),
        name="moe_permute",
    )(nch, loff, goff, ntot, h2, rinfo, loffv, xs0)


def _ffn_kernel(te_ref, nt_ref, xs_ref, wg32_ref, wu32_ref, wd32_ref, ys_ref, wg_ref, wu_ref, wd_ref):
    i = pl.program_id(0)
    active = i < nt_ref[0]
    new_expert = (i == 0) | (te_ref[i] != te_ref[jnp.maximum(i - 1, 0)])

    @pl.when(active & new_expert)
    def _():
        wg_ref[0] = wg32_ref[0].astype(BF16)
        wu_ref[0] = wu32_ref[0].astype(BF16)
        wd_ref[0] = wd32_ref[0].astype(BF16)

    @pl.when(active)
    def _():
        e = te_ref[i].astype(F32)
        xs = xs_ref[...]
        x = xs[:, :D]
        r = xs[:, D:].astype(F32)
        w_a = r[:, 0:1] + r[:, 1:2] + r[:, 2:3]
        w_b = r[:, 3:4] + r[:, 4:5] + r[:, 5:6]
        w = jnp.where(r[:, 6:7] == e, w_a, w_b)
        hid = _silu(_dot(x, wg_ref[0])) * _dot(x, wu_ref[0]) * w
        ys_ref[...] = _dot(hid.astype(BF16), wd_ref[0]).astype(BF16)

    @pl.when(i >= nt_ref[0])
    def _():
        ys_ref[...] = jnp.zeros_like(ys_ref)


def _moe_ffn(xs, tile_expert, n_tiles, layer, w_gate, w_up, w_down):
    last = lambda i, nt: jnp.minimum(i, nt[0] - 1)
    wspec = lambda a, b: pl.BlockSpec((None, 1, a, b), lambda i, te, nt: (layer, te[last(i, nt)], 0, 0))
    grid_spec = pltpu.PrefetchScalarGridSpec(
        num_scalar_prefetch=2,
        grid=(NT,),
        in_specs=[pl.BlockSpec((TMS, XW), lambda i, te, nt: (last(i, nt), 0)),
                  wspec(D, DE), wspec(D, DE), wspec(DE, D)],
        out_specs=pl.BlockSpec((TMS, D), lambda i, te, nt: (i, 0)),
        scratch_shapes=[pltpu.VMEM((1, D, DE), BF16), pltpu.VMEM((1, D, DE), BF16), pltpu.VMEM((1, DE, D), BF16)],
    )
    return pl.pallas_call(
        _ffn_kernel,
        out_shape=jax.ShapeDtypeStruct((R_TOT, D), BF16),
        grid_spec=grid_spec,
        compiler_params=_cp(("arbitrary",)),
        name="moe_ffn",
    )(tile_expert, n_tiles, xs, w_gate, w_up, w_down)


def _unperm_kernel(nch_ref, loff_ref, goff_ref, ntot_ref, pos_ref, x1_ref, gate_ref, lg_ref, lb_ref, ys_ref, o_ref,
                   ys_scr, sem):
    blk = pl.program_id(0)
    slot = blk % 2

    def copy_to(s):
        return lambda l, g, n: pltpu.make_async_copy(ys_ref.at[pl.ds(g, n)], ys_scr.at[s, pl.ds(l, n)], sem.at[s])

    def fetch(b, s):
        ys_scr[s] = jnp.zeros((R_LOC, D), BF16)
        _segment_start(nch_ref, loff_ref, goff_ref, b, copy_to(s))

    @pl.when(blk == 0)
    def _():
        fetch(0, 0)

    @pl.when(blk + 1 < NBP)
    def _():
        fetch(blk + 1, 1 - slot)

    _segment_wait(ntot_ref[blk], copy_to(slot))
    pos = pos_ref[...]
    pos_a, pos_b = pos[:, 0:1], pos[:, 1:2]
    acc = jnp.zeros((TP, D), F32)
    for t in range(R_LOC // PT):
        lio = (lax.broadcasted_iota(jnp.int32, (TP, PT), 1) + t * PT).astype(F32)
        onehot = jnp.where((lio == pos_a) | (lio == pos_b), 1.0, 0.0).astype(BF16)
        acc = acc + _dot(onehot, ys_scr[slot, t * PT:(t + 1) * PT, :])
    o_ref[...] = _layernorm(DN_ALPHA * x1_ref[...] + gate_ref[0] * acc, lg_ref[...], lb_ref[...])


def _moe_unpermute(ys, pos, x1, mods, ln_g, ln_b, tables):
    nch, loff, goff, ntot, _ = tables
    grid_spec = pltpu.PrefetchScalarGridSpec(
        num_scalar_prefetch=4,
        grid=(NBP,),
        in_specs=[pl.BlockSpec((TP, ROUTE_W), lambda i, *_: (i, 0)),
                  pl.BlockSpec((TP, D), lambda i, *_: (i, 0)),
                  _mod_spec(5, _mod_row_tp),
                  pl.BlockSpec((1, D), lambda i, *_: (0, 0)),
                  pl.BlockSpec((1, D), lambda i, *_: (0, 0)),
                  pl.BlockSpec(memory_space=pl.ANY)],
        out_specs=pl.BlockSpec((TP, D), lambda i, *_: (i, 0)),
        scratch_shapes=[pltpu.VMEM((2, R_LOC, D), BF16), pltpu.SemaphoreType.DMA((2,))],
    )
    return pl.pallas_call(
        _unperm_kernel,
        out_shape=jax.ShapeDtypeStruct((N_TOK, D), F32),
        grid_spec=grid_spec,
        compiler_params=_cp(("arbitrary",)),
        name="moe_unpermute",
    )(nch, loff, goff, ntot, pos, x1, mods, ln_g.reshape(1, D), ln_b.reshape(1, D), ys)


def _segment_tables(cnt):
    c = cnt[:, 0, :NE].astype(jnp.int32).reshape(NBP, TP // TB, NE).sum(axis=1)
    pc = (c + SEG - 1) // SEG * SEG
    loff = jnp.cumsum(pc, axis=1) - pc
    tot = pc.sum(axis=0)
    totp = (tot + TMS - 1) // TMS * TMS
    eend = jnp.cumsum(totp)
    goff = (eend - totp)[None, :] + jnp.cumsum(pc, axis=0) - pc
    n_tiles = (eend[-1] // TMS).reshape(1)
    tile_id = jnp.arange(NT, dtype=jnp.int32)
    tile_expert = jnp.minimum(jnp.sum(tile_id[:, None] >= (eend // TMS)[None, :], axis=1), NE - 1).astype(jnp.int32)
    loffv = jnp.zeros((NBP, 1, ROUTE_W), F32).at[:, 0, :NE].set(loff.astype(F32))
    flat = lambda a: a.reshape(-1).astype(jnp.int32)
    ntot = (pc.sum(axis=1) // SEG).astype(jnp.int32)
    return (flat(pc // SEG), flat(loff), flat(goff), ntot, loffv), tile_expert, n_tiles.astype(jnp.int32)


def _moe(h2, rinfo, cnt, x1, mods, ln_g, ln_b, layer, w_gate, w_up, w_down):
    tables, tile_expert, n_tiles = _segment_tables(cnt)
    pos, xs = _moe_permute(h2, rinfo, tables)
    ys = _moe_ffn(xs, tile_expert, n_tiles, layer, w_gate, w_up, w_down)
    return _moe_unpermute(ys, pos, x1, mods, ln_g, ln_b, tables)


def kernel(x_prompt, x_sample, state_gla_S, state_mlstm_C, state_mlstm_n, state_mlstm_m, c, c_ctx,
           gla_w_in, gla_w_gate, gla_b_gate, gla_norm_g, gla_w_out,
           mlstm_w_in, mlstm_b_gates, mlstm_norm_g, mlstm_w_out,
           adaln_w, adaln_b, ln_g, ln_b,
           moe_w_group, moe_b_group, moe_w_expert, moe_b_expert, moe_w_gate, moe_w_up, moe_w_down):
    cvecs = jnp.zeros((MOD_ROWS, D), F32).at[0].set(c_ctx).at[1:1 + N_LAT_SEQ].set(c)
    mods_all = _modulation(cvecs, adaln_w, adaln_b)
    x = _embed(x_prompt, x_sample, _grid_posemb(T_LAT))
    gla_states, ml_c, ml_n, ml_m = [], [], [], []
    for l in range(DEPTH):
        s = l // 2
        mods = mods_all[l].reshape(MOD_ROWS * 6, 1, D)
        if l % 2 == 0:
            q, k, v, r, g = _gla_in(x, mods, gla_w_in[s], gla_w_gate[s], gla_b_gate[s])
            o_ctx, s_new = _gla_scan(q, k, v, g, None, s)
            (o_lat,) = _gla_scan(q, k, v, g, state_gla_S, s)
            gla_states.append(s_new)
            norm_g, w_out = gla_norm_g[s], gla_w_out[s]
        else:
            q, k, v, r, gc, gr = _mlstm_in(x, mods, mlstm_w_in[s], mlstm_b_gates[s])
            o_ctx, c_new, n_new, m_new = _mlstm_scan(q, k, v, gc, gr, None, s)
            (o_lat,) = _mlstm_scan(q, k, v, gc, gr, (state_mlstm_C, state_mlstm_n, state_mlstm_m), s)
            ml_c.append(c_new)
            ml_n.append(n_new[:, :, :, 0, :])
            ml_m.append(m_new[:, :, :, 0, 0])
            norm_g, w_out = mlstm_norm_g[s], mlstm_w_out[s]
        w_route = jnp.zeros((D, ROUTE_W), F32).at[:, :NE].set(moe_w_expert[l]).at[:, NE:NE + N_GROUPS].set(
            moe_w_group[l])
        b_route = jnp.zeros((1, ROUTE_W), F32).at[0, :NE].set(moe_b_expert[l]).at[0, NE:NE + N_GROUPS].set(
            moe_b_group[l])
        x1, h2, rinfo, cnt = _out_proj(o_ctx, o_lat, r, x, mods, norm_g, w_out, ln_g[l, 0], ln_b[l, 0], w_route, b_route)
        x = _moe(h2, rinfo, cnt, x1, mods, ln_g[l, 1], ln_b[l, 1], l, moe_w_gate, moe_w_up, moe_w_down)
    y_ctx = x[:N_CTX].reshape(N_CTX_SEQ, T_CTX, D)
    y_lat = x[N_CTX:].reshape(N_LAT_SEQ, T_LAT, D)
    return (y_ctx, y_lat, jnp.stack(gla_states, 1), jnp.stack(ml_c, 1), jnp.stack(ml_n, 1), jnp.stack(ml_m, 1))
```

```python
import functools
import math

import jax
import jax.numpy as jnp
from jax import lax
from jax.experimental import pallas as pl
from jax.experimental.pallas import tpu as pltpu

F32 = jnp.float32
BF16 = jnp.bfloat16

D = 1024
N_CTX_SEQ, T_CTX = 16, 256
N_LAT_SEQ, T_LAT = 8, 1024
N_CTX = N_CTX_SEQ * T_CTX
N_LAT = N_LAT_SEQ * T_LAT
N_TOK = N_CTX + N_LAT
DEPTH = 4
GRID_W = 64
NH = 4
DK = 128
DV = 256
DKT = NH * DK
DVT = NH * DV
GATE_RANK = 16
GATE_TAU = 16.0
CH = 64
N_GROUPS = 4
EPG = 8
NE = N_GROUPS * EPG
DE = 256
DN_ALPHA = (2.0 * DEPTH) ** 0.25
LN_EPS = 1e-5
MOD_ROWS = 16
TB = 512
HPB = 4
SB = 256
NCB = SB // CH
CH_SHIFT = 6
NB = N_TOK // TB
ROUTE_W = 128
NEG_INF = float("-inf")
VMEM_LIMIT = 48 * 1024 * 1024


def _cp(sem):
    return pltpu.CompilerParams(dimension_semantics=sem, vmem_limit_bytes=VMEM_LIMIT)


def _dot(a, b):
    return jnp.dot(a, b, preferred_element_type=F32)


def _dot_nt(a, b):
    return lax.dot_general(a, b, (((1,), (1,)), ((), ())), preferred_element_type=F32)


def _dot_f32(a, b):
    return jnp.dot(a, b, preferred_element_type=F32, precision=lax.Precision.HIGHEST)


def _split2(x):
    hi = x.astype(BF16)
    lo = (x - hi.astype(F32)).astype(BF16)
    return hi, lo


def _log_sigmoid(x):
    return -(jnp.maximum(-x, 0.0) + jnp.log1p(jnp.exp(-jnp.abs(x))))


def _silu(x):
    return x * jax.nn.sigmoid(x)


def _layernorm(x, g, b):
    mu = jnp.mean(x, axis=-1, keepdims=True)
    xc = x - mu
    var = jnp.mean(xc * xc, axis=-1, keepdims=True)
    return xc * lax.rsqrt(var + LN_EPS) * g + b


def _mod_row_tb(i):
    return jnp.where(i < N_CTX // TB, 0, 1 + (i - N_CTX // TB) // (T_LAT // TB))


def _mod_spec(j, row_fn):
    return pl.BlockSpec((1, 1, D), lambda i, *_: (row_fn(i) * 6 + j, 0, 0))


def _mod_kernel(c_ref, w_ref, b_ref, o_ref):
    o_ref[0] = _dot_f32(_silu(c_ref[...]), w_ref[0]) + b_ref[0]


def _modulation(cvecs, adaln_w, adaln_b):
    nj = 6
    return pl.pallas_call(
        _mod_kernel,
        out_shape=jax.ShapeDtypeStruct((DEPTH, MOD_ROWS, 6 * D), F32),
        grid=(DEPTH, nj),
        in_specs=[
            pl.BlockSpec((MOD_ROWS, D), lambda l, j: (0, 0)),
            pl.BlockSpec((1, D, D), lambda l, j: (l, 0, j)),
            pl.BlockSpec((1, 1, D), lambda l, j: (l, 0, j)),
        ],
        out_specs=pl.BlockSpec((1, MOD_ROWS, D), lambda l, j: (l, 0, j)),
        compiler_params=_cp(("parallel", "parallel")),
        name="adaln_modulation",
    )(cvecs, adaln_w, adaln_b.reshape(DEPTH, 1, 6 * D))


def _embed_kernel(xp_ref, xs_ref, pos_ref, o_ref):
    i = pl.program_id(0)

    @pl.when(i < N_CTX // TB)
    def _():
        o_ref[...] = xp_ref[...]

    @pl.when(i >= N_CTX // TB)
    def _():
        o_ref[...] = xs_ref[...] + pos_ref[...]


def _embed(x_prompt, x_sample, pos):
    nc = N_CTX // TB
    return pl.pallas_call(
        _embed_kernel,
        out_shape=jax.ShapeDtypeStruct((N_TOK, D), F32),
        grid=(NB,),
        in_specs=[
            pl.BlockSpec((TB, D), lambda i: (jnp.minimum(i, nc - 1), 0)),
            pl.BlockSpec((TB, D), lambda i: (jnp.maximum(i - nc, 0), 0)),
            pl.BlockSpec((TB, D), lambda i: (jnp.maximum(i - nc, 0) % (T_LAT // TB), 0)),
        ],
        out_specs=pl.BlockSpec((TB, D), lambda i: (i, 0)),
        compiler_params=_cp(("parallel",)),
        name="embed_tokens",
    )(x_prompt.reshape(N_CTX, D), x_sample.reshape(N_LAT, D), pos)


def _grid_posemb(T):
    rows = T // GRID_W
    r, cidx = jnp.meshgrid(jnp.arange(rows), jnp.arange(GRID_W), indexing="ij")
    nf = D // 4
    freqs = jnp.exp(-math.log(10000.0) * jnp.arange(nf, dtype=F32) / nf)

    def emb(p):
        a = p.reshape(-1).astype(F32)[:, None] * freqs
        return jnp.concatenate([jnp.sin(a), jnp.cos(a)], -1)

    return jnp.concatenate([emb(r), emb(cidx)], -1)


def _gla_in_kernel(x_ref, sh_ref, sc_ref, wq_ref, wk_ref, wv_ref, wr_ref, wg_ref, wgate_ref, bgate_ref,
                   q_ref, k_ref, v_ref, r_ref, g_ref):
    h = (x_ref[...] * (1.0 + sc_ref[0]) + sh_ref[0]).astype(BF16)
    q_ref[...] = _dot(h, wq_ref[...]) * (DK ** -0.5)
    k_ref[...] = _dot(h, wk_ref[...])
    v_ref[...] = _dot(h, wv_ref[...]).astype(BF16)
    r_ref[...] = _silu(_dot(h, wr_ref[...])).astype(BF16)
    glr = _dot(h, wg_ref[...]).astype(BF16)
    z = _dot(glr, wgate_ref[...]) + bgate_ref[...]
    g_ref[...] = _log_sigmoid(z) * (1.0 / GATE_TAU)


def _gla_in(x, mods, w_in, w_gate, b_gate):
    wq = w_in[:, :DKT].astype(BF16)
    wk = w_in[:, DKT:2 * DKT].astype(BF16)
    wv = w_in[:, 2 * DKT:2 * DKT + DVT].astype(BF16)
    wr = w_in[:, 2 * DKT + DVT:2 * DKT + 2 * DVT].astype(BF16)
    wg = w_in[:, 2 * DKT + 2 * DVT:].astype(BF16)
    wgate = jnp.zeros((2 * GATE_RANK, 2 * DKT), F32)
    wgate = wgate.at[:GATE_RANK, :DKT].set(w_gate[0]).at[GATE_RANK:, DKT:].set(w_gate[1]).astype(BF16)
    bgate = b_gate.reshape(1, 2 * DKT)
    full = lambda s: pl.BlockSpec(s, lambda i: (0,) * len(s))
    tok = lambda n: pl.BlockSpec((TB, n), lambda i: (i, 0))
    return pl.pallas_call(
        _gla_in_kernel,
        out_shape=[jax.ShapeDtypeStruct((N_TOK, DKT), F32), jax.ShapeDtypeStruct((N_TOK, DKT), F32),
                   jax.ShapeDtypeStruct((N_TOK, DVT), BF16), jax.ShapeDtypeStruct((N_TOK, DVT), BF16),
                   jax.ShapeDtypeStruct((N_TOK, 2 * DKT), F32)],
        grid=(NB,),
        in_specs=[tok(D), _mod_spec(0, _mod_row_tb), _mod_spec(1, _mod_row_tb),
                  full((D, DKT)), full((D, DKT)), full((D, DVT)), full((D, DVT)), full((D, 2 * GATE_RANK)),
                  full((2 * GATE_RANK, 2 * DKT)), full((1, 2 * DKT))],
        out_specs=[tok(DKT), tok(DKT), tok(DVT), tok(DVT), tok(2 * DKT)],
        compiler_params=_cp(("parallel",)),
        name="gla_in_proj",
    )(x, mods, mods, wq, wk, wv, wr, wg, wgate, bgate)


def _mlstm_in_kernel(x_ref, sh_ref, sc_ref, wq_ref, wk_ref, wv_ref, wr_ref, wg_ref, wgt_ref, bg_ref, bgt_ref,
                     q_ref, k_ref, v_ref, r_ref, gc_ref, gr_ref):
    h = (x_ref[...] * (1.0 + sc_ref[0]) + sh_ref[0]).astype(BF16)
    q_ref[...] = _dot(h, wq_ref[...]) * (DK ** -0.5)
    k_ref[...] = _dot(h, wk_ref[...])
    v_ref[...] = _dot(h, wv_ref[...]).astype(BF16)
    r_ref[...] = jax.nn.sigmoid(_dot(h, wr_ref[...])).astype(BF16)
    gc = _dot(h, wg_ref[...]) + bg_ref[...]
    gr = _dot_nt(wgt_ref[...], h) + bgt_ref[...]
    is_f_c = (lax.broadcasted_iota(jnp.int32, gc.shape, 1) % 2) == 1
    is_f_r = (lax.broadcasted_iota(jnp.int32, gr.shape, 0) % 2) == 1
    gc = jnp.where(is_f_c, _log_sigmoid(gc), gc)
    gr = jnp.where(is_f_r, _log_sigmoid(gr), gr)
    for hd in range(NH):
        gc_ref[hd] = gc[:, 4 * hd:4 * hd + 4]
        for c in range(TB // SB):
            gr_ref[hd, c] = gr[4 * hd:4 * hd + 4, c * SB:(c + 1) * SB]


def _mlstm_in(x, mods, w_in, b_gates):
    wq = w_in[:, :DKT].astype(BF16)
    wk = w_in[:, DKT:2 * DKT].astype(BF16)
    wv = w_in[:, 2 * DKT:2 * DKT + DVT].astype(BF16)
    wr = w_in[:, 2 * DKT + DVT:2 * DKT + 2 * DVT].astype(BF16)
    wg = w_in[:, 2 * DKT + 2 * DVT:].reshape(D, 4, NH).transpose(0, 2, 1).reshape(D, 4 * NH).astype(BF16)
    bg = b_gates.reshape(4, NH).T.reshape(1, 4 * NH)
    full = lambda s: pl.BlockSpec(s, lambda i: (0,) * len(s))
    tok = lambda n: pl.BlockSpec((TB, n), lambda i: (i, 0))
    return pl.pallas_call(
        _mlstm_in_kernel,
        out_shape=[jax.ShapeDtypeStruct((N_TOK, DKT), F32), jax.ShapeDtypeStruct((N_TOK, DKT), F32),
                   jax.ShapeDtypeStruct((N_TOK, DVT), BF16), jax.ShapeDtypeStruct((N_TOK, DVT), BF16),
                   jax.ShapeDtypeStruct((NH, N_TOK, 4), F32),
                   jax.ShapeDtypeStruct((NH, N_TOK // SB, 4, SB), F32)],
        grid=(NB,),
        in_specs=[tok(D), _mod_spec(0, _mod_row_tb), _mod_spec(1, _mod_row_tb),
                  full((D, DKT)), full((D, DKT)), full((D, DVT)), full((D, DVT)), full((D, 4 * NH)),
                  full((4 * NH, D)), full((1, 4 * NH)), full((4 * NH, 1))],
        out_specs=[tok(DKT), tok(DKT), tok(DVT), tok(DVT),
                   pl.BlockSpec((NH, TB, 4), lambda i: (0, i, 0)),
                   pl.BlockSpec((NH, TB // SB, 4, SB), lambda i: (0, i, 0, 0))],
        compiler_params=_cp(("parallel",)),
        name="mlstm_in_proj",
    )(x, mods, mods, wq, wk, wv, wr, wg, wg.T, bg, bg.T)


def _block_tri_masks():
    row = lax.broadcasted_iota(jnp.int32, (SB, SB), 0)
    col = lax.broadcasted_iota(jnp.int32, (SB, SB), 1)
    same = (row >> CH_SHIFT) == (col >> CH_SHIFT)
    return same & (row >= col), same & (row <= col)


def _chunk_scan(x, forward, axis=0, op=jnp.add, fill=0.0):
    n = x.shape[axis]
    pos = lax.broadcasted_iota(jnp.int32, x.shape, axis) & (CH - 1)
    s = 1
    while s < CH:
        if forward:
            x = op(x, jnp.where(pos >= s, pltpu.roll(x, s, axis), fill))
        else:
            x = op(x, jnp.where(pos < CH - s, pltpu.roll(x, n - s, axis), fill))
        s *= 2
    return x


def _chunk_cumsum(x, forward):
    return _chunk_scan(x, forward)


def _cat2(x):
    return jnp.concatenate([x, x], axis=1)


def _per_chunk(vals, n):
    return jnp.concatenate([jnp.broadcast_to(v, (CH, n)) for v in vals], axis=0)


def _chunk_masked_cat(x):
    rowc = lax.broadcasted_iota(jnp.int32, x.shape, 0) >> CH_SHIFT
    return jnp.concatenate([jnp.where(rowc == c, x, 0.0).astype(BF16) for c in range(NCB)], axis=1)


def _head_norm_rows(o):
    mu = jnp.mean(o, axis=-1, keepdims=True)
    oc = o - mu
    var = jnp.mean(oc * oc, axis=-1, keepdims=True)
    return oc * lax.rsqrt(var + LN_EPS)


def _gla_superblock(q, k, v, v_t, g, st, forward, mask):
    b = _chunk_cumsum(g, forward)
    r_last = CH - 1 if forward else 0
    r_mid = CH // 2 if forward else CH - 1 - CH // 2
    lasts = [b[c * CH + r_last:c * CH + r_last + 1, :] for c in range(NCB)]
    b_last = _per_chunk(lasts, DK)
    b_mid = _per_chunk([b[c * CH + r_mid:c * CH + r_mid + 1, :] for c in range(NCB)], DK)
    qe = (q * jnp.exp(b - b_mid)).astype(BF16)
    ke = (k * jnp.exp(b_mid - b)).astype(BF16)
    qb = (q * jnp.exp(b)).astype(BF16)
    kd = k * jnp.exp(b_last - b)
    a = jnp.where(mask, _dot_nt(qe, ke), 0.0).astype(BF16)
    o_intra = _dot(a, v)
    d_t = _dot(v_t, _chunk_masked_cat(kd))
    inter = [None] * NCB
    for c in (range(NCB) if forward else reversed(range(NCB))):
        inter[c] = _dot_nt(qb[c * CH:(c + 1) * CH, :], st.astype(BF16))
        st = st * jnp.exp(lasts[c]) + d_t[:, c * DK:(c + 1) * DK]
    return o_intra + jnp.concatenate(inter, axis=0), st


def _scan_driver(T, o_ref, of_ref, ob_ref, step):
    nsb = T // SB
    if nsb == 1:
        step(0, 0)
    else:
        def body(i, carry):
            step(i, nsb - 1 - i)
            return carry

        lax.fori_loop(0, nsb, body, 0)

    def norm_body(j, carry):
        rows = pl.ds(pl.multiple_of(j * SB, SB), SB)
        for hh in range(HPB):
            cv = slice(hh * DV, (hh + 1) * DV)
            o_ref[rows, cv] = _head_norm_rows(of_ref[rows, cv] + ob_ref[rows, cv]).astype(BF16)
        return carry

    if nsb == 1:
        norm_body(0, 0)
    else:
        lax.fori_loop(0, nsb, norm_body, 0)


def _gla_scan_kernel(*refs, T, has_state, emit_state):
    q_ref, k_ref, v_ref, gf_ref, gb_ref = refs[:5]
    pos = 5
    if has_state:
        s0f_ref, s0b_ref = refs[pos:pos + 2]
        pos += 2
    o_ref = refs[pos]
    pos += 1
    if emit_state:
        sout_ref = refs[pos]
        pos += 1
    stf_ref, stb_ref, vt_ref, of_ref, ob_ref = refs[pos:pos + 5]

    nsb = T // SB
    for hh in range(HPB):
        for j in range(nsb):
            vt_ref[hh, j] = v_ref[j * SB:(j + 1) * SB, hh * DV:(hh + 1) * DV].astype(F32).T.astype(BF16)
        if has_state:
            stf_ref[hh] = s0f_ref[hh].T
            stb_ref[hh] = s0b_ref[hh].T
        else:
            stf_ref[hh] = jnp.zeros((DV, DK), F32)
            stb_ref[hh] = jnp.zeros((DV, DK), F32)

    lower, upper = _block_tri_masks()

    def one(j, forward, hh):
        rows = pl.ds(pl.multiple_of(j * SB, SB), SB)
        ck = slice(hh * DK, (hh + 1) * DK)
        cv = slice(hh * DV, (hh + 1) * DV)
        st_ref = stf_ref if forward else stb_ref
        o, st = _gla_superblock(q_ref[rows, ck], k_ref[rows, ck], v_ref[rows, cv].astype(BF16), vt_ref[hh, j],
                                (gf_ref if forward else gb_ref)[rows, ck], st_ref[hh], forward,
                                lower if forward else upper)
        st_ref[hh] = st
        (of_ref if forward else ob_ref)[rows, cv] = o

    def step(jf, jb):
        for hh in range(HPB):
            one(jf, True, hh)
            one(jb, False, hh)

    _scan_driver(T, o_ref, of_ref, ob_ref, step)
    if emit_state:
        for hh in range(HPB):
            sout_ref[0, hh] = stf_ref[hh].T
            sout_ref[1, hh] = stb_ref[hh].T


def _gla_scan(q, k, v, g, state, slot):
    ctx = state is None
    T = T_CTX if ctx else T_LAT
    n_seq = N_CTX_SEQ if ctx else N_LAT_SEQ
    off = 0 if ctx else N_CTX // T_LAT
    in_specs = [
        pl.BlockSpec((T, HPB * DK), lambda b, h: (off + b, h)),
        pl.BlockSpec((T, HPB * DK), lambda b, h: (off + b, h)),
        pl.BlockSpec((T, HPB * DV), lambda b, h: (off + b, h)),
        pl.BlockSpec((T, HPB * DK), lambda b, h: (off + b, h)),
        pl.BlockSpec((T, HPB * DK), lambda b, h: (off + b, NH // HPB + h)),
    ]
    args = [q, k, v, g, g]
    out_shape = [jax.ShapeDtypeStruct((n_seq * T, DVT), BF16)]
    out_specs = [pl.BlockSpec((T, HPB * DV), lambda b, h: (b, h))]
    if ctx:
        out_shape.append(jax.ShapeDtypeStruct((N_CTX_SEQ, 2, NH, DK, DV), F32))
        out_specs.append(pl.BlockSpec((None, 2, HPB, DK, DV), lambda b, h: (b, 0, h, 0, 0)))
    else:
        sq = (None, None, None, HPB, DK, DV)
        in_specs += [pl.BlockSpec(sq, lambda b, h: (b, slot, 0, h, 0, 0)),
                     pl.BlockSpec(sq, lambda b, h: (b, slot, 1, h, 0, 0))]
        args += [state, state]
    res = pl.pallas_call(
        functools.partial(_gla_scan_kernel, T=T, has_state=not ctx, emit_state=ctx),
        out_shape=out_shape,
        grid=(n_seq, NH // HPB),
        in_specs=in_specs,
        out_specs=out_specs,
        scratch_shapes=[pltpu.VMEM((HPB, DV, DK), F32), pltpu.VMEM((HPB, DV, DK), F32),
                        pltpu.VMEM((HPB, T // SB, DV, SB), BF16), pltpu.VMEM((T, HPB * DV), F32), pltpu.VMEM((T, HPB * DV), F32)],
        compiler_params=_cp(("parallel", "parallel")),
        name="gla_scan_ctx" if ctx else "gla_scan_lat",
    )(*args)
    return res


def _mlstm_scan_kernel(*refs, T, has_state, emit_state):
    q_ref, k_ref, v_ref, gc_ref, gr_ref = refs[:5]
    pos = 5
    if has_state:
        c0f_ref, c0b_ref, n0f_ref, n0b_ref, m0f_ref, m0b_ref = refs[pos:pos + 6]
        pos += 6
    o_ref = refs[pos]
    pos += 1
    if emit_state:
        cout_ref, nout_ref, mout_ref = refs[pos:pos + 3]
        pos += 3
    ct_ref, n_ref, m_ref, vt_ref, of_ref, ob_ref = refs[pos:pos + 6]

    nsb = T // SB
    for hh in range(HPB):
        for j in range(nsb):
            vt_ref[hh, j] = v_ref[j * SB:(j + 1) * SB, hh * DV:(hh + 1) * DV].astype(F32).T.astype(BF16)
        for d in range(2):
            sl = d * HPB + hh
            if has_state:
                c0, n0, m0 = ((c0f_ref, n0f_ref, m0f_ref), (c0b_ref, n0b_ref, m0b_ref))[d]
                ct_ref[sl] = c0[hh].T
                n_ref[sl] = n0[hh]
                m_ref[sl] = m0[hh]
            else:
                ct_ref[sl] = jnp.zeros((DV, DK), F32)
                n_ref[sl] = jnp.zeros((1, DK), F32)
                m_ref[sl] = jnp.zeros((1, DK), F32)

    lower, upper = _block_tri_masks()

    def one(j, forward, hh):
        d = 0 if forward else 1
        sl = d * HPB + hh
        rows = pl.ds(pl.multiple_of(j * SB, SB), SB)
        q = q_ref[rows, hh * DK:(hh + 1) * DK]
        k = k_ref[rows, hh * DK:(hh + 1) * DK]
        v = v_ref[rows, hh * DV:(hh + 1) * DV].astype(BF16)
        gc = gc_ref[hh, rows, :]
        gr = gr_ref[hh, j]
        o0 = 2 * d
        mask = lower if forward else upper
        order = list(range(NCB)) if forward else list(reversed(range(NCB)))
        r_last = CH - 1 if forward else 0
        i_c = jnp.broadcast_to(gc[:, o0:o0 + 1], (SB, DK))
        f_c = jnp.broadcast_to(gc[:, o0 + 1:o0 + 2], (SB, DK))
        b_c = _chunk_scan(f_c, forward)
        m_intra = b_c + _chunk_scan(i_c - b_c, forward, op=jnp.maximum, fill=NEG_INF)
        b_last = [b_c[c * CH + r_last:c * CH + r_last + 1, :] for c in range(NCB)]
        a_c = _per_chunk(b_last, DK) - b_c + i_c
        m_loc = [jnp.max(a_c[c * CH:(c + 1) * CH, :], axis=0, keepdims=True) for c in range(NCB)]
        kw = k * jnp.exp(a_c - _per_chunk(m_loc, DK))
        kv_t = _dot(vt_ref[hh, j], _chunk_masked_cat(kw))
        k_loc = [jnp.sum(kw[c * CH:(c + 1) * CH, :], axis=0, keepdims=True) for c in range(NCB)]

        m = m_ref[sl]
        m_start, s_old, s_loc = [None] * NCB, [None] * NCB, [None] * NCB
        for c in order:
            m_start[c] = m
            m_new = jnp.maximum(b_last[c] + m, m_loc[c])
            s_old[c] = jnp.exp(b_last[c] + m - m_new)
            s_loc[c] = jnp.exp(m_loc[c] - m_new)
            m = m_new
        m_ref[sl] = m

        l_inter = b_c + _per_chunk(m_start, DK)
        m_i = jnp.maximum(l_inter, m_intra)
        gcum = _chunk_scan(gr, forward, axis=1)
        brow = gcum[o0 + 1:o0 + 2, :] - gr[o0:o0 + 1, :]
        d_log = jnp.where(mask, _cat2(b_c - m_i) - brow, NEG_INF)
        qb = q.astype(BF16)
        s = (_dot_nt(qb, k.astype(BF16)) * jnp.exp(d_log)).astype(BF16)
        e_inter = jnp.exp(l_inter - m_i)
        nd = _dot(s, jnp.concatenate([v, jnp.ones((SB, DK), BF16)], axis=1))
        num, den = nd[:, :DV], nd[:, DV:]

        ct = ct_ref[sl]
        nrm = n_ref[sl]
        inter = [None] * NCB
        for c in order:
            state = jnp.concatenate([ct, jnp.broadcast_to(nrm, (DK, DK))], axis=0).astype(BF16)
            inter[c] = _dot_nt(qb[c * CH:(c + 1) * CH, :], state)
            ct = s_old[c] * ct + s_loc[c] * kv_t[:, c * DK:(c + 1) * DK]
            nrm = s_old[c] * nrm + s_loc[c] * k_loc[c]
        ct_ref[sl] = ct
        n_ref[sl] = nrm

        inter = jnp.concatenate(inter, axis=0)
        num = num + _cat2(e_inter) * inter[:, :DV]
        den = den + e_inter * inter[:, DV:]
        inv = 1.0 / jnp.maximum(jnp.abs(den), jnp.exp(-m_i))
        (of_ref if forward else ob_ref)[rows, hh * DV:(hh + 1) * DV] = num * _cat2(inv)

    def step(jf, jb):
        for hh in range(HPB):
            one(jf, True, hh)
            one(jb, False, hh)

    _scan_driver(T, o_ref, of_ref, ob_ref, step)
    if emit_state:
        for d in range(2):
            for hh in range(HPB):
                cout_ref[d, hh] = ct_ref[d * HPB + hh].T
                nout_ref[d, hh] = n_ref[d * HPB + hh]
                mout_ref[d, hh] = m_ref[d * HPB + hh]


def _mlstm_scan(q, k, v, gc, gr, states, slot):
    ctx = states is None
    T = T_CTX if ctx else T_LAT
    n_seq = N_CTX_SEQ if ctx else N_LAT_SEQ
    off = 0 if ctx else N_CTX // T_LAT
    in_specs = [
        pl.BlockSpec((T, HPB * DK), lambda b, h: (off + b, h)),
        pl.BlockSpec((T, HPB * DK), lambda b, h: (off + b, h)),
        pl.BlockSpec((T, HPB * DV), lambda b, h: (off + b, h)),
        pl.BlockSpec((HPB, T, 4), lambda b, h: (h, off + b, 0)),
        pl.BlockSpec((HPB, T // SB, 4, SB), lambda b, h: (h, off + b, 0, 0)),
    ]
    args = [q, k, v, gc, gr]
    out_shape = [jax.ShapeDtypeStruct((n_seq * T, DVT), BF16)]
    out_specs = [pl.BlockSpec((T, HPB * DV), lambda b, h: (b, h))]
    if ctx:
        out_shape += [jax.ShapeDtypeStruct((N_CTX_SEQ, 2, NH, DK, DV), F32),
                      jax.ShapeDtypeStruct((N_CTX_SEQ, 2, NH, 1, DK), F32),
                      jax.ShapeDtypeStruct((N_CTX_SEQ, 2, NH, 1, DK), F32)]
        out_specs += [pl.BlockSpec((None, 2, HPB, DK, DV), lambda b, h: (b, 0, h, 0, 0)),
                      pl.BlockSpec((None, 2, HPB, 1, DK), lambda b, h: (b, 0, h, 0, 0)),
                      pl.BlockSpec((None, 2, HPB, 1, DK), lambda b, h: (b, 0, h, 0, 0))]
    else:
        c0, n0, m0 = states
        n0 = n0.reshape(N_LAT_SEQ, -1, 2, NH, 1, DK)
        m0 = jnp.broadcast_to(m0[..., None, None], m0.shape + (1, DK))
        sq_c = (None, None, None, HPB, DK, DV)
        sq_v = (None, None, None, HPB, 1, DK)
        for arr, sq in ((c0, sq_c), (n0, sq_v), (m0, sq_v)):
            for d in range(2):
                in_specs.append(pl.BlockSpec(sq, functools.partial(lambda b, h, d: (b, slot, d, h, 0, 0), d=d)))
                args.append(arr)
    return pl.pallas_call(
        functools.partial(_mlstm_scan_kernel, T=T, has_state=not ctx, emit_state=ctx),
        out_shape=out_shape,
        grid=(n_seq, NH // HPB),
        in_specs=in_specs,
        out_specs=out_specs,
        scratch_shapes=[pltpu.VMEM((2 * HPB, DV, DK), F32), pltpu.VMEM((2 * HPB, 1, DK), F32),
                        pltpu.VMEM((2 * HPB, 1, DK), F32),
                        pltpu.VMEM((HPB, T // SB, DV, SB), BF16), pltpu.VMEM((T, HPB * DV), F32), pltpu.VMEM((T, HPB * DV), F32)],
        compiler_params=_cp(("parallel", "parallel")),
        name="mlstm_scan_ctx" if ctx else "mlstm_scan_lat",
    )(*args)


def _route(lg):
    lane = lax.broadcasted_iota(jnp.int32, lg.shape, 1)
    big = jnp.int32(ROUTE_W)
    is_g = (lane >= NE) & (lane < NE + N_GROUPS)
    gl = jnp.where(is_g, lg, NEG_INF)
    gmax = jnp.max(gl, axis=1, keepdims=True)
    gsel = jnp.min(jnp.where(gl == gmax, lane, big), axis=1, keepdims=True) - NE
    gw = 1.0 / jnp.sum(jnp.exp(gl - gmax), axis=1, keepdims=True)
    ing = (lane < NE) & ((lane >> 3) == gsel)
    el = jnp.where(ing, lg, NEG_INF)
    emax = jnp.max(el, axis=1, keepdims=True)
    p = jnp.exp(el - emax)
    prob = p / jnp.sum(p, axis=1, keepdims=True)
    p1 = jnp.max(prob, axis=1, keepdims=True)
    i1 = jnp.min(jnp.where(ing & (prob == p1), lane, big), axis=1, keepdims=True)
    rest = ing & (lane != i1)
    prob2 = jnp.where(rest, prob, -1.0)
    p2 = jnp.max(prob2, axis=1, keepdims=True)
    i2 = jnp.min(jnp.where(rest & (prob2 == p2), lane, big), axis=1, keepdims=True)
    tot = p1 + p2
    rinfo = jnp.where(lane == 0, i1.astype(F32),
                      jnp.where(lane == 1, i2.astype(F32),
                                jnp.where(lane == 2, gw * (p1 / tot), jnp.where(lane == 3, gw * (p2 / tot), 0.0))))
    count = jnp.sum(jnp.where((lane == i1) | (lane == i2), 1.0, 0.0), axis=0, keepdims=True)
    return rinfo, count


def _out_kernel(oc_ref, ol_ref, r_ref, x_ref, gate_ref, sh_ref, sc_ref, ng_ref, wo_ref, lg_ref, lb_ref, wr_ref,
                br_ref, x1_ref, h2_ref, rinfo_ref, cnt_ref):
    o = jnp.where(pl.program_id(0) < N_CTX // TB, oc_ref[...], ol_ref[...])
    y = _dot((o * ng_ref[...] * r_ref[...]).astype(BF16), wo_ref[...])
    x1 = _layernorm(DN_ALPHA * x_ref[...] + gate_ref[0] * y, lg_ref[...], lb_ref[...])
    x1_ref[...] = x1
    h2 = x1 * (1.0 + sc_ref[0]) + sh_ref[0]
    h_hi, h_lo = _split2(h2)
    h2_ref[...] = h_hi
    t = _dot(h_hi, wr_ref[...])
    lg = t[:, :ROUTE_W] + t[:, ROUTE_W:] + _dot(h_lo, wr_ref[:, :ROUTE_W]) + br_ref[...]
    rinfo_ref[...], cnt_ref[0] = _route(lg)


def _out_proj(o_ctx, o_lat, r, x, mods, norm_g, w_out, ln_g, ln_b, w_route, b_route):
    nc = N_CTX // TB
    full = lambda s: pl.BlockSpec(s, lambda i: (0,) * len(s))
    tok = lambda n: pl.BlockSpec((TB, n), lambda i: (i, 0))
    return pl.pallas_call(
        _out_kernel,
        out_shape=[jax.ShapeDtypeStruct((N_TOK, D), F32), jax.ShapeDtypeStruct((N_TOK, D), BF16),
                   jax.ShapeDtypeStruct((N_TOK, ROUTE_W), F32), jax.ShapeDtypeStruct((NB, 1, ROUTE_W), F32)],
        grid=(NB,),
        in_specs=[pl.BlockSpec((TB, DVT), lambda i: (jnp.minimum(i, nc - 1), 0)),
                  pl.BlockSpec((TB, DVT), lambda i: (jnp.maximum(i - nc, 0), 0)),
                  tok(DVT), tok(D),
                  _mod_spec(2, _mod_row_tb), _mod_spec(3, _mod_row_tb), _mod_spec(4, _mod_row_tb),
                  full((1, DVT)), full((DVT, D)), full((1, D)), full((1, D)),
                  full((D, 2 * ROUTE_W)), full((1, ROUTE_W))],
        out_specs=[tok(D), tok(D), tok(ROUTE_W), pl.BlockSpec((1, 1, ROUTE_W), lambda i: (i, 0, 0))],
        compiler_params=_cp(("parallel",)),
        name="out_proj_route",
    )(o_ctx, o_lat, r, x, mods, mods, mods, norm_g.reshape(1, DVT), w_out.astype(BF16),
      ln_g.reshape(1, D), ln_b.reshape(1, D), jnp.concatenate(_split2(w_route), axis=1), b_route)


TP = 512
NBP = N_TOK // TP
SEG = 16
R_LOC = 1536
PT = 256
XW = D + ROUTE_W
TMS = 512
R_TOT = -(-(2 * N_TOK + NBP * NE * (SEG - 1) + NE * (TMS - 1)) // TMS) * TMS
NT = R_TOT // TMS
POS_SPLIT = 64.0


def _mod_row_tp(i):
    return jnp.where(i < N_CTX // TP, 0, 1 + (i - N_CTX // TP) // (T_LAT // TP))


def _lane_pack(cols, shape):
    lane = lax.broadcasted_iota(jnp.int32, shape, 1)
    out = jnp.zeros(shape, F32)
    for j, c in enumerate(cols):
        out = jnp.where(lane == j, c, out)
    return out


WAIT_CHUNKS = (32, 4, 1)
MAXC = R_LOC // SEG
ZROWS = WAIT_CHUNKS[0] * SEG
NZ = NE + 1


def _segment_start(gtab_ref, ntot_ref, blk, make_copy):
    def per_chunk(j, carry):
        g = gtab_ref[blk * MAXC + j]
        make_copy(pl.multiple_of(j * SEG, SEG), pl.multiple_of(g, SEG), SEG).start()
        return carry

    lax.fori_loop(0, ntot_ref[blk], per_chunk, 0)


def _segment_wait(n_chunks, make_copy):
    left = n_chunks
    for chunks in WAIT_CHUNKS:
        n_wait = left // chunks

        def wait_piece(k, c, chunks=chunks):
            make_copy(0, 0, chunks * SEG).wait()
            return c

        lax.fori_loop(0, n_wait, wait_piece, 0)
        left = left - n_wait * chunks


def _perm_kernel(gtab_ref, ntot_ref, zrow_ref, zcnt_ref, h_ref, ri_ref, loffv_ref, pos_ref, xs_ref,
                 xs_scr, zero_scr, sem, zsem):
    blk = pl.program_id(0)
    slot = blk % 2

    def zero_copy(_, g, n):
        return pltpu.make_async_copy(zero_scr.at[pl.ds(0, n)], xs_ref.at[pl.ds(g, n)], zsem)

    @pl.when(blk == 0)
    def _():
        zero_scr[...] = jnp.zeros_like(zero_scr)

        def per_range(r, carry):
            n = zcnt_ref[r]
            g0 = zrow_ref[r]
            n_big = n // WAIT_CHUNKS[0]

            def per_piece(k, c, first, chunks):
                zero_copy(0, pl.multiple_of(g0 + (first + k * chunks) * SEG, SEG), chunks * SEG).start()
                return c

            lax.fori_loop(0, n_big, functools.partial(per_piece, first=0, chunks=WAIT_CHUNKS[0]), 0)
            lax.fori_loop(0, n - n_big * WAIT_CHUNKS[0],
                          functools.partial(per_piece, first=n_big * WAIT_CHUNKS[0], chunks=1), 0)
            return carry + n

        lax.fori_loop(0, NZ, per_range, 0)
    ri = ri_ref[...]
    e_a, e_b, w_a, w_b = ri[:, 0:1], ri[:, 1:2], ri[:, 2:3], ri[:, 3:4]
    lanef = lax.broadcasted_iota(jnp.int32, ri.shape, 1).astype(F32)
    is_a = lanef == e_a
    is_b = lanef == e_b
    sel = jnp.where(is_a | is_b, 1.0, 0.0).astype(BF16)
    row = lax.broadcasted_iota(jnp.int32, (TP, TP), 0)
    col = lax.broadcasted_iota(jnp.int32, (TP, TP), 1)
    earlier = jnp.where(row > col, 1.0, 0.0).astype(BF16)
    lpos = loffv_ref[0] + _dot(earlier, sel)
    pos_a = jnp.sum(jnp.where(is_a, lpos, 0.0), axis=1, keepdims=True)
    pos_b = jnp.sum(jnp.where(is_b, lpos, 0.0), axis=1, keepdims=True)
    pos_ref[...] = _lane_pack([pos_a, pos_b], ri.shape)

    hi_a = jnp.floor(pos_a * (1.0 / POS_SPLIT))
    hi_b = jnp.floor(pos_b * (1.0 / POS_SPLIT))
    parts = _lane_pack([hi_a, pos_a - POS_SPLIT * hi_a, hi_b, pos_b - POS_SPLIT * hi_b], ri.shape).astype(BF16)
    pick = jnp.where(lax.broadcasted_iota(jnp.int32, (8, ROUTE_W), 0) == lax.broadcasted_iota(jnp.int32, (8, ROUTE_W), 1),
                     1.0, 0.0).astype(BF16)
    pr = _dot_nt(pick, parts)
    pos_a_r = POS_SPLIT * pr[0:1, :] + pr[1:2, :]
    pos_b_r = POS_SPLIT * pr[2:3, :] + pr[3:4, :]

    wa1, wa2 = _split2(w_a)
    wa3 = (w_a - wa1.astype(F32) - wa2.astype(F32))
    wb1, wb2 = _split2(w_b)
    wb3 = (w_b - wb1.astype(F32) - wb2.astype(F32))
    wl = _lane_pack([wa1.astype(F32), wa2.astype(F32), wa3, wb1.astype(F32), wb2.astype(F32), wb3, e_a],
                    ri.shape).astype(BF16)
    hcat = jnp.concatenate([h_ref[...], wl], axis=1)
    for t in range(R_LOC // PT):
        rio = (lax.broadcasted_iota(jnp.int32, (PT, TP), 0) + t * PT).astype(F32)
        onehot = jnp.where((rio == pos_a_r) | (rio == pos_b_r), 1.0, 0.0).astype(BF16)
        xs_scr[slot, t * PT:(t + 1) * PT, :] = _dot(onehot, hcat).astype(BF16)

    def copy_from(s):
        return lambda l, g, n: pltpu.make_async_copy(xs_scr.at[s, pl.ds(l, n)], xs_ref.at[pl.ds(g, n)], sem.at[s])

    _segment_start(gtab_ref, ntot_ref, blk, copy_from(slot))

    @pl.when(blk > 0)
    def _():
        _segment_wait(ntot_ref[blk - 1], copy_from(1 - slot))

    @pl.when(blk == NBP - 1)
    def _():
        _segment_wait(ntot_ref[blk], copy_from(slot))
        _segment_wait(lax.fori_loop(0, NZ, lambda r, c: c + zcnt_ref[r], 0), zero_copy)


def _moe_permute(h2, rinfo, tables):
    gtab, ntot, zrow, zcnt, loffv = tables
    grid_spec = pltpu.PrefetchScalarGridSpec(
        num_scalar_prefetch=4,
        grid=(NBP,),
        in_specs=[pl.BlockSpec((TP, D), lambda i, *_: (i, 0)),
                  pl.BlockSpec((TP, ROUTE_W), lambda i, *_: (i, 0)),
                  pl.BlockSpec((1, 1, ROUTE_W), lambda i, *_: (i, 0, 0))],
        out_specs=[pl.BlockSpec((TP, ROUTE_W), lambda i, *_: (i, 0)),
                   pl.BlockSpec(memory_space=pl.ANY)],
        scratch_shapes=[pltpu.VMEM((2, R_LOC, XW), BF16), pltpu.VMEM((ZROWS, XW), BF16),
                        pltpu.SemaphoreType.DMA((2,)), pltpu.SemaphoreType.DMA],
    )
    return pl.pallas_call(
        _perm_kernel,
        out_shape=[jax.ShapeDtypeStruct((N_TOK, ROUTE_W), F32), jax.ShapeDtypeStruct((R_TOT, XW), BF16)],
        grid_spec=grid_spec,
        compiler_params=_cp(("arbitrary",)),
        name="moe_permute",
    )(gtab, ntot, zrow, zcnt, h2, rinfo, loffv)


def _ffn_kernel(te_ref, nt_ref, xs_ref, wg32_ref, wu32_ref, wd32_ref, ys_ref, wg_ref, wu_ref, wd_ref):
    i = pl.program_id(0)
    active = i < nt_ref[0]
    new_expert = (i == 0) | (te_ref[i] != te_ref[jnp.maximum(i - 1, 0)])

    @pl.when(active & new_expert)
    def _():
        wg_ref[0] = wg32_ref[0].astype(BF16)
        wu_ref[0] = wu32_ref[0].astype(BF16)
        wd_ref[0] = wd32_ref[0].astype(BF16)

    @pl.when(active)
    def _():
        e = te_ref[i].astype(F32)
        xs = xs_ref[...]
        x = xs[:, :D]
        r = xs[:, D:].astype(F32)
        w_a = r[:, 0:1] + r[:, 1:2] + r[:, 2:3]
        w_b = r[:, 3:4] + r[:, 4:5] + r[:, 5:6]
        w = jnp.where(r[:, 6:7] == e, w_a, w_b)
        hid = _silu(_dot(x, wg_ref[0])) * _dot(x, wu_ref[0]) * w
        ys_ref[...] = _dot(hid.astype(BF16), wd_ref[0]).astype(BF16)

    @pl.when(i >= nt_ref[0])
    def _():
        ys_ref[...] = jnp.zeros_like(ys_ref)


def _moe_ffn(xs, tile_expert, n_tiles, layer, w_gate, w_up, w_down):
    last = lambda i, nt: jnp.minimum(i, nt[0] - 1)
    wspec = lambda a, b: pl.BlockSpec((None, 1, a, b), lambda i, te, nt: (layer, te[last(i, nt)], 0, 0))
    grid_spec = pltpu.PrefetchScalarGridSpec(
        num_scalar_prefetch=2,
        grid=(NT,),
        in_specs=[pl.BlockSpec((TMS, XW), lambda i, te, nt: (last(i, nt), 0)),
                  wspec(D, DE), wspec(D, DE), wspec(DE, D)],
        out_specs=pl.BlockSpec((TMS, D), lambda i, te, nt: (i, 0)),
        scratch_shapes=[pltpu.VMEM((1, D, DE), BF16), pltpu.VMEM((1, D, DE), BF16), pltpu.VMEM((1, DE, D), BF16)],
    )
    return pl.pallas_call(
        _ffn_kernel,
        out_shape=jax.ShapeDtypeStruct((R_TOT, D), BF16),
        grid_spec=grid_spec,
        compiler_params=_cp(("arbitrary",)),
        name="moe_ffn",
    )(tile_expert, n_tiles, xs, w_gate, w_up, w_down)


def _unperm_kernel(gtab_ref, ntot_ref, pos_ref, x1_ref, gate_ref, lg_ref, lb_ref, ys_ref, *rest, split):
    if split:
        oc_ref, ol_ref, ys_scr, sem = rest
    else:
        o_ref, ys_scr, sem = rest
    blk = pl.program_id(0)
    slot = blk % 2

    def copy_to(s):
        return lambda l, g, n: pltpu.make_async_copy(ys_ref.at[pl.ds(g, n)], ys_scr.at[s, pl.ds(l, n)], sem.at[s])

    def fetch(b, s):
        ys_scr[s] = jnp.zeros((R_LOC, D), BF16)
        _segment_start(gtab_ref, ntot_ref, b, copy_to(s))

    @pl.when(blk == 0)
    def _():
        fetch(0, 0)

    @pl.when(blk + 1 < NBP)
    def _():
        fetch(blk + 1, 1 - slot)

    _segment_wait(ntot_ref[blk], copy_to(slot))
    pos = pos_ref[...]
    pos_a, pos_b = pos[:, 0:1], pos[:, 1:2]
    acc = jnp.zeros((TP, D), F32)
    for t in range(R_LOC // PT):
        lio = (lax.broadcasted_iota(jnp.int32, (TP, PT), 1) + t * PT).astype(F32)
        onehot = jnp.where((lio == pos_a) | (lio == pos_b), 1.0, 0.0).astype(BF16)
        acc = acc + _dot(onehot, ys_scr[slot, t * PT:(t + 1) * PT, :])
    res = _layernorm(DN_ALPHA * x1_ref[...] + gate_ref[0] * acc, lg_ref[...], lb_ref[...])
    if split:
        @pl.when(blk < N_CTX // TP)
        def _():
            oc_ref[...] = res

        @pl.when(blk >= N_CTX // TP)
        def _():
            ol_ref[...] = res
    else:
        o_ref[...] = res


def _moe_unpermute(ys, pos, x1, mods, ln_g, ln_b, tables, split):
    gtab, ntot = tables[:2]
    nc = N_CTX // TP
    if split:
        out_shape = [jax.ShapeDtypeStruct((N_CTX, D), F32), jax.ShapeDtypeStruct((N_LAT, D), F32)]
        out_specs = [pl.BlockSpec((TP, D), lambda i, *_: (jnp.minimum(i, nc - 1), 0)),
                     pl.BlockSpec((TP, D), lambda i, *_: (jnp.maximum(i - nc, 0), 0))]
    else:
        out_shape = jax.ShapeDtypeStruct((N_TOK, D), F32)
        out_specs = pl.BlockSpec((TP, D), lambda i, *_: (i, 0))
    grid_spec = pltpu.PrefetchScalarGridSpec(
        num_scalar_prefetch=2,
        grid=(NBP,),
        in_specs=[pl.BlockSpec((TP, ROUTE_W), lambda i, *_: (i, 0)),
                  pl.BlockSpec((TP, D), lambda i, *_: (i, 0)),
                  _mod_spec(5, _mod_row_tp),
                  pl.BlockSpec((1, D), lambda i, *_: (0, 0)),
                  pl.BlockSpec((1, D), lambda i, *_: (0, 0)),
                  pl.BlockSpec(memory_space=pl.ANY)],
        out_specs=out_specs,
        scratch_shapes=[pltpu.VMEM((2, R_LOC, D), BF16), pltpu.SemaphoreType.DMA((2,))],
    )
    return pl.pallas_call(
        functools.partial(_unperm_kernel, split=split),
        out_shape=out_shape,
        grid_spec=grid_spec,
        compiler_params=_cp(("arbitrary",)),
        name="moe_unpermute",
    )(gtab, ntot, pos, x1, mods, ln_g.reshape(1, D), ln_b.reshape(1, D), ys)


def _segment_tables(cnt):
    c = cnt[:, 0, :NE].astype(jnp.int32).reshape(NBP, TP // TB, NE).sum(axis=1)
    pc = (c + SEG - 1) // SEG * SEG
    loff = jnp.cumsum(pc, axis=1) - pc
    tot = pc.sum(axis=0)
    totp = (tot + TMS - 1) // TMS * TMS
    eend = jnp.cumsum(totp)
    goff = (eend - totp)[None, :] + jnp.cumsum(pc, axis=0) - pc
    n_tiles = (eend[-1] // TMS).reshape(1)
    tile_id = jnp.arange(NT, dtype=jnp.int32)
    tile_expert = jnp.minimum(jnp.sum(tile_id[:, None] >= (eend // TMS)[None, :], axis=1), NE - 1).astype(jnp.int32)
    loffv = jnp.zeros((NBP, 1, ROUTE_W), F32).at[:, 0, :NE].set(loff.astype(F32))
    ntot = (pc.sum(axis=1) // SEG).astype(jnp.int32)
    chunk = jnp.arange(MAXC, dtype=jnp.int32)
    first = loff // SEG
    e_of = jnp.minimum(jnp.sum(chunk[None, :, None] >= (first + pc // SEG)[:, None, :], axis=2), NE - 1)
    pick = lambda a: jnp.take_along_axis(a, e_of, axis=1)
    gtab = (pick(goff) + (chunk[None, :] - pick(first)) * SEG).reshape(-1).astype(jnp.int32)
    zrow = jnp.concatenate([eend - totp + tot, eend[-1:]]).astype(jnp.int32)
    zcnt = jnp.concatenate([(totp - tot) // SEG, (R_TOT - eend[-1:]) // SEG]).astype(jnp.int32)
    return (gtab, ntot, zrow, zcnt, loffv), tile_expert, n_tiles.astype(jnp.int32)


def _moe(h2, rinfo, cnt, x1, mods, ln_g, ln_b, layer, w_gate, w_up, w_down, split):
    tables, tile_expert, n_tiles = _segment_tables(cnt)
    pos, xs = _moe_permute(h2, rinfo, tables)
    ys = _moe_ffn(xs, tile_expert, n_tiles, layer, w_gate, w_up, w_down)
    return _moe_unpermute(ys, pos, x1, mods, ln_g, ln_b, tables, split)


def kernel(x_prompt, x_sample, state_gla_S, state_mlstm_C, state_mlstm_n, state_mlstm_m, c, c_ctx,
           gla_w_in, gla_w_gate, gla_b_gate, gla_norm_g, gla_w_out,
           mlstm_w_in, mlstm_b_gates, mlstm_norm_g, mlstm_w_out,
           adaln_w, adaln_b, ln_g, ln_b,
           moe_w_group, moe_b_group, moe_w_expert, moe_b_expert, moe_w_gate, moe_w_up, moe_w_down):
    cvecs = jnp.zeros((MOD_ROWS, D), F32).at[0].set(c_ctx).at[1:1 + N_LAT_SEQ].set(c)
    mods_all = _modulation(cvecs, adaln_w, adaln_b)
    x = _embed(x_prompt, x_sample, _grid_posemb(T_LAT))
    gla_states, ml_c, ml_n, ml_m = [], [], [], []
    for l in range(DEPTH):
        s = l // 2
        mods = mods_all[l].reshape(MOD_ROWS * 6, 1, D)
        if l % 2 == 0:
            q, k, v, r, g = _gla_in(x, mods, gla_w_in[s], gla_w_gate[s], gla_b_gate[s])
            o_ctx, s_new = _gla_scan(q, k, v, g, None, s)
            (o_lat,) = _gla_scan(q, k, v, g, state_gla_S, s)
            gla_states.append(s_new)
            norm_g, w_out = gla_norm_g[s], gla_w_out[s]
        else:
            q, k, v, r, gc, gr = _mlstm_in(x, mods, mlstm_w_in[s], mlstm_b_gates[s])
            o_ctx, c_new, n_new, m_new = _mlstm_scan(q, k, v, gc, gr, None, s)
            (o_lat,) = _mlstm_scan(q, k, v, gc, gr, (state_mlstm_C, state_mlstm_n, state_mlstm_m), s)
            ml_c.append(c_new)
            ml_n.append(n_new[:, :, :, 0, :])
            ml_m.append(m_new[:, :, :, 0, 0])
            norm_g, w_out = mlstm_norm_g[s], mlstm_w_out[s]
        w_route = jnp.zeros((D, ROUTE_W), F32).at[:, :NE].set(moe_w_expert[l]).at[:, NE:NE + N_GROUPS].set(
            moe_w_group[l])
        b_route = jnp.zeros((1, ROUTE_W), F32).at[0, :NE].set(moe_b_expert[l]).at[0, NE:NE + N_GROUPS].set(
            moe_b_group[l])
        x1, h2, rinfo, cnt = _out_proj(o_ctx, o_lat, r, x, mods, norm_g, w_out, ln_g[l, 0], ln_b[l, 0], w_route, b_route)
        x = _moe(h2, rinfo, cnt, x1, mods, ln_g[l, 1], ln_b[l, 1], l, moe_w_gate, moe_w_up, moe_w_down,
                 split=(l == DEPTH - 1))
    y_ctx = x[0].reshape(N_CTX_SEQ, T_CTX, D)
    y_lat = x[1].reshape(N_LAT_SEQ, T_LAT, D)
    return (y_ctx, y_lat, jnp.stack(gla_states, 1), jnp.stack(ml_c, 1), jnp.stack(ml_n, 1), jnp.stack(ml_m, 1))
```

```python
import functools
import math

import jax
import jax.numpy as jnp
from jax import lax
from jax.experimental import pallas as pl
from jax.experimental.pallas import tpu as pltpu

F32 = jnp.float32
BF16 = jnp.bfloat16

D = 1024
N_CTX_SEQ, T_CTX = 16, 256
N_LAT_SEQ, T_LAT = 8, 1024
N_CTX = N_CTX_SEQ * T_CTX
N_LAT = N_LAT_SEQ * T_LAT
N_TOK = N_CTX + N_LAT
DEPTH = 4
GRID_W = 64
NH = 4
DK = 128
DV = 256
DKT = NH * DK
DVT = NH * DV
GATE_RANK = 16
GATE_TAU = 16.0
CH = 64
N_GROUPS = 4
EPG = 8
NE = N_GROUPS * EPG
DE = 256
DN_ALPHA = (2.0 * DEPTH) ** 0.25
LN_EPS = 1e-5
MOD_ROWS = 16
TB = 1024
HPB = 4
GATE_F = 2 * NH
SB = 256
NCB = SB // CH
CH_SHIFT = 6
NB = N_TOK // TB
ROUTE_W = 128
NEG_INF = float("-inf")
VMEM_LIMIT = 48 * 1024 * 1024


def _cp(sem):
    return pltpu.CompilerParams(dimension_semantics=sem, vmem_limit_bytes=VMEM_LIMIT)


def _dot(a, b):
    return jnp.dot(a, b, preferred_element_type=F32)


def _dot_nt(a, b):
    return lax.dot_general(a, b, (((1,), (1,)), ((), ())), preferred_element_type=F32)


def _dot_f32(a, b):
    return jnp.dot(a, b, preferred_element_type=F32, precision=lax.Precision.HIGHEST)


def _split2(x):
    hi = x.astype(BF16)
    lo = (x - hi.astype(F32)).astype(BF16)
    return hi, lo


def _log_sigmoid(x):
    return -(jnp.maximum(-x, 0.0) + jnp.log1p(jnp.exp(-jnp.abs(x))))


def _silu(x):
    return x * jax.nn.sigmoid(x)


def _layernorm(x, g, b):
    mu = jnp.mean(x, axis=-1, keepdims=True)
    xc = x - mu
    var = jnp.mean(xc * xc, axis=-1, keepdims=True)
    return xc * lax.rsqrt(var + LN_EPS) * g + b


def _mod_row_tb(i):
    return jnp.where(i < N_CTX // TB, 0, 1 + (i - N_CTX // TB) // (T_LAT // TB))


def _mod_spec(j, row_fn):
    return pl.BlockSpec((1, 1, D), lambda i, *_: (row_fn(i) * 6 + j, 0, 0))


def _mod_kernel(c_ref, w_ref, b_ref, o_ref):
    o_ref[0] = _dot_f32(_silu(c_ref[...]), w_ref[0]) + b_ref[0]


def _modulation(cvecs, adaln_w, adaln_b):
    nj = 6
    return pl.pallas_call(
        _mod_kernel,
        out_shape=jax.ShapeDtypeStruct((DEPTH, MOD_ROWS, 6 * D), F32),
        grid=(DEPTH, nj),
        in_specs=[
            pl.BlockSpec((MOD_ROWS, D), lambda l, j: (0, 0)),
            pl.BlockSpec((1, D, D), lambda l, j: (l, 0, j)),
            pl.BlockSpec((1, 1, D), lambda l, j: (l, 0, j)),
        ],
        out_specs=pl.BlockSpec((1, MOD_ROWS, D), lambda l, j: (l, 0, j)),
        compiler_params=_cp(("parallel", "parallel")),
        name="adaln_modulation",
    )(cvecs, adaln_w, adaln_b.reshape(DEPTH, 1, 6 * D))


def _embed_kernel(xp_ref, xs_ref, pos_ref, o_ref):
    i = pl.program_id(0)

    @pl.when(i < N_CTX // TB)
    def _():
        o_ref[...] = xp_ref[...]

    @pl.when(i >= N_CTX // TB)
    def _():
        o_ref[...] = xs_ref[...] + pos_ref[...]


def _embed(x_prompt, x_sample, pos):
    nc = N_CTX // TB
    return pl.pallas_call(
        _embed_kernel,
        out_shape=jax.ShapeDtypeStruct((N_TOK, D), F32),
        grid=(NB,),
        in_specs=[
            pl.BlockSpec((TB, D), lambda i: (jnp.minimum(i, nc - 1), 0)),
            pl.BlockSpec((TB, D), lambda i: (jnp.maximum(i - nc, 0), 0)),
            pl.BlockSpec((TB, D), lambda i: (jnp.maximum(i - nc, 0) % (T_LAT // TB), 0)),
        ],
        out_specs=pl.BlockSpec((TB, D), lambda i: (i, 0)),
        compiler_params=_cp(("parallel",)),
        name="embed_tokens",
    )(x_prompt.reshape(N_CTX, D), x_sample.reshape(N_LAT, D), pos)


def _grid_posemb(T):
    rows = T // GRID_W
    r, cidx = jnp.meshgrid(jnp.arange(rows), jnp.arange(GRID_W), indexing="ij")
    nf = D // 4
    freqs = jnp.exp(-math.log(10000.0) * jnp.arange(nf, dtype=F32) / nf)

    def emb(p):
        a = p.reshape(-1).astype(F32)[:, None] * freqs
        return jnp.concatenate([jnp.sin(a), jnp.cos(a)], -1)

    return jnp.concatenate([emb(r), emb(cidx)], -1)


def _gla_in_kernel(x_ref, sh_ref, sc_ref, wq_ref, wk_ref, wv_ref, wr_ref, wg_ref, wgate_ref, bgate_ref,
                   q_ref, k_ref, v_ref, r_ref, g_ref):
    h = (x_ref[...] * (1.0 + sc_ref[0]) + sh_ref[0]).astype(BF16)
    q_ref[...] = _dot(h, wq_ref[...]) * (DK ** -0.5)
    k_ref[...] = _dot(h, wk_ref[...])
    v_ref[...] = _dot(h, wv_ref[...]).astype(BF16)
    r_ref[...] = _silu(_dot(h, wr_ref[...])).astype(BF16)
    glr = _dot(h, wg_ref[...]).astype(BF16)
    z = _dot(glr, wgate_ref[...]) + bgate_ref[...]
    g_ref[...] = _log_sigmoid(z) * (1.0 / GATE_TAU)


def _gla_in(x, mods, w_in, w_gate, b_gate):
    wq = w_in[:, :DKT].astype(BF16)
    wk = w_in[:, DKT:2 * DKT].astype(BF16)
    wv = w_in[:, 2 * DKT:2 * DKT + DVT].astype(BF16)
    wr = w_in[:, 2 * DKT + DVT:2 * DKT + 2 * DVT].astype(BF16)
    wg = w_in[:, 2 * DKT + 2 * DVT:].astype(BF16)
    wgate = jnp.zeros((2 * GATE_RANK, 2 * DKT), F32)
    wgate = wgate.at[:GATE_RANK, :DKT].set(w_gate[0]).at[GATE_RANK:, DKT:].set(w_gate[1]).astype(BF16)
    bgate = b_gate.reshape(1, 2 * DKT)
    full = lambda s: pl.BlockSpec(s, lambda i: (0,) * len(s), pipeline_mode=pl.Buffered(1))
    tok = lambda n: pl.BlockSpec((TB, n), lambda i: (i, 0))
    return pl.pallas_call(
        _gla_in_kernel,
        out_shape=[jax.ShapeDtypeStruct((N_TOK, DKT), F32), jax.ShapeDtypeStruct((N_TOK, DKT), F32),
                   jax.ShapeDtypeStruct((N_TOK, DVT), BF16), jax.ShapeDtypeStruct((N_TOK, DVT), BF16),
                   jax.ShapeDtypeStruct((N_TOK, 2 * DKT), F32)],
        grid=(NB,),
        in_specs=[tok(D), _mod_spec(0, _mod_row_tb), _mod_spec(1, _mod_row_tb),
                  full((D, DKT)), full((D, DKT)), full((D, DVT)), full((D, DVT)), full((D, 2 * GATE_RANK)),
                  full((2 * GATE_RANK, 2 * DKT)), full((1, 2 * DKT))],
        out_specs=[tok(DKT), tok(DKT), tok(DVT), tok(DVT), tok(2 * DKT)],
        compiler_params=_cp(("parallel",)),
        name="gla_in_proj",
    )(x, mods, mods, wq, wk, wv, wr, wg, wgate, bgate)


def _mlstm_in_kernel(x_ref, sh_ref, sc_ref, wq_ref, wk_ref, wv_ref, wr_ref, wg_ref, wgt_ref, bg_ref, bgt_ref,
                     q_ref, k_ref, v_ref, r_ref, gc_ref, gr_ref):
    h = (x_ref[...] * (1.0 + sc_ref[0]) + sh_ref[0]).astype(BF16)
    q_ref[...] = _dot(h, wq_ref[...]) * (DK ** -0.5)
    k_ref[...] = _dot(h, wk_ref[...])
    v_ref[...] = _dot(h, wv_ref[...]).astype(BF16)
    r_ref[...] = jax.nn.sigmoid(_dot(h, wr_ref[...])).astype(BF16)
    gc = _dot(h, wg_ref[...]) + bg_ref[...]
    gr = _dot_nt(wgt_ref[...], h) + bgt_ref[...]
    gc_ref[...] = jnp.where(lax.broadcasted_iota(jnp.int32, gc.shape, 1) >= GATE_F, _log_sigmoid(gc), gc)
    gr = jnp.where(lax.broadcasted_iota(jnp.int32, gr.shape, 0) >= GATE_F, _log_sigmoid(gr), gr)
    for c in range(TB // SB):
        gr_ref[c] = gr[:, c * SB:(c + 1) * SB]


def _mlstm_in(x, mods, w_in, b_gates):
    wq = w_in[:, :DKT].astype(BF16)
    wk = w_in[:, DKT:2 * DKT].astype(BF16)
    wv = w_in[:, 2 * DKT:2 * DKT + DVT].astype(BF16)
    wr = w_in[:, 2 * DKT + DVT:2 * DKT + 2 * DVT].astype(BF16)
    wg = w_in[:, 2 * DKT + 2 * DVT:].reshape(D, 2, 2, NH).transpose(0, 2, 3, 1).reshape(D, 4 * NH).astype(BF16)
    bg = b_gates.reshape(2, 2, NH).transpose(1, 2, 0).reshape(1, 4 * NH)
    full = lambda s: pl.BlockSpec(s, lambda i: (0,) * len(s), pipeline_mode=pl.Buffered(1))
    tok = lambda n: pl.BlockSpec((TB, n), lambda i: (i, 0))
    return pl.pallas_call(
        _mlstm_in_kernel,
        out_shape=[jax.ShapeDtypeStruct((N_TOK, DKT), F32), jax.ShapeDtypeStruct((N_TOK, DKT), F32),
                   jax.ShapeDtypeStruct((N_TOK, DVT), BF16), jax.ShapeDtypeStruct((N_TOK, DVT), BF16),
                   jax.ShapeDtypeStruct((N_TOK, 4 * NH), F32),
                   jax.ShapeDtypeStruct((N_TOK // SB, 4 * NH, SB), F32)],
        grid=(NB,),
        in_specs=[tok(D), _mod_spec(0, _mod_row_tb), _mod_spec(1, _mod_row_tb),
                  full((D, DKT)), full((D, DKT)), full((D, DVT)), full((D, DVT)), full((D, 4 * NH)),
                  full((4 * NH, D)), full((1, 4 * NH)), full((4 * NH, 1))],
        out_specs=[tok(DKT), tok(DKT), tok(DVT), tok(DVT),
                   tok(4 * NH),
                   pl.BlockSpec((TB // SB, 4 * NH, SB), lambda i: (i, 0, 0))],
        compiler_params=_cp(("parallel",)),
        name="mlstm_in_proj",
    )(x, mods, mods, wq, wk, wv, wr, wg, wg.T, bg, bg.T)


def _block_tri_masks():
    row = lax.broadcasted_iota(jnp.int32, (SB, SB), 0)
    col = lax.broadcasted_iota(jnp.int32, (SB, SB), 1)
    same = (row >> CH_SHIFT) == (col >> CH_SHIFT)
    return same & (row >= col), same & (row <= col)


def _chunk_scan(x, forward, axis=0, op=jnp.add, fill=0.0):
    n = x.shape[axis]
    pos = lax.broadcasted_iota(jnp.int32, x.shape, axis) & (CH - 1)
    s = 1
    while s < CH:
        if forward:
            x = op(x, jnp.where(pos >= s, pltpu.roll(x, s, axis), fill))
        else:
            x = op(x, jnp.where(pos < CH - s, pltpu.roll(x, n - s, axis), fill))
        s *= 2
    return x


def _chunk_cumsum(x, forward):
    return _chunk_scan(x, forward)


def _cat2(x):
    return jnp.concatenate([x, x], axis=1)


def _per_chunk(vals, n):
    return jnp.concatenate([jnp.broadcast_to(v, (CH, n)) for v in vals], axis=0)


def _chunk_masked_cat(x):
    rowc = lax.broadcasted_iota(jnp.int32, x.shape, 0) >> CH_SHIFT
    return jnp.concatenate([jnp.where(rowc == c, x, 0.0).astype(BF16) for c in range(NCB)], axis=1)


def _head_norm_rows(o):
    mu = jnp.mean(o, axis=-1, keepdims=True)
    oc = o - mu
    var = jnp.mean(oc * oc, axis=-1, keepdims=True)
    return oc * lax.rsqrt(var + LN_EPS)


def _gla_superblock(q, k, v, v_t, g, st, forward, mask):
    b = _chunk_cumsum(g, forward)
    r_last = CH - 1 if forward else 0
    r_mid = CH // 2 if forward else CH - 1 - CH // 2
    lasts = [b[c * CH + r_last:c * CH + r_last + 1, :] for c in range(NCB)]
    b_last = _per_chunk(lasts, DK)
    b_mid = _per_chunk([b[c * CH + r_mid:c * CH + r_mid + 1, :] for c in range(NCB)], DK)
    qe = (q * jnp.exp(b - b_mid)).astype(BF16)
    ke = (k * jnp.exp(b_mid - b)).astype(BF16)
    qb = (q * jnp.exp(b)).astype(BF16)
    kd = k * jnp.exp(b_last - b)
    a = jnp.where(mask, _dot_nt(qe, ke), 0.0).astype(BF16)
    o_intra = _dot(a, v)
    d_t = _dot(v_t, _chunk_masked_cat(kd))
    inter = [None] * NCB
    for c in (range(NCB) if forward else reversed(range(NCB))):
        inter[c] = _dot_nt(qb[c * CH:(c + 1) * CH, :], st.astype(BF16))
        st = st * jnp.exp(lasts[c]) + d_t[:, c * DK:(c + 1) * DK]
    return o_intra + jnp.concatenate(inter, axis=0), st


def _scan_driver(T, o_ref, of_ref, ob_ref, step):
    nsb = T // SB
    if nsb == 1:
        step(0, 0)
    else:
        def body(i, carry):
            step(i, nsb - 1 - i)
            return carry

        lax.fori_loop(0, nsb, body, 0)

    def norm_body(j, carry):
        rows = pl.ds(pl.multiple_of(j * SB, SB), SB)
        for hh in range(HPB):
            cv = slice(hh * DV, (hh + 1) * DV)
            o_ref[rows, cv] = _head_norm_rows(of_ref[rows, cv] + ob_ref[rows, cv]).astype(BF16)
        return carry

    if nsb == 1:
        norm_body(0, 0)
    else:
        lax.fori_loop(0, nsb, norm_body, 0)


def _gla_scan_kernel(*refs, T, has_state, emit_state):
    q_ref, k_ref, v_ref, gf_ref, gb_ref = refs[:5]
    pos = 5
    if has_state:
        s0f_ref, s0b_ref = refs[pos:pos + 2]
        pos += 2
    o_ref = refs[pos]
    pos += 1
    if emit_state:
        sout_ref = refs[pos]
        pos += 1
    stf_ref, stb_ref, vt_ref, of_ref, ob_ref = refs[pos:pos + 5]

    nsb = T // SB
    for hh in range(HPB):
        for j in range(nsb):
            vt_ref[hh, j] = v_ref[j * SB:(j + 1) * SB, hh * DV:(hh + 1) * DV].astype(F32).T.astype(BF16)
        if has_state:
            stf_ref[hh] = s0f_ref[hh].T
            stb_ref[hh] = s0b_ref[hh].T
        else:
            stf_ref[hh] = jnp.zeros((DV, DK), F32)
            stb_ref[hh] = jnp.zeros((DV, DK), F32)

    lower, upper = _block_tri_masks()

    def one(j, forward, hh):
        rows = pl.ds(pl.multiple_of(j * SB, SB), SB)
        ck = slice(hh * DK, (hh + 1) * DK)
        cv = slice(hh * DV, (hh + 1) * DV)
        st_ref = stf_ref if forward else stb_ref
        o, st = _gla_superblock(q_ref[rows, ck], k_ref[rows, ck], v_ref[rows, cv].astype(BF16), vt_ref[hh, j],
                                (gf_ref if forward else gb_ref)[rows, ck], st_ref[hh], forward,
                                lower if forward else upper)
        st_ref[hh] = st
        (of_ref if forward else ob_ref)[rows, cv] = o

    def step(jf, jb):
        for hh in range(HPB):
            one(jf, True, hh)
            one(jb, False, hh)

    _scan_driver(T, o_ref, of_ref, ob_ref, step)
    if emit_state:
        for hh in range(HPB):
            sout_ref[0, hh] = stf_ref[hh].T
            sout_ref[1, hh] = stb_ref[hh].T


def _gla_scan(q, k, v, g, state, slot):
    ctx = state is None
    T = T_CTX if ctx else T_LAT
    n_seq = N_CTX_SEQ if ctx else N_LAT_SEQ
    off = 0 if ctx else N_CTX // T_LAT
    in_specs = [
        pl.BlockSpec((T, HPB * DK), lambda b, h: (off + b, h)),
        pl.BlockSpec((T, HPB * DK), lambda b, h: (off + b, h)),
        pl.BlockSpec((T, HPB * DV), lambda b, h: (off + b, h)),
        pl.BlockSpec((T, HPB * DK), lambda b, h: (off + b, h)),
        pl.BlockSpec((T, HPB * DK), lambda b, h: (off + b, NH // HPB + h)),
    ]
    args = [q, k, v, g, g]
    out_shape = [jax.ShapeDtypeStruct((n_seq * T, DVT), BF16)]
    out_specs = [pl.BlockSpec((T, HPB * DV), lambda b, h: (b, h))]
    if ctx:
        out_shape.append(jax.ShapeDtypeStruct((N_CTX_SEQ, 2, NH, DK, DV), F32))
        out_specs.append(pl.BlockSpec((None, 2, HPB, DK, DV), lambda b, h: (b, 0, h, 0, 0)))
    else:
        sq = (None, None, None, HPB, DK, DV)
        in_specs += [pl.BlockSpec(sq, lambda b, h: (b, slot, 0, h, 0, 0)),
                     pl.BlockSpec(sq, lambda b, h: (b, slot, 1, h, 0, 0))]
        args += [state, state]
    res = pl.pallas_call(
        functools.partial(_gla_scan_kernel, T=T, has_state=not ctx, emit_state=ctx),
        out_shape=out_shape,
        grid=(n_seq, NH // HPB),
        in_specs=in_specs,
        out_specs=out_specs,
        scratch_shapes=[pltpu.VMEM((HPB, DV, DK), F32), pltpu.VMEM((HPB, DV, DK), F32),
                        pltpu.VMEM((HPB, T // SB, DV, SB), BF16), pltpu.VMEM((T, HPB * DV), F32), pltpu.VMEM((T, HPB * DV), F32)],
        compiler_params=_cp(("parallel", "parallel")),
        name="gla_scan_ctx" if ctx else "gla_scan_lat",
    )(*args)
    return res


def _mlstm_scan_kernel(*refs, T, has_state, emit_state):
    q_ref, k_ref, v_ref, gc_ref, gr_ref = refs[:5]
    pos = 5
    if has_state:
        c0f_ref, c0b_ref, n0f_ref, n0b_ref, m0f_ref, m0b_ref = refs[pos:pos + 6]
        pos += 6
    o_ref = refs[pos]
    pos += 1
    if emit_state:
        cout_ref, nout_ref, mout_ref = refs[pos:pos + 3]
        pos += 3
    ct_ref, n_ref, m_ref, vt_ref, of_ref, ob_ref = refs[pos:pos + 6]

    nsb = T // SB
    for hh in range(HPB):
        for j in range(nsb):
            vt_ref[hh, j] = v_ref[j * SB:(j + 1) * SB, hh * DV:(hh + 1) * DV].astype(F32).T.astype(BF16)
        for d in range(2):
            sl = d * HPB + hh
            if has_state:
                c0, n0, m0 = ((c0f_ref, n0f_ref, m0f_ref), (c0b_ref, n0b_ref, m0b_ref))[d]
                ct_ref[sl] = c0[hh].T
                n_ref[sl] = n0[hh]
                m_ref[sl] = m0[hh]
            else:
                ct_ref[sl] = jnp.zeros((DV, DK), F32)
                n_ref[sl] = jnp.zeros((1, DK), F32)
                m_ref[sl] = jnp.zeros((1, DK), F32)

    lower, upper = _block_tri_masks()

    def gates(j, forward):
        rows = pl.ds(pl.multiple_of(j * SB, SB), SB)
        gc = gc_ref[rows, :]
        gr = gr_ref[j]
        i_col, f_col = gc[:, :GATE_F], gc[:, GATE_F:]
        b_col = _chunk_scan(f_col, forward)
        m_col = b_col + _chunk_scan(i_col - b_col, forward, op=jnp.maximum, fill=NEG_INF)
        b_row = _chunk_scan(gr[GATE_F:, :], forward, axis=1) - gr[:GATE_F, :]
        return i_col, b_col, m_col, b_row

    def one(j, forward, hh, shared):
        d = 0 if forward else 1
        sl = d * HPB + hh
        col = 2 * hh + d
        rows = pl.ds(pl.multiple_of(j * SB, SB), SB)
        q = q_ref[rows, hh * DK:(hh + 1) * DK]
        k = k_ref[rows, hh * DK:(hh + 1) * DK]
        v = v_ref[rows, hh * DV:(hh + 1) * DV].astype(BF16)
        mask = lower if forward else upper
        order = list(range(NCB)) if forward else list(reversed(range(NCB)))
        r_last = CH - 1 if forward else 0
        i_c, b_c, m_intra = [jnp.broadcast_to(a[:, col:col + 1], (SB, DK)) for a in shared[:3]]
        brow = shared[3][col:col + 1, :]
        b_last = [b_c[c * CH + r_last:c * CH + r_last + 1, :] for c in range(NCB)]
        a_c = _per_chunk(b_last, DK) - b_c + i_c
        m_loc = [jnp.max(a_c[c * CH:(c + 1) * CH, :], axis=0, keepdims=True) for c in range(NCB)]
        kw = k * jnp.exp(a_c - _per_chunk(m_loc, DK))
        kv_t = _dot(vt_ref[hh, j], _chunk_masked_cat(kw))
        k_loc = [jnp.sum(kw[c * CH:(c + 1) * CH, :], axis=0, keepdims=True) for c in range(NCB)]

        m = m_ref[sl]
        m_start, s_old, s_loc = [None] * NCB, [None] * NCB, [None] * NCB
        for c in order:
            m_start[c] = m
            m_new = jnp.maximum(b_last[c] + m, m_loc[c])
            s_old[c] = jnp.exp(b_last[c] + m - m_new)
            s_loc[c] = jnp.exp(m_loc[c] - m_new)
            m = m_new
        m_ref[sl] = m

        l_inter = b_c + _per_chunk(m_start, DK)
        m_i = jnp.maximum(l_inter, m_intra)
        d_log = jnp.where(mask, _cat2(b_c - m_i) - brow, NEG_INF)
        qb = q.astype(BF16)
        s = (_dot_nt(qb, k.astype(BF16)) * jnp.exp(d_log)).astype(BF16)
        e_inter = jnp.exp(l_inter - m_i)
        nd = _dot(s, jnp.concatenate([v, jnp.ones((SB, DK), BF16)], axis=1))
        num, den = nd[:, :DV], nd[:, DV:]

        ct = ct_ref[sl]
        nrm = n_ref[sl]
        inter = [None] * NCB
        for c in order:
            state = jnp.concatenate([ct, jnp.broadcast_to(nrm, (DK, DK))], axis=0).astype(BF16)
            inter[c] = _dot_nt(qb[c * CH:(c + 1) * CH, :], state)
            ct = s_old[c] * ct + s_loc[c] * kv_t[:, c * DK:(c + 1) * DK]
            nrm = s_old[c] * nrm + s_loc[c] * k_loc[c]
        ct_ref[sl] = ct
        n_ref[sl] = nrm

        inter = jnp.concatenate(inter, axis=0)
        num = num + _cat2(e_inter) * inter[:, :DV]
        den = den + e_inter * inter[:, DV:]
        inv = 1.0 / jnp.maximum(jnp.abs(den), jnp.exp(-m_i))
        (of_ref if forward else ob_ref)[rows, hh * DV:(hh + 1) * DV] = num * _cat2(inv)

    def step(jf, jb):
        shared_f = gates(jf, True)
        shared_b = gates(jb, False)
        for hh in range(HPB):
            one(jf, True, hh, shared_f)
            one(jb, False, hh, shared_b)

    _scan_driver(T, o_ref, of_ref, ob_ref, step)
    if emit_state:
        for d in range(2):
            for hh in range(HPB):
                cout_ref[d, hh] = ct_ref[d * HPB + hh].T
                nout_ref[d, hh] = n_ref[d * HPB + hh]
                mout_ref[d, hh] = m_ref[d * HPB + hh]


def _mlstm_scan(q, k, v, gc, gr, states, slot):
    ctx = states is None
    T = T_CTX if ctx else T_LAT
    n_seq = N_CTX_SEQ if ctx else N_LAT_SEQ
    off = 0 if ctx else N_CTX // T_LAT
    in_specs = [
        pl.BlockSpec((T, HPB * DK), lambda b, h: (off + b, h)),
        pl.BlockSpec((T, HPB * DK), lambda b, h: (off + b, h)),
        pl.BlockSpec((T, HPB * DV), lambda b, h: (off + b, h)),
        pl.BlockSpec((T, 4 * NH), lambda b, h: (off + b, 0)),
        pl.BlockSpec((T // SB, 4 * NH, SB), lambda b, h: (off + b, 0, 0)),
    ]
    assert HPB == NH
    args = [q, k, v, gc, gr]
    out_shape = [jax.ShapeDtypeStruct((n_seq * T, DVT), BF16)]
    out_specs = [pl.BlockSpec((T, HPB * DV), lambda b, h: (b, h))]
    if ctx:
        out_shape += [jax.ShapeDtypeStruct((N_CTX_SEQ, 2, NH, DK, DV), F32),
                      jax.ShapeDtypeStruct((N_CTX_SEQ, 2, NH, 1, DK), F32),
                      jax.ShapeDtypeStruct((N_CTX_SEQ, 2, NH, 1, DK), F32)]
        out_specs += [pl.BlockSpec((None, 2, HPB, DK, DV), lambda b, h: (b, 0, h, 0, 0)),
                      pl.BlockSpec((None, 2, HPB, 1, DK), lambda b, h: (b, 0, h, 0, 0)),
                      pl.BlockSpec((None, 2, HPB, 1, DK), lambda b, h: (b, 0, h, 0, 0))]
    else:
        c0, n0, m0 = states
        n0 = n0.reshape(N_LAT_SEQ, -1, 2, NH, 1, DK)
        m0 = jnp.broadcast_to(m0[..., None, None], m0.shape + (1, DK))
        sq_c = (None, None, None, HPB, DK, DV)
        sq_v = (None, None, None, HPB, 1, DK)
        for arr, sq in ((c0, sq_c), (n0, sq_v), (m0, sq_v)):
            for d in range(2):
                in_specs.append(pl.BlockSpec(sq, functools.partial(lambda b, h, d: (b, slot, d, h, 0, 0), d=d)))
                args.append(arr)
    return pl.pallas_call(
        functools.partial(_mlstm_scan_kernel, T=T, has_state=not ctx, emit_state=ctx),
        out_shape=out_shape,
        grid=(n_seq, NH // HPB),
        in_specs=in_specs,
        out_specs=out_specs,
        scratch_shapes=[pltpu.VMEM((2 * HPB, DV, DK), F32), pltpu.VMEM((2 * HPB, 1, DK), F32),
                        pltpu.VMEM((2 * HPB, 1, DK), F32),
                        pltpu.VMEM((HPB, T // SB, DV, SB), BF16), pltpu.VMEM((T, HPB * DV), F32), pltpu.VMEM((T, HPB * DV), F32)],
        compiler_params=_cp(("parallel", "parallel")),
        name="mlstm_scan_ctx" if ctx else "mlstm_scan_lat",
    )(*args)


def _route(lg):
    lane = lax.broadcasted_iota(jnp.int32, lg.shape, 1)
    big = jnp.int32(ROUTE_W)
    is_g = (lane >= NE) & (lane < NE + N_GROUPS)
    gl = jnp.where(is_g, lg, NEG_INF)
    gmax = jnp.max(gl, axis=1, keepdims=True)
    gsel = jnp.min(jnp.where(gl == gmax, lane, big), axis=1, keepdims=True) - NE
    gw = 1.0 / jnp.sum(jnp.exp(gl - gmax), axis=1, keepdims=True)
    ing = (lane < NE) & ((lane >> 3) == gsel)
    el = jnp.where(ing, lg, NEG_INF)
    emax = jnp.max(el, axis=1, keepdims=True)
    p = jnp.exp(el - emax)
    prob = p / jnp.sum(p, axis=1, keepdims=True)
    p1 = jnp.max(prob, axis=1, keepdims=True)
    i1 = jnp.min(jnp.where(ing & (prob == p1), lane, big), axis=1, keepdims=True)
    rest = ing & (lane != i1)
    prob2 = jnp.where(rest, prob, -1.0)
    p2 = jnp.max(prob2, axis=1, keepdims=True)
    i2 = jnp.min(jnp.where(rest & (prob2 == p2), lane, big), axis=1, keepdims=True)
    tot = p1 + p2
    rinfo = jnp.where(lane == 0, i1.astype(F32),
                      jnp.where(lane == 1, i2.astype(F32),
                                jnp.where(lane == 2, gw * (p1 / tot), jnp.where(lane == 3, gw * (p2 / tot), 0.0))))
    sel = jnp.where((lane == i1) | (lane == i2), 1.0, 0.0)
    return rinfo, [jnp.sum(sel[c * TP:(c + 1) * TP, :], axis=0, keepdims=True) for c in range(lg.shape[0] // TP)]


def _out_kernel(oc_ref, ol_ref, r_ref, x_ref, gate_ref, sh_ref, sc_ref, ng_ref, wo_ref, lg_ref, lb_ref, wr_ref,
                br_ref, x1_ref, h2_ref, rinfo_ref, cnt_ref):
    o = jnp.where(pl.program_id(0) < N_CTX // TB, oc_ref[...], ol_ref[...])
    y = _dot((o * ng_ref[...] * r_ref[...]).astype(BF16), wo_ref[...])
    x1 = _layernorm(DN_ALPHA * x_ref[...] + gate_ref[0] * y, lg_ref[...], lb_ref[...])
    x1_ref[...] = x1
    h2 = x1 * (1.0 + sc_ref[0]) + sh_ref[0]
    h_hi, h_lo = _split2(h2)
    h2_ref[...] = h_hi
    t = _dot(h_hi, wr_ref[...])
    lg = t[:, :ROUTE_W] + t[:, ROUTE_W:] + _dot(h_lo, wr_ref[:, :ROUTE_W]) + br_ref[...]
    rinfo_ref[...], counts = _route(lg)
    for c, count in enumerate(counts):
        cnt_ref[c] = count


def _out_proj(o_ctx, o_lat, r, x, mods, norm_g, w_out, ln_g, ln_b, w_route, b_route):
    nc = N_CTX // TB
    full = lambda s: pl.BlockSpec(s, lambda i: (0,) * len(s), pipeline_mode=pl.Buffered(1))
    tok = lambda n: pl.BlockSpec((TB, n), lambda i: (i, 0))
    return pl.pallas_call(
        _out_kernel,
        out_shape=[jax.ShapeDtypeStruct((N_TOK, D), F32), jax.ShapeDtypeStruct((N_TOK, D), BF16),
                   jax.ShapeDtypeStruct((N_TOK, ROUTE_W), F32), jax.ShapeDtypeStruct((NBP, 1, ROUTE_W), F32)],
        grid=(NB,),
        in_specs=[pl.BlockSpec((TB, DVT), lambda i: (jnp.minimum(i, nc - 1), 0)),
                  pl.BlockSpec((TB, DVT), lambda i: (jnp.maximum(i - nc, 0), 0)),
                  tok(DVT), tok(D),
                  _mod_spec(2, _mod_row_tb), _mod_spec(3, _mod_row_tb), _mod_spec(4, _mod_row_tb),
                  full((1, DVT)), full((DVT, D)), full((1, D)), full((1, D)),
                  full((D, 2 * ROUTE_W)), full((1, ROUTE_W))],
        out_specs=[tok(D), tok(D), tok(ROUTE_W), pl.BlockSpec((TB // TP, 1, ROUTE_W), lambda i: (i, 0, 0))],
        compiler_params=_cp(("parallel",)),
        name="out_proj_route",
    )(o_ctx, o_lat, r, x, mods, mods, mods, norm_g.reshape(1, DVT), w_out.astype(BF16),
      ln_g.reshape(1, D), ln_b.reshape(1, D), jnp.concatenate(_split2(w_route), axis=1), b_route)


TP = 512
NBP = N_TOK // TP
SEG = 16
R_LOC = 1536
PT = 256
XW = D + ROUTE_W
TMS = 512
R_TOT = -(-(2 * N_TOK + NBP * NE * (SEG - 1) + NE * (TMS - 1)) // TMS) * TMS
NT = R_TOT // TMS
POS_SPLIT = 64.0


def _mod_row_tp(i):
    return jnp.where(i < N_CTX // TP, 0, 1 + (i - N_CTX // TP) // (T_LAT // TP))


def _lane_pack(cols, shape):
    lane = lax.broadcasted_iota(jnp.int32, shape, 1)
    out = jnp.zeros(shape, F32)
    for j, c in enumerate(cols):
        out = jnp.where(lane == j, c, out)
    return out


WAIT_CHUNKS = (32, 4, 1)
MAXC = R_LOC // SEG
ZROWS = WAIT_CHUNKS[0] * SEG
NZ = NE + 1


def _segment_start(gtab_ref, ntot_ref, blk, make_copy):
    def per_chunk(j, carry):
        g = gtab_ref[blk * MAXC + j]
        make_copy(pl.multiple_of(j * SEG, SEG), pl.multiple_of(g, SEG), SEG).start()
        return carry

    lax.fori_loop(0, ntot_ref[blk], per_chunk, 0)


def _segment_wait(n_chunks, make_copy):
    left = n_chunks
    for chunks in WAIT_CHUNKS:
        n_wait = left // chunks

        def wait_piece(k, c, chunks=chunks):
            make_copy(0, 0, chunks * SEG).wait()
            return c

        lax.fori_loop(0, n_wait, wait_piece, 0)
        left = left - n_wait * chunks


def _perm_kernel(gtab_ref, ntot_ref, zrow_ref, zcnt_ref, h_ref, ri_ref, loffv_ref, pos_ref, xs_ref,
                 xs_scr, zero_scr, sem, zsem):
    blk = pl.program_id(0)
    slot = blk % 2

    def zero_copy(_, g, n):
        return pltpu.make_async_copy(zero_scr.at[pl.ds(0, n)], xs_ref.at[pl.ds(g, n)], zsem)

    @pl.when(blk == 0)
    def _():
        zero_scr[...] = jnp.zeros_like(zero_scr)

        def per_range(r, carry):
            n = zcnt_ref[r]
            g0 = zrow_ref[r]
            n_big = n // WAIT_CHUNKS[0]

            def per_piece(k, c, first, chunks):
                zero_copy(0, pl.multiple_of(g0 + (first + k * chunks) * SEG, SEG), chunks * SEG).start()
                return c

            lax.fori_loop(0, n_big, functools.partial(per_piece, first=0, chunks=WAIT_CHUNKS[0]), 0)
            lax.fori_loop(0, n - n_big * WAIT_CHUNKS[0],
                          functools.partial(per_piece, first=n_big * WAIT_CHUNKS[0], chunks=1), 0)
            return carry + n

        lax.fori_loop(0, NZ, per_range, 0)
    ri = ri_ref[...]
    e_a, e_b, w_a, w_b = ri[:, 0:1], ri[:, 1:2], ri[:, 2:3], ri[:, 3:4]
    lanef = lax.broadcasted_iota(jnp.int32, ri.shape, 1).astype(F32)
    is_a = lanef == e_a
    is_b = lanef == e_b
    sel = jnp.where(is_a | is_b, 1.0, 0.0).astype(BF16)
    row = lax.broadcasted_iota(jnp.int32, (TP, TP), 0)
    col = lax.broadcasted_iota(jnp.int32, (TP, TP), 1)
    earlier = jnp.where(row > col, 1.0, 0.0).astype(BF16)
    lpos = loffv_ref[0] + _dot(earlier, sel)
    pos_a = jnp.sum(jnp.where(is_a, lpos, 0.0), axis=1, keepdims=True)
    pos_b = jnp.sum(jnp.where(is_b, lpos, 0.0), axis=1, keepdims=True)
    pos_ref[...] = _lane_pack([pos_a, pos_b], ri.shape)

    hi_a = jnp.floor(pos_a * (1.0 / POS_SPLIT))
    hi_b = jnp.floor(pos_b * (1.0 / POS_SPLIT))
    parts = _lane_pack([hi_a, pos_a - POS_SPLIT * hi_a, hi_b, pos_b - POS_SPLIT * hi_b], ri.shape).astype(BF16)
    pick = jnp.where(lax.broadcasted_iota(jnp.int32, (8, ROUTE_W), 0) == lax.broadcasted_iota(jnp.int32, (8, ROUTE_W), 1),
                     1.0, 0.0).astype(BF16)
    pr = _dot_nt(pick, parts)
    pos_a_r = POS_SPLIT * pr[0:1, :] + pr[1:2, :]
    pos_b_r = POS_SPLIT * pr[2:3, :] + pr[3:4, :]

    wa1, wa2 = _split2(w_a)
    wa3 = (w_a - wa1.astype(F32) - wa2.astype(F32))
    wb1, wb2 = _split2(w_b)
    wb3 = (w_b - wb1.astype(F32) - wb2.astype(F32))
    wl = _lane_pack([wa1.astype(F32), wa2.astype(F32), wa3, wb1.astype(F32), wb2.astype(F32), wb3, e_a],
                    ri.shape).astype(BF16)
    hcat = jnp.concatenate([h_ref[...], wl], axis=1)
    for t in range(R_LOC // PT):
        rio = (lax.broadcasted_iota(jnp.int32, (PT, TP), 0) + t * PT).astype(F32)
        onehot = jnp.where((rio == pos_a_r) | (rio == pos_b_r), 1.0, 0.0).astype(BF16)
        xs_scr[slot, t * PT:(t + 1) * PT, :] = _dot(onehot, hcat).astype(BF16)

    def copy_from(s):
        return lambda l, g, n: pltpu.make_async_copy(xs_scr.at[s, pl.ds(l, n)], xs_ref.at[pl.ds(g, n)], sem.at[s])

    _segment_start(gtab_ref, ntot_ref, blk, copy_from(slot))

    @pl.when(blk > 0)
    def _():
        _segment_wait(ntot_ref[blk - 1], copy_from(1 - slot))

    @pl.when(blk == NBP - 1)
    def _():
        _segment_wait(ntot_ref[blk], copy_from(slot))
        _segment_wait(lax.fori_loop(0, NZ, lambda r, c: c + zcnt_ref[r], 0), zero_copy)


def _moe_permute(h2, rinfo, tables):
    gtab, ntot, zrow, zcnt, loffv = tables
    grid_spec = pltpu.PrefetchScalarGridSpec(
        num_scalar_prefetch=4,
        grid=(NBP,),
        in_specs=[pl.BlockSpec((TP, D), lambda i, *_: (i, 0)),
                  pl.BlockSpec((TP, ROUTE_W), lambda i, *_: (i, 0)),
                  pl.BlockSpec((1, 1, ROUTE_W), lambda i, *_: (i, 0, 0))],
        out_specs=[pl.BlockSpec((TP, ROUTE_W), lambda i, *_: (i, 0)),
                   pl.BlockSpec(memory_space=pl.ANY)],
        scratch_shapes=[pltpu.VMEM((2, R_LOC, XW), BF16), pltpu.VMEM((ZROWS, XW), BF16),
                        pltpu.SemaphoreType.DMA((2,)), pltpu.SemaphoreType.DMA],
    )
    return pl.pallas_call(
        _perm_kernel,
        out_shape=[jax.ShapeDtypeStruct((N_TOK, ROUTE_W), F32), jax.ShapeDtypeStruct((R_TOT, XW), BF16)],
        grid_spec=grid_spec,
        compiler_params=_cp(("arbitrary",)),
        name="moe_permute",
    )(gtab, ntot, zrow, zcnt, h2, rinfo, loffv)


def _ffn_kernel(te_ref, nt_ref, xs_ref, wg32_ref, wu32_ref, wd32_ref, ys_ref, wg_ref, wu_ref, wd_ref):
    i = pl.program_id(0)
    active = i < nt_ref[0]
    new_expert = (i == 0) | (te_ref[i] != te_ref[jnp.maximum(i - 1, 0)])

    @pl.when(active & new_expert)
    def _():
        wg_ref[0] = wg32_ref[0].astype(BF16)
        wu_ref[0] = wu32_ref[0].astype(BF16)
        wd_ref[0] = wd32_ref[0].astype(BF16)

    @pl.when(active)
    def _():
        e = te_ref[i].astype(F32)
        xs = xs_ref[...]
        x = xs[:, :D]
        r = xs[:, D:].astype(F32)
        w_a = r[:, 0:1] + r[:, 1:2] + r[:, 2:3]
        w_b = r[:, 3:4] + r[:, 4:5] + r[:, 5:6]
        w = jnp.where(r[:, 6:7] == e, w_a, w_b)
        hid = _silu(_dot(x, wg_ref[0])) * _dot(x, wu_ref[0]) * w
        ys_ref[...] = _dot(hid.astype(BF16), wd_ref[0]).astype(BF16)

    @pl.when(i >= nt_ref[0])
    def _():
        ys_ref[...] = jnp.zeros_like(ys_ref)


def _moe_ffn(xs, tile_expert, n_tiles, layer, w_gate, w_up, w_down):
    last = lambda i, nt: jnp.minimum(i, nt[0] - 1)
    wspec = lambda a, b: pl.BlockSpec((None, 1, a, b), lambda i, te, nt: (layer, te[last(i, nt)], 0, 0))
    grid_spec = pltpu.PrefetchScalarGridSpec(
        num_scalar_prefetch=2,
        grid=(NT,),
        in_specs=[pl.BlockSpec((TMS, XW), lambda i, te, nt: (last(i, nt), 0)),
                  wspec(D, DE), wspec(D, DE), wspec(DE, D)],
        out_specs=pl.BlockSpec((TMS, D), lambda i, te, nt: (i, 0)),
        scratch_shapes=[pltpu.VMEM((1, D, DE), BF16), pltpu.VMEM((1, D, DE), BF16), pltpu.VMEM((1, DE, D), BF16)],
    )
    return pl.pallas_call(
        _ffn_kernel,
        out_shape=jax.ShapeDtypeStruct((R_TOT, D), BF16),
        grid_spec=grid_spec,
        compiler_params=_cp(("arbitrary",)),
        name="moe_ffn",
    )(tile_expert, n_tiles, xs, w_gate, w_up, w_down)


def _unperm_kernel(gtab_ref, ntot_ref, pos_ref, x1_ref, gate_ref, lg_ref, lb_ref, ys_ref, *rest, split):
    if split:
        oc_ref, ol_ref, ys_scr, sem = rest
    else:
        o_ref, ys_scr, sem = rest
    blk = pl.program_id(0)
    slot = blk % 2

    def copy_to(s):
        return lambda l, g, n: pltpu.make_async_copy(ys_ref.at[pl.ds(g, n)], ys_scr.at[s, pl.ds(l, n)], sem.at[s])

    def fetch(b, s):
        ys_scr[s] = jnp.zeros((R_LOC, D), BF16)
        _segment_start(gtab_ref, ntot_ref, b, copy_to(s))

    @pl.when(blk == 0)
    def _():
        fetch(0, 0)

    @pl.when(blk + 1 < NBP)
    def _():
        fetch(blk + 1, 1 - slot)

    _segment_wait(ntot_ref[blk], copy_to(slot))
    pos = pos_ref[...]
    pos_a, pos_b = pos[:, 0:1], pos[:, 1:2]
    acc = jnp.zeros((TP, D), F32)
    for t in range(R_LOC // PT):
        lio = (lax.broadcasted_iota(jnp.int32, (TP, PT), 1) + t * PT).astype(F32)
        onehot = jnp.where((lio == pos_a) | (lio == pos_b), 1.0, 0.0).astype(BF16)
        acc = acc + _dot(onehot, ys_scr[slot, t * PT:(t + 1) * PT, :])
    res = _layernorm(DN_ALPHA * x1_ref[...] + gate_ref[0] * acc, lg_ref[...], lb_ref[...])
    if split:
        @pl.when(blk < N_CTX // TP)
        def _():
            oc_ref[...] = res

        @pl.when(blk >= N_CTX // TP)
        def _():
            ol_ref[...] = res
    else:
        o_ref[...] = res


def _moe_unpermute(ys, pos, x1, mods, ln_g, ln_b, tables, split):
    gtab, ntot = tables[:2]
    nc = N_CTX // TP
    if split:
        out_shape = [jax.ShapeDtypeStruct((N_CTX, D), F32), jax.ShapeDtypeStruct((N_LAT, D), F32)]
        out_specs = [pl.BlockSpec((TP, D), lambda i, *_: (jnp.minimum(i, nc - 1), 0)),
                     pl.BlockSpec((TP, D), lambda i, *_: (jnp.maximum(i - nc, 0), 0))]
    else:
        out_shape = jax.ShapeDtypeStruct((N_TOK, D), F32)
        out_specs = pl.BlockSpec((TP, D), lambda i, *_: (i, 0))
    grid_spec = pltpu.PrefetchScalarGridSpec(
        num_scalar_prefetch=2,
        grid=(NBP,),
        in_specs=[pl.BlockSpec((TP, ROUTE_W), lambda i, *_: (i, 0)),
                  pl.BlockSpec((TP, D), lambda i, *_: (i, 0)),
                  _mod_spec(5, _mod_row_tp),
                  pl.BlockSpec((1, D), lambda i, *_: (0, 0)),
                  pl.BlockSpec((1, D), lambda i, *_: (0, 0)),
                  pl.BlockSpec(memory_space=pl.ANY)],
        out_specs=out_specs,
        scratch_shapes=[pltpu.VMEM((2, R_LOC, D), BF16), pltpu.SemaphoreType.DMA((2,))],
    )
    return pl.pallas_call(
        functools.partial(_unperm_kernel, split=split),
        out_shape=out_shape,
        grid_spec=grid_spec,
        compiler_params=_cp(("arbitrary",)),
        name="moe_unpermute",
    )(gtab, ntot, pos, x1, mods, ln_g.reshape(1, D), ln_b.reshape(1, D), ys)


def _segment_tables(cnt):
    c = cnt[:, 0, :NE].astype(jnp.int32)
    pc = (c + SEG - 1) // SEG * SEG
    loff = jnp.cumsum(pc, axis=1) - pc
    tot = pc.sum(axis=0)
    totp = (tot + TMS - 1) // TMS * TMS
    eend = jnp.cumsum(totp)
    goff = (eend - totp)[None, :] + jnp.cumsum(pc, axis=0) - pc
    n_tiles = (eend[-1] // TMS).reshape(1)
    tile_id = jnp.arange(NT, dtype=jnp.int32)
    tile_expert = jnp.minimum(jnp.sum(tile_id[:, None] >= (eend // TMS)[None, :], axis=1), NE - 1).astype(jnp.int32)
    loffv = jnp.zeros((NBP, 1, ROUTE_W), F32).at[:, 0, :NE].set(loff.astype(F32))
    ntot = (pc.sum(axis=1) // SEG).astype(jnp.int32)
    chunk = jnp.arange(MAXC, dtype=jnp.int32)
    first = loff // SEG
    e_of = jnp.minimum(jnp.sum(chunk[None, :, None] >= (first + pc // SEG)[:, None, :], axis=2), NE - 1)
    is_e = e_of[:, :, None] == jnp.arange(NE, dtype=jnp.int32)[None, None, :]
    pick = lambda a: jnp.sum(jnp.where(is_e, a[:, None, :], 0), axis=2)
    gtab = (pick(goff) + (chunk[None, :] - pick(first)) * SEG).reshape(-1).astype(jnp.int32)
    zrow = jnp.concatenate([eend - totp + tot, eend[-1:]]).astype(jnp.int32)
    zcnt = jnp.concatenate([(totp - tot) // SEG, (R_TOT - eend[-1:]) // SEG]).astype(jnp.int32)
    return (gtab, ntot, zrow, zcnt, loffv), tile_expert, n_tiles.astype(jnp.int32)


def _moe(h2, rinfo, cnt, x1, mods, ln_g, ln_b, layer, w_gate, w_up, w_down, split):
    tables, tile_expert, n_tiles = _segment_tables(cnt)
    pos, xs = _moe_permute(h2, rinfo, tables)
    ys = _moe_ffn(xs, tile_expert, n_tiles, layer, w_gate, w_up, w_down)
    return _moe_unpermute(ys, pos, x1, mods, ln_g, ln_b, tables, split)


def kernel(x_prompt, x_sample, state_gla_S, state_mlstm_C, state_mlstm_n, state_mlstm_m, c, c_ctx,
           gla_w_in, gla_w_gate, gla_b_gate, gla_norm_g, gla_w_out,
           mlstm_w_in, mlstm_b_gates, mlstm_norm_g, mlstm_w_out,
           adaln_w, adaln_b, ln_g, ln_b,
           moe_w_group, moe_b_group, moe_w_expert, moe_b_expert, moe_w_gate, moe_w_up, moe_w_down):
    cvecs = jnp.zeros((MOD_ROWS, D), F32).at[0].set(c_ctx).at[1:1 + N_LAT_SEQ].set(c)
    mods_all = _modulation(cvecs, adaln_w, adaln_b)
    x = _embed(x_prompt, x_sample, _grid_posemb(T_LAT))
    gla_states, ml_c, ml_n, ml_m = [], [], [], []
    for l in range(DEPTH):
        s = l // 2
        mods = mods_all[l].reshape(MOD_ROWS * 6, 1, D)
        if l % 2 == 0:
            q, k, v, r, g = _gla_in(x, mods, gla_w_in[s], gla_w_gate[s], gla_b_gate[s])
            o_ctx, s_new = _gla_scan(q, k, v, g, None, s)
            (o_lat,) = _gla_scan(q, k, v, g, state_gla_S, s)
            gla_states.append(s_new)
            norm_g, w_out = gla_norm_g[s], gla_w_out[s]
        else:
            q, k, v, r, gc, gr = _mlstm_in(x, mods, mlstm_w_in[s], mlstm_b_gates[s])
            o_ctx, c_new, n_new, m_new = _mlstm_scan(q, k, v, gc, gr, None, s)
            (o_lat,) = _mlstm_scan(q, k, v, gc, gr, (state_mlstm_C, state_mlstm_n, state_mlstm_m), s)
            ml_c.append(c_new)
            ml_n.append(n_new[:, :, :, 0, :])
            ml_m.append(m_new[:, :, :, 0, 0])
            norm_g, w_out = mlstm_norm_g[s], mlstm_w_out[s]
        w_route = jnp.zeros((D, ROUTE_W), F32).at[:, :NE].set(moe_w_expert[l]).at[:, NE:NE + N_GROUPS].set(
            moe_w_group[l])
        b_route = jnp.zeros((1, ROUTE_W), F32).at[0, :NE].set(moe_b_expert[l]).at[0, NE:NE + N_GROUPS].set(
            moe_b_group[l])
        x1, h2, rinfo, cnt = _out_proj(o_ctx, o_lat, r, x, mods, norm_g, w_out, ln_g[l, 0], ln_b[l, 0], w_route, b_route)
        x = _moe(h2, rinfo, cnt, x1, mods, ln_g[l, 1], ln_b[l, 1], l, moe_w_gate, moe_w_up, moe_w_down,
                 split=(l == DEPTH - 1))
    y_ctx = x[0].reshape(N_CTX_SEQ, T_CTX, D)
    y_lat = x[1].reshape(N_LAT_SEQ, T_LAT, D)
    return (y_ctx, y_lat, jnp.stack(gla_states, 1), jnp.stack(ml_c, 1), jnp.stack(ml_n, 1), jnp.stack(ml_m, 1))
```

```python
import functools
import math

import jax
import jax.numpy as jnp
from jax import lax
from jax.experimental import pallas as pl
from jax.experimental.pallas import tpu as pltpu

F32 = jnp.float32
BF16 = jnp.bfloat16

D = 1024
N_CTX_SEQ, T_CTX = 16, 256
N_LAT_SEQ, T_LAT = 8, 1024
N_CTX = N_CTX_SEQ * T_CTX
N_LAT = N_LAT_SEQ * T_LAT
N_TOK = N_CTX + N_LAT
DEPTH = 4
GRID_W = 64
NH = 4
DK = 128
DV = 256
DKT = NH * DK
DVT = NH * DV
GATE_RANK = 16
GATE_TAU = 16.0
CH = 64
N_GROUPS = 4
EPG = 8
NE = N_GROUPS * EPG
DE = 256
DN_ALPHA = (2.0 * DEPTH) ** 0.25
LN_EPS = 1e-5
MOD_ROWS = 16
TB = 1024
HPB = 4
GATE_F = 2 * NH
SB = 256
NCB = SB // CH
CH_SHIFT = 6
NB = N_TOK // TB
ROUTE_W = 128
NEG_INF = float("-inf")
VMEM_LIMIT = 48 * 1024 * 1024


def _cp(sem):
    return pltpu.CompilerParams(dimension_semantics=sem, vmem_limit_bytes=VMEM_LIMIT)


def _dot(a, b):
    return jnp.dot(a, b, preferred_element_type=F32)


def _dot_nt(a, b):
    return lax.dot_general(a, b, (((1,), (1,)), ((), ())), preferred_element_type=F32)


def _dot_f32(a, b):
    return jnp.dot(a, b, preferred_element_type=F32, precision=lax.Precision.HIGHEST)


def _split2(x):
    hi = x.astype(BF16)
    lo = (x - hi.astype(F32)).astype(BF16)
    return hi, lo


def _log_sigmoid(x):
    return -(jnp.maximum(-x, 0.0) + jnp.log1p(jnp.exp(-jnp.abs(x))))


def _silu(x):
    return x * jax.nn.sigmoid(x)


def _layernorm(x, g, b):
    mu = jnp.mean(x, axis=-1, keepdims=True)
    xc = x - mu
    var = jnp.mean(xc * xc, axis=-1, keepdims=True)
    return xc * lax.rsqrt(var + LN_EPS) * g + b


def _mod_row_tb(i):
    return jnp.where(i < N_CTX // TB, 0, 1 + (i - N_CTX // TB) // (T_LAT // TB))


def _mod_spec(j, row_fn):
    return pl.BlockSpec((1, 1, D), lambda i, *_: (row_fn(i) * 6 + j, 0, 0))


def _mod_kernel(c_ref, w_ref, b_ref, o_ref):
    o_ref[0] = _dot_f32(_silu(c_ref[...]), w_ref[0]) + b_ref[0]


def _modulation(cvecs, adaln_w, adaln_b):
    nj = 6
    return pl.pallas_call(
        _mod_kernel,
        out_shape=jax.ShapeDtypeStruct((DEPTH, MOD_ROWS, 6 * D), F32),
        grid=(DEPTH, nj),
        in_specs=[
            pl.BlockSpec((MOD_ROWS, D), lambda l, j: (0, 0)),
            pl.BlockSpec((1, D, D), lambda l, j: (l, 0, j)),
            pl.BlockSpec((1, 1, D), lambda l, j: (l, 0, j)),
        ],
        out_specs=pl.BlockSpec((1, MOD_ROWS, D), lambda l, j: (l, 0, j)),
        compiler_params=_cp(("parallel", "parallel")),
        name="adaln_modulation",
    )(cvecs, adaln_w, adaln_b.reshape(DEPTH, 1, 6 * D))


def _embed_kernel(xp_ref, xs_ref, pos_ref, o_ref):
    i = pl.program_id(0)

    @pl.when(i < N_CTX // TB)
    def _():
        o_ref[...] = xp_ref[...]

    @pl.when(i >= N_CTX // TB)
    def _():
        o_ref[...] = xs_ref[...] + pos_ref[...]


def _embed(x_prompt, x_sample, pos):
    nc = N_CTX // TB
    return pl.pallas_call(
        _embed_kernel,
        out_shape=jax.ShapeDtypeStruct((N_TOK, D), F32),
        grid=(NB,),
        in_specs=[
            pl.BlockSpec((TB, D), lambda i: (jnp.minimum(i, nc - 1), 0)),
            pl.BlockSpec((TB, D), lambda i: (jnp.maximum(i - nc, 0), 0)),
            pl.BlockSpec((TB, D), lambda i: (jnp.maximum(i - nc, 0) % (T_LAT // TB), 0)),
        ],
        out_specs=pl.BlockSpec((TB, D), lambda i: (i, 0)),
        compiler_params=_cp(("parallel",)),
        name="embed_tokens",
    )(x_prompt.reshape(N_CTX, D), x_sample.reshape(N_LAT, D), pos)


def _grid_posemb(T):
    rows = T // GRID_W
    r, cidx = jnp.meshgrid(jnp.arange(rows), jnp.arange(GRID_W), indexing="ij")
    nf = D // 4
    freqs = jnp.exp(-math.log(10000.0) * jnp.arange(nf, dtype=F32) / nf)

    def emb(p):
        a = p.reshape(-1).astype(F32)[:, None] * freqs
        return jnp.concatenate([jnp.sin(a), jnp.cos(a)], -1)

    return jnp.concatenate([emb(r), emb(cidx)], -1)


def _gla_in_kernel(x_ref, sh_ref, sc_ref, wq_ref, wk_ref, wv_ref, wr_ref, wg_ref, wgate_ref, bgate_ref,
                   q_ref, k_ref, v_ref, r_ref, g_ref):
    h = (x_ref[...] * (1.0 + sc_ref[0]) + sh_ref[0]).astype(BF16)
    q_ref[...] = _dot(h, wq_ref[...]) * (DK ** -0.5)
    k_ref[...] = _dot(h, wk_ref[...])
    v_ref[...] = _dot(h, wv_ref[...]).astype(BF16)
    r_ref[...] = _silu(_dot(h, wr_ref[...])).astype(BF16)
    glr = _dot(h, wg_ref[...]).astype(BF16)
    z = _dot(glr, wgate_ref[...]) + bgate_ref[...]
    g_ref[...] = _log_sigmoid(z) * (1.0 / GATE_TAU)


def _gla_in(x, mods, w_in, w_gate, b_gate):
    wq = w_in[:, :DKT].astype(BF16)
    wk = w_in[:, DKT:2 * DKT].astype(BF16)
    wv = w_in[:, 2 * DKT:2 * DKT + DVT].astype(BF16)
    wr = w_in[:, 2 * DKT + DVT:2 * DKT + 2 * DVT].astype(BF16)
    wg = w_in[:, 2 * DKT + 2 * DVT:].astype(BF16)
    wgate = jnp.zeros((2 * GATE_RANK, 2 * DKT), F32)
    wgate = wgate.at[:GATE_RANK, :DKT].set(w_gate[0]).at[GATE_RANK:, DKT:].set(w_gate[1]).astype(BF16)
    bgate = b_gate.reshape(1, 2 * DKT)
    full = lambda s: pl.BlockSpec(s, lambda i: (0,) * len(s), pipeline_mode=pl.Buffered(1))
    tok = lambda n: pl.BlockSpec((TB, n), lambda i: (i, 0))
    return pl.pallas_call(
        _gla_in_kernel,
        out_shape=[jax.ShapeDtypeStruct((N_TOK, DKT), F32), jax.ShapeDtypeStruct((N_TOK, DKT), F32),
                   jax.ShapeDtypeStruct((N_TOK, DVT), BF16), jax.ShapeDtypeStruct((N_TOK, DVT), BF16),
                   jax.ShapeDtypeStruct((N_TOK, 2 * DKT), F32)],
        grid=(NB,),
        in_specs=[tok(D), _mod_spec(0, _mod_row_tb), _mod_spec(1, _mod_row_tb),
                  full((D, DKT)), full((D, DKT)), full((D, DVT)), full((D, DVT)), full((D, 2 * GATE_RANK)),
                  full((2 * GATE_RANK, 2 * DKT)), full((1, 2 * DKT))],
        out_specs=[tok(DKT), tok(DKT), tok(DVT), tok(DVT), tok(2 * DKT)],
        compiler_params=_cp(("parallel",)),
        name="gla_in_proj",
    )(x, mods, mods, wq, wk, wv, wr, wg, wgate, bgate)


def _mlstm_in_kernel(x_ref, sh_ref, sc_ref, wq_ref, wk_ref, wv_ref, wr_ref, wg_ref, wgt_ref, bg_ref, bgt_ref,
                     q_ref, k_ref, v_ref, r_ref, gc_ref, gr_ref):
    h = (x_ref[...] * (1.0 + sc_ref[0]) + sh_ref[0]).astype(BF16)
    q_ref[...] = _dot(h, wq_ref[...]) * (DK ** -0.5)
    k_ref[...] = _dot(h, wk_ref[...])
    v_ref[...] = _dot(h, wv_ref[...]).astype(BF16)
    r_ref[...] = jax.nn.sigmoid(_dot(h, wr_ref[...])).astype(BF16)
    gc = _dot(h, wg_ref[...]) + bg_ref[...]
    gr = _dot_nt(wgt_ref[...], h) + bgt_ref[...]
    gc_ref[...] = jnp.where(lax.broadcasted_iota(jnp.int32, gc.shape, 1) >= GATE_F, _log_sigmoid(gc), gc)
    gr = jnp.where(lax.broadcasted_iota(jnp.int32, gr.shape, 0) >= GATE_F, _log_sigmoid(gr), gr)
    for c in range(TB // SB):
        gr_ref[c] = gr[:, c * SB:(c + 1) * SB]


def _mlstm_in(x, mods, w_in, b_gates):
    wq = w_in[:, :DKT].astype(BF16)
    wk = w_in[:, DKT:2 * DKT].astype(BF16)
    wv = w_in[:, 2 * DKT:2 * DKT + DVT].astype(BF16)
    wr = w_in[:, 2 * DKT + DVT:2 * DKT + 2 * DVT].astype(BF16)
    wg = w_in[:, 2 * DKT + 2 * DVT:].reshape(D, 2, 2, NH).transpose(0, 2, 3, 1).reshape(D, 4 * NH).astype(BF16)
    bg = b_gates.reshape(2, 2, NH).transpose(1, 2, 0).reshape(1, 4 * NH)
    full = lambda s: pl.BlockSpec(s, lambda i: (0,) * len(s), pipeline_mode=pl.Buffered(1))
    tok = lambda n: pl.BlockSpec((TB, n), lambda i: (i, 0))
    return pl.pallas_call(
        _mlstm_in_kernel,
        out_shape=[jax.ShapeDtypeStruct((N_TOK, DKT), F32), jax.ShapeDtypeStruct((N_TOK, DKT), F32),
                   jax.ShapeDtypeStruct((N_TOK, DVT), BF16), jax.ShapeDtypeStruct((N_TOK, DVT), BF16),
                   jax.ShapeDtypeStruct((N_TOK, 4 * NH), F32),
                   jax.ShapeDtypeStruct((N_TOK // SB, 4 * NH, SB), F32)],
        grid=(NB,),
        in_specs=[tok(D), _mod_spec(0, _mod_row_tb), _mod_spec(1, _mod_row_tb),
                  full((D, DKT)), full((D, DKT)), full((D, DVT)), full((D, DVT)), full((D, 4 * NH)),
                  full((4 * NH, D)), full((1, 4 * NH)), full((4 * NH, 1))],
        out_specs=[tok(DKT), tok(DKT), tok(DVT), tok(DVT),
                   tok(4 * NH),
                   pl.BlockSpec((TB // SB, 4 * NH, SB), lambda i: (i, 0, 0))],
        compiler_params=_cp(("parallel",)),
        name="mlstm_in_proj",
    )(x, mods, mods, wq, wk, wv, wr, wg, wg.T, bg, bg.T)


def _block_tri_masks():
    row = lax.broadcasted_iota(jnp.int32, (SB, SB), 0)
    col = lax.broadcasted_iota(jnp.int32, (SB, SB), 1)
    same = (row >> CH_SHIFT) == (col >> CH_SHIFT)
    return same & (row >= col), same & (row <= col)


def _chunk_scan(x, forward, axis=0, op=jnp.add, fill=0.0):
    n = x.shape[axis]
    pos = lax.broadcasted_iota(jnp.int32, x.shape, axis) & (CH - 1)
    s = 1
    while s < CH:
        if forward:
            x = op(x, jnp.where(pos >= s, pltpu.roll(x, s, axis), fill))
        else:
            x = op(x, jnp.where(pos < CH - s, pltpu.roll(x, n - s, axis), fill))
        s *= 2
    return x


def _chunk_cumsum(x, forward):
    return _chunk_scan(x, forward)


def _cat2(x):
    return jnp.concatenate([x, x], axis=1)


def _per_chunk(vals, n):
    return jnp.concatenate([jnp.broadcast_to(v, (CH, n)) for v in vals], axis=0)


def _chunk_masked_cat(x):
    xb = x.astype(BF16)
    n = x.shape[1]
    cols = []
    for c in range(NCB):
        parts = [jnp.zeros((c * CH, n), BF16), xb[c * CH:(c + 1) * CH, :], jnp.zeros((SB - (c + 1) * CH, n), BF16)]
        cols.append(jnp.concatenate([p for p in parts if p.shape[0]], axis=0))
    return jnp.concatenate(cols, axis=1)


def _head_norm_rows(o):
    mu = jnp.mean(o, axis=-1, keepdims=True)
    oc = o - mu
    var = jnp.mean(oc * oc, axis=-1, keepdims=True)
    return oc * lax.rsqrt(var + LN_EPS)


def _gla_superblock(q, k, v, v_t, g, st, forward, mask):
    b = _chunk_cumsum(g, forward)
    r_last = CH - 1 if forward else 0
    r_mid = CH // 2 if forward else CH - 1 - CH // 2
    lasts = [b[c * CH + r_last:c * CH + r_last + 1, :] for c in range(NCB)]
    b_last = _per_chunk(lasts, DK)
    b_mid = _per_chunk([b[c * CH + r_mid:c * CH + r_mid + 1, :] for c in range(NCB)], DK)
    qe = (q * jnp.exp(b - b_mid)).astype(BF16)
    ke = (k * jnp.exp(b_mid - b)).astype(BF16)
    qb = (q * jnp.exp(b)).astype(BF16)
    kd = k * jnp.exp(b_last - b)
    a = jnp.where(mask, _dot_nt(qe, ke), 0.0).astype(BF16)
    o_intra = _dot(a, v)
    d_t = _dot(v_t, _chunk_masked_cat(kd))
    inter = [None] * NCB
    for c in (range(NCB) if forward else reversed(range(NCB))):
        inter[c] = _dot_nt(qb[c * CH:(c + 1) * CH, :], st.astype(BF16))
        st = st * jnp.exp(lasts[c]) + d_t[:, c * DK:(c + 1) * DK]
    return o_intra + jnp.concatenate(inter, axis=0), st


def _scan_driver(T, o_ref, of_ref, ob_ref, step):
    nsb = T // SB
    if nsb == 1:
        step(0, 0)
    else:
        def body(i, carry):
            step(i, nsb - 1 - i)
            return carry

        lax.fori_loop(0, nsb, body, 0)

    def norm_body(j, carry):
        rows = pl.ds(pl.multiple_of(j * SB, SB), SB)
        for hh in range(HPB):
            cv = slice(hh * DV, (hh + 1) * DV)
            o_ref[rows, cv] = _head_norm_rows(of_ref[rows, cv] + ob_ref[rows, cv]).astype(BF16)
        return carry

    if nsb == 1:
        norm_body(0, 0)
    else:
        lax.fori_loop(0, nsb, norm_body, 0)


def _gla_scan_kernel(*refs, T, has_state, emit_state):
    q_ref, k_ref, v_ref, gf_ref, gb_ref = refs[:5]
    pos = 5
    if has_state:
        s0f_ref, s0b_ref = refs[pos:pos + 2]
        pos += 2
    o_ref = refs[pos]
    pos += 1
    if emit_state:
        sout_ref = refs[pos]
        pos += 1
    stf_ref, stb_ref, vt_ref, of_ref, ob_ref = refs[pos:pos + 5]

    nsb = T // SB
    for hh in range(HPB):
        for j in range(nsb):
            vt_ref[hh, j] = v_ref[j * SB:(j + 1) * SB, hh * DV:(hh + 1) * DV].astype(F32).T.astype(BF16)
        if has_state:
            stf_ref[hh] = s0f_ref[hh].T
            stb_ref[hh] = s0b_ref[hh].T
        else:
            stf_ref[hh] = jnp.zeros((DV, DK), F32)
            stb_ref[hh] = jnp.zeros((DV, DK), F32)

    lower, upper = _block_tri_masks()

    def one(j, forward, hh):
        rows = pl.ds(pl.multiple_of(j * SB, SB), SB)
        ck = slice(hh * DK, (hh + 1) * DK)
        cv = slice(hh * DV, (hh + 1) * DV)
        st_ref = stf_ref if forward else stb_ref
        o, st = _gla_superblock(q_ref[rows, ck], k_ref[rows, ck], v_ref[rows, cv].astype(BF16), vt_ref[hh, j],
                                (gf_ref if forward else gb_ref)[rows, ck], st_ref[hh], forward,
                                lower if forward else upper)
        st_ref[hh] = st
        (of_ref if forward else ob_ref)[rows, cv] = o

    def step(jf, jb):
        for hh in range(HPB):
            one(jf, True, hh)
            one(jb, False, hh)

    _scan_driver(T, o_ref, of_ref, ob_ref, step)
    if emit_state:
        for hh in range(HPB):
            sout_ref[0, hh] = stf_ref[hh].T
            sout_ref[1, hh] = stb_ref[hh].T


def _gla_scan(q, k, v, g, state, slot):
    ctx = state is None
    T = T_CTX if ctx else T_LAT
    n_seq = N_CTX_SEQ if ctx else N_LAT_SEQ
    off = 0 if ctx else N_CTX // T_LAT
    in_specs = [
        pl.BlockSpec((T, HPB * DK), lambda b, h: (off + b, h)),
        pl.BlockSpec((T, HPB * DK), lambda b, h: (off + b, h)),
        pl.BlockSpec((T, HPB * DV), lambda b, h: (off + b, h)),
        pl.BlockSpec((T, HPB * DK), lambda b, h: (off + b, h)),
        pl.BlockSpec((T, HPB * DK), lambda b, h: (off + b, NH // HPB + h)),
    ]
    args = [q, k, v, g, g]
    out_shape = [jax.ShapeDtypeStruct((n_seq * T, DVT), BF16)]
    out_specs = [pl.BlockSpec((T, HPB * DV), lambda b, h: (b, h))]
    if ctx:
        out_shape.append(jax.ShapeDtypeStruct((N_CTX_SEQ, 2, NH, DK, DV), F32))
        out_specs.append(pl.BlockSpec((None, 2, HPB, DK, DV), lambda b, h: (b, 0, h, 0, 0)))
    else:
        sq = (None, None, None, HPB, DK, DV)
        in_specs += [pl.BlockSpec(sq, lambda b, h: (b, slot, 0, h, 0, 0)),
                     pl.BlockSpec(sq, lambda b, h: (b, slot, 1, h, 0, 0))]
        args += [state, state]
    res = pl.pallas_call(
        functools.partial(_gla_scan_kernel, T=T, has_state=not ctx, emit_state=ctx),
        out_shape=out_shape,
        grid=(n_seq, NH // HPB),
        in_specs=in_specs,
        out_specs=out_specs,
        scratch_shapes=[pltpu.VMEM((HPB, DV, DK), F32), pltpu.VMEM((HPB, DV, DK), F32),
                        pltpu.VMEM((HPB, T // SB, DV, SB), BF16), pltpu.VMEM((T, HPB * DV), F32), pltpu.VMEM((T, HPB * DV), F32)],
        compiler_params=_cp(("parallel", "parallel")),
        name="gla_scan_ctx" if ctx else "gla_scan_lat",
    )(*args)
    return res


def _mlstm_scan_kernel(*refs, T, has_state, emit_state):
    q_ref, k_ref, v_ref, gc_ref, gr_ref = refs[:5]
    pos = 5
    if has_state:
        c0f_ref, c0b_ref, n0f_ref, n0b_ref, m0f_ref, m0b_ref = refs[pos:pos + 6]
        pos += 6
    o_ref = refs[pos]
    pos += 1
    if emit_state:
        cout_ref, nout_ref, mout_ref = refs[pos:pos + 3]
        pos += 3
    ct_ref, n_ref, m_ref, vt_ref, of_ref, ob_ref = refs[pos:pos + 6]

    nsb = T // SB
    for hh in range(HPB):
        for j in range(nsb):
            vt_ref[hh, j] = v_ref[j * SB:(j + 1) * SB, hh * DV:(hh + 1) * DV].astype(F32).T.astype(BF16)
        for d in range(2):
            sl = d * HPB + hh
            if has_state:
                c0, n0, m0 = ((c0f_ref, n0f_ref, m0f_ref), (c0b_ref, n0b_ref, m0b_ref))[d]
                ct_ref[sl] = c0[hh].T
                n_ref[sl] = n0[hh]
                m_ref[sl] = m0[hh]
            else:
                ct_ref[sl] = jnp.zeros((DV, DK), F32)
                n_ref[sl] = jnp.zeros((1, DK), F32)
                m_ref[sl] = jnp.zeros((1, DK), F32)

    lower, upper = _block_tri_masks()

    def gates(j, forward):
        rows = pl.ds(pl.multiple_of(j * SB, SB), SB)
        gc = gc_ref[rows, :]
        gr = gr_ref[j]
        i_col, f_col = gc[:, :GATE_F], gc[:, GATE_F:]
        b_col = _chunk_scan(f_col, forward)
        m_col = b_col + _chunk_scan(i_col - b_col, forward, op=jnp.maximum, fill=NEG_INF)
        b_row = _chunk_scan(gr[GATE_F:, :], forward, axis=1) - gr[:GATE_F, :]
        return i_col, b_col, m_col, b_row

    def one(j, forward, hh, shared):
        d = 0 if forward else 1
        sl = d * HPB + hh
        col = 2 * hh + d
        rows = pl.ds(pl.multiple_of(j * SB, SB), SB)
        q = q_ref[rows, hh * DK:(hh + 1) * DK]
        k = k_ref[rows, hh * DK:(hh + 1) * DK]
        v = v_ref[rows, hh * DV:(hh + 1) * DV].astype(BF16)
        mask = lower if forward else upper
        order = list(range(NCB)) if forward else list(reversed(range(NCB)))
        r_last = CH - 1 if forward else 0
        i_c, b_c, m_intra = [jnp.broadcast_to(a[:, col:col + 1], (SB, DK)) for a in shared[:3]]
        brow = shared[3][col:col + 1, :]
        b_last = [b_c[c * CH + r_last:c * CH + r_last + 1, :] for c in range(NCB)]
        a_c = _per_chunk(b_last, DK) - b_c + i_c
        m_loc = [jnp.max(a_c[c * CH:(c + 1) * CH, :], axis=0, keepdims=True) for c in range(NCB)]
        kw = k * jnp.exp(a_c - _per_chunk(m_loc, DK))
        kv_t = _dot(vt_ref[hh, j], _chunk_masked_cat(kw))
        k_loc = [jnp.sum(kw[c * CH:(c + 1) * CH, :], axis=0, keepdims=True) for c in range(NCB)]

        m = m_ref[sl]
        m_start, s_old, s_loc = [None] * NCB, [None] * NCB, [None] * NCB
        for c in order:
            m_start[c] = m
            m_new = jnp.maximum(b_last[c] + m, m_loc[c])
            s_old[c] = jnp.exp(b_last[c] + m - m_new)
            s_loc[c] = jnp.exp(m_loc[c] - m_new)
            m = m_new
        m_ref[sl] = m

        l_inter = b_c + _per_chunk(m_start, DK)
        m_i = jnp.maximum(l_inter, m_intra)
        d_log = jnp.where(mask, _cat2(b_c - m_i) - brow, NEG_INF)
        qb = q.astype(BF16)
        s = (_dot_nt(qb, k.astype(BF16)) * jnp.exp(d_log)).astype(BF16)
        e_inter = jnp.exp(l_inter - m_i)
        nd = _dot(s, jnp.concatenate([v, jnp.ones((SB, DK), BF16)], axis=1))
        num, den = nd[:, :DV], nd[:, DV:]

        ct = ct_ref[sl]
        nrm = n_ref[sl]
        inter = [None] * NCB
        for c in order:
            state = jnp.concatenate([ct, jnp.broadcast_to(nrm, (DK, DK))], axis=0).astype(BF16)
            inter[c] = _dot_nt(qb[c * CH:(c + 1) * CH, :], state)
            ct = s_old[c] * ct + s_loc[c] * kv_t[:, c * DK:(c + 1) * DK]
            nrm = s_old[c] * nrm + s_loc[c] * k_loc[c]
        ct_ref[sl] = ct
        n_ref[sl] = nrm

        inter = jnp.concatenate(inter, axis=0)
        num = num + _cat2(e_inter) * inter[:, :DV]
        den = den + e_inter * inter[:, DV:]
        inv = 1.0 / jnp.maximum(jnp.abs(den), jnp.exp(-m_i))
        (of_ref if forward else ob_ref)[rows, hh * DV:(hh + 1) * DV] = num * _cat2(inv)

    def step(jf, jb):
        shared_f = gates(jf, True)
        shared_b = gates(jb, False)
        for hh in range(HPB):
            one(jf, True, hh, shared_f)
            one(jb, False, hh, shared_b)

    _scan_driver(T, o_ref, of_ref, ob_ref, step)
    if emit_state:
        for d in range(2):
            for hh in range(HPB):
                cout_ref[d, hh] = ct_ref[d * HPB + hh].T
                nout_ref[d, hh] = n_ref[d * HPB + hh]
                mout_ref[d, hh] = m_ref[d * HPB + hh]


def _mlstm_scan(q, k, v, gc, gr, states, slot):
    ctx = states is None
    T = T_CTX if ctx else T_LAT
    n_seq = N_CTX_SEQ if ctx else N_LAT_SEQ
    off = 0 if ctx else N_CTX // T_LAT
    in_specs = [
        pl.BlockSpec((T, HPB * DK), lambda b, h: (off + b, h)),
        pl.BlockSpec((T, HPB * DK), lambda b, h: (off + b, h)),
        pl.BlockSpec((T, HPB * DV), lambda b, h: (off + b, h)),
        pl.BlockSpec((T, 4 * NH), lambda b, h: (off + b, 0)),
        pl.BlockSpec((T // SB, 4 * NH, SB), lambda b, h: (off + b, 0, 0)),
    ]
    assert HPB == NH
    args = [q, k, v, gc, gr]
    out_shape = [jax.ShapeDtypeStruct((n_seq * T, DVT), BF16)]
    out_specs = [pl.BlockSpec((T, HPB * DV), lambda b, h: (b, h))]
    if ctx:
        out_shape += [jax.ShapeDtypeStruct((N_CTX_SEQ, 2, NH, DK, DV), F32),
                      jax.ShapeDtypeStruct((N_CTX_SEQ, 2, NH, 1, DK), F32),
                      jax.ShapeDtypeStruct((N_CTX_SEQ, 2, NH, 1, DK), F32)]
        out_specs += [pl.BlockSpec((None, 2, HPB, DK, DV), lambda b, h: (b, 0, h, 0, 0)),
                      pl.BlockSpec((None, 2, HPB, 1, DK), lambda b, h: (b, 0, h, 0, 0)),
                      pl.BlockSpec((None, 2, HPB, 1, DK), lambda b, h: (b, 0, h, 0, 0))]
    else:
        c0, n0, m0 = states
        n0 = n0.reshape(N_LAT_SEQ, -1, 2, NH, 1, DK)
        m0 = jnp.broadcast_to(m0[..., None, None], m0.shape + (1, DK))
        sq_c = (None, None, None, HPB, DK, DV)
        sq_v = (None, None, None, HPB, 1, DK)
        for arr, sq in ((c0, sq_c), (n0, sq_v), (m0, sq_v)):
            for d in range(2):
                in_specs.append(pl.BlockSpec(sq, functools.partial(lambda b, h, d: (b, slot, d, h, 0, 0), d=d)))
                args.append(arr)
    return pl.pallas_call(
        functools.partial(_mlstm_scan_kernel, T=T, has_state=not ctx, emit_state=ctx),
        out_shape=out_shape,
        grid=(n_seq, NH // HPB),
        in_specs=in_specs,
        out_specs=out_specs,
        scratch_shapes=[pltpu.VMEM((2 * HPB, DV, DK), F32), pltpu.VMEM((2 * HPB, 1, DK), F32),
                        pltpu.VMEM((2 * HPB, 1, DK), F32),
                        pltpu.VMEM((HPB, T // SB, DV, SB), BF16), pltpu.VMEM((T, HPB * DV), F32), pltpu.VMEM((T, HPB * DV), F32)],
        compiler_params=_cp(("parallel", "parallel")),
        name="mlstm_scan_ctx" if ctx else "mlstm_scan_lat",
    )(*args)


def _route(lg):
    lane = lax.broadcasted_iota(jnp.int32, lg.shape, 1)
    big = jnp.int32(ROUTE_W)
    is_g = (lane >= NE) & (lane < NE + N_GROUPS)
    gl = jnp.where(is_g, lg, NEG_INF)
    gmax = jnp.max(gl, axis=1, keepdims=True)
    gsel = jnp.min(jnp.where(gl == gmax, lane, big), axis=1, keepdims=True) - NE
    gw = 1.0 / jnp.sum(jnp.exp(gl - gmax), axis=1, keepdims=True)
    ing = (lane < NE) & ((lane >> 3) == gsel)
    el = jnp.where(ing, lg, NEG_INF)
    emax = jnp.max(el, axis=1, keepdims=True)
    p = jnp.exp(el - emax)
    prob = p / jnp.sum(p, axis=1, keepdims=True)
    p1 = jnp.max(prob, axis=1, keepdims=True)
    i1 = jnp.min(jnp.where(ing & (prob == p1), lane, big), axis=1, keepdims=True)
    rest = ing & (lane != i1)
    prob2 = jnp.where(rest, prob, -1.0)
    p2 = jnp.max(prob2, axis=1, keepdims=True)
    i2 = jnp.min(jnp.where(rest & (prob2 == p2), lane, big), axis=1, keepdims=True)
    tot = p1 + p2
    rinfo = jnp.where(lane == 0, i1.astype(F32),
                      jnp.where(lane == 1, i2.astype(F32),
                                jnp.where(lane == 2, gw * (p1 / tot), jnp.where(lane == 3, gw * (p2 / tot), 0.0))))
    sel = jnp.where((lane == i1) | (lane == i2), 1.0, 0.0)
    return rinfo, [jnp.sum(sel[c * TP:(c + 1) * TP, :], axis=0, keepdims=True) for c in range(lg.shape[0] // TP)]


def _out_kernel(oc_ref, ol_ref, r_ref, x_ref, gate_ref, sh_ref, sc_ref, ng_ref, wo_ref, lg_ref, lb_ref, wr_ref,
                br_ref, x1_ref, h2_ref, rinfo_ref, cnt_ref):
    o = jnp.where(pl.program_id(0) < N_CTX // TB, oc_ref[...], ol_ref[...])
    y = _dot((o * ng_ref[...] * r_ref[...]).astype(BF16), wo_ref[...])
    x1 = _layernorm(DN_ALPHA * x_ref[...] + gate_ref[0] * y, lg_ref[...], lb_ref[...])
    x1_ref[...] = x1
    h2 = x1 * (1.0 + sc_ref[0]) + sh_ref[0]
    h_hi, h_lo = _split2(h2)
    h2_ref[...] = h_hi
    t = _dot(h_hi, wr_ref[...])
    lg = t[:, :ROUTE_W] + t[:, ROUTE_W:] + _dot(h_lo, wr_ref[:, :ROUTE_W]) + br_ref[...]
    rinfo_ref[...], counts = _route(lg)
    for c, count in enumerate(counts):
        cnt_ref[c] = count


def _out_proj(o_ctx, o_lat, r, x, mods, norm_g, w_out, ln_g, ln_b, w_route, b_route):
    nc = N_CTX // TB
    full = lambda s: pl.BlockSpec(s, lambda i: (0,) * len(s), pipeline_mode=pl.Buffered(1))
    tok = lambda n: pl.BlockSpec((TB, n), lambda i: (i, 0))
    return pl.pallas_call(
        _out_kernel,
        out_shape=[jax.ShapeDtypeStruct((N_TOK, D), F32), jax.ShapeDtypeStruct((N_TOK, D), BF16),
                   jax.ShapeDtypeStruct((N_TOK, ROUTE_W), F32), jax.ShapeDtypeStruct((NBP, 1, ROUTE_W), F32)],
        grid=(NB,),
        in_specs=[pl.BlockSpec((TB, DVT), lambda i: (jnp.minimum(i, nc - 1), 0)),
                  pl.BlockSpec((TB, DVT), lambda i: (jnp.maximum(i - nc, 0), 0)),
                  tok(DVT), tok(D),
                  _mod_spec(2, _mod_row_tb), _mod_spec(3, _mod_row_tb), _mod_spec(4, _mod_row_tb),
                  full((1, DVT)), full((DVT, D)), full((1, D)), full((1, D)),
                  full((D, 2 * ROUTE_W)), full((1, ROUTE_W))],
        out_specs=[tok(D), tok(D), tok(ROUTE_W), pl.BlockSpec((TB // TP, 1, ROUTE_W), lambda i: (i, 0, 0))],
        compiler_params=_cp(("parallel",)),
        name="out_proj_route",
    )(o_ctx, o_lat, r, x, mods, mods, mods, norm_g.reshape(1, DVT), w_out.astype(BF16),
      ln_g.reshape(1, D), ln_b.reshape(1, D), jnp.concatenate(_split2(w_route), axis=1), b_route)


TP = 512
NBP = N_TOK // TP
SEG = 16
R_LOC = 1536
PT = 256
XW = D + ROUTE_W
TMS = 512
R_TOT = -(-(2 * N_TOK + NBP * NE * (SEG - 1) + NE * (TMS - 1)) // TMS) * TMS
NT = R_TOT // TMS
POS_SPLIT = 64.0


def _mod_row_tp(i):
    return jnp.where(i < N_CTX // TP, 0, 1 + (i - N_CTX // TP) // (T_LAT // TP))


def _lane_pack(cols, shape):
    lane = lax.broadcasted_iota(jnp.int32, shape, 1)
    out = jnp.zeros(shape, F32)
    for j, c in enumerate(cols):
        out = jnp.where(lane == j, c, out)
    return out


WAIT_CHUNKS = (32, 4, 1)
MAXC = R_LOC // SEG
ZROWS = WAIT_CHUNKS[0] * SEG
NZ = NE + 1


def _segment_start(gtab_ref, ntot_ref, blk, make_copy):
    def per_chunk(j, carry):
        g = gtab_ref[blk * MAXC + j]
        make_copy(pl.multiple_of(j * SEG, SEG), pl.multiple_of(g, SEG), SEG).start()
        return carry

    lax.fori_loop(0, ntot_ref[blk], per_chunk, 0)


def _segment_wait(n_chunks, make_copy):
    left = n_chunks
    for chunks in WAIT_CHUNKS:
        n_wait = left // chunks

        def wait_piece(k, c, chunks=chunks):
            make_copy(0, 0, chunks * SEG).wait()
            return c

        lax.fori_loop(0, n_wait, wait_piece, 0)
        left = left - n_wait * chunks


def _perm_kernel(gtab_ref, ntot_ref, zrow_ref, zcnt_ref, h_ref, ri_ref, loffv_ref, pos_ref, xs_ref,
                 xs_scr, zero_scr, sem, zsem):
    blk = pl.program_id(0)
    slot = blk % 2

    def zero_copy(_, g, n):
        return pltpu.make_async_copy(zero_scr.at[pl.ds(0, n)], xs_ref.at[pl.ds(g, n)], zsem)

    @pl.when(blk == 0)
    def _():
        zero_scr[...] = jnp.zeros_like(zero_scr)

        def per_range(r, carry):
            n = zcnt_ref[r]
            g0 = zrow_ref[r]
            n_big = n // WAIT_CHUNKS[0]

            def per_piece(k, c, first, chunks):
                zero_copy(0, pl.multiple_of(g0 + (first + k * chunks) * SEG, SEG), chunks * SEG).start()
                return c

            lax.fori_loop(0, n_big, functools.partial(per_piece, first=0, chunks=WAIT_CHUNKS[0]), 0)
            lax.fori_loop(0, n - n_big * WAIT_CHUNKS[0],
                          functools.partial(per_piece, first=n_big * WAIT_CHUNKS[0], chunks=1), 0)
            return carry + n

        lax.fori_loop(0, NZ, per_range, 0)
    ri = ri_ref[...]
    e_a, e_b, w_a, w_b = ri[:, 0:1], ri[:, 1:2], ri[:, 2:3], ri[:, 3:4]
    lanef = lax.broadcasted_iota(jnp.int32, ri.shape, 1).astype(F32)
    is_a = lanef == e_a
    is_b = lanef == e_b
    sel = jnp.where(is_a | is_b, 1.0, 0.0).astype(BF16)
    row = lax.broadcasted_iota(jnp.int32, (TP, TP), 0)
    col = lax.broadcasted_iota(jnp.int32, (TP, TP), 1)
    earlier = jnp.where(row > col, 1.0, 0.0).astype(BF16)
    lpos = loffv_ref[0] + _dot(earlier, sel)
    pos_a = jnp.sum(jnp.where(is_a, lpos, 0.0), axis=1, keepdims=True)
    pos_b = jnp.sum(jnp.where(is_b, lpos, 0.0), axis=1, keepdims=True)
    pos_ref[...] = _lane_pack([pos_a, pos_b], ri.shape)

    hi_a = jnp.floor(pos_a * (1.0 / POS_SPLIT))
    hi_b = jnp.floor(pos_b * (1.0 / POS_SPLIT))
    parts = _lane_pack([hi_a, pos_a - POS_SPLIT * hi_a, hi_b, pos_b - POS_SPLIT * hi_b], ri.shape).astype(BF16)
    pick = jnp.where(lax.broadcasted_iota(jnp.int32, (8, ROUTE_W), 0) == lax.broadcasted_iota(jnp.int32, (8, ROUTE_W), 1),
                     1.0, 0.0).astype(BF16)
    pr = _dot_nt(pick, parts)
    pos_a_r = POS_SPLIT * pr[0:1, :] + pr[1:2, :]
    pos_b_r = POS_SPLIT * pr[2:3, :] + pr[3:4, :]

    wa1, wa2 = _split2(w_a)
    wa3 = (w_a - wa1.astype(F32) - wa2.astype(F32))
    wb1, wb2 = _split2(w_b)
    wb3 = (w_b - wb1.astype(F32) - wb2.astype(F32))
    wl = _lane_pack([wa1.astype(F32), wa2.astype(F32), wa3, wb1.astype(F32), wb2.astype(F32), wb3, e_a],
                    ri.shape).astype(BF16)
    hcat = jnp.concatenate([h_ref[...], wl], axis=1)
    for t in range(R_LOC // PT):
        rio = (lax.broadcasted_iota(jnp.int32, (PT, TP), 0) + t * PT).astype(F32)
        onehot = jnp.where((rio == pos_a_r) | (rio == pos_b_r), 1.0, 0.0).astype(BF16)
        xs_scr[slot, t * PT:(t + 1) * PT, :] = _dot(onehot, hcat).astype(BF16)

    def copy_from(s):
        return lambda l, g, n: pltpu.make_async_copy(xs_scr.at[s, pl.ds(l, n)], xs_ref.at[pl.ds(g, n)], sem.at[s])

    _segment_start(gtab_ref, ntot_ref, blk, copy_from(slot))

    @pl.when(blk > 0)
    def _():
        _segment_wait(ntot_ref[blk - 1], copy_from(1 - slot))

    @pl.when(blk == NBP - 1)
    def _():
        _segment_wait(ntot_ref[blk], copy_from(slot))
        _segment_wait(lax.fori_loop(0, NZ, lambda r, c: c + zcnt_ref[r], 0), zero_copy)


def _moe_permute(h2, rinfo, tables):
    gtab, ntot, zrow, zcnt, loffv = tables
    grid_spec = pltpu.PrefetchScalarGridSpec(
        num_scalar_prefetch=4,
        grid=(NBP,),
        in_specs=[pl.BlockSpec((TP, D), lambda i, *_: (i, 0)),
                  pl.BlockSpec((TP, ROUTE_W), lambda i, *_: (i, 0)),
                  pl.BlockSpec((1, 1, ROUTE_W), lambda i, *_: (i, 0, 0))],
        out_specs=[pl.BlockSpec((TP, ROUTE_W), lambda i, *_: (i, 0)),
                   pl.BlockSpec(memory_space=pl.ANY)],
        scratch_shapes=[pltpu.VMEM((2, R_LOC, XW), BF16), pltpu.VMEM((ZROWS, XW), BF16),
                        pltpu.SemaphoreType.DMA((2,)), pltpu.SemaphoreType.DMA],
    )
    return pl.pallas_call(
        _perm_kernel,
        out_shape=[jax.ShapeDtypeStruct((N_TOK, ROUTE_W), F32), jax.ShapeDtypeStruct((R_TOT, XW), BF16)],
        grid_spec=grid_spec,
        compiler_params=_cp(("arbitrary",)),
        name="moe_permute",
    )(gtab, ntot, zrow, zcnt, h2, rinfo, loffv)


def _ffn_kernel(te_ref, nt_ref, enext_ref, eslot_ref, xs_ref, wg_hbm, wu_hbm, wd_hbm, ys_ref,
                wg32_ref, wu32_ref, wd32_ref, wg_ref, wu_ref, wd_ref, sem, *, layer):
    i = pl.program_id(0)
    active = i < nt_ref[0]
    e_i = te_ref[i]
    new_expert = (i == 0) | (e_i != te_ref[jnp.maximum(i - 1, 0)])

    def fetch(e, s):
        return [pltpu.make_async_copy(src.at[layer, e], dst.at[s], sem.at[s])
                for src, dst in ((wg_hbm, wg32_ref), (wu_hbm, wu32_ref), (wd_hbm, wd32_ref))]

    @pl.when(active & new_expert)
    def _():
        s = eslot_ref[e_i]
        nxt = enext_ref[e_i]

        @pl.when(i == 0)
        def _():
            for c in fetch(e_i, s):
                c.start()

        @pl.when(nxt >= 0)
        def _():
            for c in fetch(nxt, 1 - s):
                c.start()

        for c in fetch(e_i, s):
            c.wait()
        wg_ref[0] = wg32_ref[s].astype(BF16)
        wu_ref[0] = wu32_ref[s].astype(BF16)
        wd_ref[0] = wd32_ref[s].astype(BF16)

    @pl.when(active)
    def _():
        e = te_ref[i].astype(F32)
        xs = xs_ref[...]
        x = xs[:, :D]
        r = xs[:, D:].astype(F32)
        w_a = r[:, 0:1] + r[:, 1:2] + r[:, 2:3]
        w_b = r[:, 3:4] + r[:, 4:5] + r[:, 5:6]
        w = jnp.where(r[:, 6:7] == e, w_a, w_b)
        hid = _silu(_dot(x, wg_ref[0])) * _dot(x, wu_ref[0]) * w
        ys_ref[...] = _dot(hid.astype(BF16), wd_ref[0]).astype(BF16)

    @pl.when(i >= nt_ref[0])
    def _():
        ys_ref[...] = jnp.zeros_like(ys_ref)


def _moe_ffn(xs, schedule, layer, w_gate, w_up, w_down):
    tile_expert, n_tiles, e_next, e_slot = schedule
    grid_spec = pltpu.PrefetchScalarGridSpec(
        num_scalar_prefetch=4,
        grid=(NT,),
        in_specs=[pl.BlockSpec((TMS, XW), lambda i, te, nt, *_: (jnp.minimum(i, nt[0] - 1), 0)),
                  pl.BlockSpec(memory_space=pl.ANY), pl.BlockSpec(memory_space=pl.ANY),
                  pl.BlockSpec(memory_space=pl.ANY)],
        out_specs=pl.BlockSpec((TMS, D), lambda i, *_: (i, 0)),
        scratch_shapes=[pltpu.VMEM((2, D, DE), F32), pltpu.VMEM((2, D, DE), F32), pltpu.VMEM((2, DE, D), F32),
                        pltpu.VMEM((1, D, DE), BF16), pltpu.VMEM((1, D, DE), BF16), pltpu.VMEM((1, DE, D), BF16),
                        pltpu.SemaphoreType.DMA((2,))],
    )
    return pl.pallas_call(
        functools.partial(_ffn_kernel, layer=layer),
        out_shape=jax.ShapeDtypeStruct((R_TOT, D), BF16),
        grid_spec=grid_spec,
        compiler_params=_cp(("arbitrary",)),
        name="moe_ffn",
    )(tile_expert, n_tiles, e_next, e_slot, xs, w_gate, w_up, w_down)


def _unperm_kernel(gtab_ref, ntot_ref, pos_ref, x1_ref, gate_ref, lg_ref, lb_ref, ys_ref, *rest, split):
    if split:
        oc_ref, ol_ref, ys_scr, sem = rest
    else:
        o_ref, ys_scr, sem = rest
    blk = pl.program_id(0)
    slot = blk % 2

    def copy_to(s):
        return lambda l, g, n: pltpu.make_async_copy(ys_ref.at[pl.ds(g, n)], ys_scr.at[s, pl.ds(l, n)], sem.at[s])

    def fetch(b, s):
        ys_scr[s] = jnp.zeros((R_LOC, D), BF16)
        _segment_start(gtab_ref, ntot_ref, b, copy_to(s))

    @pl.when(blk == 0)
    def _():
        fetch(0, 0)

    @pl.when(blk + 1 < NBP)
    def _():
        fetch(blk + 1, 1 - slot)

    _segment_wait(ntot_ref[blk], copy_to(slot))
    pos = pos_ref[...]
    pos_a, pos_b = pos[:, 0:1], pos[:, 1:2]
    acc = jnp.zeros((TP, D), F32)
    for t in range(R_LOC // PT):
        lio = (lax.broadcasted_iota(jnp.int32, (TP, PT), 1) + t * PT).astype(F32)
        onehot = jnp.where((lio == pos_a) | (lio == pos_b), 1.0, 0.0).astype(BF16)
        acc = acc + _dot(onehot, ys_scr[slot, t * PT:(t + 1) * PT, :])
    res = _layernorm(DN_ALPHA * x1_ref[...] + gate_ref[0] * acc, lg_ref[...], lb_ref[...])
    if split:
        @pl.when(blk < N_CTX // TP)
        def _():
            oc_ref[...] = res

        @pl.when(blk >= N_CTX // TP)
        def _():
            ol_ref[...] = res
    else:
        o_ref[...] = res


def _moe_unpermute(ys, pos, x1, mods, ln_g, ln_b, tables, split):
    gtab, ntot = tables[:2]
    nc = N_CTX // TP
    if split:
        out_shape = [jax.ShapeDtypeStruct((N_CTX, D), F32), jax.ShapeDtypeStruct((N_LAT, D), F32)]
        out_specs = [pl.BlockSpec((TP, D), lambda i, *_: (jnp.minimum(i, nc - 1), 0)),
                     pl.BlockSpec((TP, D), lambda i, *_: (jnp.maximum(i - nc, 0), 0))]
    else:
        out_shape = jax.ShapeDtypeStruct((N_TOK, D), F32)
        out_specs = pl.BlockSpec((TP, D), lambda i, *_: (i, 0))
    grid_spec = pltpu.PrefetchScalarGridSpec(
        num_scalar_prefetch=2,
        grid=(NBP,),
        in_specs=[pl.BlockSpec((TP, ROUTE_W), lambda i, *_: (i, 0)),
                  pl.BlockSpec((TP, D), lambda i, *_: (i, 0)),
                  _mod_spec(5, _mod_row_tp),
                  pl.BlockSpec((1, D), lambda i, *_: (0, 0)),
                  pl.BlockSpec((1, D), lambda i, *_: (0, 0)),
                  pl.BlockSpec(memory_space=pl.ANY)],
        out_specs=out_specs,
        scratch_shapes=[pltpu.VMEM((2, R_LOC, D), BF16), pltpu.SemaphoreType.DMA((2,))],
    )
    return pl.pallas_call(
        functools.partial(_unperm_kernel, split=split),
        out_shape=out_shape,
        grid_spec=grid_spec,
        compiler_params=_cp(("arbitrary",)),
        name="moe_unpermute",
    )(gtab, ntot, pos, x1, mods, ln_g.reshape(1, D), ln_b.reshape(1, D), ys)


def _segment_tables(cnt):
    c = cnt[:, 0, :NE].astype(jnp.int32)
    pc = (c + SEG - 1) // SEG * SEG
    loff = jnp.cumsum(pc, axis=1) - pc
    tot = pc.sum(axis=0)
    totp = (tot + TMS - 1) // TMS * TMS
    eend = jnp.cumsum(totp)
    goff = (eend - totp)[None, :] + jnp.cumsum(pc, axis=0) - pc
    n_tiles = (eend[-1] // TMS).reshape(1)
    tile_id = jnp.arange(NT, dtype=jnp.int32)
    tile_expert = jnp.minimum(jnp.sum(tile_id[:, None] >= (eend // TMS)[None, :], axis=1), NE - 1).astype(jnp.int32)
    loffv = jnp.zeros((NBP, 1, ROUTE_W), F32).at[:, 0, :NE].set(loff.astype(F32))
    ntot = (pc.sum(axis=1) // SEG).astype(jnp.int32)
    chunk = jnp.arange(MAXC, dtype=jnp.int32)
    first = loff // SEG
    e_of = jnp.minimum(jnp.sum(chunk[None, :, None] >= (first + pc // SEG)[:, None, :], axis=2), NE - 1)
    is_e = e_of[:, :, None] == jnp.arange(NE, dtype=jnp.int32)[None, None, :]
    pick = lambda a: jnp.sum(jnp.where(is_e, a[:, None, :], 0), axis=2)
    gtab = (pick(goff) + (chunk[None, :] - pick(first)) * SEG).reshape(-1).astype(jnp.int32)
    zrow = jnp.concatenate([eend - totp + tot, eend[-1:]]).astype(jnp.int32)
    zcnt = jnp.concatenate([(totp - tot) // SEG, (R_TOT - eend[-1:]) // SEG]).astype(jnp.int32)
    ids = jnp.arange(NE, dtype=jnp.int32)
    used = totp > 0
    later = (ids[None, :] > ids[:, None]) & used[None, :]
    e_next = jnp.min(jnp.where(later, ids[None, :], NE), axis=1)
    e_next = jnp.where(e_next == NE, -1, e_next).astype(jnp.int32)
    e_slot = ((jnp.cumsum(used.astype(jnp.int32)) - 1) % 2).astype(jnp.int32)
    return (gtab, ntot, zrow, zcnt, loffv), (tile_expert, n_tiles.astype(jnp.int32), e_next, e_slot)


def _moe(h2, rinfo, cnt, x1, mods, ln_g, ln_b, layer, w_gate, w_up, w_down, split):
    tables, schedule = _segment_tables(cnt)
    pos, xs = _moe_permute(h2, rinfo, tables)
    ys = _moe_ffn(xs, schedule, layer, w_gate, w_up, w_down)
    return _moe_unpermute(ys, pos, x1, mods, ln_g, ln_b, tables, split)


def kernel(x_prompt, x_sample, state_gla_S, state_mlstm_C, state_mlstm_n, state_mlstm_m, c, c_ctx,
           gla_w_in, gla_w_gate, gla_b_gate, gla_norm_g, gla_w_out,
           mlstm_w_in, mlstm_b_gates, mlstm_norm_g, mlstm_w_out,
           adaln_w, adaln_b, ln_g, ln_b,
           moe_w_group, moe_b_group, moe_w_expert, moe_b_expert, moe_w_gate, moe_w_up, moe_w_down):
    cvecs = jnp.zeros((MOD_ROWS, D), F32).at[0].set(c_ctx).at[1:1 + N_LAT_SEQ].set(c)
    mods_all = _modulation(cvecs, adaln_w, adaln_b)
    x = _embed(x_prompt, x_sample, _grid_posemb(T_LAT))
    gla_states, ml_c, ml_n, ml_m = [], [], [], []
    for l in range(DEPTH):
        s = l // 2
        mods = mods_all[l].reshape(MOD_ROWS * 6, 1, D)
        if l % 2 == 0:
            q, k, v, r, g = _gla_in(x, mods, gla_w_in[s], gla_w_gate[s], gla_b_gate[s])
            o_ctx, s_new = _gla_scan(q, k, v, g, None, s)
            (o_lat,) = _gla_scan(q, k, v, g, state_gla_S, s)
            gla_states.append(s_new)
            norm_g, w_out = gla_norm_g[s], gla_w_out[s]
        else:
            q, k, v, r, gc, gr = _mlstm_in(x, mods, mlstm_w_in[s], mlstm_b_gates[s])
            o_ctx, c_new, n_new, m_new = _mlstm_scan(q, k, v, gc, gr, None, s)
            (o_lat,) = _mlstm_scan(q, k, v, gc, gr, (state_mlstm_C, state_mlstm_n, state_mlstm_m), s)
            ml_c.append(c_new)
            ml_n.append(n_new[:, :, :, 0, :])
            ml_m.append(m_new[:, :, :, 0, 0])
            norm_g, w_out = mlstm_norm_g[s], mlstm_w_out[s]
        w_route = jnp.zeros((D, ROUTE_W), F32).at[:, :NE].set(moe_w_expert[l]).at[:, NE:NE + N_GROUPS].set(
            moe_w_group[l])
        b_route = jnp.zeros((1, ROUTE_W), F32).at[0, :NE].set(moe_b_expert[l]).at[0, NE:NE + N_GROUPS].set(
            moe_b_group[l])
        x1, h2, rinfo, cnt = _out_proj(o_ctx, o_lat, r, x, mods, norm_g, w_out, ln_g[l, 0], ln_b[l, 0], w_route, b_route)
        x = _moe(h2, rinfo, cnt, x1, mods, ln_g[l, 1], ln_b[l, 1], l, moe_w_gate, moe_w_up, moe_w_down,
                 split=(l == DEPTH - 1))
    y_ctx = x[0].reshape(N_CTX_SEQ, T_CTX, D)
    y_lat = x[1].reshape(N_LAT_SEQ, T_LAT, D)
    return (y_ctx, y_lat, jnp.stack(gla_states, 1), jnp.stack(ml_c, 1), jnp.stack(ml_n, 1), jnp.stack(ml_m, 1))
```

```python
import functools
import math

import jax
import jax.numpy as jnp
from jax import lax
from jax.experimental import pallas as pl
from jax.experimental.pallas import tpu as pltpu

F32 = jnp.float32
BF16 = jnp.bfloat16

D = 1024
N_CTX_SEQ, T_CTX = 16, 256
N_LAT_SEQ, T_LAT = 8, 1024
N_CTX = N_CTX_SEQ * T_CTX
N_LAT = N_LAT_SEQ * T_LAT
N_TOK = N_CTX + N_LAT
DEPTH = 4
GRID_W = 64
NH = 4
DK = 128
DV = 256
DKT = NH * DK
DVT = NH * DV
GATE_RANK = 16
GATE_TAU = 16.0
CH = 64
N_GROUPS = 4
EPG = 8
NE = N_GROUPS * EPG
DE = 256
DN_ALPHA = (2.0 * DEPTH) ** 0.25
LN_EPS = 1e-5
MOD_ROWS = 16
TB = 1024
HPB = 4
GATE_F = 2 * NH
SB = 256
NCB = SB // CH
CH_SHIFT = 6
NB = N_TOK // TB
ROUTE_W = 128
NEG_INF = float("-inf")
VMEM_LIMIT = 56 * 1024 * 1024


def _cp(sem):
    return pltpu.CompilerParams(dimension_semantics=sem, vmem_limit_bytes=VMEM_LIMIT)


def _dot(a, b):
    return jnp.dot(a, b, preferred_element_type=F32)


def _dot_nt(a, b):
    return lax.dot_general(a, b, (((1,), (1,)), ((), ())), preferred_element_type=F32)


def _dot_f32(a, b):
    return jnp.dot(a, b, preferred_element_type=F32, precision=lax.Precision.HIGHEST)


def _split2(x):
    hi = x.astype(BF16)
    lo = (x - hi.astype(F32)).astype(BF16)
    return hi, lo


def _log_sigmoid(x):
    return -(jnp.maximum(-x, 0.0) + jnp.log1p(jnp.exp(-jnp.abs(x))))


def _silu(x):
    return x * jax.nn.sigmoid(x)


def _layernorm(x, g, b):
    mu = jnp.mean(x, axis=-1, keepdims=True)
    xc = x - mu
    var = jnp.mean(xc * xc, axis=-1, keepdims=True)
    return xc * lax.rsqrt(var + LN_EPS) * g + b


def _mod_row_tb(i):
    return jnp.where(i < N_CTX // TB, 0, 1 + (i - N_CTX // TB) // (T_LAT // TB))


def _mod_spec(j, row_fn):
    return pl.BlockSpec((1, 1, D), lambda i, *_: (row_fn(i) * 6 + j, 0, 0))


def _mod_kernel(c_ref, w_ref, b_ref, o_ref):
    o_ref[0] = _dot_f32(_silu(c_ref[...]), w_ref[0]) + b_ref[0]


def _modulation(cvecs, adaln_w, adaln_b):
    bw = 2 * D
    return pl.pallas_call(
        _mod_kernel,
        out_shape=jax.ShapeDtypeStruct((DEPTH, MOD_ROWS, 6 * D), F32),
        grid=(DEPTH, 6 * D // bw),
        in_specs=[
            pl.BlockSpec((MOD_ROWS, D), lambda l, j: (0, 0)),
            pl.BlockSpec((1, D, bw), lambda l, j: (l, 0, j)),
            pl.BlockSpec((1, 1, bw), lambda l, j: (l, 0, j)),
        ],
        out_specs=pl.BlockSpec((1, MOD_ROWS, bw), lambda l, j: (l, 0, j)),
        compiler_params=_cp(("parallel", "parallel")),
        name="adaln_modulation",
    )(cvecs, adaln_w, adaln_b.reshape(DEPTH, 1, 6 * D))


def _token_specs(x):
    if not isinstance(x, tuple):
        return [pl.BlockSpec((TB, D), lambda i: (i, 0))], [x]
    nc = N_CTX // TB
    assert TB == T_LAT
    specs = [pl.BlockSpec((TB, D), lambda i: (jnp.minimum(i, nc - 1), 0)),
             pl.BlockSpec((TB, D), lambda i: (jnp.maximum(i - nc, 0), 0)),
             pl.BlockSpec((TB, D), lambda i: (0, 0), pipeline_mode=pl.Buffered(1))]
    return specs, list(x)


def _token_block(x_refs):
    if len(x_refs) == 1:
        return x_refs[0][...]
    xp_ref, xs_ref, pos_ref = x_refs
    return jnp.where(pl.program_id(0) < N_CTX // TB, xp_ref[...], xs_ref[...] + pos_ref[...])


def _grid_posemb(T):
    rows = T // GRID_W
    r, cidx = jnp.meshgrid(jnp.arange(rows), jnp.arange(GRID_W), indexing="ij")
    nf = D // 4
    freqs = jnp.exp(-math.log(10000.0) * jnp.arange(nf, dtype=F32) / nf)

    def emb(p):
        a = p.reshape(-1).astype(F32)[:, None] * freqs
        return jnp.concatenate([jnp.sin(a), jnp.cos(a)], -1)

    return jnp.concatenate([emb(r), emb(cidx)], -1)


def _gla_in_kernel(*refs, n_x):
    (sh_ref, sc_ref, wq_ref, wk_ref, wv_ref, wr_ref, wg_ref, wgate_ref, bgate_ref,
     q_ref, k_ref, v_ref, r_ref, g_ref) = refs[n_x:]
    h = (_token_block(refs[:n_x]) * (1.0 + sc_ref[0]) + sh_ref[0]).astype(BF16)
    q_ref[...] = _dot(h, wq_ref[...]) * (DK ** -0.5)
    k_ref[...] = _dot(h, wk_ref[...])
    v_ref[...] = _dot(h, wv_ref[...]).astype(BF16)
    r_ref[...] = _silu(_dot(h, wr_ref[...])).astype(BF16)
    glr = _dot(h, wg_ref[...]).astype(BF16)
    z = _dot(glr, wgate_ref[...]) + bgate_ref[...]
    g_ref[...] = _log_sigmoid(z) * (1.0 / GATE_TAU)


def _gla_in(x, mods, w_in, w_gate, b_gate):
    wq = w_in[:, :DKT].astype(BF16)
    wk = w_in[:, DKT:2 * DKT].astype(BF16)
    wv = w_in[:, 2 * DKT:2 * DKT + DVT].astype(BF16)
    wr = w_in[:, 2 * DKT + DVT:2 * DKT + 2 * DVT].astype(BF16)
    wg = w_in[:, 2 * DKT + 2 * DVT:].astype(BF16)
    wgate = jnp.zeros((2 * GATE_RANK, 2 * DKT), F32)
    wgate = wgate.at[:GATE_RANK, :DKT].set(w_gate[0]).at[GATE_RANK:, DKT:].set(w_gate[1]).astype(BF16)
    bgate = b_gate.reshape(1, 2 * DKT)
    full = lambda s: pl.BlockSpec(s, lambda i: (0,) * len(s), pipeline_mode=pl.Buffered(1))
    tok = lambda n: pl.BlockSpec((TB, n), lambda i: (i, 0))
    x_specs, x_args = _token_specs(x)
    return pl.pallas_call(
        functools.partial(_gla_in_kernel, n_x=len(x_args)),
        out_shape=[jax.ShapeDtypeStruct((N_TOK, DKT), F32), jax.ShapeDtypeStruct((N_TOK, DKT), F32),
                   jax.ShapeDtypeStruct((N_TOK, DVT), BF16), jax.ShapeDtypeStruct((N_TOK, DVT), BF16),
                   jax.ShapeDtypeStruct((N_TOK, 2 * DKT), F32)],
        grid=(NB,),
        in_specs=x_specs + [_mod_spec(0, _mod_row_tb), _mod_spec(1, _mod_row_tb),
                            full((D, DKT)), full((D, DKT)), full((D, DVT)), full((D, DVT)),
                            full((D, 2 * GATE_RANK)), full((2 * GATE_RANK, 2 * DKT)), full((1, 2 * DKT))],
        out_specs=[tok(DKT), tok(DKT), tok(DVT), tok(DVT), tok(2 * DKT)],
        compiler_params=_cp(("parallel",)),
        name="gla_in_proj",
    )(*x_args, mods, mods, wq, wk, wv, wr, wg, wgate, bgate)


def _mlstm_in_kernel(x_ref, sh_ref, sc_ref, wq_ref, wk_ref, wv_ref, wr_ref, wg_ref, wgt_ref, bg_ref, bgt_ref,
                     q_ref, k_ref, v_ref, r_ref, gc_ref, gr_ref):
    h = (x_ref[...] * (1.0 + sc_ref[0]) + sh_ref[0]).astype(BF16)
    q_ref[...] = _dot(h, wq_ref[...]) * (DK ** -0.5)
    k_ref[...] = _dot(h, wk_ref[...])
    v_ref[...] = _dot(h, wv_ref[...]).astype(BF16)
    r_ref[...] = jax.nn.sigmoid(_dot(h, wr_ref[...])).astype(BF16)
    gc = _dot(h, wg_ref[...]) + bg_ref[...]
    gr = _dot_nt(wgt_ref[...], h) + bgt_ref[...]
    gc_ref[...] = jnp.where(lax.broadcasted_iota(jnp.int32, gc.shape, 1) >= GATE_F, _log_sigmoid(gc), gc)
    gr = jnp.where(lax.broadcasted_iota(jnp.int32, gr.shape, 0) >= GATE_F, _log_sigmoid(gr), gr)
    for c in range(TB // SB):
        gr_ref[c] = gr[:, c * SB:(c + 1) * SB]


def _mlstm_in(x, mods, w_in, b_gates):
    wq = w_in[:, :DKT].astype(BF16)
    wk = w_in[:, DKT:2 * DKT].astype(BF16)
    wv = w_in[:, 2 * DKT:2 * DKT + DVT].astype(BF16)
    wr = w_in[:, 2 * DKT + DVT:2 * DKT + 2 * DVT].astype(BF16)
    wg = w_in[:, 2 * DKT + 2 * DVT:].reshape(D, 2, 2, NH).transpose(0, 2, 3, 1).reshape(D, 4 * NH).astype(BF16)
    bg = b_gates.reshape(2, 2, NH).transpose(1, 2, 0).reshape(1, 4 * NH)
    full = lambda s: pl.BlockSpec(s, lambda i: (0,) * len(s), pipeline_mode=pl.Buffered(1))
    tok = lambda n: pl.BlockSpec((TB, n), lambda i: (i, 0))
    return pl.pallas_call(
        _mlstm_in_kernel,
        out_shape=[jax.ShapeDtypeStruct((N_TOK, DKT), F32), jax.ShapeDtypeStruct((N_TOK, DKT), F32),
                   jax.ShapeDtypeStruct((N_TOK, DVT), BF16), jax.ShapeDtypeStruct((N_TOK, DVT), BF16),
                   jax.ShapeDtypeStruct((N_TOK, 4 * NH), F32),
                   jax.ShapeDtypeStruct((N_TOK // SB, 4 * NH, SB), F32)],
        grid=(NB,),
        in_specs=[tok(D), _mod_spec(0, _mod_row_tb), _mod_spec(1, _mod_row_tb),
                  full((D, DKT)), full((D, DKT)), full((D, DVT)), full((D, DVT)), full((D, 4 * NH)),
                  full((4 * NH, D)), full((1, 4 * NH)), full((4 * NH, 1))],
        out_specs=[tok(DKT), tok(DKT), tok(DVT), tok(DVT),
                   tok(4 * NH),
                   pl.BlockSpec((TB // SB, 4 * NH, SB), lambda i: (i, 0, 0))],
        compiler_params=_cp(("parallel",)),
        name="mlstm_in_proj",
    )(x, mods, mods, wq, wk, wv, wr, wg, wg.T, bg, bg.T)


def _block_tri_masks():
    row = lax.broadcasted_iota(jnp.int32, (SB, SB), 0)
    col = lax.broadcasted_iota(jnp.int32, (SB, SB), 1)
    same = (row >> CH_SHIFT) == (col >> CH_SHIFT)
    return same & (row >= col), same & (row <= col)


def _chunk_scan(x, forward, axis=0, op=jnp.add, fill=0.0):
    n = x.shape[axis]
    pos = lax.broadcasted_iota(jnp.int32, x.shape, axis) & (CH - 1)
    s = 1
    while s < CH:
        if forward:
            x = op(x, jnp.where(pos >= s, pltpu.roll(x, s, axis), fill))
        else:
            x = op(x, jnp.where(pos < CH - s, pltpu.roll(x, n - s, axis), fill))
        s *= 2
    return x


def _chunk_cumsum(x, forward):
    return _chunk_scan(x, forward)


def _cat2(x):
    return jnp.concatenate([x, x], axis=1)


def _per_chunk(vals, n):
    return jnp.concatenate([jnp.broadcast_to(v, (CH, n)) for v in vals], axis=0)


def _chunk_masked_cat(x):
    xb = x.astype(BF16)
    n = x.shape[1]
    cols = []
    for c in range(NCB):
        parts = [jnp.zeros((c * CH, n), BF16), xb[c * CH:(c + 1) * CH, :], jnp.zeros((SB - (c + 1) * CH, n), BF16)]
        cols.append(jnp.concatenate([p for p in parts if p.shape[0]], axis=0))
    return jnp.concatenate(cols, axis=1)


def _head_norm_rows(o):
    mu = jnp.mean(o, axis=-1, keepdims=True)
    oc = o - mu
    var = jnp.mean(oc * oc, axis=-1, keepdims=True)
    return oc * lax.rsqrt(var + LN_EPS)


def _gla_superblock(q, k, v, v_t, g, st, forward, mask):
    b = _chunk_cumsum(g, forward)
    r_last = CH - 1 if forward else 0
    r_mid = CH // 2 if forward else CH - 1 - CH // 2
    lasts = [b[c * CH + r_last:c * CH + r_last + 1, :] for c in range(NCB)]
    b_last = _per_chunk(lasts, DK)
    b_mid = _per_chunk([b[c * CH + r_mid:c * CH + r_mid + 1, :] for c in range(NCB)], DK)
    qe = (q * jnp.exp(b - b_mid)).astype(BF16)
    ke = (k * jnp.exp(b_mid - b)).astype(BF16)
    qb = (q * jnp.exp(b)).astype(BF16)
    kd = k * jnp.exp(b_last - b)
    a = jnp.where(mask, _dot_nt(qe, ke), 0.0).astype(BF16)
    o_intra = _dot(a, v)
    d_t = _dot(v_t, _chunk_masked_cat(kd))
    inter = [None] * NCB
    for c in (range(NCB) if forward else reversed(range(NCB))):
        inter[c] = _dot_nt(qb[c * CH:(c + 1) * CH, :], st.astype(BF16))
        st = st * jnp.exp(lasts[c]) + d_t[:, c * DK:(c + 1) * DK]
    return o_intra + jnp.concatenate(inter, axis=0), st


def _scan_driver(T, o_ref, of_ref, ob_ref, step):
    nsb = T // SB
    if nsb == 1:
        step(0, 0)
    else:
        def body(i, carry):
            step(i, nsb - 1 - i)
            return carry

        lax.fori_loop(0, nsb, body, 0)

    def norm_body(j, carry):
        rows = pl.ds(pl.multiple_of(j * SB, SB), SB)
        for hh in range(HPB):
            cv = slice(hh * DV, (hh + 1) * DV)
            o_ref[rows, cv] = _head_norm_rows(of_ref[rows, cv] + ob_ref[rows, cv]).astype(BF16)
        return carry

    if nsb == 1:
        norm_body(0, 0)
    else:
        lax.fori_loop(0, nsb, norm_body, 0)


def _gla_scan_kernel(*refs, T, has_state, emit_state):
    q_ref, k_ref, v_ref, gf_ref, gb_ref = refs[:5]
    pos = 5
    if has_state:
        s0f_ref, s0b_ref = refs[pos:pos + 2]
        pos += 2
    o_ref = refs[pos]
    pos += 1
    if emit_state:
        sout_ref = refs[pos]
        pos += 1
    stf_ref, stb_ref, vt_ref, of_ref, ob_ref = refs[pos:pos + 5]

    nsb = T // SB
    for hh in range(HPB):
        for j in range(nsb):
            vt_ref[hh, j] = v_ref[j * SB:(j + 1) * SB, hh * DV:(hh + 1) * DV].astype(F32).T.astype(BF16)
        if has_state:
            stf_ref[hh] = s0f_ref[hh].T
            stb_ref[hh] = s0b_ref[hh].T
        else:
            stf_ref[hh] = jnp.zeros((DV, DK), F32)
            stb_ref[hh] = jnp.zeros((DV, DK), F32)

    lower, upper = _block_tri_masks()

    def one(j, forward, hh):
        rows = pl.ds(pl.multiple_of(j * SB, SB), SB)
        ck = slice(hh * DK, (hh + 1) * DK)
        cv = slice(hh * DV, (hh + 1) * DV)
        st_ref = stf_ref if forward else stb_ref
        o, st = _gla_superblock(q_ref[rows, ck], k_ref[rows, ck], v_ref[rows, cv].astype(BF16), vt_ref[hh, j],
                                (gf_ref if forward else gb_ref)[rows, ck], st_ref[hh], forward,
                                lower if forward else upper)
        st_ref[hh] = st
        (of_ref if forward else ob_ref)[rows, cv] = o

    def step(jf, jb):
        for hh in range(HPB):
            one(jf, True, hh)
            one(jb, False, hh)

    _scan_driver(T, o_ref, of_ref, ob_ref, step)
    if emit_state:
        for hh in range(HPB):
            sout_ref[0, hh] = stf_ref[hh].T
            sout_ref[1, hh] = stb_ref[hh].T


def _gla_scan(q, k, v, g, state, slot):
    ctx = state is None
    T = T_CTX if ctx else T_LAT
    n_seq = N_CTX_SEQ if ctx else N_LAT_SEQ
    off = 0 if ctx else N_CTX // T_LAT
    in_specs = [
        pl.BlockSpec((T, HPB * DK), lambda b, h: (off + b, h)),
        pl.BlockSpec((T, HPB * DK), lambda b, h: (off + b, h)),
        pl.BlockSpec((T, HPB * DV), lambda b, h: (off + b, h)),
        pl.BlockSpec((T, HPB * DK), lambda b, h: (off + b, h)),
        pl.BlockSpec((T, HPB * DK), lambda b, h: (off + b, NH // HPB + h)),
    ]
    args = [q, k, v, g, g]
    out_shape = [jax.ShapeDtypeStruct((n_seq * T, DVT), BF16)]
    out_specs = [pl.BlockSpec((T, HPB * DV), lambda b, h: (b, h))]
    if ctx:
        out_shape.append(jax.ShapeDtypeStruct((N_CTX_SEQ, 2, NH, DK, DV), F32))
        out_specs.append(pl.BlockSpec((None, 2, HPB, DK, DV), lambda b, h: (b, 0, h, 0, 0)))
    else:
        sq = (None, None, None, HPB, DK, DV)
        in_specs += [pl.BlockSpec(sq, lambda b, h: (b, slot, 0, h, 0, 0)),
                     pl.BlockSpec(sq, lambda b, h: (b, slot, 1, h, 0, 0))]
        args += [state, state]
    res = pl.pallas_call(
        functools.partial(_gla_scan_kernel, T=T, has_state=not ctx, emit_state=ctx),
        out_shape=out_shape,
        grid=(n_seq, NH // HPB),
        in_specs=in_specs,
        out_specs=out_specs,
        scratch_shapes=[pltpu.VMEM((HPB, DV, DK), F32), pltpu.VMEM((HPB, DV, DK), F32),
                        pltpu.VMEM((HPB, T // SB, DV, SB), BF16), pltpu.VMEM((T, HPB * DV), F32), pltpu.VMEM((T, HPB * DV), F32)],
        compiler_params=_cp(("parallel", "parallel")),
        name="gla_scan_ctx" if ctx else "gla_scan_lat",
    )(*args)
    return res


def _mlstm_scan_kernel(*refs, T, has_state, emit_state):
    q_ref, k_ref, v_ref, gc_ref, gr_ref = refs[:5]
    pos = 5
    if has_state:
        c0f_ref, c0b_ref, n0f_ref, n0b_ref, m0f_ref, m0b_ref = refs[pos:pos + 6]
        pos += 6
    o_ref = refs[pos]
    pos += 1
    if emit_state:
        cout_ref, nout_ref, mout_ref = refs[pos:pos + 3]
        pos += 3
    ct_ref, n_ref, m_ref, vt_ref, of_ref, ob_ref = refs[pos:pos + 6]

    nsb = T // SB
    for hh in range(HPB):
        for j in range(nsb):
            vt_ref[hh, j] = v_ref[j * SB:(j + 1) * SB, hh * DV:(hh + 1) * DV].astype(F32).T.astype(BF16)
        for d in range(2):
            sl = d * HPB + hh
            if has_state:
                c0, n0, m0 = ((c0f_ref, n0f_ref, m0f_ref), (c0b_ref, n0b_ref, m0b_ref))[d]
                ct_ref[sl] = c0[hh].T
                n_ref[sl] = n0[hh]
                m_ref[sl] = m0[hh]
            else:
                ct_ref[sl] = jnp.zeros((DV, DK), F32)
                n_ref[sl] = jnp.zeros((1, DK), F32)
                m_ref[sl] = jnp.zeros((1, DK), F32)

    lower, upper = _block_tri_masks()

    def gates(j, forward):
        rows = pl.ds(pl.multiple_of(j * SB, SB), SB)
        gc = gc_ref[rows, :]
        gr = gr_ref[j]
        i_col, f_col = gc[:, :GATE_F], gc[:, GATE_F:]
        b_col = _chunk_scan(f_col, forward)
        m_col = b_col + _chunk_scan(i_col - b_col, forward, op=jnp.maximum, fill=NEG_INF)
        b_row = _chunk_scan(gr[GATE_F:, :], forward, axis=1) - gr[:GATE_F, :]
        return i_col, b_col, m_col, b_row

    def one(j, forward, hh, shared):
        d = 0 if forward else 1
        sl = d * HPB + hh
        col = 2 * hh + d
        rows = pl.ds(pl.multiple_of(j * SB, SB), SB)
        q = q_ref[rows, hh * DK:(hh + 1) * DK]
        k = k_ref[rows, hh * DK:(hh + 1) * DK]
        v = v_ref[rows, hh * DV:(hh + 1) * DV].astype(BF16)
        mask = lower if forward else upper
        order = list(range(NCB)) if forward else list(reversed(range(NCB)))
        r_last = CH - 1 if forward else 0
        i_c, b_c, m_intra = [jnp.broadcast_to(a[:, col:col + 1], (SB, DK)) for a in shared[:3]]
        brow = shared[3][col:col + 1, :]
        b_last = [b_c[c * CH + r_last:c * CH + r_last + 1, :] for c in range(NCB)]
        a_c = _per_chunk(b_last, DK) - b_c + i_c
        m_loc = [jnp.max(a_c[c * CH:(c + 1) * CH, :], axis=0, keepdims=True) for c in range(NCB)]
        kw = k * jnp.exp(a_c - _per_chunk(m_loc, DK))
        kv_t = _dot(vt_ref[hh, j], _chunk_masked_cat(kw))
        k_loc = [jnp.sum(kw[c * CH:(c + 1) * CH, :], axis=0, keepdims=True) for c in range(NCB)]

        m = m_ref[sl]
        m_start, s_old, s_loc = [None] * NCB, [None] * NCB, [None] * NCB
        for c in order:
            m_start[c] = m
            m_new = jnp.maximum(b_last[c] + m, m_loc[c])
            s_old[c] = jnp.exp(b_last[c] + m - m_new)
            s_loc[c] = jnp.exp(m_loc[c] - m_new)
            m = m_new
        m_ref[sl] = m

        l_inter = b_c + _per_chunk(m_start, DK)
        m_i = jnp.maximum(l_inter, m_intra)
        d_log = jnp.where(mask, _cat2(b_c - m_i) - brow, NEG_INF)
        qb = q.astype(BF16)
        s = (_dot_nt(qb, k.astype(BF16)) * jnp.exp(d_log)).astype(BF16)
        e_inter = jnp.exp(l_inter - m_i)
        nd = _dot(s, jnp.concatenate([v, jnp.ones((SB, DK), BF16)], axis=1))
        num, den = nd[:, :DV], nd[:, DV:]

        ct = ct_ref[sl]
        nrm = n_ref[sl]
        inter = [None] * NCB
        for c in order:
            state = jnp.concatenate([ct, jnp.broadcast_to(nrm, (DK, DK))], axis=0).astype(BF16)
            inter[c] = _dot_nt(qb[c * CH:(c + 1) * CH, :], state)
            ct = s_old[c] * ct + s_loc[c] * kv_t[:, c * DK:(c + 1) * DK]
            nrm = s_old[c] * nrm + s_loc[c] * k_loc[c]
        ct_ref[sl] = ct
        n_ref[sl] = nrm

        inter = jnp.concatenate(inter, axis=0)
        num = num + _cat2(e_inter) * inter[:, :DV]
        den = den + e_inter * inter[:, DV:]
        inv = 1.0 / jnp.maximum(jnp.abs(den), jnp.exp(-m_i))
        (of_ref if forward else ob_ref)[rows, hh * DV:(hh + 1) * DV] = num * _cat2(inv)

    def step(jf, jb):
        shared_f = gates(jf, True)
        shared_b = gates(jb, False)
        for hh in range(HPB):
            one(jf, True, hh, shared_f)
            one(jb, False, hh, shared_b)

    _scan_driver(T, o_ref, of_ref, ob_ref, step)
    if emit_state:
        for d in range(2):
            for hh in range(HPB):
                cout_ref[d, hh] = ct_ref[d * HPB + hh].T
                nout_ref[d, hh] = n_ref[d * HPB + hh]
                mout_ref[d, hh] = m_ref[d * HPB + hh]


def _mlstm_scan(q, k, v, gc, gr, states, slot):
    ctx = states is None
    T = T_CTX if ctx else T_LAT
    n_seq = N_CTX_SEQ if ctx else N_LAT_SEQ
    off = 0 if ctx else N_CTX // T_LAT
    in_specs = [
        pl.BlockSpec((T, HPB * DK), lambda b, h: (off + b, h)),
        pl.BlockSpec((T, HPB * DK), lambda b, h: (off + b, h)),
        pl.BlockSpec((T, HPB * DV), lambda b, h: (off + b, h)),
        pl.BlockSpec((T, 4 * NH), lambda b, h: (off + b, 0)),
        pl.BlockSpec((T // SB, 4 * NH, SB), lambda b, h: (off + b, 0, 0)),
    ]
    assert HPB == NH
    args = [q, k, v, gc, gr]
    out_shape = [jax.ShapeDtypeStruct((n_seq * T, DVT), BF16)]
    out_specs = [pl.BlockSpec((T, HPB * DV), lambda b, h: (b, h))]
    if ctx:
        out_shape += [jax.ShapeDtypeStruct((N_CTX_SEQ, 2, NH, DK, DV), F32),
                      jax.ShapeDtypeStruct((N_CTX_SEQ, 2, NH, 1, DK), F32),
                      jax.ShapeDtypeStruct((N_CTX_SEQ, 2, NH, 1, DK), F32)]
        out_specs += [pl.BlockSpec((None, 2, HPB, DK, DV), lambda b, h: (b, 0, h, 0, 0)),
                      pl.BlockSpec((None, 2, HPB, 1, DK), lambda b, h: (b, 0, h, 0, 0)),
                      pl.BlockSpec((None, 2, HPB, 1, DK), lambda b, h: (b, 0, h, 0, 0))]
    else:
        c0, n0, m0 = states
        n0 = n0.reshape(N_LAT_SEQ, -1, 2, NH, 1, DK)
        m0 = jnp.broadcast_to(m0[..., None, None], m0.shape + (1, DK))
        sq_c = (None, None, None, HPB, DK, DV)
        sq_v = (None, None, None, HPB, 1, DK)
        for arr, sq in ((c0, sq_c), (n0, sq_v), (m0, sq_v)):
            for d in range(2):
                in_specs.append(pl.BlockSpec(sq, functools.partial(lambda b, h, d: (b, slot, d, h, 0, 0), d=d)))
                args.append(arr)
    return pl.pallas_call(
        functools.partial(_mlstm_scan_kernel, T=T, has_state=not ctx, emit_state=ctx),
        out_shape=out_shape,
        grid=(n_seq, NH // HPB),
        in_specs=in_specs,
        out_specs=out_specs,
        scratch_shapes=[pltpu.VMEM((2 * HPB, DV, DK), F32), pltpu.VMEM((2 * HPB, 1, DK), F32),
                        pltpu.VMEM((2 * HPB, 1, DK), F32),
                        pltpu.VMEM((HPB, T // SB, DV, SB), BF16), pltpu.VMEM((T, HPB * DV), F32), pltpu.VMEM((T, HPB * DV), F32)],
        compiler_params=_cp(("parallel", "parallel")),
        name="mlstm_scan_ctx" if ctx else "mlstm_scan_lat",
    )(*args)


def _route(lg):
    lane = lax.broadcasted_iota(jnp.int32, lg.shape, 1)
    big = jnp.int32(ROUTE_W)
    is_g = (lane >= NE) & (lane < NE + N_GROUPS)
    gl = jnp.where(is_g, lg, NEG_INF)
    gmax = jnp.max(gl, axis=1, keepdims=True)
    gsel = jnp.min(jnp.where(gl == gmax, lane, big), axis=1, keepdims=True) - NE
    gw = 1.0 / jnp.sum(jnp.exp(gl - gmax), axis=1, keepdims=True)
    ing = (lane < NE) & ((lane >> 3) == gsel)
    el = jnp.where(ing, lg, NEG_INF)
    emax = jnp.max(el, axis=1, keepdims=True)
    p = jnp.exp(el - emax)
    prob = p / jnp.sum(p, axis=1, keepdims=True)
    p1 = jnp.max(prob, axis=1, keepdims=True)
    i1 = jnp.min(jnp.where(ing & (prob == p1), lane, big), axis=1, keepdims=True)
    rest = ing & (lane != i1)
    prob2 = jnp.where(rest, prob, -1.0)
    p2 = jnp.max(prob2, axis=1, keepdims=True)
    i2 = jnp.min(jnp.where(rest & (prob2 == p2), lane, big), axis=1, keepdims=True)
    tot = p1 + p2
    rinfo = jnp.where(lane == 0, i1.astype(F32),
                      jnp.where(lane == 1, i2.astype(F32),
                                jnp.where(lane == 2, gw * (p1 / tot), jnp.where(lane == 3, gw * (p2 / tot), 0.0))))
    sel = jnp.where((lane == i1) | (lane == i2), 1.0, 0.0)
    return rinfo, [jnp.sum(sel[c * TP:(c + 1) * TP, :], axis=0, keepdims=True) for c in range(lg.shape[0] // TP)]


def _out_kernel(*refs, n_x):
    (oc_ref, ol_ref, r_ref, gate_ref, sh_ref, sc_ref, ng_ref, wo_ref, lg_ref, lb_ref, wr_ref,
     br_ref, x1_ref, h2_ref, rinfo_ref, cnt_ref) = refs[n_x:]
    o = jnp.where(pl.program_id(0) < N_CTX // TB, oc_ref[...], ol_ref[...])
    y = _dot((o * ng_ref[...] * r_ref[...]).astype(BF16), wo_ref[...])
    x1 = _layernorm(DN_ALPHA * _token_block(refs[:n_x]) + gate_ref[0] * y, lg_ref[...], lb_ref[...])
    x1_ref[...] = x1
    h2 = x1 * (1.0 + sc_ref[0]) + sh_ref[0]
    h_hi, h_lo = _split2(h2)
    h2_ref[...] = h_hi
    t = _dot(h_hi, wr_ref[...])
    lg = t[:, :ROUTE_W] + t[:, ROUTE_W:] + _dot(h_lo, wr_ref[:, :ROUTE_W]) + br_ref[...]
    rinfo_ref[...], counts = _route(lg)
    for c, count in enumerate(counts):
        cnt_ref[c] = count


def _out_proj(o_ctx, o_lat, r, x, mods, norm_g, w_out, ln_g, ln_b, w_route, b_route):
    nc = N_CTX // TB
    full = lambda s: pl.BlockSpec(s, lambda i: (0,) * len(s), pipeline_mode=pl.Buffered(1))
    tok = lambda n: pl.BlockSpec((TB, n), lambda i: (i, 0))
    x_specs, x_args = _token_specs(x)
    return pl.pallas_call(
        functools.partial(_out_kernel, n_x=len(x_args)),
        out_shape=[jax.ShapeDtypeStruct((N_TOK, D), F32), jax.ShapeDtypeStruct((N_TOK, D), BF16),
                   jax.ShapeDtypeStruct((N_TOK, ROUTE_W), F32), jax.ShapeDtypeStruct((NBP, 1, ROUTE_W), F32)],
        grid=(NB,),
        in_specs=x_specs + [pl.BlockSpec((TB, DVT), lambda i: (jnp.minimum(i, nc - 1), 0)),
                            pl.BlockSpec((TB, DVT), lambda i: (jnp.maximum(i - nc, 0), 0)),
                            tok(DVT),
                            _mod_spec(2, _mod_row_tb), _mod_spec(3, _mod_row_tb), _mod_spec(4, _mod_row_tb),
                            full((1, DVT)), full((DVT, D)), full((1, D)), full((1, D)),
                            full((D, 2 * ROUTE_W)), full((1, ROUTE_W))],
        out_specs=[tok(D), tok(D), tok(ROUTE_W), pl.BlockSpec((TB // TP, 1, ROUTE_W), lambda i: (i, 0, 0))],
        compiler_params=_cp(("parallel",)),
        name="out_proj_route",
    )(*x_args, o_ctx, o_lat, r, mods, mods, mods, norm_g.reshape(1, DVT), w_out.astype(BF16),
      ln_g.reshape(1, D), ln_b.reshape(1, D), jnp.concatenate(_split2(w_route), axis=1), b_route)


TP = 512
NBP = N_TOK // TP
SEG = 16
R_LOC = 1536
PT = 256
XW = D + ROUTE_W
TMS = 512
R_TOT = -(-(2 * N_TOK + NBP * NE * (SEG - 1) + NE * (TMS - 1)) // TMS) * TMS
NT = R_TOT // TMS
POS_SPLIT = 64.0


def _mod_row_tp(i):
    return jnp.where(i < N_CTX // TP, 0, 1 + (i - N_CTX // TP) // (T_LAT // TP))


def _lane_pack(cols, shape):
    lane = lax.broadcasted_iota(jnp.int32, shape, 1)
    out = jnp.zeros(shape, F32)
    for j, c in enumerate(cols):
        out = jnp.where(lane == j, c, out)
    return out


WAIT_CHUNKS = (32, 4, 1)
MAXC = R_LOC // SEG
ZROWS = WAIT_CHUNKS[0] * SEG
NZ = NE + 1


def _segment_start(gtab_ref, ntot_ref, blk, make_copy):
    def per_chunk(j, carry):
        g = gtab_ref[blk * MAXC + j]
        make_copy(pl.multiple_of(j * SEG, SEG), pl.multiple_of(g, SEG), SEG).start()
        return carry

    lax.fori_loop(0, ntot_ref[blk], per_chunk, 0)


def _segment_wait(n_chunks, make_copy):
    left = n_chunks
    for chunks in WAIT_CHUNKS:
        n_wait = left // chunks

        def wait_piece(k, c, chunks=chunks):
            make_copy(0, 0, chunks * SEG).wait()
            return c

        lax.fori_loop(0, n_wait, wait_piece, 0)
        left = left - n_wait * chunks


def _perm_kernel(gtab_ref, ntot_ref, zrow_ref, zcnt_ref, h_ref, ri_ref, loffv_ref, pos_ref, xs_ref,
                 xs_scr, zero_scr, sem, zsem):
    blk = pl.program_id(0)
    slot = blk % 2

    def zero_copy(_, g, n):
        return pltpu.make_async_copy(zero_scr.at[pl.ds(0, n)], xs_ref.at[pl.ds(g, n)], zsem)

    @pl.when(blk == 0)
    def _():
        zero_scr[...] = jnp.zeros_like(zero_scr)

        def per_range(r, carry):
            n = zcnt_ref[r]
            g0 = zrow_ref[r]
            n_big = n // WAIT_CHUNKS[0]

            def per_piece(k, c, first, chunks):
                zero_copy(0, pl.multiple_of(g0 + (first + k * chunks) * SEG, SEG), chunks * SEG).start()
                return c

            lax.fori_loop(0, n_big, functools.partial(per_piece, first=0, chunks=WAIT_CHUNKS[0]), 0)
            lax.fori_loop(0, n - n_big * WAIT_CHUNKS[0],
                          functools.partial(per_piece, first=n_big * WAIT_CHUNKS[0], chunks=1), 0)
            return carry + n

        lax.fori_loop(0, NZ, per_range, 0)
    ri = ri_ref[...]
    e_a, e_b, w_a, w_b = ri[:, 0:1], ri[:, 1:2], ri[:, 2:3], ri[:, 3:4]
    lanef = lax.broadcasted_iota(jnp.int32, ri.shape, 1).astype(F32)
    is_a = lanef == e_a
    is_b = lanef == e_b
    sel = jnp.where(is_a | is_b, 1.0, 0.0).astype(BF16)
    row = lax.broadcasted_iota(jnp.int32, (TP, TP), 0)
    col = lax.broadcasted_iota(jnp.int32, (TP, TP), 1)
    earlier = jnp.where(row > col, 1.0, 0.0).astype(BF16)
    lpos = loffv_ref[0] + _dot(earlier, sel)
    pos_a = jnp.sum(jnp.where(is_a, lpos, 0.0), axis=1, keepdims=True)
    pos_b = jnp.sum(jnp.where(is_b, lpos, 0.0), axis=1, keepdims=True)
    pos_ref[...] = _lane_pack([pos_a, pos_b], ri.shape)

    hi_a = jnp.floor(pos_a * (1.0 / POS_SPLIT))
    hi_b = jnp.floor(pos_b * (1.0 / POS_SPLIT))
    parts = _lane_pack([hi_a, pos_a - POS_SPLIT * hi_a, hi_b, pos_b - POS_SPLIT * hi_b], ri.shape).astype(BF16)
    pick = jnp.where(lax.broadcasted_iota(jnp.int32, (8, ROUTE_W), 0) == lax.broadcasted_iota(jnp.int32, (8, ROUTE_W), 1),
                     1.0, 0.0).astype(BF16)
    pr = _dot_nt(pick, parts)
    pos_a_r = POS_SPLIT * pr[0:1, :] + pr[1:2, :]
    pos_b_r = POS_SPLIT * pr[2:3, :] + pr[3:4, :]

    wa1, wa2 = _split2(w_a)
    wa3 = (w_a - wa1.astype(F32) - wa2.astype(F32))
    wb1, wb2 = _split2(w_b)
    wb3 = (w_b - wb1.astype(F32) - wb2.astype(F32))
    wl = _lane_pack([wa1.astype(F32), wa2.astype(F32), wa3, wb1.astype(F32), wb2.astype(F32), wb3, e_a],
                    ri.shape).astype(BF16)
    hcat = jnp.concatenate([h_ref[...], wl], axis=1)
    rio = lax.broadcasted_iota(jnp.int32, (PT, TP), 0)
    pos_a_i = pos_a_r.astype(jnp.int32)
    pos_b_i = pos_b_r.astype(jnp.int32)
    for t in range(R_LOC // PT):
        onehot = jnp.where((rio == pos_a_i - t * PT) | (rio == pos_b_i - t * PT), 1.0, 0.0).astype(BF16)
        xs_scr[slot, t * PT:(t + 1) * PT, :] = _dot(onehot, hcat).astype(BF16)

    def copy_from(s):
        return lambda l, g, n: pltpu.make_async_copy(xs_scr.at[s, pl.ds(l, n)], xs_ref.at[pl.ds(g, n)], sem.at[s])

    _segment_start(gtab_ref, ntot_ref, blk, copy_from(slot))

    @pl.when(blk > 0)
    def _():
        _segment_wait(ntot_ref[blk - 1], copy_from(1 - slot))

    @pl.when(blk == NBP - 1)
    def _():
        _segment_wait(ntot_ref[blk], copy_from(slot))
        _segment_wait(lax.fori_loop(0, NZ, lambda r, c: c + zcnt_ref[r], 0), zero_copy)


def _moe_permute(h2, rinfo, tables):
    gtab, ntot, zrow, zcnt, loffv = tables
    grid_spec = pltpu.PrefetchScalarGridSpec(
        num_scalar_prefetch=4,
        grid=(NBP,),
        in_specs=[pl.BlockSpec((TP, D), lambda i, *_: (i, 0)),
                  pl.BlockSpec((TP, ROUTE_W), lambda i, *_: (i, 0)),
                  pl.BlockSpec((1, 1, ROUTE_W), lambda i, *_: (i, 0, 0))],
        out_specs=[pl.BlockSpec((TP, ROUTE_W), lambda i, *_: (i, 0)),
                   pl.BlockSpec(memory_space=pl.ANY)],
        scratch_shapes=[pltpu.VMEM((2, R_LOC, XW), BF16), pltpu.VMEM((ZROWS, XW), BF16),
                        pltpu.SemaphoreType.DMA((2,)), pltpu.SemaphoreType.DMA],
    )
    return pl.pallas_call(
        _perm_kernel,
        out_shape=[jax.ShapeDtypeStruct((N_TOK, ROUTE_W), F32), jax.ShapeDtypeStruct((R_TOT, XW), BF16)],
        grid_spec=grid_spec,
        compiler_params=_cp(("arbitrary",)),
        name="moe_permute",
    )(gtab, ntot, zrow, zcnt, h2, rinfo, loffv)


def _ffn_kernel(te_ref, nt_ref, enext_ref, eslot_ref, xs_ref, wg_hbm, wu_hbm, wd_hbm, ys_ref,
                wg32_ref, wu32_ref, wd32_ref, wg_ref, wu_ref, wd_ref, sem, *, layer):
    i = pl.program_id(0)
    active = i < nt_ref[0]
    e_i = te_ref[i]
    new_expert = (i == 0) | (e_i != te_ref[jnp.maximum(i - 1, 0)])

    def fetch(e, s):
        return [pltpu.make_async_copy(src.at[layer, e], dst.at[s], sem.at[s])
                for src, dst in ((wg_hbm, wg32_ref), (wu_hbm, wu32_ref), (wd_hbm, wd32_ref))]

    @pl.when(active & new_expert)
    def _():
        s = eslot_ref[e_i]
        nxt = enext_ref[e_i]

        @pl.when(i == 0)
        def _():
            for c in fetch(e_i, s):
                c.start()

        @pl.when(nxt >= 0)
        def _():
            for c in fetch(nxt, 1 - s):
                c.start()

        for c in fetch(e_i, s):
            c.wait()
        wg_ref[0] = wg32_ref[s].astype(BF16)
        wu_ref[0] = wu32_ref[s].astype(BF16)
        wd_ref[0] = wd32_ref[s].astype(BF16)

    @pl.when(active)
    def _():
        e = te_ref[i].astype(F32)
        xs = xs_ref[...]
        x = xs[:, :D]
        r = xs[:, D:].astype(F32)
        w_a = r[:, 0:1] + r[:, 1:2] + r[:, 2:3]
        w_b = r[:, 3:4] + r[:, 4:5] + r[:, 5:6]
        w = jnp.where(r[:, 6:7] == e, w_a, w_b)
        hid = _silu(_dot(x, wg_ref[0])) * _dot(x, wu_ref[0]) * w
        ys_ref[...] = _dot(hid.astype(BF16), wd_ref[0]).astype(BF16)

    @pl.when(i >= nt_ref[0])
    def _():
        ys_ref[...] = jnp.zeros_like(ys_ref)


def _moe_ffn(xs, schedule, layer, w_gate, w_up, w_down):
    tile_expert, n_tiles, e_next, e_slot = schedule
    grid_spec = pltpu.PrefetchScalarGridSpec(
        num_scalar_prefetch=4,
        grid=(NT,),
        in_specs=[pl.BlockSpec((TMS, XW), lambda i, te, nt, *_: (jnp.minimum(i, nt[0] - 1), 0)),
                  pl.BlockSpec(memory_space=pl.ANY), pl.BlockSpec(memory_space=pl.ANY),
                  pl.BlockSpec(memory_space=pl.ANY)],
        out_specs=pl.BlockSpec((TMS, D), lambda i, *_: (i, 0)),
        scratch_shapes=[pltpu.VMEM((2, D, DE), F32), pltpu.VMEM((2, D, DE), F32), pltpu.VMEM((2, DE, D), F32),
                        pltpu.VMEM((1, D, DE), BF16), pltpu.VMEM((1, D, DE), BF16), pltpu.VMEM((1, DE, D), BF16),
                        pltpu.SemaphoreType.DMA((2,))],
    )
    return pl.pallas_call(
        functools.partial(_ffn_kernel, layer=layer),
        out_shape=jax.ShapeDtypeStruct((R_TOT, D), BF16),
        grid_spec=grid_spec,
        compiler_params=_cp(("arbitrary",)),
        name="moe_ffn",
    )(tile_expert, n_tiles, e_next, e_slot, xs, w_gate, w_up, w_down)


def _unperm_kernel(gtab_ref, ntot_ref, pos_ref, x1_ref, gate_ref, lg_ref, lb_ref, ys_ref, *rest, split):
    if split:
        oc_ref, ol_ref, ys_scr, sem = rest
    else:
        o_ref, ys_scr, sem = rest
    blk = pl.program_id(0)
    slot = blk % 2

    def copy_to(s):
        return lambda l, g, n: pltpu.make_async_copy(ys_ref.at[pl.ds(g, n)], ys_scr.at[s, pl.ds(l, n)], sem.at[s])

    def fetch(b, s):
        ys_scr[s] = jnp.zeros((R_LOC, D), BF16)
        _segment_start(gtab_ref, ntot_ref, b, copy_to(s))

    @pl.when(blk == 0)
    def _():
        fetch(0, 0)

    @pl.when(blk + 1 < NBP)
    def _():
        fetch(blk + 1, 1 - slot)

    _segment_wait(ntot_ref[blk], copy_to(slot))
    pos = pos_ref[...]
    pos_a, pos_b = pos[:, 0:1].astype(jnp.int32), pos[:, 1:2].astype(jnp.int32)
    acc = jnp.zeros((TP, D), F32)
    lio = lax.broadcasted_iota(jnp.int32, (TP, PT), 1)
    for t in range(R_LOC // PT):
        onehot = jnp.where((lio == pos_a - t * PT) | (lio == pos_b - t * PT), 1.0, 0.0).astype(BF16)
        acc = acc + _dot(onehot, ys_scr[slot, t * PT:(t + 1) * PT, :])
    res = _layernorm(DN_ALPHA * x1_ref[...] + gate_ref[0] * acc, lg_ref[...], lb_ref[...])
    if split:
        @pl.when(blk < N_CTX // TP)
        def _():
            oc_ref[...] = res

        @pl.when(blk >= N_CTX // TP)
        def _():
            ol_ref[...] = res
    else:
        o_ref[...] = res


def _moe_unpermute(ys, pos, x1, mods, ln_g, ln_b, tables, split):
    gtab, ntot = tables[:2]
    nc = N_CTX // TP
    if split:
        out_shape = [jax.ShapeDtypeStruct((N_CTX, D), F32), jax.ShapeDtypeStruct((N_LAT, D), F32)]
        out_specs = [pl.BlockSpec((TP, D), lambda i, *_: (jnp.minimum(i, nc - 1), 0)),
                     pl.BlockSpec((TP, D), lambda i, *_: (jnp.maximum(i - nc, 0), 0))]
    else:
        out_shape = jax.ShapeDtypeStruct((N_TOK, D), F32)
        out_specs = pl.BlockSpec((TP, D), lambda i, *_: (i, 0))
    grid_spec = pltpu.PrefetchScalarGridSpec(
        num_scalar_prefetch=2,
        grid=(NBP,),
        in_specs=[pl.BlockSpec((TP, ROUTE_W), lambda i, *_: (i, 0)),
                  pl.BlockSpec((TP, D), lambda i, *_: (i, 0)),
                  _mod_spec(5, _mod_row_tp),
                  pl.BlockSpec((1, D), lambda i, *_: (0, 0)),
                  pl.BlockSpec((1, D), lambda i, *_: (0, 0)),
                  pl.BlockSpec(memory_space=pl.ANY)],
        out_specs=out_specs,
        scratch_shapes=[pltpu.VMEM((2, R_LOC, D), BF16), pltpu.SemaphoreType.DMA((2,))],
    )
    return pl.pallas_call(
        functools.partial(_unperm_kernel, split=split),
        out_shape=out_shape,
        grid_spec=grid_spec,
        compiler_params=_cp(("arbitrary",)),
        name="moe_unpermute",
    )(gtab, ntot, pos, x1, mods, ln_g.reshape(1, D), ln_b.reshape(1, D), ys)


def _segment_tables(cnt):
    c = cnt[:, 0, :NE].astype(jnp.int32)
    pc = (c + SEG - 1) // SEG * SEG
    loff = jnp.cumsum(pc, axis=1) - pc
    tot = pc.sum(axis=0)
    totp = (tot + TMS - 1) // TMS * TMS
    eend = jnp.cumsum(totp)
    goff = (eend - totp)[None, :] + jnp.cumsum(pc, axis=0) - pc
    n_tiles = (eend[-1] // TMS).reshape(1)
    tile_id = jnp.arange(NT, dtype=jnp.int32)
    tile_expert = jnp.minimum(jnp.sum(tile_id[:, None] >= (eend // TMS)[None, :], axis=1), NE - 1).astype(jnp.int32)
    loffv = jnp.zeros((NBP, 1, ROUTE_W), F32).at[:, 0, :NE].set(loff.astype(F32))
    ntot = (pc.sum(axis=1) // SEG).astype(jnp.int32)
    chunk = jnp.arange(MAXC, dtype=jnp.int32)
    first = loff // SEG
    e_of = jnp.minimum(jnp.sum(chunk[None, :, None] >= (first + pc // SEG)[:, None, :], axis=2), NE - 1)
    is_e = e_of[:, :, None] == jnp.arange(NE, dtype=jnp.int32)[None, None, :]
    pick = lambda a: jnp.sum(jnp.where(is_e, a[:, None, :], 0), axis=2)
    gtab = (pick(goff) + (chunk[None, :] - pick(first)) * SEG).reshape(-1).astype(jnp.int32)
    zrow = jnp.concatenate([eend - totp + tot, eend[-1:]]).astype(jnp.int32)
    zcnt = jnp.concatenate([(totp - tot) // SEG, (R_TOT - eend[-1:]) // SEG]).astype(jnp.int32)
    ids = jnp.arange(NE, dtype=jnp.int32)
    used = totp > 0
    later = (ids[None, :] > ids[:, None]) & used[None, :]
    e_next = jnp.min(jnp.where(later, ids[None, :], NE), axis=1)
    e_next = jnp.where(e_next == NE, -1, e_next).astype(jnp.int32)
    e_slot = ((jnp.cumsum(used.astype(jnp.int32)) - 1) % 2).astype(jnp.int32)
    return (gtab, ntot, zrow, zcnt, loffv), (tile_expert, n_tiles.astype(jnp.int32), e_next, e_slot)


def _moe(h2, rinfo, cnt, x1, mods, ln_g, ln_b, layer, w_gate, w_up, w_down, split):
    tables, schedule = _segment_tables(cnt)
    pos, xs = _moe_permute(h2, rinfo, tables)
    ys = _moe_ffn(xs, schedule, layer, w_gate, w_up, w_down)
    return _moe_unpermute(ys, pos, x1, mods, ln_g, ln_b, tables, split)


def kernel(x_prompt, x_sample, state_gla_S, state_mlstm_C, state_mlstm_n, state_mlstm_m, c, c_ctx,
           gla_w_in, gla_w_gate, gla_b_gate, gla_norm_g, gla_w_out,
           mlstm_w_in, mlstm_b_gates, mlstm_norm_g, mlstm_w_out,
           adaln_w, adaln_b, ln_g, ln_b,
           moe_w_group, moe_b_group, moe_w_expert, moe_b_expert, moe_w_gate, moe_w_up, moe_w_down):
    cvecs = jnp.zeros((MOD_ROWS, D), F32).at[0].set(c_ctx).at[1:1 + N_LAT_SEQ].set(c)
    mods_all = _modulation(cvecs, adaln_w, adaln_b)
    x = (x_prompt.reshape(N_CTX, D), x_sample.reshape(N_LAT, D), _grid_posemb(T_LAT))
    gla_states, ml_c, ml_n, ml_m = [], [], [], []
    for l in range(DEPTH):
        s = l // 2
        mods = mods_all[l].reshape(MOD_ROWS * 6, 1, D)
        if l % 2 == 0:
            q, k, v, r, g = _gla_in(x, mods, gla_w_in[s], gla_w_gate[s], gla_b_gate[s])
            o_ctx, s_new = _gla_scan(q, k, v, g, None, s)
            (o_lat,) = _gla_scan(q, k, v, g, state_gla_S, s)
            gla_states.append(s_new)
            norm_g, w_out = gla_norm_g[s], gla_w_out[s]
        else:
            q, k, v, r, gc, gr = _mlstm_in(x, mods, mlstm_w_in[s], mlstm_b_gates[s])
            o_ctx, c_new, n_new, m_new = _mlstm_scan(q, k, v, gc, gr, None, s)
            (o_lat,) = _mlstm_scan(q, k, v, gc, gr, (state_mlstm_C, state_mlstm_n, state_mlstm_m), s)
            ml_c.append(c_new)
            ml_n.append(n_new[:, :, :, 0, :])
            ml_m.append(m_new[:, :, :, 0, 0])
            norm_g, w_out = mlstm_norm_g[s], mlstm_w_out[s]
        w_route = jnp.zeros((D, ROUTE_W), F32).at[:, :NE].set(moe_w_expert[l]).at[:, NE:NE + N_GROUPS].set(
            moe_w_group[l])
        b_route = jnp.zeros((1, ROUTE_W), F32).at[0, :NE].set(moe_b_expert[l]).at[0, NE:NE + N_GROUPS].set(
            moe_b_group[l])
        x1, h2, rinfo, cnt = _out_proj(o_ctx, o_lat, r, x, mods, norm_g, w_out, ln_g[l, 0], ln_b[l, 0], w_route, b_route)
        x = _moe(h2, rinfo, cnt, x1, mods, ln_g[l, 1], ln_b[l, 1], l, moe_w_gate, moe_w_up, moe_w_down,
                 split=(l == DEPTH - 1))
    y_ctx = x[0].reshape(N_CTX_SEQ, T_CTX, D)
    y_lat = x[1].reshape(N_LAT_SEQ, T_LAT, D)
    return (y_ctx, y_lat, jnp.stack(gla_states, 1), jnp.stack(ml_c, 1), jnp.stack(ml_n, 1), jnp.stack(ml_m, 1))
```

```python
import functools
import math

import jax
import jax.numpy as jnp
from jax import lax
from jax.experimental import pallas as pl
from jax.experimental.pallas import tpu as pltpu

F32 = jnp.float32
BF16 = jnp.bfloat16

D = 1024
N_CTX_SEQ, T_CTX = 16, 256
N_LAT_SEQ, T_LAT = 8, 1024
N_CTX = N_CTX_SEQ * T_CTX
N_LAT = N_LAT_SEQ * T_LAT
N_TOK = N_CTX + N_LAT
DEPTH = 4
GRID_W = 64
NH = 4
DK = 128
DV = 256
DKT = NH * DK
DVT = NH * DV
GATE_RANK = 16
GATE_TAU = 16.0
CH = 64
N_GROUPS = 4
EPG = 8
NE = N_GROUPS * EPG
DE = 256
DN_ALPHA = (2.0 * DEPTH) ** 0.25
LN_EPS = 1e-5
MOD_ROWS = 16
TB = 1024
HPB = 4
GATE_F = 2 * NH
SB = 256
NCB = SB // CH
CH_SHIFT = 6
NB = N_TOK // TB
ROUTE_W = 128
NEG_INF = float("-inf")
VMEM_LIMIT = 56 * 1024 * 1024


def _cp(sem):
    return pltpu.CompilerParams(dimension_semantics=sem, vmem_limit_bytes=VMEM_LIMIT)


def _dot(a, b):
    return jnp.dot(a, b, preferred_element_type=F32)


def _dot_nt(a, b):
    return lax.dot_general(a, b, (((1,), (1,)), ((), ())), preferred_element_type=F32)


def _dot_f32(a, b):
    return jnp.dot(a, b, preferred_element_type=F32, precision=lax.Precision.HIGHEST)


def _split2(x):
    hi = x.astype(BF16)
    lo = (x - hi.astype(F32)).astype(BF16)
    return hi, lo


def _log_sigmoid(x):
    return -(jnp.maximum(-x, 0.0) + jnp.log1p(jnp.exp(-jnp.abs(x))))


def _silu(x):
    return x * jax.nn.sigmoid(x)


def _layernorm(x, g, b):
    mu = jnp.mean(x, axis=-1, keepdims=True)
    xc = x - mu
    var = jnp.mean(xc * xc, axis=-1, keepdims=True)
    return xc * lax.rsqrt(var + LN_EPS) * g + b


def _mod_row_tb(i):
    return jnp.where(i < N_CTX // TB, 0, 1 + (i - N_CTX // TB) // (T_LAT // TB))


def _mod_spec(j, row_fn):
    return pl.BlockSpec((1, 1, D), lambda i, *_: (row_fn(i) * 6 + j, 0, 0))


def _mod_kernel(c_ref, w_ref, b_ref, o_ref):
    o_ref[0] = _dot_f32(_silu(c_ref[...]), w_ref[0]) + b_ref[0]


def _modulation(cvecs, adaln_w, adaln_b):
    bw = 2 * D
    return pl.pallas_call(
        _mod_kernel,
        out_shape=jax.ShapeDtypeStruct((DEPTH, MOD_ROWS, 6 * D), F32),
        grid=(DEPTH, 6 * D // bw),
        in_specs=[
            pl.BlockSpec((MOD_ROWS, D), lambda l, j: (0, 0)),
            pl.BlockSpec((1, D, bw), lambda l, j: (l, 0, j)),
            pl.BlockSpec((1, 1, bw), lambda l, j: (l, 0, j)),
        ],
        out_specs=pl.BlockSpec((1, MOD_ROWS, bw), lambda l, j: (l, 0, j)),
        compiler_params=_cp(("parallel", "parallel")),
        name="adaln_modulation",
    )(cvecs, adaln_w, adaln_b.reshape(DEPTH, 1, 6 * D))


def _token_specs(x):
    if not isinstance(x, tuple):
        return [pl.BlockSpec((TB, D), lambda i: (i, 0))], [x]
    nc = N_CTX // TB
    assert TB == T_LAT
    specs = [pl.BlockSpec((TB, D), lambda i: (jnp.minimum(i, nc - 1), 0)),
             pl.BlockSpec((TB, D), lambda i: (jnp.maximum(i - nc, 0), 0)),
             pl.BlockSpec((TB, D), lambda i: (0, 0), pipeline_mode=pl.Buffered(1))]
    return specs, list(x)


def _token_block(x_refs):
    if len(x_refs) == 1:
        return x_refs[0][...]
    xp_ref, xs_ref, pos_ref = x_refs
    return jnp.where(pl.program_id(0) < N_CTX // TB, xp_ref[...], xs_ref[...] + pos_ref[...])


def _grid_posemb(T):
    rows = T // GRID_W
    r, cidx = jnp.meshgrid(jnp.arange(rows), jnp.arange(GRID_W), indexing="ij")
    nf = D // 4
    freqs = jnp.exp(-math.log(10000.0) * jnp.arange(nf, dtype=F32) / nf)

    def emb(p):
        a = p.reshape(-1).astype(F32)[:, None] * freqs
        return jnp.concatenate([jnp.sin(a), jnp.cos(a)], -1)

    return jnp.concatenate([emb(r), emb(cidx)], -1)


def _gla_in_kernel(*refs, n_x):
    (sh_ref, sc_ref, wq_ref, wk_ref, wv_ref, wr_ref, wg_ref, wgate_ref, bgate_ref,
     q_ref, k_ref, v_ref, r_ref, g_ref) = refs[n_x:]
    h = (_token_block(refs[:n_x]) * (1.0 + sc_ref[0]) + sh_ref[0]).astype(BF16)
    q_ref[...] = _dot(h, wq_ref[...]) * (DK ** -0.5)
    k_ref[...] = _dot(h, wk_ref[...])
    v_ref[...] = _dot(h, wv_ref[...]).astype(BF16)
    r_ref[...] = _silu(_dot(h, wr_ref[...])).astype(BF16)
    glr = _dot(h, wg_ref[...]).astype(BF16)
    z = _dot(glr, wgate_ref[...]) + bgate_ref[...]
    g_ref[...] = _log_sigmoid(z) * (1.0 / GATE_TAU)


def _gla_in(x, mods, w_in, w_gate, b_gate):
    wq = w_in[:, :DKT].astype(BF16)
    wk = w_in[:, DKT:2 * DKT].astype(BF16)
    wv = w_in[:, 2 * DKT:2 * DKT + DVT].astype(BF16)
    wr = w_in[:, 2 * DKT + DVT:2 * DKT + 2 * DVT].astype(BF16)
    wg = w_in[:, 2 * DKT + 2 * DVT:].astype(BF16)
    wgate = jnp.zeros((2 * GATE_RANK, 2 * DKT), F32)
    wgate = wgate.at[:GATE_RANK, :DKT].set(w_gate[0]).at[GATE_RANK:, DKT:].set(w_gate[1]).astype(BF16)
    bgate = b_gate.reshape(1, 2 * DKT)
    full = lambda s: pl.BlockSpec(s, lambda i: (0,) * len(s), pipeline_mode=pl.Buffered(1))
    tok = lambda n: pl.BlockSpec((TB, n), lambda i: (i, 0))
    x_specs, x_args = _token_specs(x)
    return pl.pallas_call(
        functools.partial(_gla_in_kernel, n_x=len(x_args)),
        out_shape=[jax.ShapeDtypeStruct((N_TOK, DKT), F32), jax.ShapeDtypeStruct((N_TOK, DKT), F32),
                   jax.ShapeDtypeStruct((N_TOK, DVT), BF16), jax.ShapeDtypeStruct((N_TOK, DVT), BF16),
                   jax.ShapeDtypeStruct((N_TOK, 2 * DKT), F32)],
        grid=(NB,),
        in_specs=x_specs + [_mod_spec(0, _mod_row_tb), _mod_spec(1, _mod_row_tb),
                            full((D, DKT)), full((D, DKT)), full((D, DVT)), full((D, DVT)),
                            full((D, 2 * GATE_RANK)), full((2 * GATE_RANK, 2 * DKT)), full((1, 2 * DKT))],
        out_specs=[tok(DKT), tok(DKT), tok(DVT), tok(DVT), tok(2 * DKT)],
        compiler_params=_cp(("parallel",)),
        name="gla_in_proj",
    )(*x_args, mods, mods, wq, wk, wv, wr, wg, wgate, bgate)


def _mlstm_in_kernel(x_ref, sh_ref, sc_ref, wq_ref, wk_ref, wv_ref, wr_ref, wg_ref, wgt_ref, bg_ref, bgt_ref,
                     q_ref, k_ref, v_ref, r_ref, gc_ref, gr_ref):
    h = (x_ref[...] * (1.0 + sc_ref[0]) + sh_ref[0]).astype(BF16)
    q_ref[...] = _dot(h, wq_ref[...]) * (DK ** -0.5)
    k_ref[...] = _dot(h, wk_ref[...])
    v_ref[...] = _dot(h, wv_ref[...]).astype(BF16)
    r_ref[...] = jax.nn.sigmoid(_dot(h, wr_ref[...])).astype(BF16)
    gc = _dot(h, wg_ref[...]) + bg_ref[...]
    gr = _dot_nt(wgt_ref[...], h) + bgt_ref[...]
    gc_ref[...] = jnp.where(lax.broadcasted_iota(jnp.int32, gc.shape, 1) >= GATE_F, _log_sigmoid(gc), gc)
    gr = jnp.where(lax.broadcasted_iota(jnp.int32, gr.shape, 0) >= GATE_F, _log_sigmoid(gr), gr)
    for c in range(TB // SB):
        gr_ref[c] = gr[:, c * SB:(c + 1) * SB]


def _mlstm_in(x, mods, w_in, b_gates):
    wq = w_in[:, :DKT].astype(BF16)
    wk = w_in[:, DKT:2 * DKT].astype(BF16)
    wv = w_in[:, 2 * DKT:2 * DKT + DVT].astype(BF16)
    wr = w_in[:, 2 * DKT + DVT:2 * DKT + 2 * DVT].astype(BF16)
    wg = w_in[:, 2 * DKT + 2 * DVT:].reshape(D, 2, 2, NH).transpose(0, 2, 3, 1).reshape(D, 4 * NH).astype(BF16)
    bg = b_gates.reshape(2, 2, NH).transpose(1, 2, 0).reshape(1, 4 * NH)
    full = lambda s: pl.BlockSpec(s, lambda i: (0,) * len(s), pipeline_mode=pl.Buffered(1))
    tok = lambda n: pl.BlockSpec((TB, n), lambda i: (i, 0))
    return pl.pallas_call(
        _mlstm_in_kernel,
        out_shape=[jax.ShapeDtypeStruct((N_TOK, DKT), F32), jax.ShapeDtypeStruct((N_TOK, DKT), F32),
                   jax.ShapeDtypeStruct((N_TOK, DVT), BF16), jax.ShapeDtypeStruct((N_TOK, DVT), BF16),
                   jax.ShapeDtypeStruct((N_TOK, 4 * NH), F32),
                   jax.ShapeDtypeStruct((N_TOK // SB, 4 * NH, SB), F32)],
        grid=(NB,),
        in_specs=[tok(D), _mod_spec(0, _mod_row_tb), _mod_spec(1, _mod_row_tb),
                  full((D, DKT)), full((D, DKT)), full((D, DVT)), full((D, DVT)), full((D, 4 * NH)),
                  full((4 * NH, D)), full((1, 4 * NH)), full((4 * NH, 1))],
        out_specs=[tok(DKT), tok(DKT), tok(DVT), tok(DVT),
                   tok(4 * NH),
                   pl.BlockSpec((TB // SB, 4 * NH, SB), lambda i: (i, 0, 0))],
        compiler_params=_cp(("parallel",)),
        name="mlstm_in_proj",
    )(x, mods, mods, wq, wk, wv, wr, wg, wg.T, bg, bg.T)


def _block_tri_masks():
    row = lax.broadcasted_iota(jnp.int32, (SB, SB), 0)
    col = lax.broadcasted_iota(jnp.int32, (SB, SB), 1)
    same = (row >> CH_SHIFT) == (col >> CH_SHIFT)
    return same & (row >= col), same & (row <= col)


def _chunk_scan(x, forward, axis=0, op=jnp.add, fill=0.0):
    n = x.shape[axis]
    pos = lax.broadcasted_iota(jnp.int32, x.shape, axis) & (CH - 1)
    s = 1
    while s < CH:
        if forward:
            x = op(x, jnp.where(pos >= s, pltpu.roll(x, s, axis), fill))
        else:
            x = op(x, jnp.where(pos < CH - s, pltpu.roll(x, n - s, axis), fill))
        s *= 2
    return x


def _chunk_cumsum(x, forward):
    return _chunk_scan(x, forward)


def _cat2(x):
    return jnp.concatenate([x, x], axis=1)


def _per_chunk(vals, n):
    return jnp.concatenate([jnp.broadcast_to(v, (CH, n)) for v in vals], axis=0)


def _chunk_masked_cat(x):
    xb = x.astype(BF16)
    n = x.shape[1]
    cols = []
    for c in range(NCB):
        parts = [jnp.zeros((c * CH, n), BF16), xb[c * CH:(c + 1) * CH, :], jnp.zeros((SB - (c + 1) * CH, n), BF16)]
        cols.append(jnp.concatenate([p for p in parts if p.shape[0]], axis=0))
    return jnp.concatenate(cols, axis=1)


def _head_norm_rows(o):
    mu = jnp.mean(o, axis=-1, keepdims=True)
    oc = o - mu
    var = jnp.mean(oc * oc, axis=-1, keepdims=True)
    return oc * lax.rsqrt(var + LN_EPS)


def _gla_superblock(q, k, v, v_t, g, st, forward, mask):
    b = _chunk_cumsum(g, forward)
    r_last = CH - 1 if forward else 0
    r_mid = CH // 2 if forward else CH - 1 - CH // 2
    lasts = [b[c * CH + r_last:c * CH + r_last + 1, :] for c in range(NCB)]
    b_last = _per_chunk(lasts, DK)
    b_mid = _per_chunk([b[c * CH + r_mid:c * CH + r_mid + 1, :] for c in range(NCB)], DK)
    qe = (q * jnp.exp(b - b_mid)).astype(BF16)
    ke = (k * jnp.exp(b_mid - b)).astype(BF16)
    qb = (q * jnp.exp(b)).astype(BF16)
    kd = k * jnp.exp(b_last - b)
    a = jnp.where(mask, _dot_nt(qe, ke), 0.0).astype(BF16)
    o_intra = _dot(a, v)
    d_t = _dot(v_t, _chunk_masked_cat(kd))
    inter = [None] * NCB
    for c in (range(NCB) if forward else reversed(range(NCB))):
        inter[c] = _dot_nt(qb[c * CH:(c + 1) * CH, :], st.astype(BF16))
        st = st * jnp.exp(lasts[c]) + d_t[:, c * DK:(c + 1) * DK]
    return o_intra + jnp.concatenate(inter, axis=0), st


def _scan_driver(T, o_ref, of_ref, ob_ref, step):
    nsb = T // SB
    if nsb == 1:
        step(0, 0)
    else:
        def body(i, carry):
            step(i, nsb - 1 - i)
            return carry

        lax.fori_loop(0, nsb, body, 0)

    def norm_body(j, carry):
        rows = pl.ds(pl.multiple_of(j * SB, SB), SB)
        for hh in range(HPB):
            cv = slice(hh * DV, (hh + 1) * DV)
            o_ref[rows, cv] = _head_norm_rows(of_ref[rows, cv] + ob_ref[rows, cv]).astype(BF16)
        return carry

    if nsb == 1:
        norm_body(0, 0)
    else:
        lax.fori_loop(0, nsb, norm_body, 0)


def _gla_scan_kernel(*refs, T, has_state, emit_state):
    q_ref, k_ref, v_ref, gf_ref, gb_ref = refs[:5]
    pos = 5
    if has_state:
        s0f_ref, s0b_ref = refs[pos:pos + 2]
        pos += 2
    o_ref = refs[pos]
    pos += 1
    if emit_state:
        sout_ref = refs[pos]
        pos += 1
    stf_ref, stb_ref, vt_ref, of_ref, ob_ref = refs[pos:pos + 5]

    nsb = T // SB
    for hh in range(HPB):
        for j in range(nsb):
            vt_ref[hh, j] = v_ref[j * SB:(j + 1) * SB, hh * DV:(hh + 1) * DV].astype(F32).T.astype(BF16)
        if has_state:
            stf_ref[hh] = s0f_ref[hh].T
            stb_ref[hh] = s0b_ref[hh].T
        else:
            stf_ref[hh] = jnp.zeros((DV, DK), F32)
            stb_ref[hh] = jnp.zeros((DV, DK), F32)

    lower, upper = _block_tri_masks()

    def one(j, forward, hh):
        rows = pl.ds(pl.multiple_of(j * SB, SB), SB)
        ck = slice(hh * DK, (hh + 1) * DK)
        cv = slice(hh * DV, (hh + 1) * DV)
        st_ref = stf_ref if forward else stb_ref
        o, st = _gla_superblock(q_ref[rows, ck], k_ref[rows, ck], v_ref[rows, cv].astype(BF16), vt_ref[hh, j],
                                (gf_ref if forward else gb_ref)[rows, ck], st_ref[hh], forward,
                                lower if forward else upper)
        st_ref[hh] = st
        (of_ref if forward else ob_ref)[rows, cv] = o

    def step(jf, jb):
        for hh in range(HPB):
            one(jf, True, hh)
            one(jb, False, hh)

    _scan_driver(T, o_ref, of_ref, ob_ref, step)
    if emit_state:
        for hh in range(HPB):
            sout_ref[0, hh] = stf_ref[hh].T
            sout_ref[1, hh] = stb_ref[hh].T


def _gla_scan(q, k, v, g, state, slot):
    ctx = state is None
    T = T_CTX if ctx else T_LAT
    n_seq = N_CTX_SEQ if ctx else N_LAT_SEQ
    off = 0 if ctx else N_CTX // T_LAT
    in_specs = [
        pl.BlockSpec((T, HPB * DK), lambda b, h: (off + b, h)),
        pl.BlockSpec((T, HPB * DK), lambda b, h: (off + b, h)),
        pl.BlockSpec((T, HPB * DV), lambda b, h: (off + b, h)),
        pl.BlockSpec((T, HPB * DK), lambda b, h: (off + b, h)),
        pl.BlockSpec((T, HPB * DK), lambda b, h: (off + b, NH // HPB + h)),
    ]
    args = [q, k, v, g, g]
    out_shape = [jax.ShapeDtypeStruct((n_seq * T, DVT), BF16)]
    out_specs = [pl.BlockSpec((T, HPB * DV), lambda b, h: (b, h))]
    if ctx:
        out_shape.append(jax.ShapeDtypeStruct((N_CTX_SEQ, 2, NH, DK, DV), F32))
        out_specs.append(pl.BlockSpec((None, 2, HPB, DK, DV), lambda b, h: (b, 0, h, 0, 0)))
    else:
        sq = (None, None, None, HPB, DK, DV)
        in_specs += [pl.BlockSpec(sq, lambda b, h: (b, slot, 0, h, 0, 0)),
                     pl.BlockSpec(sq, lambda b, h: (b, slot, 1, h, 0, 0))]
        args += [state, state]
    res = pl.pallas_call(
        functools.partial(_gla_scan_kernel, T=T, has_state=not ctx, emit_state=ctx),
        out_shape=out_shape,
        grid=(n_seq, NH // HPB),
        in_specs=in_specs,
        out_specs=out_specs,
        scratch_shapes=[pltpu.VMEM((HPB, DV, DK), F32), pltpu.VMEM((HPB, DV, DK), F32),
                        pltpu.VMEM((HPB, T // SB, DV, SB), BF16), pltpu.VMEM((T, HPB * DV), F32), pltpu.VMEM((T, HPB * DV), F32)],
        compiler_params=_cp(("parallel", "parallel")),
        name="gla_scan_ctx" if ctx else "gla_scan_lat",
    )(*args)
    return res


def _mlstm_scan_kernel(*refs, T, has_state, emit_state):
    q_ref, k_ref, v_ref, gc_ref, gr_ref = refs[:5]
    pos = 5
    if has_state:
        c0f_ref, c0b_ref, n0f_ref, n0b_ref, m0f_ref, m0b_ref = refs[pos:pos + 6]
        pos += 6
    o_ref = refs[pos]
    pos += 1
    if emit_state:
        cout_ref, nout_ref, mout_ref = refs[pos:pos + 3]
        pos += 3
    ct_ref, n_ref, m_ref, vt_ref, of_ref, ob_ref = refs[pos:pos + 6]

    nsb = T // SB
    for hh in range(HPB):
        for j in range(nsb):
            vt_ref[hh, j] = v_ref[j * SB:(j + 1) * SB, hh * DV:(hh + 1) * DV].astype(F32).T.astype(BF16)
        for d in range(2):
            sl = d * HPB + hh
            if has_state:
                c0, n0, m0 = ((c0f_ref, n0f_ref, m0f_ref), (c0b_ref, n0b_ref, m0b_ref))[d]
                ct_ref[sl] = c0[hh].T
                n_ref[sl] = n0[hh]
                m_ref[sl] = m0[hh]
            else:
                ct_ref[sl] = jnp.zeros((DV, DK), F32)
                n_ref[sl] = jnp.zeros((1, DK), F32)
                m_ref[sl] = jnp.zeros((1, DK), F32)

    lower, upper = _block_tri_masks()

    def gates(j, forward):
        rows = pl.ds(pl.multiple_of(j * SB, SB), SB)
        gc = gc_ref[rows, :]
        gr = gr_ref[j]
        i_col, f_col = gc[:, :GATE_F], gc[:, GATE_F:]
        b_col = _chunk_scan(f_col, forward)
        m_col = b_col + _chunk_scan(i_col - b_col, forward, op=jnp.maximum, fill=NEG_INF)
        b_row = _chunk_scan(gr[GATE_F:, :], forward, axis=1) - gr[:GATE_F, :]
        return i_col, b_col, m_col, b_row

    def one(j, forward, hh, shared):
        d = 0 if forward else 1
        sl = d * HPB + hh
        col = 2 * hh + d
        rows = pl.ds(pl.multiple_of(j * SB, SB), SB)
        q = q_ref[rows, hh * DK:(hh + 1) * DK]
        k = k_ref[rows, hh * DK:(hh + 1) * DK]
        v = v_ref[rows, hh * DV:(hh + 1) * DV].astype(BF16)
        mask = lower if forward else upper
        order = list(range(NCB)) if forward else list(reversed(range(NCB)))
        r_last = CH - 1 if forward else 0
        i_c, b_c, m_intra = [jnp.broadcast_to(a[:, col:col + 1], (SB, DK)) for a in shared[:3]]
        brow = shared[3][col:col + 1, :]
        b_last = [b_c[c * CH + r_last:c * CH + r_last + 1, :] for c in range(NCB)]
        a_c = _per_chunk(b_last, DK) - b_c + i_c
        m_loc = [jnp.max(a_c[c * CH:(c + 1) * CH, :], axis=0, keepdims=True) for c in range(NCB)]
        kw = k * jnp.exp(a_c - _per_chunk(m_loc, DK))
        kv_t = _dot(vt_ref[hh, j], _chunk_masked_cat(kw))
        k_loc = [jnp.sum(kw[c * CH:(c + 1) * CH, :], axis=0, keepdims=True) for c in range(NCB)]

        m = m_ref[sl]
        m_start, s_old, s_loc = [None] * NCB, [None] * NCB, [None] * NCB
        for c in order:
            m_start[c] = m
            m_new = jnp.maximum(b_last[c] + m, m_loc[c])
            s_old[c] = jnp.exp(b_last[c] + m - m_new)
            s_loc[c] = jnp.exp(m_loc[c] - m_new)
            m = m_new
        m_ref[sl] = m

        l_inter = b_c + _per_chunk(m_start, DK)
        m_i = jnp.maximum(l_inter, m_intra)
        d_log = jnp.where(mask, _cat2(b_c - m_i) - brow, NEG_INF)
        qb = q.astype(BF16)
        s = (_dot_nt(qb, k.astype(BF16)) * jnp.exp(d_log)).astype(BF16)
        e_inter = jnp.exp(l_inter - m_i)
        nd = _dot(s, jnp.concatenate([v, jnp.ones((SB, DK), BF16)], axis=1))
        num, den = nd[:, :DV], nd[:, DV:]

        ct = ct_ref[sl]
        nrm = n_ref[sl]
        inter = [None] * NCB
        for c in order:
            state = jnp.concatenate([ct, jnp.broadcast_to(nrm, (DK, DK))], axis=0).astype(BF16)
            inter[c] = _dot_nt(qb[c * CH:(c + 1) * CH, :], state)
            ct = s_old[c] * ct + s_loc[c] * kv_t[:, c * DK:(c + 1) * DK]
            nrm = s_old[c] * nrm + s_loc[c] * k_loc[c]
        ct_ref[sl] = ct
        n_ref[sl] = nrm

        inter = jnp.concatenate(inter, axis=0)
        num = num + _cat2(e_inter) * inter[:, :DV]
        den = den + e_inter * inter[:, DV:]
        inv = 1.0 / jnp.maximum(jnp.abs(den), jnp.exp(-m_i))
        (of_ref if forward else ob_ref)[rows, hh * DV:(hh + 1) * DV] = num * _cat2(inv)

    def step(jf, jb):
        shared_f = gates(jf, True)
        shared_b = gates(jb, False)
        for hh in range(HPB):
            one(jf, True, hh, shared_f)
            one(jb, False, hh, shared_b)

    _scan_driver(T, o_ref, of_ref, ob_ref, step)
    if emit_state:
        for d in range(2):
            for hh in range(HPB):
                cout_ref[d, hh] = ct_ref[d * HPB + hh].T
                nout_ref[d, hh] = n_ref[d * HPB + hh]
                mout_ref[d, hh] = m_ref[d * HPB + hh]


def _mlstm_scan(q, k, v, gc, gr, states, slot):
    ctx = states is None
    T = T_CTX if ctx else T_LAT
    n_seq = N_CTX_SEQ if ctx else N_LAT_SEQ
    off = 0 if ctx else N_CTX // T_LAT
    in_specs = [
        pl.BlockSpec((T, HPB * DK), lambda b, h: (off + b, h)),
        pl.BlockSpec((T, HPB * DK), lambda b, h: (off + b, h)),
        pl.BlockSpec((T, HPB * DV), lambda b, h: (off + b, h)),
        pl.BlockSpec((T, 4 * NH), lambda b, h: (off + b, 0)),
        pl.BlockSpec((T // SB, 4 * NH, SB), lambda b, h: (off + b, 0, 0)),
    ]
    assert HPB == NH
    args = [q, k, v, gc, gr]
    out_shape = [jax.ShapeDtypeStruct((n_seq * T, DVT), BF16)]
    out_specs = [pl.BlockSpec((T, HPB * DV), lambda b, h: (b, h))]
    if ctx:
        out_shape += [jax.ShapeDtypeStruct((N_CTX_SEQ, 2, NH, DK, DV), F32),
                      jax.ShapeDtypeStruct((N_CTX_SEQ, 2, NH, 1, DK), F32),
                      jax.ShapeDtypeStruct((N_CTX_SEQ, 2, NH, 1, DK), F32)]
        out_specs += [pl.BlockSpec((None, 2, HPB, DK, DV), lambda b, h: (b, 0, h, 0, 0)),
                      pl.BlockSpec((None, 2, HPB, 1, DK), lambda b, h: (b, 0, h, 0, 0)),
                      pl.BlockSpec((None, 2, HPB, 1, DK), lambda b, h: (b, 0, h, 0, 0))]
    else:
        c0, n0, m0 = states
        n0 = n0.reshape(N_LAT_SEQ, -1, 2, NH, 1, DK)
        m0 = jnp.broadcast_to(m0[..., None, None], m0.shape + (1, DK))
        sq_c = (None, None, None, HPB, DK, DV)
        sq_v = (None, None, None, HPB, 1, DK)
        for arr, sq in ((c0, sq_c), (n0, sq_v), (m0, sq_v)):
            for d in range(2):
                in_specs.append(pl.BlockSpec(sq, functools.partial(lambda b, h, d: (b, slot, d, h, 0, 0), d=d)))
                args.append(arr)
    return pl.pallas_call(
        functools.partial(_mlstm_scan_kernel, T=T, has_state=not ctx, emit_state=ctx),
        out_shape=out_shape,
        grid=(n_seq, NH // HPB),
        in_specs=in_specs,
        out_specs=out_specs,
        scratch_shapes=[pltpu.VMEM((2 * HPB, DV, DK), F32), pltpu.VMEM((2 * HPB, 1, DK), F32),
                        pltpu.VMEM((2 * HPB, 1, DK), F32),
                        pltpu.VMEM((HPB, T // SB, DV, SB), BF16), pltpu.VMEM((T, HPB * DV), F32), pltpu.VMEM((T, HPB * DV), F32)],
        compiler_params=_cp(("parallel", "parallel")),
        name="mlstm_scan_ctx" if ctx else "mlstm_scan_lat",
    )(*args)


def _route(lg):
    lane = lax.broadcasted_iota(jnp.int32, lg.shape, 1)
    big = jnp.int32(ROUTE_W)
    is_g = (lane >= NE) & (lane < NE + N_GROUPS)
    gl = jnp.where(is_g, lg, NEG_INF)
    gmax = jnp.max(gl, axis=1, keepdims=True)
    gsel = jnp.min(jnp.where(gl == gmax, lane, big), axis=1, keepdims=True) - NE
    gw = 1.0 / jnp.sum(jnp.exp(gl - gmax), axis=1, keepdims=True)
    ing = (lane < NE) & ((lane >> 3) == gsel)
    el = jnp.where(ing, lg, NEG_INF)
    emax = jnp.max(el, axis=1, keepdims=True)
    p = jnp.exp(el - emax)
    prob = p / jnp.sum(p, axis=1, keepdims=True)
    p1 = jnp.max(prob, axis=1, keepdims=True)
    i1 = jnp.min(jnp.where(ing & (prob == p1), lane, big), axis=1, keepdims=True)
    rest = ing & (lane != i1)
    prob2 = jnp.where(rest, prob, -1.0)
    p2 = jnp.max(prob2, axis=1, keepdims=True)
    i2 = jnp.min(jnp.where(rest & (prob2 == p2), lane, big), axis=1, keepdims=True)
    tot = p1 + p2
    rinfo = jnp.where(lane == 0, i1.astype(F32),
                      jnp.where(lane == 1, i2.astype(F32),
                                jnp.where(lane == 2, gw * (p1 / tot), jnp.where(lane == 3, gw * (p2 / tot), 0.0))))
    sel = jnp.where((lane == i1) | (lane == i2), 1.0, 0.0)
    return rinfo, [jnp.sum(sel[c * TP:(c + 1) * TP, :], axis=0, keepdims=True) for c in range(lg.shape[0] // TP)]


def _out_kernel(*refs, n_x):
    (oc_ref, ol_ref, r_ref, gate_ref, sh_ref, sc_ref, ng_ref, wo_ref, lg_ref, lb_ref, wr_ref,
     br_ref, x1_ref, h2_ref, rinfo_ref, cnt_ref) = refs[n_x:]
    o = jnp.where(pl.program_id(0) < N_CTX // TB, oc_ref[...], ol_ref[...])
    y = _dot((o * ng_ref[...] * r_ref[...]).astype(BF16), wo_ref[...])
    x1 = _layernorm(DN_ALPHA * _token_block(refs[:n_x]) + gate_ref[0] * y, lg_ref[...], lb_ref[...])
    x1_ref[...] = x1
    h2 = x1 * (1.0 + sc_ref[0]) + sh_ref[0]
    h_hi, h_lo = _split2(h2)
    h2_ref[...] = h_hi
    t = _dot(h_hi, wr_ref[...])
    lg = t[:, :ROUTE_W] + t[:, ROUTE_W:] + _dot(h_lo, wr_ref[:, :ROUTE_W]) + br_ref[...]
    rinfo_ref[...], counts = _route(lg)
    for c, count in enumerate(counts):
        cnt_ref[c] = count


def _out_proj(o_ctx, o_lat, r, x, mods, norm_g, w_out, ln_g, ln_b, w_route, b_route):
    nc = N_CTX // TB
    full = lambda s: pl.BlockSpec(s, lambda i: (0,) * len(s), pipeline_mode=pl.Buffered(1))
    tok = lambda n: pl.BlockSpec((TB, n), lambda i: (i, 0))
    x_specs, x_args = _token_specs(x)
    return pl.pallas_call(
        functools.partial(_out_kernel, n_x=len(x_args)),
        out_shape=[jax.ShapeDtypeStruct((N_TOK, D), F32), jax.ShapeDtypeStruct((N_TOK, D), BF16),
                   jax.ShapeDtypeStruct((N_TOK, ROUTE_W), F32), jax.ShapeDtypeStruct((NBP, 1, ROUTE_W), F32)],
        grid=(NB,),
        in_specs=x_specs + [pl.BlockSpec((TB, DVT), lambda i: (jnp.minimum(i, nc - 1), 0)),
                            pl.BlockSpec((TB, DVT), lambda i: (jnp.maximum(i - nc, 0), 0)),
                            tok(DVT),
                            _mod_spec(2, _mod_row_tb), _mod_spec(3, _mod_row_tb), _mod_spec(4, _mod_row_tb),
                            full((1, DVT)), full((DVT, D)), full((1, D)), full((1, D)),
                            full((D, 2 * ROUTE_W)), full((1, ROUTE_W))],
        out_specs=[tok(D), tok(D), tok(ROUTE_W), pl.BlockSpec((TB // TP, 1, ROUTE_W), lambda i: (i, 0, 0))],
        compiler_params=_cp(("parallel",)),
        name="out_proj_route",
    )(*x_args, o_ctx, o_lat, r, mods, mods, mods, norm_g.reshape(1, DVT), w_out.astype(BF16),
      ln_g.reshape(1, D), ln_b.reshape(1, D), jnp.concatenate(_split2(w_route), axis=1), b_route)


TP = 512
NBP = N_TOK // TP
SEG = 16
R_LOC = 1536
PT = 256
LAST_PT = R_LOC // PT - 1
XW = D + ROUTE_W
TMS = 512
R_TOT = -(-(2 * N_TOK + NBP * NE * (SEG - 1) + NE * (TMS - 1)) // TMS) * TMS
NT = R_TOT // TMS
POS_SPLIT = 64.0


def _mod_row_tp(i):
    return jnp.where(i < N_CTX // TP, 0, 1 + (i - N_CTX // TP) // (T_LAT // TP))


def _lane_pack(cols, shape):
    lane = lax.broadcasted_iota(jnp.int32, shape, 1)
    out = jnp.zeros(shape, F32)
    for j, c in enumerate(cols):
        out = jnp.where(lane == j, c, out)
    return out


WAIT_CHUNKS = (32, 4, 1)
MAXC = R_LOC // SEG
ZROWS = WAIT_CHUNKS[0] * SEG
NZ = NE + 1


def _segment_start(gtab_ref, ntot_ref, blk, make_copy):
    def per_chunk(j, carry):
        g = gtab_ref[blk * MAXC + j]
        make_copy(pl.multiple_of(j * SEG, SEG), pl.multiple_of(g, SEG), SEG).start()
        return carry

    lax.fori_loop(0, ntot_ref[blk], per_chunk, 0)


def _segment_wait(n_chunks, make_copy):
    left = n_chunks
    for chunks in WAIT_CHUNKS:
        n_wait = left // chunks

        def wait_piece(k, c, chunks=chunks):
            make_copy(0, 0, chunks * SEG).wait()
            return c

        lax.fori_loop(0, n_wait, wait_piece, 0)
        left = left - n_wait * chunks


def _perm_kernel(gtab_ref, ntot_ref, zrow_ref, zcnt_ref, h_ref, ri_ref, loffv_ref, pos_ref, xs_ref,
                 xs_scr, zero_scr, sem, zsem):
    blk = pl.program_id(0)
    slot = blk % 2

    def zero_copy(_, g, n):
        return pltpu.make_async_copy(zero_scr.at[pl.ds(0, n)], xs_ref.at[pl.ds(g, n)], zsem)

    @pl.when(blk == 0)
    def _():
        zero_scr[...] = jnp.zeros_like(zero_scr)

        def per_range(r, carry):
            n = zcnt_ref[r]
            g0 = zrow_ref[r]
            n_big = n // WAIT_CHUNKS[0]

            def per_piece(k, c, first, chunks):
                zero_copy(0, pl.multiple_of(g0 + (first + k * chunks) * SEG, SEG), chunks * SEG).start()
                return c

            lax.fori_loop(0, n_big, functools.partial(per_piece, first=0, chunks=WAIT_CHUNKS[0]), 0)
            lax.fori_loop(0, n - n_big * WAIT_CHUNKS[0],
                          functools.partial(per_piece, first=n_big * WAIT_CHUNKS[0], chunks=1), 0)
            return carry + n

        lax.fori_loop(0, NZ, per_range, 0)
    ri = ri_ref[...]
    e_a, e_b, w_a, w_b = ri[:, 0:1], ri[:, 1:2], ri[:, 2:3], ri[:, 3:4]
    lanef = lax.broadcasted_iota(jnp.int32, ri.shape, 1).astype(F32)
    is_a = lanef == e_a
    is_b = lanef == e_b
    sel = jnp.where(is_a | is_b, 1.0, 0.0).astype(BF16)
    row = lax.broadcasted_iota(jnp.int32, (TP, TP), 0)
    col = lax.broadcasted_iota(jnp.int32, (TP, TP), 1)
    earlier = jnp.where(row > col, 1.0, 0.0).astype(BF16)
    lpos = loffv_ref[0] + _dot(earlier, sel)
    pos_a = jnp.sum(jnp.where(is_a, lpos, 0.0), axis=1, keepdims=True)
    pos_b = jnp.sum(jnp.where(is_b, lpos, 0.0), axis=1, keepdims=True)
    pos_ref[...] = _lane_pack([pos_a, pos_b], ri.shape)

    hi_a = jnp.floor(pos_a * (1.0 / POS_SPLIT))
    hi_b = jnp.floor(pos_b * (1.0 / POS_SPLIT))
    parts = _lane_pack([hi_a, pos_a - POS_SPLIT * hi_a, hi_b, pos_b - POS_SPLIT * hi_b], ri.shape).astype(BF16)
    pick = jnp.where(lax.broadcasted_iota(jnp.int32, (8, ROUTE_W), 0) == lax.broadcasted_iota(jnp.int32, (8, ROUTE_W), 1),
                     1.0, 0.0).astype(BF16)
    pr = _dot_nt(pick, parts)
    pos_a_r = POS_SPLIT * pr[0:1, :] + pr[1:2, :]
    pos_b_r = POS_SPLIT * pr[2:3, :] + pr[3:4, :]

    wa1, wa2 = _split2(w_a)
    wa3 = (w_a - wa1.astype(F32) - wa2.astype(F32))
    wb1, wb2 = _split2(w_b)
    wb3 = (w_b - wb1.astype(F32) - wb2.astype(F32))
    wl = _lane_pack([wa1.astype(F32), wa2.astype(F32), wa3, wb1.astype(F32), wb2.astype(F32), wb3, e_a],
                    ri.shape).astype(BF16)
    hcat = jnp.concatenate([h_ref[...], wl], axis=1)
    rio = lax.broadcasted_iota(jnp.int32, (PT, TP), 0)
    pos_a_i = pos_a_r.astype(jnp.int32)
    pos_b_i = pos_b_r.astype(jnp.int32)
    def tile(t):
        onehot = jnp.where((rio == pos_a_i - t * PT) | (rio == pos_b_i - t * PT), 1.0, 0.0).astype(BF16)
        xs_scr[slot, t * PT:(t + 1) * PT, :] = _dot(onehot, hcat).astype(BF16)

    for t in range(LAST_PT):
        tile(t)
    pl.when(ntot_ref[blk] * SEG > LAST_PT * PT)(functools.partial(tile, LAST_PT))

    def copy_from(s):
        return lambda l, g, n: pltpu.make_async_copy(xs_scr.at[s, pl.ds(l, n)], xs_ref.at[pl.ds(g, n)], sem.at[s])

    _segment_start(gtab_ref, ntot_ref, blk, copy_from(slot))

    @pl.when(blk > 0)
    def _():
        _segment_wait(ntot_ref[blk - 1], copy_from(1 - slot))

    @pl.when(blk == NBP - 1)
    def _():
        _segment_wait(ntot_ref[blk], copy_from(slot))
        _segment_wait(lax.fori_loop(0, NZ, lambda r, c: c + zcnt_ref[r], 0), zero_copy)


def _moe_permute(h2, rinfo, tables):
    gtab, ntot, zrow, zcnt, loffv = tables
    grid_spec = pltpu.PrefetchScalarGridSpec(
        num_scalar_prefetch=4,
        grid=(NBP,),
        in_specs=[pl.BlockSpec((TP, D), lambda i, *_: (i, 0)),
                  pl.BlockSpec((TP, ROUTE_W), lambda i, *_: (i, 0)),
                  pl.BlockSpec((1, 1, ROUTE_W), lambda i, *_: (i, 0, 0))],
        out_specs=[pl.BlockSpec((TP, ROUTE_W), lambda i, *_: (i, 0)),
                   pl.BlockSpec(memory_space=pl.ANY)],
        scratch_shapes=[pltpu.VMEM((2, R_LOC, XW), BF16), pltpu.VMEM((ZROWS, XW), BF16),
                        pltpu.SemaphoreType.DMA((2,)), pltpu.SemaphoreType.DMA],
    )
    return pl.pallas_call(
        _perm_kernel,
        out_shape=[jax.ShapeDtypeStruct((N_TOK, ROUTE_W), F32), jax.ShapeDtypeStruct((R_TOT, XW), BF16)],
        grid_spec=grid_spec,
        compiler_params=_cp(("arbitrary",)),
        name="moe_permute",
    )(gtab, ntot, zrow, zcnt, h2, rinfo, loffv)


def _ffn_kernel(te_ref, nt_ref, enext_ref, eslot_ref, xs_ref, wg_hbm, wu_hbm, wd_hbm, ys_ref,
                wg32_ref, wu32_ref, wd32_ref, wg_ref, wu_ref, wd_ref, sem, *, layer):
    i = pl.program_id(0)
    active = i < nt_ref[0]
    e_i = te_ref[i]
    new_expert = (i == 0) | (e_i != te_ref[jnp.maximum(i - 1, 0)])

    def fetch(e, s):
        return [pltpu.make_async_copy(src.at[layer, e], dst.at[s], sem.at[s])
                for src, dst in ((wg_hbm, wg32_ref), (wu_hbm, wu32_ref), (wd_hbm, wd32_ref))]

    @pl.when(active & new_expert)
    def _():
        s = eslot_ref[e_i]
        nxt = enext_ref[e_i]

        @pl.when(i == 0)
        def _():
            for c in fetch(e_i, s):
                c.start()

        @pl.when(nxt >= 0)
        def _():
            for c in fetch(nxt, 1 - s):
                c.start()

        for c in fetch(e_i, s):
            c.wait()
        wg_ref[0] = wg32_ref[s].astype(BF16)
        wu_ref[0] = wu32_ref[s].astype(BF16)
        wd_ref[0] = wd32_ref[s].astype(BF16)

    @pl.when(active)
    def _():
        e = te_ref[i].astype(F32)
        xs = xs_ref[...]
        x = xs[:, :D]
        r = xs[:, D:].astype(F32)
        w_a = r[:, 0:1] + r[:, 1:2] + r[:, 2:3]
        w_b = r[:, 3:4] + r[:, 4:5] + r[:, 5:6]
        w = jnp.where(r[:, 6:7] == e, w_a, w_b)
        hid = _silu(_dot(x, wg_ref[0])) * _dot(x, wu_ref[0]) * w
        ys_ref[...] = _dot(hid.astype(BF16), wd_ref[0]).astype(BF16)

    @pl.when(i >= nt_ref[0])
    def _():
        ys_ref[...] = jnp.zeros_like(ys_ref)


def _moe_ffn(xs, schedule, layer, w_gate, w_up, w_down):
    tile_expert, n_tiles, e_next, e_slot = schedule
    grid_spec = pltpu.PrefetchScalarGridSpec(
        num_scalar_prefetch=4,
        grid=(NT,),
        in_specs=[pl.BlockSpec((TMS, XW), lambda i, te, nt, *_: (jnp.minimum(i, nt[0] - 1), 0)),
                  pl.BlockSpec(memory_space=pl.ANY), pl.BlockSpec(memory_space=pl.ANY),
                  pl.BlockSpec(memory_space=pl.ANY)],
        out_specs=pl.BlockSpec((TMS, D), lambda i, *_: (i, 0)),
        scratch_shapes=[pltpu.VMEM((2, D, DE), F32), pltpu.VMEM((2, D, DE), F32), pltpu.VMEM((2, DE, D), F32),
                        pltpu.VMEM((1, D, DE), BF16), pltpu.VMEM((1, D, DE), BF16), pltpu.VMEM((1, DE, D), BF16),
                        pltpu.SemaphoreType.DMA((2,))],
    )
    return pl.pallas_call(
        functools.partial(_ffn_kernel, layer=layer),
        out_shape=jax.ShapeDtypeStruct((R_TOT, D), BF16),
        grid_spec=grid_spec,
        compiler_params=_cp(("arbitrary",)),
        name="moe_ffn",
    )(tile_expert, n_tiles, e_next, e_slot, xs, w_gate, w_up, w_down)


def _unperm_kernel(gtab_ref, ntot_ref, pos_ref, x1_ref, gate_ref, lg_ref, lb_ref, ys_ref, *rest, split):
    if split:
        oc_ref, ol_ref, ys_scr, sem = rest
    else:
        o_ref, ys_scr, sem = rest
    blk = pl.program_id(0)
    slot = blk % 2

    def copy_to(s):
        return lambda l, g, n: pltpu.make_async_copy(ys_ref.at[pl.ds(g, n)], ys_scr.at[s, pl.ds(l, n)], sem.at[s])

    def fetch(b, s):
        ys_scr[s] = jnp.zeros((R_LOC, D), BF16)
        _segment_start(gtab_ref, ntot_ref, b, copy_to(s))

    @pl.when(blk == 0)
    def _():
        fetch(0, 0)

    @pl.when(blk + 1 < NBP)
    def _():
        fetch(blk + 1, 1 - slot)

    _segment_wait(ntot_ref[blk], copy_to(slot))
    pos = pos_ref[...]
    pos_a, pos_b = pos[:, 0:1].astype(jnp.int32), pos[:, 1:2].astype(jnp.int32)
    lio = lax.broadcasted_iota(jnp.int32, (TP, PT), 1)

    def term(t):
        onehot = jnp.where((lio == pos_a - t * PT) | (lio == pos_b - t * PT), 1.0, 0.0).astype(BF16)
        return _dot(onehot, ys_scr[slot, t * PT:(t + 1) * PT, :])

    def finish(acc):
        res = _layernorm(DN_ALPHA * x1_ref[...] + gate_ref[0] * acc, lg_ref[...], lb_ref[...])
        if split:
            @pl.when(blk < N_CTX // TP)
            def _():
                oc_ref[...] = res

            @pl.when(blk >= N_CTX // TP)
            def _():
                ol_ref[...] = res
        else:
            o_ref[...] = res

    acc = term(0)
    for t in range(1, LAST_PT):
        acc = acc + term(t)
    need_last = ntot_ref[blk] * SEG > LAST_PT * PT
    pl.when(need_last)(lambda: finish(acc + term(LAST_PT)))
    pl.when(jnp.logical_not(need_last))(lambda: finish(acc))


def _moe_unpermute(ys, pos, x1, mods, ln_g, ln_b, tables, split):
    gtab, ntot = tables[:2]
    nc = N_CTX // TP
    if split:
        out_shape = [jax.ShapeDtypeStruct((N_CTX, D), F32), jax.ShapeDtypeStruct((N_LAT, D), F32)]
        out_specs = [pl.BlockSpec((TP, D), lambda i, *_: (jnp.minimum(i, nc - 1), 0)),
                     pl.BlockSpec((TP, D), lambda i, *_: (jnp.maximum(i - nc, 0), 0))]
    else:
        out_shape = jax.ShapeDtypeStruct((N_TOK, D), F32)
        out_specs = pl.BlockSpec((TP, D), lambda i, *_: (i, 0))
    grid_spec = pltpu.PrefetchScalarGridSpec(
        num_scalar_prefetch=2,
        grid=(NBP,),
        in_specs=[pl.BlockSpec((TP, ROUTE_W), lambda i, *_: (i, 0)),
                  pl.BlockSpec((TP, D), lambda i, *_: (i, 0)),
                  _mod_spec(5, _mod_row_tp),
                  pl.BlockSpec((1, D), lambda i, *_: (0, 0)),
                  pl.BlockSpec((1, D), lambda i, *_: (0, 0)),
                  pl.BlockSpec(memory_space=pl.ANY)],
        out_specs=out_specs,
        scratch_shapes=[pltpu.VMEM((2, R_LOC, D), BF16), pltpu.SemaphoreType.DMA((2,))],
    )
    return pl.pallas_call(
        functools.partial(_unperm_kernel, split=split),
        out_shape=out_shape,
        grid_spec=grid_spec,
        compiler_params=_cp(("arbitrary",)),
        name="moe_unpermute",
    )(gtab, ntot, pos, x1, mods, ln_g.reshape(1, D), ln_b.reshape(1, D), ys)


def _segment_tables(cnt):
    c = cnt[:, 0, :NE].astype(jnp.int32)
    pc = (c + SEG - 1) // SEG * SEG
    loff = jnp.cumsum(pc, axis=1) - pc
    tot = pc.sum(axis=0)
    totp = (tot + TMS - 1) // TMS * TMS
    eend = jnp.cumsum(totp)
    goff = (eend - totp)[None, :] + jnp.cumsum(pc, axis=0) - pc
    n_tiles = (eend[-1] // TMS).reshape(1)
    tile_id = jnp.arange(NT, dtype=jnp.int32)
    tile_expert = jnp.minimum(jnp.sum(tile_id[:, None] >= (eend // TMS)[None, :], axis=1), NE - 1).astype(jnp.int32)
    loffv = jnp.zeros((NBP, 1, ROUTE_W), F32).at[:, 0, :NE].set(loff.astype(F32))
    ntot = (pc.sum(axis=1) // SEG).astype(jnp.int32)
    chunk = jnp.arange(MAXC, dtype=jnp.int32)
    first = loff // SEG
    e_of = jnp.minimum(jnp.sum(chunk[None, :, None] >= (first + pc // SEG)[:, None, :], axis=2), NE - 1)
    is_e = e_of[:, :, None] == jnp.arange(NE, dtype=jnp.int32)[None, None, :]
    pick = lambda a: jnp.sum(jnp.where(is_e, a[:, None, :], 0), axis=2)
    gtab = (pick(goff) + (chunk[None, :] - pick(first)) * SEG).reshape(-1).astype(jnp.int32)
    zrow = jnp.concatenate([eend - totp + tot, eend[-1:]]).astype(jnp.int32)
    zcnt = jnp.concatenate([(totp - tot) // SEG, (R_TOT - eend[-1:]) // SEG]).astype(jnp.int32)
    ids = jnp.arange(NE, dtype=jnp.int32)
    used = totp > 0
    later = (ids[None, :] > ids[:, None]) & used[None, :]
    e_next = jnp.min(jnp.where(later, ids[None, :], NE), axis=1)
    e_next = jnp.where(e_next == NE, -1, e_next).astype(jnp.int32)
    e_slot = ((jnp.cumsum(used.astype(jnp.int32)) - 1) % 2).astype(jnp.int32)
    return (gtab, ntot, zrow, zcnt, loffv), (tile_expert, n_tiles.astype(jnp.int32), e_next, e_slot)


def _moe(h2, rinfo, cnt, x1, mods, ln_g, ln_b, layer, w_gate, w_up, w_down, split):
    tables, schedule = _segment_tables(cnt)
    pos, xs = _moe_permute(h2, rinfo, tables)
    ys = _moe_ffn(xs, schedule, layer, w_gate, w_up, w_down)
    return _moe_unpermute(ys, pos, x1, mods, ln_g, ln_b, tables, split)


def kernel(x_prompt, x_sample, state_gla_S, state_mlstm_C, state_mlstm_n, state_mlstm_m, c, c_ctx,
           gla_w_in, gla_w_gate, gla_b_gate, gla_norm_g, gla_w_out,
           mlstm_w_in, mlstm_b_gates, mlstm_norm_g, mlstm_w_out,
           adaln_w, adaln_b, ln_g, ln_b,
           moe_w_group, moe_b_group, moe_w_expert, moe_b_expert, moe_w_gate, moe_w_up, moe_w_down):
    cvecs = jnp.zeros((MOD_ROWS, D), F32).at[0].set(c_ctx).at[1:1 + N_LAT_SEQ].set(c)
    mods_all = _modulation(cvecs, adaln_w, adaln_b)
    x = (x_prompt.reshape(N_CTX, D), x_sample.reshape(N_LAT, D), _grid_posemb(T_LAT))
    gla_states, ml_c, ml_n, ml_m = [], [], [], []
    for l in range(DEPTH):
        s = l // 2
        mods = mods_all[l].reshape(MOD_ROWS * 6, 1, D)
        if l % 2 == 0:
            q, k, v, r, g = _gla_in(x, mods, gla_w_in[s], gla_w_gate[s], gla_b_gate[s])
            o_ctx, s_new = _gla_scan(q, k, v, g, None, s)
            (o_lat,) = _gla_scan(q, k, v, g, state_gla_S, s)
            gla_states.append(s_new)
            norm_g, w_out = gla_norm_g[s], gla_w_out[s]
        else:
            q, k, v, r, gc, gr = _mlstm_in(x, mods, mlstm_w_in[s], mlstm_b_gates[s])
            o_ctx, c_new, n_new, m_new = _mlstm_scan(q, k, v, gc, gr, None, s)
            (o_lat,) = _mlstm_scan(q, k, v, gc, gr, (state_mlstm_C, state_mlstm_n, state_mlstm_m), s)
            ml_c.append(c_new)
            ml_n.append(n_new[:, :, :, 0, :])
            ml_m.append(m_new[:, :, :, 0, 0])
            norm_g, w_out = mlstm_norm_g[s], mlstm_w_out[s]
        w_route = jnp.zeros((D, ROUTE_W), F32).at[:, :NE].set(moe_w_expert[l]).at[:, NE:NE + N_GROUPS].set(
            moe_w_group[l])
        b_route = jnp.zeros((1, ROUTE_W), F32).at[0, :NE].set(moe_b_expert[l]).at[0, NE:NE + N_GROUPS].set(
            moe_b_group[l])
        x1, h2, rinfo, cnt = _out_proj(o_ctx, o_lat, r, x, mods, norm_g, w_out, ln_g[l, 0], ln_b[l, 0], w_route, b_route)
        x = _moe(h2, rinfo, cnt, x1, mods, ln_g[l, 1], ln_b[l, 1], l, moe_w_gate, moe_w_up, moe_w_down,
                 split=(l == DEPTH - 1))
    y_ctx = x[0].reshape(N_CTX_SEQ, T_CTX, D)
    y_lat = x[1].reshape(N_LAT_SEQ, T_LAT, D)
    return (y_ctx, y_lat, jnp.stack(gla_states, 1), jnp.stack(ml_c, 1), jnp.stack(ml_n, 1), jnp.stack(ml_m, 1))
```

```python
import functools
import math

import jax
import jax.numpy as jnp
from jax import lax
from jax.experimental import pallas as pl
from jax.experimental.pallas import tpu as pltpu

F32 = jnp.float32
BF16 = jnp.bfloat16

D = 1024
N_CTX_SEQ, T_CTX = 16, 256
N_LAT_SEQ, T_LAT = 8, 1024
N_CTX = N_CTX_SEQ * T_CTX
N_LAT = N_LAT_SEQ * T_LAT
N_TOK = N_CTX + N_LAT
DEPTH = 4
GRID_W = 64
NH = 4
DK = 128
DV = 256
DKT = NH * DK
DVT = NH * DV
GATE_RANK = 16
GATE_TAU = 16.0
CH = 64
N_GROUPS = 4
EPG = 8
NE = N_GROUPS * EPG
DE = 256
DN_ALPHA = (2.0 * DEPTH) ** 0.25
LN_EPS = 1e-5
MOD_ROWS = 16
TB = 1024
HPB = 4
GATE_F = 2 * NH
SB = 256
NCB = SB // CH
CH_SHIFT = 6
NB = N_TOK // TB
ROUTE_W = 128
NEG_INF = float("-inf")
VMEM_LIMIT = 48 * 1024 * 1024


def _cp(sem):
    return pltpu.CompilerParams(dimension_semantics=sem, vmem_limit_bytes=VMEM_LIMIT)


def _dot(a, b):
    return jnp.dot(a, b, preferred_element_type=F32)


def _dot_nt(a, b):
    return lax.dot_general(a, b, (((1,), (1,)), ((), ())), preferred_element_type=F32)


def _dot_f32(a, b):
    return jnp.dot(a, b, preferred_element_type=F32, precision=lax.Precision.HIGHEST)


def _split2(x):
    hi = x.astype(BF16)
    lo = (x - hi.astype(F32)).astype(BF16)
    return hi, lo


def _log_sigmoid(x):
    return -(jnp.maximum(-x, 0.0) + jnp.log1p(jnp.exp(-jnp.abs(x))))


def _silu(x):
    return x * jax.nn.sigmoid(x)


def _layernorm(x, g, b):
    mu = jnp.mean(x, axis=-1, keepdims=True)
    xc = x - mu
    var = jnp.mean(xc * xc, axis=-1, keepdims=True)
    return xc * lax.rsqrt(var + LN_EPS) * g + b


def _mod_row_tb(i):
    return jnp.where(i < N_CTX // TB, 0, 1 + (i - N_CTX // TB) // (T_LAT // TB))


def _mod_spec(j, row_fn):
    return pl.BlockSpec((1, 1, D), lambda i, *_: (row_fn(i) * 6 + j, 0, 0))


def _mod_kernel(c_ref, w_ref, b_ref, o_ref):
    o_ref[0] = _dot_f32(_silu(c_ref[...]), w_ref[0]) + b_ref[0]


def _modulation(cvecs, adaln_w, adaln_b):
    nj = 6
    return pl.pallas_call(
        _mod_kernel,
        out_shape=jax.ShapeDtypeStruct((DEPTH, MOD_ROWS, 6 * D), F32),
        grid=(DEPTH, nj),
        in_specs=[
            pl.BlockSpec((MOD_ROWS, D), lambda l, j: (0, 0)),
            pl.BlockSpec((1, D, D), lambda l, j: (l, 0, j)),
            pl.BlockSpec((1, 1, D), lambda l, j: (l, 0, j)),
        ],
        out_specs=pl.BlockSpec((1, MOD_ROWS, D), lambda l, j: (l, 0, j)),
        compiler_params=_cp(("parallel", "parallel")),
        name="adaln_modulation",
    )(cvecs, adaln_w, adaln_b.reshape(DEPTH, 1, 6 * D))


def _embed_kernel(xp_ref, xs_ref, pos_ref, o_ref):
    i = pl.program_id(0)

    @pl.when(i < N_CTX // TB)
    def _():
        o_ref[...] = xp_ref[...]

    @pl.when(i >= N_CTX // TB)
    def _():
        o_ref[...] = xs_ref[...] + pos_ref[...]


def _embed(x_prompt, x_sample, pos):
    nc = N_CTX // TB
    return pl.pallas_call(
        _embed_kernel,
        out_shape=jax.ShapeDtypeStruct((N_TOK, D), F32),
        grid=(NB,),
        in_specs=[
            pl.BlockSpec((TB, D), lambda i: (jnp.minimum(i, nc - 1), 0)),
            pl.BlockSpec((TB, D), lambda i: (jnp.maximum(i - nc, 0), 0)),
            pl.BlockSpec((TB, D), lambda i: (jnp.maximum(i - nc, 0) % (T_LAT // TB), 0)),
        ],
        out_specs=pl.BlockSpec((TB, D), lambda i: (i, 0)),
        compiler_params=_cp(("parallel",)),
        name="embed_tokens",
    )(x_prompt.reshape(N_CTX, D), x_sample.reshape(N_LAT, D), pos)


def _grid_posemb(T):
    rows = T // GRID_W
    r, cidx = jnp.meshgrid(jnp.arange(rows), jnp.arange(GRID_W), indexing="ij")
    nf = D // 4
    freqs = jnp.exp(-math.log(10000.0) * jnp.arange(nf, dtype=F32) / nf)

    def emb(p):
        a = p.reshape(-1).astype(F32)[:, None] * freqs
        return jnp.concatenate([jnp.sin(a), jnp.cos(a)], -1)

    return jnp.concatenate([emb(r), emb(cidx)], -1)


def _gla_in_kernel(x_ref, sh_ref, sc_ref, wq_ref, wk_ref, wv_ref, wr_ref, wg_ref, wgate_ref, bgate_ref,
                   q_ref, k_ref, v_ref, r_ref, g_ref):
    h = (x_ref[...] * (1.0 + sc_ref[0]) + sh_ref[0]).astype(BF16)
    q_ref[...] = _dot(h, wq_ref[...]) * (DK ** -0.5)
    k_ref[...] = _dot(h, wk_ref[...])
    v_ref[...] = _dot(h, wv_ref[...]).astype(BF16)
    r_ref[...] = _silu(_dot(h, wr_ref[...])).astype(BF16)
    glr = _dot(h, wg_ref[...]).astype(BF16)
    z = _dot(glr, wgate_ref[...]) + bgate_ref[...]
    g_ref[...] = _log_sigmoid(z) * (1.0 / GATE_TAU)


def _gla_in(x, mods, w_in, w_gate, b_gate):
    wq = w_in[:, :DKT].astype(BF16)
    wk = w_in[:, DKT:2 * DKT].astype(BF16)
    wv = w_in[:, 2 * DKT:2 * DKT + DVT].astype(BF16)
    wr = w_in[:, 2 * DKT + DVT:2 * DKT + 2 * DVT].astype(BF16)
    wg = w_in[:, 2 * DKT + 2 * DVT:].astype(BF16)
    wgate = jnp.zeros((2 * GATE_RANK, 2 * DKT), F32)
    wgate = wgate.at[:GATE_RANK, :DKT].set(w_gate[0]).at[GATE_RANK:, DKT:].set(w_gate[1]).astype(BF16)
    bgate = b_gate.reshape(1, 2 * DKT)
    full = lambda s: pl.BlockSpec(s, lambda i: (0,) * len(s), pipeline_mode=pl.Buffered(1))
    tok = lambda n: pl.BlockSpec((TB, n), lambda i: (i, 0))
    return pl.pallas_call(
        _gla_in_kernel,
        out_shape=[jax.ShapeDtypeStruct((N_TOK, DKT), F32), jax.ShapeDtypeStruct((N_TOK, DKT), F32),
                   jax.ShapeDtypeStruct((N_TOK, DVT), BF16), jax.ShapeDtypeStruct((N_TOK, DVT), BF16),
                   jax.ShapeDtypeStruct((N_TOK, 2 * DKT), F32)],
        grid=(NB,),
        in_specs=[tok(D), _mod_spec(0, _mod_row_tb), _mod_spec(1, _mod_row_tb),
                  full((D, DKT)), full((D, DKT)), full((D, DVT)), full((D, DVT)), full((D, 2 * GATE_RANK)),
                  full((2 * GATE_RANK, 2 * DKT)), full((1, 2 * DKT))],
        out_specs=[tok(DKT), tok(DKT), tok(DVT), tok(DVT), tok(2 * DKT)],
        compiler_params=_cp(("parallel",)),
        name="gla_in_proj",
    )(x, mods, mods, wq, wk, wv, wr, wg, wgate, bgate)


def _mlstm_in_kernel(x_ref, sh_ref, sc_ref, wq_ref, wk_ref, wv_ref, wr_ref, wg_ref, wgt_ref, bg_ref, bgt_ref,
                     q_ref, k_ref, v_ref, r_ref, gc_ref, gr_ref):
    h = (x_ref[...] * (1.0 + sc_ref[0]) + sh_ref[0]).astype(BF16)
    q_ref[...] = _dot(h, wq_ref[...]) * (DK ** -0.5)
    k_ref[...] = _dot(h, wk_ref[...])
    v_ref[...] = _dot(h, wv_ref[...]).astype(BF16)
    r_ref[...] = jax.nn.sigmoid(_dot(h, wr_ref[...])).astype(BF16)
    gc = _dot(h, wg_ref[...]) + bg_ref[...]
    gr = _dot_nt(wgt_ref[...], h) + bgt_ref[...]
    gc_ref[...] = jnp.where(lax.broadcasted_iota(jnp.int32, gc.shape, 1) >= GATE_F, _log_sigmoid(gc), gc)
    gr = jnp.where(lax.broadcasted_iota(jnp.int32, gr.shape, 0) >= GATE_F, _log_sigmoid(gr), gr)
    for c in range(TB // SB):
        gr_ref[c] = gr[:, c * SB:(c + 1) * SB]


def _mlstm_in(x, mods, w_in, b_gates):
    wq = w_in[:, :DKT].astype(BF16)
    wk = w_in[:, DKT:2 * DKT].astype(BF16)
    wv = w_in[:, 2 * DKT:2 * DKT + DVT].astype(BF16)
    wr = w_in[:, 2 * DKT + DVT:2 * DKT + 2 * DVT].astype(BF16)
    wg = w_in[:, 2 * DKT + 2 * DVT:].reshape(D, 2, 2, NH).transpose(0, 2, 3, 1).reshape(D, 4 * NH).astype(BF16)
    bg = b_gates.reshape(2, 2, NH).transpose(1, 2, 0).reshape(1, 4 * NH)
    full = lambda s: pl.BlockSpec(s, lambda i: (0,) * len(s), pipeline_mode=pl.Buffered(1))
    tok = lambda n: pl.BlockSpec((TB, n), lambda i: (i, 0))
    return pl.pallas_call(
        _mlstm_in_kernel,
        out_shape=[jax.ShapeDtypeStruct((N_TOK, DKT), F32), jax.ShapeDtypeStruct((N_TOK, DKT), F32),
                   jax.ShapeDtypeStruct((N_TOK, DVT), BF16), jax.ShapeDtypeStruct((N_TOK, DVT), BF16),
                   jax.ShapeDtypeStruct((N_TOK, 4 * NH), F32),
                   jax.ShapeDtypeStruct((N_TOK // SB, 4 * NH, SB), F32)],
        grid=(NB,),
        in_specs=[tok(D), _mod_spec(0, _mod_row_tb), _mod_spec(1, _mod_row_tb),
                  full((D, DKT)), full((D, DKT)), full((D, DVT)), full((D, DVT)), full((D, 4 * NH)),
                  full((4 * NH, D)), full((1, 4 * NH)), full((4 * NH, 1))],
        out_specs=[tok(DKT), tok(DKT), tok(DVT), tok(DVT),
                   tok(4 * NH),
                   pl.BlockSpec((TB // SB, 4 * NH, SB), lambda i: (i, 0, 0))],
        compiler_params=_cp(("parallel",)),
        name="mlstm_in_proj",
    )(x, mods, mods, wq, wk, wv, wr, wg, wg.T, bg, bg.T)


def _block_tri_masks():
    row = lax.broadcasted_iota(jnp.int32, (SB, SB), 0)
    col = lax.broadcasted_iota(jnp.int32, (SB, SB), 1)
    same = (row >> CH_SHIFT) == (col >> CH_SHIFT)
    return same & (row >= col), same & (row <= col)


def _chunk_scan(x, forward, axis=0, op=jnp.add, fill=0.0):
    n = x.shape[axis]
    pos = lax.broadcasted_iota(jnp.int32, x.shape, axis) & (CH - 1)
    s = 1
    while s < CH:
        if forward:
            x = op(x, jnp.where(pos >= s, pltpu.roll(x, s, axis), fill))
        else:
            x = op(x, jnp.where(pos < CH - s, pltpu.roll(x, n - s, axis), fill))
        s *= 2
    return x


def _chunk_cumsum(x, forward):
    return _chunk_scan(x, forward)


def _cat2(x):
    return jnp.concatenate([x, x], axis=1)


def _per_chunk(vals, n):
    return jnp.concatenate([jnp.broadcast_to(v, (CH, n)) for v in vals], axis=0)


def _chunk_masked_cat(x):
    xb = x.astype(BF16)
    n = x.shape[1]
    cols = []
    for c in range(NCB):
        parts = [jnp.zeros((c * CH, n), BF16), xb[c * CH:(c + 1) * CH, :], jnp.zeros((SB - (c + 1) * CH, n), BF16)]
        cols.append(jnp.concatenate([p for p in parts if p.shape[0]], axis=0))
    return jnp.concatenate(cols, axis=1)


def _head_norm_rows(o):
    mu = jnp.mean(o, axis=-1, keepdims=True)
    oc = o - mu
    var = jnp.mean(oc * oc, axis=-1, keepdims=True)
    return oc * lax.rsqrt(var + LN_EPS)


def _gla_superblock(q, k, v, v_t, g, st, forward, mask):
    b = _chunk_cumsum(g, forward)
    r_last = CH - 1 if forward else 0
    r_mid = CH // 2 if forward else CH - 1 - CH // 2
    lasts = [b[c * CH + r_last:c * CH + r_last + 1, :] for c in range(NCB)]
    b_last = _per_chunk(lasts, DK)
    b_mid = _per_chunk([b[c * CH + r_mid:c * CH + r_mid + 1, :] for c in range(NCB)], DK)
    qe = (q * jnp.exp(b - b_mid)).astype(BF16)
    ke = (k * jnp.exp(b_mid - b)).astype(BF16)
    qb = (q * jnp.exp(b)).astype(BF16)
    kd = k * jnp.exp(b_last - b)
    a = jnp.where(mask, _dot_nt(qe, ke), 0.0).astype(BF16)
    o_intra = _dot(a, v)
    d_t = _dot(v_t, _chunk_masked_cat(kd))
    inter = [None] * NCB
    for c in (range(NCB) if forward else reversed(range(NCB))):
        inter[c] = _dot_nt(qb[c * CH:(c + 1) * CH, :], st.astype(BF16))
        st = st * jnp.exp(lasts[c]) + d_t[:, c * DK:(c + 1) * DK]
    return o_intra + jnp.concatenate(inter, axis=0), st


def _scan_driver(T, o_ref, of_ref, ob_ref, step):
    nsb = T // SB
    if nsb == 1:
        step(0, 0)
    else:
        def body(i, carry):
            step(i, nsb - 1 - i)
            return carry

        lax.fori_loop(0, nsb, body, 0)

    def norm_body(j, carry):
        rows = pl.ds(pl.multiple_of(j * SB, SB), SB)
        for hh in range(HPB):
            cv = slice(hh * DV, (hh + 1) * DV)
            o_ref[rows, cv] = _head_norm_rows(of_ref[rows, cv] + ob_ref[rows, cv]).astype(BF16)
        return carry

    if nsb == 1:
        norm_body(0, 0)
    else:
        lax.fori_loop(0, nsb, norm_body, 0)


def _gla_scan_kernel(*refs, T, has_state, emit_state):
    q_ref, k_ref, v_ref, gf_ref, gb_ref = refs[:5]
    pos = 5
    if has_state:
        s0f_ref, s0b_ref = refs[pos:pos + 2]
        pos += 2
    o_ref = refs[pos]
    pos += 1
    if emit_state:
        sout_ref = refs[pos]
        pos += 1
    stf_ref, stb_ref, vt_ref, of_ref, ob_ref = refs[pos:pos + 5]

    nsb = T // SB
    for hh in range(HPB):
        for j in range(nsb):
            vt_ref[hh, j] = v_ref[j * SB:(j + 1) * SB, hh * DV:(hh + 1) * DV].astype(F32).T.astype(BF16)
        if has_state:
            stf_ref[hh] = s0f_ref[hh].T
            stb_ref[hh] = s0b_ref[hh].T
        else:
            stf_ref[hh] = jnp.zeros((DV, DK), F32)
            stb_ref[hh] = jnp.zeros((DV, DK), F32)

    lower, upper = _block_tri_masks()

    def one(j, forward, hh):
        rows = pl.ds(pl.multiple_of(j * SB, SB), SB)
        ck = slice(hh * DK, (hh + 1) * DK)
        cv = slice(hh * DV, (hh + 1) * DV)
        st_ref = stf_ref if forward else stb_ref
        o, st = _gla_superblock(q_ref[rows, ck], k_ref[rows, ck], v_ref[rows, cv].astype(BF16), vt_ref[hh, j],
                                (gf_ref if forward else gb_ref)[rows, ck], st_ref[hh], forward,
                                lower if forward else upper)
        st_ref[hh] = st
        (of_ref if forward else ob_ref)[rows, cv] = o

    def step(jf, jb):
        for hh in range(HPB):
            one(jf, True, hh)
            one(jb, False, hh)

    _scan_driver(T, o_ref, of_ref, ob_ref, step)
    if emit_state:
        for hh in range(HPB):
            sout_ref[0, hh] = stf_ref[hh].T
            sout_ref[1, hh] = stb_ref[hh].T


def _gla_scan(q, k, v, g, state, slot):
    ctx = state is None
    T = T_CTX if ctx else T_LAT
    n_seq = N_CTX_SEQ if ctx else N_LAT_SEQ
    off = 0 if ctx else N_CTX // T_LAT
    in_specs = [
        pl.BlockSpec((T, HPB * DK), lambda b, h: (off + b, h)),
        pl.BlockSpec((T, HPB * DK), lambda b, h: (off + b, h)),
        pl.BlockSpec((T, HPB * DV), lambda b, h: (off + b, h)),
        pl.BlockSpec((T, HPB * DK), lambda b, h: (off + b, h)),
        pl.BlockSpec((T, HPB * DK), lambda b, h: (off + b, NH // HPB + h)),
    ]
    args = [q, k, v, g, g]
    out_shape = [jax.ShapeDtypeStruct((n_seq * T, DVT), BF16)]
    out_specs = [pl.BlockSpec((T, HPB * DV), lambda b, h: (b, h))]
    if ctx:
        out_shape.append(jax.ShapeDtypeStruct((N_CTX_SEQ, 2, NH, DK, DV), F32))
        out_specs.append(pl.BlockSpec((None, 2, HPB, DK, DV), lambda b, h: (b, 0, h, 0, 0)))
    else:
        sq = (None, None, None, HPB, DK, DV)
        in_specs += [pl.BlockSpec(sq, lambda b, h: (b, slot, 0, h, 0, 0)),
                     pl.BlockSpec(sq, lambda b, h: (b, slot, 1, h, 0, 0))]
        args += [state, state]
    res = pl.pallas_call(
        functools.partial(_gla_scan_kernel, T=T, has_state=not ctx, emit_state=ctx),
        out_shape=out_shape,
        grid=(n_seq, NH // HPB),
        in_specs=in_specs,
        out_specs=out_specs,
        scratch_shapes=[pltpu.VMEM((HPB, DV, DK), F32), pltpu.VMEM((HPB, DV, DK), F32),
                        pltpu.VMEM((HPB, T // SB, DV, SB), BF16), pltpu.VMEM((T, HPB * DV), F32), pltpu.VMEM((T, HPB * DV), F32)],
        compiler_params=_cp(("parallel", "parallel")),
        name="gla_scan_ctx" if ctx else "gla_scan_lat",
    )(*args)
    return res


def _mlstm_scan_kernel(*refs, T, has_state, emit_state):
    q_ref, k_ref, v_ref, gc_ref, gr_ref = refs[:5]
    pos = 5
    if has_state:
        c0f_ref, c0b_ref, n0f_ref, n0b_ref, m0f_ref, m0b_ref = refs[pos:pos + 6]
        pos += 6
    o_ref = refs[pos]
    pos += 1
    if emit_state:
        cout_ref, nout_ref, mout_ref = refs[pos:pos + 3]
        pos += 3
    ct_ref, n_ref, m_ref, vt_ref, of_ref, ob_ref = refs[pos:pos + 6]

    nsb = T // SB
    for hh in range(HPB):
        for j in range(nsb):
            vt_ref[hh, j] = v_ref[j * SB:(j + 1) * SB, hh * DV:(hh + 1) * DV].astype(F32).T.astype(BF16)
        for d in range(2):
            sl = d * HPB + hh
            if has_state:
                c0, n0, m0 = ((c0f_ref, n0f_ref, m0f_ref), (c0b_ref, n0b_ref, m0b_ref))[d]
                ct_ref[sl] = c0[hh].T
                n_ref[sl] = n0[hh]
                m_ref[sl] = m0[hh]
            else:
                ct_ref[sl] = jnp.zeros((DV, DK), F32)
                n_ref[sl] = jnp.zeros((1, DK), F32)
                m_ref[sl] = jnp.zeros((1, DK), F32)

    lower, upper = _block_tri_masks()

    def gates(j, forward):
        rows = pl.ds(pl.multiple_of(j * SB, SB), SB)
        gc = gc_ref[rows, :]
        gr = gr_ref[j]
        i_col, f_col = gc[:, :GATE_F], gc[:, GATE_F:]
        b_col = _chunk_scan(f_col, forward)
        m_col = b_col + _chunk_scan(i_col - b_col, forward, op=jnp.maximum, fill=NEG_INF)
        b_row = _chunk_scan(gr[GATE_F:, :], forward, axis=1) - gr[:GATE_F, :]
        return i_col, b_col, m_col, b_row

    def one(j, forward, hh, shared):
        d = 0 if forward else 1
        sl = d * HPB + hh
        col = 2 * hh + d
        rows = pl.ds(pl.multiple_of(j * SB, SB), SB)
        q = q_ref[rows, hh * DK:(hh + 1) * DK]
        k = k_ref[rows, hh * DK:(hh + 1) * DK]
        v = v_ref[rows, hh * DV:(hh + 1) * DV].astype(BF16)
        mask = lower if forward else upper
        order = list(range(NCB)) if forward else list(reversed(range(NCB)))
        r_last = CH - 1 if forward else 0
        i_c, b_c, m_intra = [jnp.broadcast_to(a[:, col:col + 1], (SB, DK)) for a in shared[:3]]
        brow = shared[3][col:col + 1, :]
        b_last = [b_c[c * CH + r_last:c * CH + r_last + 1, :] for c in range(NCB)]
        a_c = _per_chunk(b_last, DK) - b_c + i_c
        m_loc = [jnp.max(a_c[c * CH:(c + 1) * CH, :], axis=0, keepdims=True) for c in range(NCB)]
        kw = k * jnp.exp(a_c - _per_chunk(m_loc, DK))
        kv_t = _dot(vt_ref[hh, j], _chunk_masked_cat(kw))
        k_loc = [jnp.sum(kw[c * CH:(c + 1) * CH, :], axis=0, keepdims=True) for c in range(NCB)]

        m = m_ref[sl]
        m_start, s_old, s_loc = [None] * NCB, [None] * NCB, [None] * NCB
        for c in order:
            m_start[c] = m
            m_new = jnp.maximum(b_last[c] + m, m_loc[c])
            s_old[c] = jnp.exp(b_last[c] + m - m_new)
            s_loc[c] = jnp.exp(m_loc[c] - m_new)
            m = m_new
        m_ref[sl] = m

        l_inter = b_c + _per_chunk(m_start, DK)
        m_i = jnp.maximum(l_inter, m_intra)
        d_log = jnp.where(mask, _cat2(b_c - m_i) - brow, NEG_INF)
        qb = q.astype(BF16)
        s = (_dot_nt(qb, k.astype(BF16)) * jnp.exp(d_log)).astype(BF16)
        e_inter = jnp.exp(l_inter - m_i)
        nd = _dot(s, jnp.concatenate([v, jnp.ones((SB, DK), BF16)], axis=1))
        num, den = nd[:, :DV], nd[:, DV:]

        ct = ct_ref[sl]
        nrm = n_ref[sl]
        inter = [None] * NCB
        for c in order:
            state = jnp.concatenate([ct, jnp.broadcast_to(nrm, (DK, DK))], axis=0).astype(BF16)
            inter[c] = _dot_nt(qb[c * CH:(c + 1) * CH, :], state)
            ct = s_old[c] * ct + s_loc[c] * kv_t[:, c * DK:(c + 1) * DK]
            nrm = s_old[c] * nrm + s_loc[c] * k_loc[c]
        ct_ref[sl] = ct
        n_ref[sl] = nrm

        inter = jnp.concatenate(inter, axis=0)
        num = num + _cat2(e_inter) * inter[:, :DV]
        den = den + e_inter * inter[:, DV:]
        inv = 1.0 / jnp.maximum(jnp.abs(den), jnp.exp(-m_i))
        (of_ref if forward else ob_ref)[rows, hh * DV:(hh + 1) * DV] = num * _cat2(inv)

    def step(jf, jb):
        shared_f = gates(jf, True)
        shared_b = gates(jb, False)
        for hh in range(HPB):
            one(jf, True, hh, shared_f)
            one(jb, False, hh, shared_b)

    _scan_driver(T, o_ref, of_ref, ob_ref, step)
    if emit_state:
        for d in range(2):
            for hh in range(HPB):
                cout_ref[d, hh] = ct_ref[d * HPB + hh].T
                nout_ref[d, hh] = n_ref[d * HPB + hh]
                mout_ref[d, hh] = m_ref[d * HPB + hh]


def _mlstm_scan(q, k, v, gc, gr, states, slot):
    ctx = states is None
    T = T_CTX if ctx else T_LAT
    n_seq = N_CTX_SEQ if ctx else N_LAT_SEQ
    off = 0 if ctx else N_CTX // T_LAT
    in_specs = [
        pl.BlockSpec((T, HPB * DK), lambda b, h: (off + b, h)),
        pl.BlockSpec((T, HPB * DK), lambda b, h: (off + b, h)),
        pl.BlockSpec((T, HPB * DV), lambda b, h: (off + b, h)),
        pl.BlockSpec((T, 4 * NH), lambda b, h: (off + b, 0)),
        pl.BlockSpec((T // SB, 4 * NH, SB), lambda b, h: (off + b, 0, 0)),
    ]
    assert HPB == NH
    args = [q, k, v, gc, gr]
    out_shape = [jax.ShapeDtypeStruct((n_seq * T, DVT), BF16)]
    out_specs = [pl.BlockSpec((T, HPB * DV), lambda b, h: (b, h))]
    if ctx:
        out_shape += [jax.ShapeDtypeStruct((N_CTX_SEQ, 2, NH, DK, DV), F32),
                      jax.ShapeDtypeStruct((N_CTX_SEQ, 2, NH, 1, DK), F32),
                      jax.ShapeDtypeStruct((N_CTX_SEQ, 2, NH, 1, DK), F32)]
        out_specs += [pl.BlockSpec((None, 2, HPB, DK, DV), lambda b, h: (b, 0, h, 0, 0)),
                      pl.BlockSpec((None, 2, HPB, 1, DK), lambda b, h: (b, 0, h, 0, 0)),
                      pl.BlockSpec((None, 2, HPB, 1, DK), lambda b, h: (b, 0, h, 0, 0))]
    else:
        c0, n0, m0 = states
        n0 = n0.reshape(N_LAT_SEQ, -1, 2, NH, 1, DK)
        m0 = jnp.broadcast_to(m0[..., None, None], m0.shape + (1, DK))
        sq_c = (None, None, None, HPB, DK, DV)
        sq_v = (None, None, None, HPB, 1, DK)
        for arr, sq in ((c0, sq_c), (n0, sq_v), (m0, sq_v)):
            for d in range(2):
                in_specs.append(pl.BlockSpec(sq, functools.partial(lambda b, h, d: (b, slot, d, h, 0, 0), d=d)))
                args.append(arr)
    return pl.pallas_call(
        functools.partial(_mlstm_scan_kernel, T=T, has_state=not ctx, emit_state=ctx),
        out_shape=out_shape,
        grid=(n_seq, NH // HPB),
        in_specs=in_specs,
        out_specs=out_specs,
        scratch_shapes=[pltpu.VMEM((2 * HPB, DV, DK), F32), pltpu.VMEM((2 * HPB, 1, DK), F32),
                        pltpu.VMEM((2 * HPB, 1, DK), F32),
                        pltpu.VMEM((HPB, T // SB, DV, SB), BF16), pltpu.VMEM((T, HPB * DV), F32), pltpu.VMEM((T, HPB * DV), F32)],
        compiler_params=_cp(("parallel", "parallel")),
        name="mlstm_scan_ctx" if ctx else "mlstm_scan_lat",
    )(*args)


def _route(lg):
    lane = lax.broadcasted_iota(jnp.int32, lg.shape, 1)
    big = jnp.int32(ROUTE_W)
    is_g = (lane >= NE) & (lane < NE + N_GROUPS)
    gl = jnp.where(is_g, lg, NEG_INF)
    gmax = jnp.max(gl, axis=1, keepdims=True)
    gsel = jnp.min(jnp.where(gl == gmax, lane, big), axis=1, keepdims=True) - NE
    gw = 1.0 / jnp.sum(jnp.exp(gl - gmax), axis=1, keepdims=True)
    ing = (lane < NE) & ((lane >> 3) == gsel)
    el = jnp.where(ing, lg, NEG_INF)
    emax = jnp.max(el, axis=1, keepdims=True)
    p = jnp.exp(el - emax)
    prob = p / jnp.sum(p, axis=1, keepdims=True)
    p1 = jnp.max(prob, axis=1, keepdims=True)
    i1 = jnp.min(jnp.where(ing & (prob == p1), lane, big), axis=1, keepdims=True)
    rest = ing & (lane != i1)
    prob2 = jnp.where(rest, prob, -1.0)
    p2 = jnp.max(prob2, axis=1, keepdims=True)
    i2 = jnp.min(jnp.where(rest & (prob2 == p2), lane, big), axis=1, keepdims=True)
    tot = p1 + p2
    rinfo = jnp.where(lane == 0, i1.astype(F32),
                      jnp.where(lane == 1, i2.astype(F32),
                                jnp.where(lane == 2, gw * (p1 / tot), jnp.where(lane == 3, gw * (p2 / tot), 0.0))))
    sel = jnp.where((lane == i1) | (lane == i2), 1.0, 0.0)
    return rinfo, [jnp.sum(sel[c * TP:(c + 1) * TP, :], axis=0, keepdims=True) for c in range(lg.shape[0] // TP)]


def _out_kernel(oc_ref, ol_ref, r_ref, x_ref, gate_ref, sh_ref, sc_ref, ng_ref, wo_ref, lg_ref, lb_ref, wr_ref,
                br_ref, x1_ref, h2_ref, rinfo_ref, cnt_ref):
    o = jnp.where(pl.program_id(0) < N_CTX // TB, oc_ref[...], ol_ref[...])
    y = _dot((o * ng_ref[...] * r_ref[...]).astype(BF16), wo_ref[...])
    x1 = _layernorm(DN_ALPHA * x_ref[...] + gate_ref[0] * y, lg_ref[...], lb_ref[...])
    x1_ref[...] = x1
    h2 = x1 * (1.0 + sc_ref[0]) + sh_ref[0]
    h_hi, h_lo = _split2(h2)
    h2_ref[...] = h_hi
    t = _dot(h_hi, wr_ref[...])
    lg = t[:, :ROUTE_W] + t[:, ROUTE_W:] + _dot(h_lo, wr_ref[:, :ROUTE_W]) + br_ref[...]
    rinfo_ref[...], counts = _route(lg)
    for c, count in enumerate(counts):
        cnt_ref[c] = count


def _out_proj(o_ctx, o_lat, r, x, mods, norm_g, w_out, ln_g, ln_b, w_route, b_route):
    nc = N_CTX // TB
    full = lambda s: pl.BlockSpec(s, lambda i: (0,) * len(s), pipeline_mode=pl.Buffered(1))
    tok = lambda n: pl.BlockSpec((TB, n), lambda i: (i, 0))
    return pl.pallas_call(
        _out_kernel,
        out_shape=[jax.ShapeDtypeStruct((N_TOK, D), F32), jax.ShapeDtypeStruct((N_TOK, D), BF16),
                   jax.ShapeDtypeStruct((N_TOK, ROUTE_W), F32), jax.ShapeDtypeStruct((NBP, 1, ROUTE_W), F32)],
        grid=(NB,),
        in_specs=[pl.BlockSpec((TB, DVT), lambda i: (jnp.minimum(i, nc - 1), 0)),
                  pl.BlockSpec((TB, DVT), lambda i: (jnp.maximum(i - nc, 0), 0)),
                  tok(DVT), tok(D),
                  _mod_spec(2, _mod_row_tb), _mod_spec(3, _mod_row_tb), _mod_spec(4, _mod_row_tb),
                  full((1, DVT)), full((DVT, D)), full((1, D)), full((1, D)),
                  full((D, 2 * ROUTE_W)), full((1, ROUTE_W))],
        out_specs=[tok(D), tok(D), tok(ROUTE_W), pl.BlockSpec((TB // TP, 1, ROUTE_W), lambda i: (i, 0, 0))],
        compiler_params=_cp(("parallel",)),
        name="out_proj_route",
    )(o_ctx, o_lat, r, x, mods, mods, mods, norm_g.reshape(1, DVT), w_out.astype(BF16),
      ln_g.reshape(1, D), ln_b.reshape(1, D), jnp.concatenate(_split2(w_route), axis=1), b_route)


TP = 512
NBP = N_TOK // TP
SEG = 16
R_LOC = 1536
PT = 256
XW = D + ROUTE_W
TMS = 256
R_TOT = -(-(2 * N_TOK + NBP * NE * (SEG - 1) + NE * (TMS - 1)) // TMS) * TMS
NT = R_TOT // TMS
POS_SPLIT = 64.0


def _mod_row_tp(i):
    return jnp.where(i < N_CTX // TP, 0, 1 + (i - N_CTX // TP) // (T_LAT // TP))


def _lane_pack(cols, shape):
    lane = lax.broadcasted_iota(jnp.int32, shape, 1)
    out = jnp.zeros(shape, F32)
    for j, c in enumerate(cols):
        out = jnp.where(lane == j, c, out)
    return out


WAIT_CHUNKS = (32, 4, 1)
MAXC = R_LOC // SEG
ZROWS = WAIT_CHUNKS[0] * SEG
NZ = NE + 1


def _segment_start(gtab_ref, ntot_ref, blk, make_copy):
    def per_chunk(j, carry):
        g = gtab_ref[blk * MAXC + j]
        make_copy(pl.multiple_of(j * SEG, SEG), pl.multiple_of(g, SEG), SEG).start()
        return carry

    lax.fori_loop(0, ntot_ref[blk], per_chunk, 0)


def _segment_wait(n_chunks, make_copy):
    left = n_chunks
    for chunks in WAIT_CHUNKS:
        n_wait = left // chunks

        def wait_piece(k, c, chunks=chunks):
            make_copy(0, 0, chunks * SEG).wait()
            return c

        lax.fori_loop(0, n_wait, wait_piece, 0)
        left = left - n_wait * chunks


def _perm_kernel(gtab_ref, ntot_ref, zrow_ref, zcnt_ref, h_ref, ri_ref, loffv_ref, pos_ref, xs_ref,
                 xs_scr, zero_scr, sem, zsem):
    blk = pl.program_id(0)
    slot = blk % 2

    def zero_copy(_, g, n):
        return pltpu.make_async_copy(zero_scr.at[pl.ds(0, n)], xs_ref.at[pl.ds(g, n)], zsem)

    @pl.when(blk == 0)
    def _():
        zero_scr[...] = jnp.zeros_like(zero_scr)

        def per_range(r, carry):
            n = zcnt_ref[r]
            g0 = zrow_ref[r]
            n_big = n // WAIT_CHUNKS[0]

            def per_piece(k, c, first, chunks):
                zero_copy(0, pl.multiple_of(g0 + (first + k * chunks) * SEG, SEG), chunks * SEG).start()
                return c

            lax.fori_loop(0, n_big, functools.partial(per_piece, first=0, chunks=WAIT_CHUNKS[0]), 0)
            lax.fori_loop(0, n - n_big * WAIT_CHUNKS[0],
                          functools.partial(per_piece, first=n_big * WAIT_CHUNKS[0], chunks=1), 0)
            return carry + n

        lax.fori_loop(0, NZ, per_range, 0)
    ri = ri_ref[...]
    e_a, e_b, w_a, w_b = ri[:, 0:1], ri[:, 1:2], ri[:, 2:3], ri[:, 3:4]
    lanef = lax.broadcasted_iota(jnp.int32, ri.shape, 1).astype(F32)
    is_a = lanef == e_a
    is_b = lanef == e_b
    sel = jnp.where(is_a | is_b, 1.0, 0.0).astype(BF16)
    row = lax.broadcasted_iota(jnp.int32, (TP, TP), 0)
    col = lax.broadcasted_iota(jnp.int32, (TP, TP), 1)
    earlier = jnp.where(row > col, 1.0, 0.0).astype(BF16)
    lpos = loffv_ref[0] + _dot(earlier, sel)
    pos_a = jnp.sum(jnp.where(is_a, lpos, 0.0), axis=1, keepdims=True)
    pos_b = jnp.sum(jnp.where(is_b, lpos, 0.0), axis=1, keepdims=True)
    pos_ref[...] = _lane_pack([pos_a, pos_b], ri.shape)

    hi_a = jnp.floor(pos_a * (1.0 / POS_SPLIT))
    hi_b = jnp.floor(pos_b * (1.0 / POS_SPLIT))
    parts = _lane_pack([hi_a, pos_a - POS_SPLIT * hi_a, hi_b, pos_b - POS_SPLIT * hi_b], ri.shape).astype(BF16)
    pick = jnp.where(lax.broadcasted_iota(jnp.int32, (8, ROUTE_W), 0) == lax.broadcasted_iota(jnp.int32, (8, ROUTE_W), 1),
                     1.0, 0.0).astype(BF16)
    pr = _dot_nt(pick, parts)
    pos_a_r = POS_SPLIT * pr[0:1, :] + pr[1:2, :]
    pos_b_r = POS_SPLIT * pr[2:3, :] + pr[3:4, :]

    wa1, wa2 = _split2(w_a)
    wa3 = (w_a - wa1.astype(F32) - wa2.astype(F32))
    wb1, wb2 = _split2(w_b)
    wb3 = (w_b - wb1.astype(F32) - wb2.astype(F32))
    wl = _lane_pack([wa1.astype(F32), wa2.astype(F32), wa3, wb1.astype(F32), wb2.astype(F32), wb3, e_a],
                    ri.shape).astype(BF16)
    hcat = jnp.concatenate([h_ref[...], wl], axis=1)
    def tile(t):
        rio = (lax.broadcasted_iota(jnp.int32, (PT, TP), 0) + t * PT).astype(F32)
        onehot = jnp.where((rio == pos_a_r) | (rio == pos_b_r), 1.0, 0.0).astype(BF16)
        xs_scr[slot, t * PT:(t + 1) * PT, :] = _dot(onehot, hcat).astype(BF16)

    last = R_LOC // PT - 1
    for t in range(last):
        tile(t)
    pl.when(ntot_ref[blk] * SEG > last * PT)(functools.partial(tile, last))

    def copy_from(s):
        return lambda l, g, n: pltpu.make_async_copy(xs_scr.at[s, pl.ds(l, n)], xs_ref.at[pl.ds(g, n)], sem.at[s])

    _segment_start(gtab_ref, ntot_ref, blk, copy_from(slot))

    @pl.when(blk > 0)
    def _():
        _segment_wait(ntot_ref[blk - 1], copy_from(1 - slot))

    @pl.when(blk == NBP - 1)
    def _():
        _segment_wait(ntot_ref[blk], copy_from(slot))
        _segment_wait(lax.fori_loop(0, NZ, lambda r, c: c + zcnt_ref[r], 0), zero_copy)


def _moe_permute(h2, rinfo, tables):
    gtab, ntot, zrow, zcnt, loffv = tables
    grid_spec = pltpu.PrefetchScalarGridSpec(
        num_scalar_prefetch=4,
        grid=(NBP,),
        in_specs=[pl.BlockSpec((TP, D), lambda i, *_: (i, 0)),
                  pl.BlockSpec((TP, ROUTE_W), lambda i, *_: (i, 0)),
                  pl.BlockSpec((1, 1, ROUTE_W), lambda i, *_: (i, 0, 0))],
        out_specs=[pl.BlockSpec((TP, ROUTE_W), lambda i, *_: (i, 0)),
                   pl.BlockSpec(memory_space=pl.ANY)],
        scratch_shapes=[pltpu.VMEM((2, R_LOC, XW), BF16), pltpu.VMEM((ZROWS, XW), BF16),
                        pltpu.SemaphoreType.DMA((2,)), pltpu.SemaphoreType.DMA],
    )
    return pl.pallas_call(
        _perm_kernel,
        out_shape=[jax.ShapeDtypeStruct((N_TOK, ROUTE_W), F32), jax.ShapeDtypeStruct((R_TOT, XW), BF16)],
        grid_spec=grid_spec,
        compiler_params=_cp(("arbitrary",)),
        name="moe_permute",
    )(gtab, ntot, zrow, zcnt, h2, rinfo, loffv)


def _ffn_kernel(te_ref, nt_ref, enext_ref, eslot_ref, xs_ref, wg_hbm, wu_hbm, wd_hbm, ys_ref,
                wg32_ref, wu32_ref, wd32_ref, wg_ref, wu_ref, wd_ref, sem, *, layer):
    i = pl.program_id(0)
    active = i < nt_ref[0]
    e_i = te_ref[i]
    new_expert = (i == 0) | (e_i != te_ref[jnp.maximum(i - 1, 0)])

    def fetch(e, s):
        return [pltpu.make_async_copy(src.at[layer, e], dst.at[s], sem.at[s])
                for src, dst in ((wg_hbm, wg32_ref), (wu_hbm, wu32_ref), (wd_hbm, wd32_ref))]

    @pl.when(active & new_expert)
    def _():
        s = eslot_ref[e_i]
        nxt = enext_ref[e_i]

        @pl.when(i == 0)
        def _():
            for c in fetch(e_i, s):
                c.start()

        @pl.when(nxt >= 0)
        def _():
            for c in fetch(nxt, 1 - s):
                c.start()

        for c in fetch(e_i, s):
            c.wait()
        wg_ref[0] = wg32_ref[s].astype(BF16)
        wu_ref[0] = wu32_ref[s].astype(BF16)
        wd_ref[0] = wd32_ref[s].astype(BF16)

    @pl.when(active)
    def _():
        e = te_ref[i].astype(F32)
        xs = xs_ref[...]
        x = xs[:, :D]
        r = xs[:, D:].astype(F32)
        w_a = r[:, 0:1] + r[:, 1:2] + r[:, 2:3]
        w_b = r[:, 3:4] + r[:, 4:5] + r[:, 5:6]
        w = jnp.where(r[:, 6:7] == e, w_a, w_b)
        hid = _silu(_dot(x, wg_ref[0])) * _dot(x, wu_ref[0]) * w
        ys_ref[...] = _dot(hid.astype(BF16), wd_ref[0]).astype(BF16)

    @pl.when(i >= nt_ref[0])
    def _():
        ys_ref[...] = jnp.zeros_like(ys_ref)


def _moe_ffn(xs, schedule, layer, w_gate, w_up, w_down):
    tile_expert, n_tiles, e_next, e_slot = schedule
    grid_spec = pltpu.PrefetchScalarGridSpec(
        num_scalar_prefetch=4,
        grid=(NT,),
        in_specs=[pl.BlockSpec((TMS, XW), lambda i, te, nt, *_: (jnp.minimum(i, nt[0] - 1), 0)),
                  pl.BlockSpec(memory_space=pl.ANY), pl.BlockSpec(memory_space=pl.ANY),
                  pl.BlockSpec(memory_space=pl.ANY)],
        out_specs=pl.BlockSpec((TMS, D), lambda i, *_: (i, 0)),
        scratch_shapes=[pltpu.VMEM((2, D, DE), F32), pltpu.VMEM((2, D, DE), F32), pltpu.VMEM((2, DE, D), F32),
                        pltpu.VMEM((1, D, DE), BF16), pltpu.VMEM((1, D, DE), BF16), pltpu.VMEM((1, DE, D), BF16),
                        pltpu.SemaphoreType.DMA((2,))],
    )
    return pl.pallas_call(
        functools.partial(_ffn_kernel, layer=layer),
        out_shape=jax.ShapeDtypeStruct((R_TOT, D), BF16),
        grid_spec=grid_spec,
        compiler_params=_cp(("arbitrary",)),
        name="moe_ffn",
    )(tile_expert, n_tiles, e_next, e_slot, xs, w_gate, w_up, w_down)


def _unperm_kernel(gtab_ref, ntot_ref, pos_ref, x1_ref, gate_ref, lg_ref, lb_ref, ys_ref, *rest, split):
    if split:
        oc_ref, ol_ref, ys_scr, sem = rest
    else:
        o_ref, ys_scr, sem = rest
    blk = pl.program_id(0)
    slot = blk % 2

    def copy_to(s):
        return lambda l, g, n: pltpu.make_async_copy(ys_ref.at[pl.ds(g, n)], ys_scr.at[s, pl.ds(l, n)], sem.at[s])

    def fetch(b, s):
        ys_scr[s] = jnp.zeros((R_LOC, D), BF16)
        _segment_start(gtab_ref, ntot_ref, b, copy_to(s))

    @pl.when(blk == 0)
    def _():
        fetch(0, 0)

    @pl.when(blk + 1 < NBP)
    def _():
        fetch(blk + 1, 1 - slot)

    _segment_wait(ntot_ref[blk], copy_to(slot))
    pos = pos_ref[...]
    pos_a, pos_b = pos[:, 0:1], pos[:, 1:2]
    acc = jnp.zeros((TP, D), F32)
    for t in range(R_LOC // PT):
        lio = (lax.broadcasted_iota(jnp.int32, (TP, PT), 1) + t * PT).astype(F32)
        onehot = jnp.where((lio == pos_a) | (lio == pos_b), 1.0, 0.0).astype(BF16)
        acc = acc + _dot(onehot, ys_scr[slot, t * PT:(t + 1) * PT, :])
    res = _layernorm(DN_ALPHA * x1_ref[...] + gate_ref[0] * acc, lg_ref[...], lb_ref[...])
    if split:
        @pl.when(blk < N_CTX // TP)
        def _():
            oc_ref[...] = res

        @pl.when(blk >= N_CTX // TP)
        def _():
            ol_ref[...] = res
    else:
        o_ref[...] = res


def _moe_unpermute(ys, pos, x1, mods, ln_g, ln_b, tables, split):
    gtab, ntot = tables[:2]
    nc = N_CTX // TP
    if split:
        out_shape = [jax.ShapeDtypeStruct((N_CTX, D), F32), jax.ShapeDtypeStruct((N_LAT, D), F32)]
        out_specs = [pl.BlockSpec((TP, D), lambda i, *_: (jnp.minimum(i, nc - 1), 0)),
                     pl.BlockSpec((TP, D), lambda i, *_: (jnp.maximum(i - nc, 0), 0))]
    else:
        out_shape = jax.ShapeDtypeStruct((N_TOK, D), F32)
        out_specs = pl.BlockSpec((TP, D), lambda i, *_: (i, 0))
    grid_spec = pltpu.PrefetchScalarGridSpec(
        num_scalar_prefetch=2,
        grid=(NBP,),
        in_specs=[pl.BlockSpec((TP, ROUTE_W), lambda i, *_: (i, 0)),
                  pl.BlockSpec((TP, D), lambda i, *_: (i, 0)),
                  _mod_spec(5, _mod_row_tp),
                  pl.BlockSpec((1, D), lambda i, *_: (0, 0)),
                  pl.BlockSpec((1, D), lambda i, *_: (0, 0)),
                  pl.BlockSpec(memory_space=pl.ANY)],
        out_specs=out_specs,
        scratch_shapes=[pltpu.VMEM((2, R_LOC, D), BF16), pltpu.SemaphoreType.DMA((2,))],
    )
    return pl.pallas_call(
        functools.partial(_unperm_kernel, split=split),
        out_shape=out_shape,
        grid_spec=grid_spec,
        compiler_params=_cp(("arbitrary",)),
        name="moe_unpermute",
    )(gtab, ntot, pos, x1, mods, ln_g.reshape(1, D), ln_b.reshape(1, D), ys)


def _segment_tables(cnt):
    c = cnt[:, 0, :NE].astype(jnp.int32)
    pc = (c + SEG - 1) // SEG * SEG
    loff = jnp.cumsum(pc, axis=1) - pc
    tot = pc.sum(axis=0)
    totp = (tot + TMS - 1) // TMS * TMS
    eend = jnp.cumsum(totp)
    goff = (eend - totp)[None, :] + jnp.cumsum(pc, axis=0) - pc
    n_tiles = (eend[-1] // TMS).reshape(1)
    tile_id = jnp.arange(NT, dtype=jnp.int32)
    tile_expert = jnp.minimum(jnp.sum(tile_id[:, None] >= (eend // TMS)[None, :], axis=1), NE - 1).astype(jnp.int32)
    loffv = jnp.zeros((NBP, 1, ROUTE_W), F32).at[:, 0, :NE].set(loff.astype(F32))
    ntot = (pc.sum(axis=1) // SEG).astype(jnp.int32)
    chunk = jnp.arange(MAXC, dtype=jnp.int32)
    first = loff // SEG
    e_of = jnp.minimum(jnp.sum(chunk[None, :, None] >= (first + pc // SEG)[:, None, :], axis=2), NE - 1)
    is_e = e_of[:, :, None] == jnp.arange(NE, dtype=jnp.int32)[None, None, :]
    pick = lambda a: jnp.sum(jnp.where(is_e, a[:, None, :], 0), axis=2)
    gtab = (pick(goff) + (chunk[None, :] - pick(first)) * SEG).reshape(-1).astype(jnp.int32)
    zrow = jnp.concatenate([eend - totp + tot, eend[-1:]]).astype(jnp.int32)
    zcnt = jnp.concatenate([(totp - tot) // SEG, (R_TOT - eend[-1:]) // SEG]).astype(jnp.int32)
    ids = jnp.arange(NE, dtype=jnp.int32)
    used = totp > 0
    later = (ids[None, :] > ids[:, None]) & used[None, :]
    e_next = jnp.min(jnp.where(later, ids[None, :], NE), axis=1)
    e_next = jnp.where(e_next == NE, -1, e_next).astype(jnp.int32)
    e_slot = ((jnp.cumsum(used.astype(jnp.int32)) - 1) % 2).astype(jnp.int32)
    return (gtab, ntot, zrow, zcnt, loffv), (tile_expert, n_tiles.astype(jnp.int32), e_next, e_slot)


def _moe(h2, rinfo, cnt, x1, mods, ln_g, ln_b, layer, w_gate, w_up, w_down, split):
    tables, schedule = _segment_tables(cnt)
    pos, xs = _moe_permute(h2, rinfo, tables)
    ys = _moe_ffn(xs, schedule, layer, w_gate, w_up, w_down)
    return _moe_unpermute(ys, pos, x1, mods, ln_g, ln_b, tables, split)


def kernel(x_prompt, x_sample, state_gla_S, state_mlstm_C, state_mlstm_n, state_mlstm_m, c, c_ctx,
           gla_w_in, gla_w_gate, gla_b_gate, gla_norm_g, gla_w_out,
           mlstm_w_in, mlstm_b_gates, mlstm_norm_g, mlstm_w_out,
           adaln_w, adaln_b, ln_g, ln_b,
           moe_w_group, moe_b_group, moe_w_expert, moe_b_expert, moe_w_gate, moe_w_up, moe_w_down):
    cvecs = jnp.zeros((MOD_ROWS, D), F32).at[0].set(c_ctx).at[1:1 + N_LAT_SEQ].set(c)
    mods_all = _modulation(cvecs, adaln_w, adaln_b)
    x = _embed(x_prompt, x_sample, _grid_posemb(T_LAT))
    gla_states, ml_c, ml_n, ml_m = [], [], [], []
    for l in range(DEPTH):
        s = l // 2
        mods = mods_all[l].reshape(MOD_ROWS * 6, 1, D)
        if l % 2 == 0:
            q, k, v, r, g = _gla_in(x, mods, gla_w_in[s], gla_w_gate[s], gla_b_gate[s])
            o_ctx, s_new = _gla_scan(q, k, v, g, None, s)
            (o_lat,) = _gla_scan(q, k, v, g, state_gla_S, s)
            gla_states.append(s_new)
            norm_g, w_out = gla_norm_g[s], gla_w_out[s]
        else:
            q, k, v, r, gc, gr = _mlstm_in(x, mods, mlstm_w_in[s], mlstm_b_gates[s])
            o_ctx, c_new, n_new, m_new = _mlstm_scan(q, k, v, gc, gr, None, s)
            (o_lat,) = _mlstm_scan(q, k, v, gc, gr, (state_mlstm_C, state_mlstm_n, state_mlstm_m), s)
            ml_c.append(c_new)
            ml_n.append(n_new[:, :, :, 0, :])
            ml_m.append(m_new[:, :, :, 0, 0])
            norm_g, w_out = mlstm_norm_g[s], mlstm_w_out[s]
        w_route = jnp.zeros((D, ROUTE_W), F32).at[:, :NE].set(moe_w_expert[l]).at[:, NE:NE + N_GROUPS].set(
            moe_w_group[l])
        b_route = jnp.zeros((1, ROUTE_W), F32).at[0, :NE].set(moe_b_expert[l]).at[0, NE:NE + N_GROUPS].set(
            moe_b_group[l])
        x1, h2, rinfo, cnt = _out_proj(o_ctx, o_lat, r, x, mods, norm_g, w_out, ln_g[l, 0], ln_b[l, 0], w_route, b_route)
        x = _moe(h2, rinfo, cnt, x1, mods, ln_g[l, 1], ln_b[l, 1], l, moe_w_gate, moe_w_up, moe_w_down,
                 split=(l == DEPTH - 1))
    y_ctx = x[0].reshape(N_CTX_SEQ, T_CTX, D)
    y_lat = x[1].reshape(N_LAT_SEQ, T_LAT, D)
    return (y_ctx, y_lat, jnp.stack(gla_states, 1), jnp.stack(ml_c, 1), jnp.stack(ml_n, 1), jnp.stack(ml_m, 1))
```

```python
import functools
import math

import jax
import jax.numpy as jnp
from jax import lax
from jax.experimental import pallas as pl
from jax.experimental.pallas import tpu as pltpu

F32 = jnp.float32
BF16 = jnp.bfloat16

D = 1024
N_CTX_SEQ, T_CTX = 16, 256
N_LAT_SEQ, T_LAT = 8, 1024
N_CTX = N_CTX_SEQ * T_CTX
N_LAT = N_LAT_SEQ * T_LAT
N_TOK = N_CTX + N_LAT
DEPTH = 4
GRID_W = 64
NH = 4
DK = 128
DV = 256
DKT = NH * DK
DVT = NH * DV
GATE_RANK = 16
GATE_TAU = 16.0
CH = 64
N_GROUPS = 4
EPG = 8
NE = N_GROUPS * EPG
DE = 256
DN_ALPHA = (2.0 * DEPTH) ** 0.25
LN_EPS = 1e-5
MOD_ROWS = 16
TB = 1024
HPB = 4
GATE_F = 2 * NH
SB = 256
NCB = SB // CH
CH_SHIFT = 6
NB = N_TOK // TB
ROUTE_W = 128
NEG_INF = float("-inf")
VMEM_LIMIT = 52 * 1024 * 1024


def _cp(sem):
    return pltpu.CompilerParams(dimension_semantics=sem, vmem_limit_bytes=VMEM_LIMIT)


def _dot(a, b):
    return jnp.dot(a, b, preferred_element_type=F32)


def _dot_nt(a, b):
    return lax.dot_general(a, b, (((1,), (1,)), ((), ())), preferred_element_type=F32)


def _dot_f32(a, b):
    return jnp.dot(a, b, preferred_element_type=F32, precision=lax.Precision.HIGHEST)


def _split2(x):
    hi = x.astype(BF16)
    lo = (x - hi.astype(F32)).astype(BF16)
    return hi, lo


def _log_sigmoid(x):
    return -(jnp.maximum(-x, 0.0) + jnp.log1p(jnp.exp(-jnp.abs(x))))


def _silu(x):
    return x * jax.nn.sigmoid(x)


def _layernorm(x, g, b):
    mu = jnp.mean(x, axis=-1, keepdims=True)
    xc = x - mu
    var = jnp.mean(xc * xc, axis=-1, keepdims=True)
    return xc * lax.rsqrt(var + LN_EPS) * g + b


def _mod_row_tb(i):
    return jnp.where(i < N_CTX // TB, 0, 1 + (i - N_CTX // TB) // (T_LAT // TB))


def _mod_spec(j, row_fn):
    return pl.BlockSpec((1, 1, D), lambda i, *_: (row_fn(i) * 6 + j, 0, 0))


def _mod_kernel(c_ref, w_ref, b_ref, o_ref):
    o_ref[0] = _dot_f32(_silu(c_ref[...]), w_ref[0]) + b_ref[0]


def _modulation(cvecs, adaln_w, adaln_b):
    nj = 6
    return pl.pallas_call(
        _mod_kernel,
        out_shape=jax.ShapeDtypeStruct((DEPTH, MOD_ROWS, 6 * D), F32),
        grid=(DEPTH, nj),
        in_specs=[
            pl.BlockSpec((MOD_ROWS, D), lambda l, j: (0, 0)),
            pl.BlockSpec((1, D, D), lambda l, j: (l, 0, j)),
            pl.BlockSpec((1, 1, D), lambda l, j: (l, 0, j)),
        ],
        out_specs=pl.BlockSpec((1, MOD_ROWS, D), lambda l, j: (l, 0, j)),
        compiler_params=_cp(("parallel", "parallel")),
        name="adaln_modulation",
    )(cvecs, adaln_w, adaln_b.reshape(DEPTH, 1, 6 * D))


def _token_specs(x):
    if not isinstance(x, tuple):
        return [pl.BlockSpec((TB, D), lambda i: (i, 0))], [x]
    nc = N_CTX // TB
    assert TB == T_LAT
    specs = [pl.BlockSpec((TB, D), lambda i: (jnp.minimum(i, nc - 1), 0)),
             pl.BlockSpec((TB, D), lambda i: (jnp.maximum(i - nc, 0), 0)),
             pl.BlockSpec((TB, D), lambda i: (0, 0), pipeline_mode=pl.Buffered(1))]
    return specs, list(x)


def _token_block(x_refs):
    if len(x_refs) == 1:
        return x_refs[0][...]
    xp_ref, xs_ref, pos_ref = x_refs
    return jnp.where(pl.program_id(0) < N_CTX // TB, xp_ref[...], xs_ref[...] + pos_ref[...])


def _grid_posemb(T):
    rows = T // GRID_W
    r, cidx = jnp.meshgrid(jnp.arange(rows), jnp.arange(GRID_W), indexing="ij")
    nf = D // 4
    freqs = jnp.exp(-math.log(10000.0) * jnp.arange(nf, dtype=F32) / nf)

    def emb(p):
        a = p.reshape(-1).astype(F32)[:, None] * freqs
        return jnp.concatenate([jnp.sin(a), jnp.cos(a)], -1)

    return jnp.concatenate([emb(r), emb(cidx)], -1)


def _gla_in_kernel(*refs, n_x):
    (sh_ref, sc_ref, wq_ref, wk_ref, wv_ref, wr_ref, wg_ref, wgate_ref, bgate_ref,
     q_ref, k_ref, v_ref, r_ref, g_ref) = refs[n_x:]
    h = (_token_block(refs[:n_x]) * (1.0 + sc_ref[0]) + sh_ref[0]).astype(BF16)
    q_ref[...] = _dot(h, wq_ref[...]) * (DK ** -0.5)
    k_ref[...] = _dot(h, wk_ref[...])
    v_ref[...] = _dot(h, wv_ref[...]).astype(BF16)
    r_ref[...] = _silu(_dot(h, wr_ref[...])).astype(BF16)
    glr = _dot(h, wg_ref[...]).astype(BF16)
    z = _dot(glr, wgate_ref[...]) + bgate_ref[...]
    g_ref[...] = _log_sigmoid(z) * (1.0 / GATE_TAU)


def _gla_in(x, mods, w_in, w_gate, b_gate):
    wq = w_in[:, :DKT].astype(BF16)
    wk = w_in[:, DKT:2 * DKT].astype(BF16)
    wv = w_in[:, 2 * DKT:2 * DKT + DVT].astype(BF16)
    wr = w_in[:, 2 * DKT + DVT:2 * DKT + 2 * DVT].astype(BF16)
    wg = w_in[:, 2 * DKT + 2 * DVT:].astype(BF16)
    wgate = jnp.zeros((2 * GATE_RANK, 2 * DKT), F32)
    wgate = wgate.at[:GATE_RANK, :DKT].set(w_gate[0]).at[GATE_RANK:, DKT:].set(w_gate[1]).astype(BF16)
    bgate = b_gate.reshape(1, 2 * DKT)
    full = lambda s: pl.BlockSpec(s, lambda i: (0,) * len(s), pipeline_mode=pl.Buffered(1))
    tok = lambda n: pl.BlockSpec((TB, n), lambda i: (i, 0))
    x_specs, x_args = _token_specs(x)
    return pl.pallas_call(
        functools.partial(_gla_in_kernel, n_x=len(x_args)),
        out_shape=[jax.ShapeDtypeStruct((N_TOK, DKT), F32), jax.ShapeDtypeStruct((N_TOK, DKT), F32),
                   jax.ShapeDtypeStruct((N_TOK, DVT), BF16), jax.ShapeDtypeStruct((N_TOK, DVT), BF16),
                   jax.ShapeDtypeStruct((N_TOK, 2 * DKT), F32)],
        grid=(NB,),
        in_specs=x_specs + [_mod_spec(0, _mod_row_tb), _mod_spec(1, _mod_row_tb),
                            full((D, DKT)), full((D, DKT)), full((D, DVT)), full((D, DVT)),
                            full((D, 2 * GATE_RANK)), full((2 * GATE_RANK, 2 * DKT)), full((1, 2 * DKT))],
        out_specs=[tok(DKT), tok(DKT), tok(DVT), tok(DVT), tok(2 * DKT)],
        compiler_params=_cp(("parallel",)),
        name="gla_in_proj",
    )(*x_args, mods, mods, wq, wk, wv, wr, wg, wgate, bgate)


def _mlstm_in_kernel(x_ref, sh_ref, sc_ref, wq_ref, wk_ref, wv_ref, wr_ref, wg_ref, wgt_ref, bg_ref, bgt_ref,
                     q_ref, k_ref, v_ref, r_ref, gc_ref, gr_ref):
    h = (x_ref[...] * (1.0 + sc_ref[0]) + sh_ref[0]).astype(BF16)
    q_ref[...] = _dot(h, wq_ref[...]) * (DK ** -0.5)
    k_ref[...] = _dot(h, wk_ref[...])
    v_ref[...] = _dot(h, wv_ref[...]).astype(BF16)
    r_ref[...] = jax.nn.sigmoid(_dot(h, wr_ref[...])).astype(BF16)
    gc = _dot(h, wg_ref[...]) + bg_ref[...]
    gr = _dot_nt(wgt_ref[...], h) + bgt_ref[...]
    gc_ref[...] = jnp.where(lax.broadcasted_iota(jnp.int32, gc.shape, 1) >= GATE_F, _log_sigmoid(gc), gc)
    gr = jnp.where(lax.broadcasted_iota(jnp.int32, gr.shape, 0) >= GATE_F, _log_sigmoid(gr), gr)
    for c in range(TB // SB):
        gr_ref[c] = gr[:, c * SB:(c + 1) * SB]


def _mlstm_in(x, mods, w_in, b_gates):
    wq = w_in[:, :DKT].astype(BF16)
    wk = w_in[:, DKT:2 * DKT].astype(BF16)
    wv = w_in[:, 2 * DKT:2 * DKT + DVT].astype(BF16)
    wr = w_in[:, 2 * DKT + DVT:2 * DKT + 2 * DVT].astype(BF16)
    wg = w_in[:, 2 * DKT + 2 * DVT:].reshape(D, 2, 2, NH).transpose(0, 2, 3, 1).reshape(D, 4 * NH).astype(BF16)
    bg = b_gates.reshape(2, 2, NH).transpose(1, 2, 0).reshape(1, 4 * NH)
    full = lambda s: pl.BlockSpec(s, lambda i: (0,) * len(s), pipeline_mode=pl.Buffered(1))
    tok = lambda n: pl.BlockSpec((TB, n), lambda i: (i, 0))
    return pl.pallas_call(
        _mlstm_in_kernel,
        out_shape=[jax.ShapeDtypeStruct((N_TOK, DKT), F32), jax.ShapeDtypeStruct((N_TOK, DKT), F32),
                   jax.ShapeDtypeStruct((N_TOK, DVT), BF16), jax.ShapeDtypeStruct((N_TOK, DVT), BF16),
                   jax.ShapeDtypeStruct((N_TOK, 4 * NH), F32),
                   jax.ShapeDtypeStruct((N_TOK // SB, 4 * NH, SB), F32)],
        grid=(NB,),
        in_specs=[tok(D), _mod_spec(0, _mod_row_tb), _mod_spec(1, _mod_row_tb),
                  full((D, DKT)), full((D, DKT)), full((D, DVT)), full((D, DVT)), full((D, 4 * NH)),
                  full((4 * NH, D)), full((1, 4 * NH)), full((4 * NH, 1))],
        out_specs=[tok(DKT), tok(DKT), tok(DVT), tok(DVT),
                   tok(4 * NH),
                   pl.BlockSpec((TB // SB, 4 * NH, SB), lambda i: (i, 0, 0))],
        compiler_params=_cp(("parallel",)),
        name="mlstm_in_proj",
    )(x, mods, mods, wq, wk, wv, wr, wg, wg.T, bg, bg.T)


def _block_tri_masks():
    row = lax.broadcasted_iota(jnp.int32, (SB, SB), 0)
    col = lax.broadcasted_iota(jnp.int32, (SB, SB), 1)
    same = (row >> CH_SHIFT) == (col >> CH_SHIFT)
    return same & (row >= col), same & (row <= col)


def _chunk_scan(x, forward, axis=0, op=jnp.add, fill=0.0):
    n = x.shape[axis]
    pos = lax.broadcasted_iota(jnp.int32, x.shape, axis) & (CH - 1)
    s = 1
    while s < CH:
        if forward:
            x = op(x, jnp.where(pos >= s, pltpu.roll(x, s, axis), fill))
        else:
            x = op(x, jnp.where(pos < CH - s, pltpu.roll(x, n - s, axis), fill))
        s *= 2
    return x


def _chunk_cumsum(x, forward):
    return _chunk_scan(x, forward)


def _cat2(x):
    return jnp.concatenate([x, x], axis=1)


def _per_chunk(vals, n):
    return jnp.concatenate([jnp.broadcast_to(v, (CH, n)) for v in vals], axis=0)


def _chunk_masked_cat(x):
    xb = x.astype(BF16)
    n = x.shape[1]
    cols = []
    for c in range(NCB):
        parts = [jnp.zeros((c * CH, n), BF16), xb[c * CH:(c + 1) * CH, :], jnp.zeros((SB - (c + 1) * CH, n), BF16)]
        cols.append(jnp.concatenate([p for p in parts if p.shape[0]], axis=0))
    return jnp.concatenate(cols, axis=1)


def _head_norm_rows(o):
    mu = jnp.mean(o, axis=-1, keepdims=True)
    oc = o - mu
    var = jnp.mean(oc * oc, axis=-1, keepdims=True)
    return oc * lax.rsqrt(var + LN_EPS)


def _gla_superblock(q, k, v, v_t, g, st, forward, mask):
    b = _chunk_cumsum(g, forward)
    r_last = CH - 1 if forward else 0
    r_mid = CH // 2 if forward else CH - 1 - CH // 2
    lasts = [b[c * CH + r_last:c * CH + r_last + 1, :] for c in range(NCB)]
    b_last = _per_chunk(lasts, DK)
    b_mid = _per_chunk([b[c * CH + r_mid:c * CH + r_mid + 1, :] for c in range(NCB)], DK)
    qe = (q * jnp.exp(b - b_mid)).astype(BF16)
    ke = (k * jnp.exp(b_mid - b)).astype(BF16)
    qb = (q * jnp.exp(b)).astype(BF16)
    kd = k * jnp.exp(b_last - b)
    a = jnp.where(mask, _dot_nt(qe, ke), 0.0).astype(BF16)
    o_intra = _dot(a, v)
    d_t = _dot(v_t, _chunk_masked_cat(kd))
    inter = [None] * NCB
    for c in (range(NCB) if forward else reversed(range(NCB))):
        inter[c] = _dot_nt(qb[c * CH:(c + 1) * CH, :], st.astype(BF16))
        st = st * jnp.exp(lasts[c]) + d_t[:, c * DK:(c + 1) * DK]
    return o_intra + jnp.concatenate(inter, axis=0), st


def _scan_driver(T, o_ref, of_ref, ob_ref, step):
    nsb = T // SB
    if nsb == 1:
        step(0, 0)
    else:
        def body(i, carry):
            step(i, nsb - 1 - i)
            return carry

        lax.fori_loop(0, nsb, body, 0)

    def norm_body(j, carry):
        rows = pl.ds(pl.multiple_of(j * SB, SB), SB)
        for hh in range(HPB):
            cv = slice(hh * DV, (hh + 1) * DV)
            o_ref[rows, cv] = _head_norm_rows(of_ref[rows, cv] + ob_ref[rows, cv]).astype(BF16)
        return carry

    if nsb == 1:
        norm_body(0, 0)
    else:
        lax.fori_loop(0, nsb, norm_body, 0)


def _gla_scan_kernel(*refs, T, has_state, emit_state):
    q_ref, k_ref, v_ref, gf_ref, gb_ref = refs[:5]
    pos = 5
    if has_state:
        s0f_ref, s0b_ref = refs[pos:pos + 2]
        pos += 2
    o_ref = refs[pos]
    pos += 1
    if emit_state:
        sout_ref = refs[pos]
        pos += 1
    stf_ref, stb_ref, vt_ref, of_ref, ob_ref = refs[pos:pos + 5]

    nsb = T // SB
    for hh in range(HPB):
        for j in range(nsb):
            vt_ref[hh, j] = v_ref[j * SB:(j + 1) * SB, hh * DV:(hh + 1) * DV].astype(F32).T.astype(BF16)
        if has_state:
            stf_ref[hh] = s0f_ref[hh].T
            stb_ref[hh] = s0b_ref[hh].T
        else:
            stf_ref[hh] = jnp.zeros((DV, DK), F32)
            stb_ref[hh] = jnp.zeros((DV, DK), F32)

    lower, upper = _block_tri_masks()

    def one(j, forward, hh):
        rows = pl.ds(pl.multiple_of(j * SB, SB), SB)
        ck = slice(hh * DK, (hh + 1) * DK)
        cv = slice(hh * DV, (hh + 1) * DV)
        st_ref = stf_ref if forward else stb_ref
        o, st = _gla_superblock(q_ref[rows, ck], k_ref[rows, ck], v_ref[rows, cv].astype(BF16), vt_ref[hh, j],
                                (gf_ref if forward else gb_ref)[rows, ck], st_ref[hh], forward,
                                lower if forward else upper)
        st_ref[hh] = st
        (of_ref if forward else ob_ref)[rows, cv] = o

    def step(jf, jb):
        for hh in range(HPB):
            one(jf, True, hh)
            one(jb, False, hh)

    _scan_driver(T, o_ref, of_ref, ob_ref, step)
    if emit_state:
        for hh in range(HPB):
            sout_ref[0, hh] = stf_ref[hh].T
            sout_ref[1, hh] = stb_ref[hh].T


def _gla_scan(q, k, v, g, state, slot):
    ctx = state is None
    T = T_CTX if ctx else T_LAT
    n_seq = N_CTX_SEQ if ctx else N_LAT_SEQ
    off = 0 if ctx else N_CTX // T_LAT
    in_specs = [
        pl.BlockSpec((T, HPB * DK), lambda b, h: (off + b, h)),
        pl.BlockSpec((T, HPB * DK), lambda b, h: (off + b, h)),
        pl.BlockSpec((T, HPB * DV), lambda b, h: (off + b, h)),
        pl.BlockSpec((T, HPB * DK), lambda b, h: (off + b, h)),
        pl.BlockSpec((T, HPB * DK), lambda b, h: (off + b, NH // HPB + h)),
    ]
    args = [q, k, v, g, g]
    out_shape = [jax.ShapeDtypeStruct((n_seq * T, DVT), BF16)]
    out_specs = [pl.BlockSpec((T, HPB * DV), lambda b, h: (b, h))]
    if ctx:
        out_shape.append(jax.ShapeDtypeStruct((N_CTX_SEQ, 2, NH, DK, DV), F32))
        out_specs.append(pl.BlockSpec((None, 2, HPB, DK, DV), lambda b, h: (b, 0, h, 0, 0)))
    else:
        sq = (None, None, None, HPB, DK, DV)
        in_specs += [pl.BlockSpec(sq, lambda b, h: (b, slot, 0, h, 0, 0)),
                     pl.BlockSpec(sq, lambda b, h: (b, slot, 1, h, 0, 0))]
        args += [state, state]
    res = pl.pallas_call(
        functools.partial(_gla_scan_kernel, T=T, has_state=not ctx, emit_state=ctx),
        out_shape=out_shape,
        grid=(n_seq, NH // HPB),
        in_specs=in_specs,
        out_specs=out_specs,
        scratch_shapes=[pltpu.VMEM((HPB, DV, DK), F32), pltpu.VMEM((HPB, DV, DK), F32),
                        pltpu.VMEM((HPB, T // SB, DV, SB), BF16), pltpu.VMEM((T, HPB * DV), F32), pltpu.VMEM((T, HPB * DV), F32)],
        compiler_params=_cp(("parallel", "parallel")),
        name="gla_scan_ctx" if ctx else "gla_scan_lat",
    )(*args)
    return res


def _mlstm_scan_kernel(*refs, T, has_state, emit_state):
    q_ref, k_ref, v_ref, gc_ref, gr_ref = refs[:5]
    pos = 5
    if has_state:
        c0f_ref, c0b_ref, n0f_ref, n0b_ref, m0f_ref, m0b_ref = refs[pos:pos + 6]
        pos += 6
    o_ref = refs[pos]
    pos += 1
    if emit_state:
        cout_ref, nout_ref, mout_ref = refs[pos:pos + 3]
        pos += 3
    ct_ref, n_ref, m_ref, vt_ref, of_ref, ob_ref = refs[pos:pos + 6]

    nsb = T // SB
    for hh in range(HPB):
        for j in range(nsb):
            vt_ref[hh, j] = v_ref[j * SB:(j + 1) * SB, hh * DV:(hh + 1) * DV].astype(F32).T.astype(BF16)
        for d in range(2):
            sl = d * HPB + hh
            if has_state:
                c0, n0, m0 = ((c0f_ref, n0f_ref, m0f_ref), (c0b_ref, n0b_ref, m0b_ref))[d]
                ct_ref[sl] = c0[hh].T
                n_ref[sl] = n0[hh]
                m_ref[sl] = m0[hh]
            else:
                ct_ref[sl] = jnp.zeros((DV, DK), F32)
                n_ref[sl] = jnp.zeros((1, DK), F32)
                m_ref[sl] = jnp.zeros((1, DK), F32)

    lower, upper = _block_tri_masks()

    def gates(j, forward):
        rows = pl.ds(pl.multiple_of(j * SB, SB), SB)
        gc = gc_ref[rows, :]
        gr = gr_ref[j]
        i_col, f_col = gc[:, :GATE_F], gc[:, GATE_F:]
        b_col = _chunk_scan(f_col, forward)
        m_col = b_col + _chunk_scan(i_col - b_col, forward, op=jnp.maximum, fill=NEG_INF)
        b_row = _chunk_scan(gr[GATE_F:, :], forward, axis=1) - gr[:GATE_F, :]
        return i_col, b_col, m_col, b_row

    def one(j, forward, hh, shared):
        d = 0 if forward else 1
        sl = d * HPB + hh
        col = 2 * hh + d
        rows = pl.ds(pl.multiple_of(j * SB, SB), SB)
        q = q_ref[rows, hh * DK:(hh + 1) * DK]
        k = k_ref[rows, hh * DK:(hh + 1) * DK]
        v = v_ref[rows, hh * DV:(hh + 1) * DV].astype(BF16)
        mask = lower if forward else upper
        order = list(range(NCB)) if forward else list(reversed(range(NCB)))
        r_last = CH - 1 if forward else 0
        i_c, b_c, m_intra = [jnp.broadcast_to(a[:, col:col + 1], (SB, DK)) for a in shared[:3]]
        brow = shared[3][col:col + 1, :]
        b_last = [b_c[c * CH + r_last:c * CH + r_last + 1, :] for c in range(NCB)]
        a_c = _per_chunk(b_last, DK) - b_c + i_c
        m_loc = [jnp.max(a_c[c * CH:(c + 1) * CH, :], axis=0, keepdims=True) for c in range(NCB)]
        kw = k * jnp.exp(a_c - _per_chunk(m_loc, DK))
        kv_t = _dot(vt_ref[hh, j], _chunk_masked_cat(kw))
        k_loc = [jnp.sum(kw[c * CH:(c + 1) * CH, :], axis=0, keepdims=True) for c in range(NCB)]

        m = m_ref[sl]
        m_start, s_old, s_loc = [None] * NCB, [None] * NCB, [None] * NCB
        for c in order:
            m_start[c] = m
            m_new = jnp.maximum(b_last[c] + m, m_loc[c])
            s_old[c] = jnp.exp(b_last[c] + m - m_new)
            s_loc[c] = jnp.exp(m_loc[c] - m_new)
            m = m_new
        m_ref[sl] = m

        l_inter = b_c + _per_chunk(m_start, DK)
        m_i = jnp.maximum(l_inter, m_intra)
        d_log = jnp.where(mask, _cat2(b_c - m_i) - brow, NEG_INF)
        qb = q.astype(BF16)
        s = (_dot_nt(qb, k.astype(BF16)) * jnp.exp(d_log)).astype(BF16)
        e_inter = jnp.exp(l_inter - m_i)
        nd = _dot(s, jnp.concatenate([v, jnp.ones((SB, DK), BF16)], axis=1))
        num, den = nd[:, :DV], nd[:, DV:]

        ct = ct_ref[sl]
        nrm = n_ref[sl]
        inter = [None] * NCB
        for c in order:
            state = jnp.concatenate([ct, jnp.broadcast_to(nrm, (DK, DK))], axis=0).astype(BF16)
            inter[c] = _dot_nt(qb[c * CH:(c + 1) * CH, :], state)
            ct = s_old[c] * ct + s_loc[c] * kv_t[:, c * DK:(c + 1) * DK]
            nrm = s_old[c] * nrm + s_loc[c] * k_loc[c]
        ct_ref[sl] = ct
        n_ref[sl] = nrm

        inter = jnp.concatenate(inter, axis=0)
        num = num + _cat2(e_inter) * inter[:, :DV]
        den = den + e_inter * inter[:, DV:]
        inv = 1.0 / jnp.maximum(jnp.abs(den), jnp.exp(-m_i))
        (of_ref if forward else ob_ref)[rows, hh * DV:(hh + 1) * DV] = num * _cat2(inv)

    def step(jf, jb):
        shared_f = gates(jf, True)
        shared_b = gates(jb, False)
        for hh in range(HPB):
            one(jf, True, hh, shared_f)
            one(jb, False, hh, shared_b)

    _scan_driver(T, o_ref, of_ref, ob_ref, step)
    if emit_state:
        for d in range(2):
            for hh in range(HPB):
                cout_ref[d, hh] = ct_ref[d * HPB + hh].T
                nout_ref[d, hh] = n_ref[d * HPB + hh]
                mout_ref[d, hh] = m_ref[d * HPB + hh]


def _mlstm_scan(q, k, v, gc, gr, states, slot):
    ctx = states is None
    T = T_CTX if ctx else T_LAT
    n_seq = N_CTX_SEQ if ctx else N_LAT_SEQ
    off = 0 if ctx else N_CTX // T_LAT
    in_specs = [
        pl.BlockSpec((T, HPB * DK), lambda b, h: (off + b, h)),
        pl.BlockSpec((T, HPB * DK), lambda b, h: (off + b, h)),
        pl.BlockSpec((T, HPB * DV), lambda b, h: (off + b, h)),
        pl.BlockSpec((T, 4 * NH), lambda b, h: (off + b, 0)),
        pl.BlockSpec((T // SB, 4 * NH, SB), lambda b, h: (off + b, 0, 0)),
    ]
    assert HPB == NH
    args = [q, k, v, gc, gr]
    out_shape = [jax.ShapeDtypeStruct((n_seq * T, DVT), BF16)]
    out_specs = [pl.BlockSpec((T, HPB * DV), lambda b, h: (b, h))]
    if ctx:
        out_shape += [jax.ShapeDtypeStruct((N_CTX_SEQ, 2, NH, DK, DV), F32),
                      jax.ShapeDtypeStruct((N_CTX_SEQ, 2, NH, 1, DK), F32),
                      jax.ShapeDtypeStruct((N_CTX_SEQ, 2, NH, 1, DK), F32)]
        out_specs += [pl.BlockSpec((None, 2, HPB, DK, DV), lambda b, h: (b, 0, h, 0, 0)),
                      pl.BlockSpec((None, 2, HPB, 1, DK), lambda b, h: (b, 0, h, 0, 0)),
                      pl.BlockSpec((None, 2, HPB, 1, DK), lambda b, h: (b, 0, h, 0, 0))]
    else:
        c0, n0, m0 = states
        n0 = n0.reshape(N_LAT_SEQ, -1, 2, NH, 1, DK)
        m0 = jnp.broadcast_to(m0[..., None, None], m0.shape + (1, DK))
        sq_c = (None, None, None, HPB, DK, DV)
        sq_v = (None, None, None, HPB, 1, DK)
        for arr, sq in ((c0, sq_c), (n0, sq_v), (m0, sq_v)):
            for d in range(2):
                in_specs.append(pl.BlockSpec(sq, functools.partial(lambda b, h, d: (b, slot, d, h, 0, 0), d=d)))
                args.append(arr)
    return pl.pallas_call(
        functools.partial(_mlstm_scan_kernel, T=T, has_state=not ctx, emit_state=ctx),
        out_shape=out_shape,
        grid=(n_seq, NH // HPB),
        in_specs=in_specs,
        out_specs=out_specs,
        scratch_shapes=[pltpu.VMEM((2 * HPB, DV, DK), F32), pltpu.VMEM((2 * HPB, 1, DK), F32),
                        pltpu.VMEM((2 * HPB, 1, DK), F32),
                        pltpu.VMEM((HPB, T // SB, DV, SB), BF16), pltpu.VMEM((T, HPB * DV), F32), pltpu.VMEM((T, HPB * DV), F32)],
        compiler_params=_cp(("parallel", "parallel")),
        name="mlstm_scan_ctx" if ctx else "mlstm_scan_lat",
    )(*args)


def _route(lg):
    lane = lax.broadcasted_iota(jnp.int32, lg.shape, 1)
    big = jnp.int32(ROUTE_W)
    is_g = (lane >= NE) & (lane < NE + N_GROUPS)
    gl = jnp.where(is_g, lg, NEG_INF)
    gmax = jnp.max(gl, axis=1, keepdims=True)
    gsel = jnp.min(jnp.where(gl == gmax, lane, big), axis=1, keepdims=True) - NE
    gw = 1.0 / jnp.sum(jnp.exp(gl - gmax), axis=1, keepdims=True)
    ing = (lane < NE) & ((lane >> 3) == gsel)
    el = jnp.where(ing, lg, NEG_INF)
    emax = jnp.max(el, axis=1, keepdims=True)
    p = jnp.exp(el - emax)
    prob = p / jnp.sum(p, axis=1, keepdims=True)
    p1 = jnp.max(prob, axis=1, keepdims=True)
    i1 = jnp.min(jnp.where(ing & (prob == p1), lane, big), axis=1, keepdims=True)
    rest = ing & (lane != i1)
    prob2 = jnp.where(rest, prob, -1.0)
    p2 = jnp.max(prob2, axis=1, keepdims=True)
    i2 = jnp.min(jnp.where(rest & (prob2 == p2), lane, big), axis=1, keepdims=True)
    tot = p1 + p2
    rinfo = jnp.where(lane == 0, i1.astype(F32),
                      jnp.where(lane == 1, i2.astype(F32),
                                jnp.where(lane == 2, gw * (p1 / tot), jnp.where(lane == 3, gw * (p2 / tot), 0.0))))
    sel = jnp.where((lane == i1) | (lane == i2), 1.0, 0.0)
    return rinfo, [jnp.sum(sel[c * TP:(c + 1) * TP, :], axis=0, keepdims=True) for c in range(lg.shape[0] // TP)]


def _out_kernel(*refs, n_x):
    (oc_ref, ol_ref, r_ref, gate_ref, sh_ref, sc_ref, ng_ref, wo_ref, lg_ref, lb_ref, wr_ref,
     br_ref, x1_ref, h2_ref, rinfo_ref, cnt_ref) = refs[n_x:]
    o = jnp.where(pl.program_id(0) < N_CTX // TB, oc_ref[...], ol_ref[...])
    y = _dot((o * ng_ref[...] * r_ref[...]).astype(BF16), wo_ref[...])
    x1 = _layernorm(DN_ALPHA * _token_block(refs[:n_x]) + gate_ref[0] * y, lg_ref[...], lb_ref[...])
    x1_ref[...] = x1
    h2 = x1 * (1.0 + sc_ref[0]) + sh_ref[0]
    h_hi, h_lo = _split2(h2)
    h2_ref[...] = h_hi
    t = _dot(h_hi, wr_ref[...])
    lg = t[:, :ROUTE_W] + t[:, ROUTE_W:] + _dot(h_lo, wr_ref[:, :ROUTE_W]) + br_ref[...]
    rinfo_ref[...], counts = _route(lg)
    for c, count in enumerate(counts):
        cnt_ref[c] = count


def _out_proj(o_ctx, o_lat, r, x, mods, norm_g, w_out, ln_g, ln_b, w_route, b_route):
    nc = N_CTX // TB
    full = lambda s: pl.BlockSpec(s, lambda i: (0,) * len(s), pipeline_mode=pl.Buffered(1))
    tok = lambda n: pl.BlockSpec((TB, n), lambda i: (i, 0))
    x_specs, x_args = _token_specs(x)
    return pl.pallas_call(
        functools.partial(_out_kernel, n_x=len(x_args)),
        out_shape=[jax.ShapeDtypeStruct((N_TOK, D), F32), jax.ShapeDtypeStruct((N_TOK, D), BF16),
                   jax.ShapeDtypeStruct((N_TOK, ROUTE_W), F32), jax.ShapeDtypeStruct((NBP, 1, ROUTE_W), F32)],
        grid=(NB,),
        in_specs=x_specs + [pl.BlockSpec((TB, DVT), lambda i: (jnp.minimum(i, nc - 1), 0)),
                            pl.BlockSpec((TB, DVT), lambda i: (jnp.maximum(i - nc, 0), 0)),
                            tok(DVT),
                            _mod_spec(2, _mod_row_tb), _mod_spec(3, _mod_row_tb), _mod_spec(4, _mod_row_tb),
                            full((1, DVT)), full((DVT, D)), full((1, D)), full((1, D)),
                            full((D, 2 * ROUTE_W)), full((1, ROUTE_W))],
        out_specs=[tok(D), tok(D), tok(ROUTE_W), pl.BlockSpec((TB // TP, 1, ROUTE_W), lambda i: (i, 0, 0))],
        compiler_params=_cp(("parallel",)),
        name="out_proj_route",
    )(*x_args, o_ctx, o_lat, r, mods, mods, mods, norm_g.reshape(1, DVT), w_out.astype(BF16),
      ln_g.reshape(1, D), ln_b.reshape(1, D), jnp.concatenate(_split2(w_route), axis=1), b_route)


TP = 512
NBP = N_TOK // TP
SEG = 16
R_LOC = 1536
PT = 256
XW = D + ROUTE_W
TMS = 512
R_TOT = -(-(2 * N_TOK + NBP * NE * (SEG - 1) + NE * (TMS - 1)) // TMS) * TMS
NT = R_TOT // TMS
POS_SPLIT = 64.0


def _mod_row_tp(i):
    return jnp.where(i < N_CTX // TP, 0, 1 + (i - N_CTX // TP) // (T_LAT // TP))


def _lane_pack(cols, shape):
    lane = lax.broadcasted_iota(jnp.int32, shape, 1)
    out = jnp.zeros(shape, F32)
    for j, c in enumerate(cols):
        out = jnp.where(lane == j, c, out)
    return out


WAIT_CHUNKS = (32, 4, 1)
MAXC = R_LOC // SEG
ZROWS = WAIT_CHUNKS[0] * SEG
NZ = NE + 1


def _segment_start(gtab_ref, ntot_ref, blk, make_copy):
    def per_chunk(j, carry):
        g = gtab_ref[blk * MAXC + j]
        make_copy(pl.multiple_of(j * SEG, SEG), pl.multiple_of(g, SEG), SEG).start()
        return carry

    lax.fori_loop(0, ntot_ref[blk], per_chunk, 0)


def _segment_wait(n_chunks, make_copy):
    left = n_chunks
    for chunks in WAIT_CHUNKS:
        n_wait = left // chunks

        def wait_piece(k, c, chunks=chunks):
            make_copy(0, 0, chunks * SEG).wait()
            return c

        lax.fori_loop(0, n_wait, wait_piece, 0)
        left = left - n_wait * chunks


def _perm_kernel(gtab_ref, ntot_ref, zrow_ref, zcnt_ref, h_ref, ri_ref, loffv_ref, pos_ref, xs_ref,
                 xs_scr, zero_scr, sem, zsem):
    blk = pl.program_id(0)
    slot = blk % 2

    def zero_copy(_, g, n):
        return pltpu.make_async_copy(zero_scr.at[pl.ds(0, n)], xs_ref.at[pl.ds(g, n)], zsem)

    @pl.when(blk == 0)
    def _():
        zero_scr[...] = jnp.zeros_like(zero_scr)

        def per_range(r, carry):
            n = zcnt_ref[r]
            g0 = zrow_ref[r]
            n_big = n // WAIT_CHUNKS[0]

            def per_piece(k, c, first, chunks):
                zero_copy(0, pl.multiple_of(g0 + (first + k * chunks) * SEG, SEG), chunks * SEG).start()
                return c

            lax.fori_loop(0, n_big, functools.partial(per_piece, first=0, chunks=WAIT_CHUNKS[0]), 0)
            lax.fori_loop(0, n - n_big * WAIT_CHUNKS[0],
                          functools.partial(per_piece, first=n_big * WAIT_CHUNKS[0], chunks=1), 0)
            return carry + n

        lax.fori_loop(0, NZ, per_range, 0)
    ri = ri_ref[...]
    e_a, e_b, w_a, w_b = ri[:, 0:1], ri[:, 1:2], ri[:, 2:3], ri[:, 3:4]
    lanef = lax.broadcasted_iota(jnp.int32, ri.shape, 1).astype(F32)
    is_a = lanef == e_a
    is_b = lanef == e_b
    sel = jnp.where(is_a | is_b, 1.0, 0.0).astype(BF16)
    row = lax.broadcasted_iota(jnp.int32, (TP, TP), 0)
    col = lax.broadcasted_iota(jnp.int32, (TP, TP), 1)
    earlier = jnp.where(row > col, 1.0, 0.0).astype(BF16)
    lpos = loffv_ref[0] + _dot(earlier, sel)
    pos_a = jnp.sum(jnp.where(is_a, lpos, 0.0), axis=1, keepdims=True)
    pos_b = jnp.sum(jnp.where(is_b, lpos, 0.0), axis=1, keepdims=True)
    pos_ref[...] = _lane_pack([pos_a, pos_b], ri.shape)

    hi_a = jnp.floor(pos_a * (1.0 / POS_SPLIT))
    hi_b = jnp.floor(pos_b * (1.0 / POS_SPLIT))
    parts = _lane_pack([hi_a, pos_a - POS_SPLIT * hi_a, hi_b, pos_b - POS_SPLIT * hi_b], ri.shape).astype(BF16)
    pick = jnp.where(lax.broadcasted_iota(jnp.int32, (8, ROUTE_W), 0) == lax.broadcasted_iota(jnp.int32, (8, ROUTE_W), 1),
                     1.0, 0.0).astype(BF16)
    pr = _dot_nt(pick, parts)
    pos_a_r = POS_SPLIT * pr[0:1, :] + pr[1:2, :]
    pos_b_r = POS_SPLIT * pr[2:3, :] + pr[3:4, :]

    wa1, wa2 = _split2(w_a)
    wa3 = (w_a - wa1.astype(F32) - wa2.astype(F32))
    wb1, wb2 = _split2(w_b)
    wb3 = (w_b - wb1.astype(F32) - wb2.astype(F32))
    wl = _lane_pack([wa1.astype(F32), wa2.astype(F32), wa3, wb1.astype(F32), wb2.astype(F32), wb3, e_a],
                    ri.shape).astype(BF16)
    hcat = jnp.concatenate([h_ref[...], wl], axis=1)
    def tile(t):
        rio = (lax.broadcasted_iota(jnp.int32, (PT, TP), 0) + t * PT).astype(F32)
        onehot = jnp.where((rio == pos_a_r) | (rio == pos_b_r), 1.0, 0.0).astype(BF16)
        xs_scr[slot, t * PT:(t + 1) * PT, :] = _dot(onehot, hcat).astype(BF16)

    last = R_LOC // PT - 1
    for t in range(last):
        tile(t)
    pl.when(ntot_ref[blk] * SEG > last * PT)(functools.partial(tile, last))

    def copy_from(s):
        return lambda l, g, n: pltpu.make_async_copy(xs_scr.at[s, pl.ds(l, n)], xs_ref.at[pl.ds(g, n)], sem.at[s])

    _segment_start(gtab_ref, ntot_ref, blk, copy_from(slot))

    @pl.when(blk > 0)
    def _():
        _segment_wait(ntot_ref[blk - 1], copy_from(1 - slot))

    @pl.when(blk == NBP - 1)
    def _():
        _segment_wait(ntot_ref[blk], copy_from(slot))
        _segment_wait(lax.fori_loop(0, NZ, lambda r, c: c + zcnt_ref[r], 0), zero_copy)


def _moe_permute(h2, rinfo, tables):
    gtab, ntot, zrow, zcnt, loffv = tables
    grid_spec = pltpu.PrefetchScalarGridSpec(
        num_scalar_prefetch=4,
        grid=(NBP,),
        in_specs=[pl.BlockSpec((TP, D), lambda i, *_: (i, 0)),
                  pl.BlockSpec((TP, ROUTE_W), lambda i, *_: (i, 0)),
                  pl.BlockSpec((1, 1, ROUTE_W), lambda i, *_: (i, 0, 0))],
        out_specs=[pl.BlockSpec((TP, ROUTE_W), lambda i, *_: (i, 0)),
                   pl.BlockSpec(memory_space=pl.ANY)],
        scratch_shapes=[pltpu.VMEM((2, R_LOC, XW), BF16), pltpu.VMEM((ZROWS, XW), BF16),
                        pltpu.SemaphoreType.DMA((2,)), pltpu.SemaphoreType.DMA],
    )
    return pl.pallas_call(
        _perm_kernel,
        out_shape=[jax.ShapeDtypeStruct((N_TOK, ROUTE_W), F32), jax.ShapeDtypeStruct((R_TOT, XW), BF16)],
        grid_spec=grid_spec,
        compiler_params=_cp(("arbitrary",)),
        name="moe_permute",
    )(gtab, ntot, zrow, zcnt, h2, rinfo, loffv)


def _ffn_kernel(te_ref, nt_ref, enext_ref, eslot_ref, xs_ref, wg_hbm, wu_hbm, wd_hbm, ys_ref,
                wg32_ref, wu32_ref, wd32_ref, wg_ref, wu_ref, wd_ref, sem, *, layer):
    i = pl.program_id(0)
    active = i < nt_ref[0]
    e_i = te_ref[i]
    new_expert = (i == 0) | (e_i != te_ref[jnp.maximum(i - 1, 0)])

    def fetch(e, s):
        return [pltpu.make_async_copy(src.at[layer, e], dst.at[s], sem.at[s])
                for src, dst in ((wg_hbm, wg32_ref), (wu_hbm, wu32_ref), (wd_hbm, wd32_ref))]

    @pl.when(active & new_expert)
    def _():
        s = eslot_ref[e_i]
        nxt = enext_ref[e_i]

        @pl.when(i == 0)
        def _():
            for c in fetch(e_i, s):
                c.start()

        @pl.when(nxt >= 0)
        def _():
            for c in fetch(nxt, 1 - s):
                c.start()

        for c in fetch(e_i, s):
            c.wait()
        wg_ref[0] = wg32_ref[s].astype(BF16)
        wu_ref[0] = wu32_ref[s].astype(BF16)
        wd_ref[0] = wd32_ref[s].astype(BF16)

    @pl.when(active)
    def _():
        e = te_ref[i].astype(F32)
        xs = xs_ref[...]
        x = xs[:, :D]
        r = xs[:, D:].astype(F32)
        w_a = r[:, 0:1] + r[:, 1:2] + r[:, 2:3]
        w_b = r[:, 3:4] + r[:, 4:5] + r[:, 5:6]
        w = jnp.where(r[:, 6:7] == e, w_a, w_b)
        hid = _silu(_dot(x, wg_ref[0])) * _dot(x, wu_ref[0]) * w
        ys_ref[...] = _dot(hid.astype(BF16), wd_ref[0]).astype(BF16)

    @pl.when(i >= nt_ref[0])
    def _():
        ys_ref[...] = jnp.zeros_like(ys_ref)


def _moe_ffn(xs, schedule, layer, w_gate, w_up, w_down):
    tile_expert, n_tiles, e_next, e_slot = schedule
    grid_spec = pltpu.PrefetchScalarGridSpec(
        num_scalar_prefetch=4,
        grid=(NT,),
        in_specs=[pl.BlockSpec((TMS, XW), lambda i, te, nt, *_: (jnp.minimum(i, nt[0] - 1), 0)),
                  pl.BlockSpec(memory_space=pl.ANY), pl.BlockSpec(memory_space=pl.ANY),
                  pl.BlockSpec(memory_space=pl.ANY)],
        out_specs=pl.BlockSpec((TMS, D), lambda i, *_: (i, 0)),
        scratch_shapes=[pltpu.VMEM((2, D, DE), F32), pltpu.VMEM((2, D, DE), F32), pltpu.VMEM((2, DE, D), F32),
                        pltpu.VMEM((1, D, DE), BF16), pltpu.VMEM((1, D, DE), BF16), pltpu.VMEM((1, DE, D), BF16),
                        pltpu.SemaphoreType.DMA((2,))],
    )
    return pl.pallas_call(
        functools.partial(_ffn_kernel, layer=layer),
        out_shape=jax.ShapeDtypeStruct((R_TOT, D), BF16),
        grid_spec=grid_spec,
        compiler_params=_cp(("arbitrary",)),
        name="moe_ffn",
    )(tile_expert, n_tiles, e_next, e_slot, xs, w_gate, w_up, w_down)


def _unperm_kernel(gtab_ref, ntot_ref, pos_ref, x1_ref, gate_ref, lg_ref, lb_ref, ys_ref, *rest, split):
    if split:
        oc_ref, ol_ref, ys_scr, sem = rest
    else:
        o_ref, ys_scr, sem = rest
    blk = pl.program_id(0)
    slot = blk % 2

    def copy_to(s):
        return lambda l, g, n: pltpu.make_async_copy(ys_ref.at[pl.ds(g, n)], ys_scr.at[s, pl.ds(l, n)], sem.at[s])

    def fetch(b, s):
        ys_scr[s] = jnp.zeros((R_LOC, D), BF16)
        _segment_start(gtab_ref, ntot_ref, b, copy_to(s))

    @pl.when(blk == 0)
    def _():
        fetch(0, 0)

    @pl.when(blk + 1 < NBP)
    def _():
        fetch(blk + 1, 1 - slot)

    _segment_wait(ntot_ref[blk], copy_to(slot))
    pos = pos_ref[...]
    pos_a, pos_b = pos[:, 0:1], pos[:, 1:2]
    acc = jnp.zeros((TP, D), F32)
    for t in range(R_LOC // PT):
        lio = (lax.broadcasted_iota(jnp.int32, (TP, PT), 1) + t * PT).astype(F32)
        onehot = jnp.where((lio == pos_a) | (lio == pos_b), 1.0, 0.0).astype(BF16)
        acc = acc + _dot(onehot, ys_scr[slot, t * PT:(t + 1) * PT, :])
    res = _layernorm(DN_ALPHA * x1_ref[...] + gate_ref[0] * acc, lg_ref[...], lb_ref[...])
    if split:
        @pl.when(blk < N_CTX // TP)
        def _():
            oc_ref[...] = res

        @pl.when(blk >= N_CTX // TP)
        def _():
            ol_ref[...] = res
    else:
        o_ref[...] = res


def _moe_unpermute(ys, pos, x1, mods, ln_g, ln_b, tables, split):
    gtab, ntot = tables[:2]
    nc = N_CTX // TP
    if split:
        out_shape = [jax.ShapeDtypeStruct((N_CTX, D), F32), jax.ShapeDtypeStruct((N_LAT, D), F32)]
        out_specs = [pl.BlockSpec((TP, D), lambda i, *_: (jnp.minimum(i, nc - 1), 0)),
                     pl.BlockSpec((TP, D), lambda i, *_: (jnp.maximum(i - nc, 0), 0))]
    else:
        out_shape = jax.ShapeDtypeStruct((N_TOK, D), F32)
        out_specs = pl.BlockSpec((TP, D), lambda i, *_: (i, 0))
    grid_spec = pltpu.PrefetchScalarGridSpec(
        num_scalar_prefetch=2,
        grid=(NBP,),
        in_specs=[pl.BlockSpec((TP, ROUTE_W), lambda i, *_: (i, 0)),
                  pl.BlockSpec((TP, D), lambda i, *_: (i, 0)),
                  _mod_spec(5, _mod_row_tp),
                  pl.BlockSpec((1, D), lambda i, *_: (0, 0)),
                  pl.BlockSpec((1, D), lambda i, *_: (0, 0)),
                  pl.BlockSpec(memory_space=pl.ANY)],
        out_specs=out_specs,
        scratch_shapes=[pltpu.VMEM((2, R_LOC, D), BF16), pltpu.SemaphoreType.DMA((2,))],
    )
    return pl.pallas_call(
        functools.partial(_unperm_kernel, split=split),
        out_shape=out_shape,
        grid_spec=grid_spec,
        compiler_params=_cp(("arbitrary",)),
        name="moe_unpermute",
    )(gtab, ntot, pos, x1, mods, ln_g.reshape(1, D), ln_b.reshape(1, D), ys)


def _segment_tables(cnt):
    c = cnt[:, 0, :NE].astype(jnp.int32)
    pc = (c + SEG - 1) // SEG * SEG
    loff = jnp.cumsum(pc, axis=1) - pc
    tot = pc.sum(axis=0)
    totp = (tot + TMS - 1) // TMS * TMS
    eend = jnp.cumsum(totp)
    goff = (eend - totp)[None, :] + jnp.cumsum(pc, axis=0) - pc
    n_tiles = (eend[-1] // TMS).reshape(1)
    tile_id = jnp.arange(NT, dtype=jnp.int32)
    tile_expert = jnp.minimum(jnp.sum(tile_id[:, None] >= (eend // TMS)[None, :], axis=1), NE - 1).astype(jnp.int32)
    loffv = jnp.zeros((NBP, 1, ROUTE_W), F32).at[:, 0, :NE].set(loff.astype(F32))
    ntot = (pc.sum(axis=1) // SEG).astype(jnp.int32)
    chunk = jnp.arange(MAXC, dtype=jnp.int32)
    first = loff // SEG
    e_of = jnp.minimum(jnp.sum(chunk[None, :, None] >= (first + pc // SEG)[:, None, :], axis=2), NE - 1)
    is_e = e_of[:, :, None] == jnp.arange(NE, dtype=jnp.int32)[None, None, :]
    pick = lambda a: jnp.sum(jnp.where(is_e, a[:, None, :], 0), axis=2)
    gtab = (pick(goff) + (chunk[None, :] - pick(first)) * SEG).reshape(-1).astype(jnp.int32)
    zrow = jnp.concatenate([eend - totp + tot, eend[-1:]]).astype(jnp.int32)
    zcnt = jnp.concatenate([(totp - tot) // SEG, (R_TOT - eend[-1:]) // SEG]).astype(jnp.int32)
    ids = jnp.arange(NE, dtype=jnp.int32)
    used = totp > 0
    later = (ids[None, :] > ids[:, None]) & used[None, :]
    e_next = jnp.min(jnp.where(later, ids[None, :], NE), axis=1)
    e_next = jnp.where(e_next == NE, -1, e_next).astype(jnp.int32)
    e_slot = ((jnp.cumsum(used.astype(jnp.int32)) - 1) % 2).astype(jnp.int32)
    return (gtab, ntot, zrow, zcnt, loffv), (tile_expert, n_tiles.astype(jnp.int32), e_next, e_slot)


def _moe(h2, rinfo, cnt, x1, mods, ln_g, ln_b, layer, w_gate, w_up, w_down, split):
    tables, schedule = _segment_tables(cnt)
    pos, xs = _moe_permute(h2, rinfo, tables)
    ys = _moe_ffn(xs, schedule, layer, w_gate, w_up, w_down)
    return _moe_unpermute(ys, pos, x1, mods, ln_g, ln_b, tables, split)


def kernel(x_prompt, x_sample, state_gla_S, state_mlstm_C, state_mlstm_n, state_mlstm_m, c, c_ctx,
           gla_w_in, gla_w_gate, gla_b_gate, gla_norm_g, gla_w_out,
           mlstm_w_in, mlstm_b_gates, mlstm_norm_g, mlstm_w_out,
           adaln_w, adaln_b, ln_g, ln_b,
           moe_w_group, moe_b_group, moe_w_expert, moe_b_expert, moe_w_gate, moe_w_up, moe_w_down):
    cvecs = jnp.zeros((MOD_ROWS, D), F32).at[0].set(c_ctx).at[1:1 + N_LAT_SEQ].set(c)
    mods_all = _modulation(cvecs, adaln_w, adaln_b)
    x = (x_prompt.reshape(N_CTX, D), x_sample.reshape(N_LAT, D), _grid_posemb(T_LAT))
    gla_states, ml_c, ml_n, ml_m = [], [], [], []
    for l in range(DEPTH):
        s = l // 2
        mods = mods_all[l].reshape(MOD_ROWS * 6, 1, D)
        if l % 2 == 0:
            q, k, v, r, g = _gla_in(x, mods, gla_w_in[s], gla_w_gate[s], gla_b_gate[s])
            o_ctx, s_new = _gla_scan(q, k, v, g, None, s)
            (o_lat,) = _gla_scan(q, k, v, g, state_gla_S, s)
            gla_states.append(s_new)
            norm_g, w_out = gla_norm_g[s], gla_w_out[s]
        else:
            q, k, v, r, gc, gr = _mlstm_in(x, mods, mlstm_w_in[s], mlstm_b_gates[s])
            o_ctx, c_new, n_new, m_new = _mlstm_scan(q, k, v, gc, gr, None, s)
            (o_lat,) = _mlstm_scan(q, k, v, gc, gr, (state_mlstm_C, state_mlstm_n, state_mlstm_m), s)
            ml_c.append(c_new)
            ml_n.append(n_new[:, :, :, 0, :])
            ml_m.append(m_new[:, :, :, 0, 0])
            norm_g, w_out = mlstm_norm_g[s], mlstm_w_out[s]
        w_route = jnp.zeros((D, ROUTE_W), F32).at[:, :NE].set(moe_w_expert[l]).at[:, NE:NE + N_GROUPS].set(
            moe_w_group[l])
        b_route = jnp.zeros((1, ROUTE_W), F32).at[0, :NE].set(moe_b_expert[l]).at[0, NE:NE + N_GROUPS].set(
            moe_b_group[l])
        x1, h2, rinfo, cnt = _out_proj(o_ctx, o_lat, r, x, mods, norm_g, w_out, ln_g[l, 0], ln_b[l, 0], w_route, b_route)
        x = _moe(h2, rinfo, cnt, x1, mods, ln_g[l, 1], ln_b[l, 1], l, moe_w_gate, moe_w_up, moe_w_down,
                 split=(l == DEPTH - 1))
    y_ctx = x[0].reshape(N_CTX_SEQ, T_CTX, D)
    y_lat = x[1].reshape(N_LAT_SEQ, T_LAT, D)
    return (y_ctx, y_lat, jnp.stack(gla_states, 1), jnp.stack(ml_c, 1), jnp.stack(ml_n, 1), jnp.stack(ml_m, 1))
```
